```python
import jax, jax.numpy as jnp
from jax import lax
import numpy as np

D_MODEL = 1024
BATCH = 8
SEQ = 2048
DEPTH = 2

HEAD_DIM = 64
ROPE_THETA = 10000.0
NORM_EPS = 1e-6
N_BRANCH = 4
A_WIDTH = 512
A_CONV = 3
B_WIDTH = 512
B_CONV = 31
C_HEADS = 8
C_KV_HEADS = 2
C_GROUP = C_HEADS // C_KV_HEADS
C_WIDTH = C_HEADS * HEAD_DIM
C_KV_WIDTH = C_KV_HEADS * HEAD_DIM
CMP_BLOCK = 32
CMP_STRIDE = 16
CMP_HIDDEN = 128
SLC_BLOCK = 64
SLC_TOPN = 16
SLC_QCHUNK = 32
WIN = 512
WIN_QBLOCK = 128
D_HEADS = 8
D_WIDTH = D_HEADS * HEAD_DIM
MOBA_BLOCK = 256
MOBA_TOPK = 3
MOBA_QCHUNK = 16
PEER_HEADS = 8
PEER_NKEYS = 128
PEER_EXPERTS = PEER_NKEYS * PEER_NKEYS
PEER_QDIM = 256
PEER_TOPK = 16
PEER_TCHUNK = 128

IN_SIZES = (A_WIDTH, A_WIDTH, A_WIDTH, B_WIDTH, B_WIDTH, C_WIDTH) + (C_KV_WIDTH,) * 6 + (3 * C_HEADS, D_WIDTH, D_WIDTH, D_WIDTH, N_BRANCH * D_MODEL)
IN_COLS = sum(IN_SIZES)
IN_SPLITS = [int(v) for v in np.cumsum(IN_SIZES)[:-1]]

kernel_name = 'hybrid_conv_nsa_moba_peer_block'


def rms_norm(x, g):
    xf = x.astype(jnp.float32)
    y = xf * lax.rsqrt(jnp.mean(xf * xf, axis=-1, keepdims=True) + NORM_EPS)
    return (y * g.astype(jnp.float32)).astype(x.dtype)


def layer_norm(x, g, b):
    xf = x.astype(jnp.float32)
    mu = jnp.mean(xf, axis=-1, keepdims=True)
    var = jnp.mean(jnp.square(xf - mu), axis=-1, keepdims=True)
    y = (xf - mu) * lax.rsqrt(var + NORM_EPS)
    return (y * g.astype(jnp.float32) + b.astype(jnp.float32)).astype(x.dtype)


def rope_tables(positions):
    inv = 1.0 / (ROPE_THETA ** (jnp.arange(0, HEAD_DIM, 2, dtype=jnp.float32) / HEAD_DIM))
    ang = positions.astype(jnp.float32)[..., None] * inv
    return jnp.cos(ang)[:, :, None, :], jnp.sin(ang)[:, :, None, :]


def apply_rope(x, cos, sin):
    xf = x.astype(jnp.float32)
    x1, x2 = jnp.split(xf, 2, axis=-1)
    return jnp.concatenate([x1 * cos - x2 * sin, x2 * cos + x1 * sin], axis=-1).astype(x.dtype)


def causal_dwconv(x, w):
    k, ch = w.shape
    return lax.conv_general_dilated(x, w[:, None, :].astype(x.dtype), window_strides=(1,), padding=[(k - 1, 0)], dimension_numbers=('NWC', 'WIO', 'NWC'), feature_group_count=ch)


def masked_softmax(s, mask, axis):
    s = jnp.where(mask, s.astype(jnp.float32), -jnp.inf)
    m = jnp.max(s, axis=axis, keepdims=True)
    m = jnp.where(jnp.isfinite(m), m, 0.0)
    e = jnp.where(mask, jnp.exp(s - m), 0.0)
    return e / jnp.maximum(jnp.sum(e, axis=axis, keepdims=True), 1e-30)


def short_conv_mixer(a_b, a_c, a_x, conv_w, w_out):
    return (a_b * causal_dwconv(a_c * a_x, conv_w)) @ w_out


def conformer_conv_module(b_a, b_g, conv_w, conv_b, ln_g, ln_b, w_out):
    u = b_a * jax.nn.sigmoid(b_g)
    u = causal_dwconv(u, conv_w) + conv_b
    u = jax.nn.silu(layer_norm(u, ln_g, ln_b))
    return u @ w_out


def nsa_compress(k, pos, w1, w2):
    bsz, s = k.shape[:2]
    n_cmp = (s - CMP_BLOCK) // CMP_STRIDE + 1
    idx = jnp.arange(n_cmp)[:, None] * CMP_STRIDE + jnp.arange(CMP_BLOCK)[None, :]
    blk = k[:, idx] + pos[None, None, :, None, :]
    blk = blk.transpose(0, 1, 3, 2, 4).reshape(bsz, n_cmp, C_KV_HEADS, CMP_BLOCK * HEAD_DIM)
    return jax.nn.gelu(blk @ w1, approximate=False) @ w2


def nsa_attention(q, kc, vc, ks, vs, kw, vw, gate_logits, cmp_pos, cmp_w1, cmp_w2, cos, sin, w_out):
    bsz, s = q.shape[:2]
    scale = HEAD_DIM ** -0.5
    t = jnp.arange(s)
    q = apply_rope(q.reshape(bsz, s, C_HEADS, HEAD_DIM), cos, sin).reshape(bsz, s, C_KV_HEADS, C_GROUP, HEAD_DIM)
    heads = lambda z: z.reshape(bsz, s, C_KV_HEADS, HEAD_DIM)
    kc = apply_rope(heads(kc), cos, sin)
    ks = apply_rope(heads(ks), cos, sin)
    kw = apply_rope(heads(kw), cos, sin)
    vc, vs, vw = heads(vc), heads(vs), heads(vw)

    k_cmp = nsa_compress(kc, cmp_pos[0], cmp_w1[0], cmp_w2[0])
    v_cmp = nsa_compress(vc, cmp_pos[1], cmp_w1[1], cmp_w2[1])
    n_cmp = k_cmp.shape[1]
    cmp_start = jnp.arange(n_cmp) * CMP_STRIDE
    s_cmp = jnp.einsum('bsgrd,bngd->bgrsn', q, k_cmp) * scale
    p_cmp = masked_softmax(s_cmp, (cmp_start + CMP_BLOCK - 1)[None, :] <= t[:, None], -1)
    o_cmp = jnp.einsum('bgrsn,bngd->bsgrd', p_cmp.astype(v_cmp.dtype), v_cmp)

    n_slc = s // SLC_BLOCK
    top_n = min(SLC_TOPN, n_slc)
    slc_start = jnp.arange(n_slc) * SLC_BLOCK
    overlap = ((cmp_start[:, None] < slc_start[None, :] + SLC_BLOCK) & (cmp_start[:, None] + CMP_BLOCK > slc_start[None, :])).astype(jnp.float32)
    imp = jnp.einsum('bgrsn,nj->bgsj', p_cmp, overlap)
    cur = t // SLC_BLOCK
    j = jnp.arange(n_slc)
    forced = (j[None, :] == 0) | (j[None, :] == cur[:, None]) | (j[None, :] == cur[:, None] - 1)
    imp = jnp.where(forced, jnp.inf, jnp.where(slc_start[None, :] > t[:, None], -jnp.inf, imp))
    _, sel = lax.top_k(imp, top_n)

    def to_blocks(z):
        return z.reshape(bsz, n_slc, SLC_BLOCK, C_KV_HEADS, HEAD_DIM).transpose(0, 3, 1, 2, 4).reshape(bsz, C_KV_HEADS, n_slc, SLC_BLOCK * HEAD_DIM)

    ks_blk, vs_blk = to_blocks(ks), to_blocks(vs)
    nq = s // SLC_QCHUNK
    q_ch = jnp.moveaxis(q.reshape(bsz, nq, SLC_QCHUNK, C_KV_HEADS, C_GROUP, HEAD_DIM), 1, 0)
    sel_ch = jnp.moveaxis(sel.reshape(bsz, C_KV_HEADS, nq, SLC_QCHUNK, top_n), 2, 0)
    t_ch = t.reshape(nq, SLC_QCHUNK)

    def slc_chunk(args):
        qc, ic, tc = args
        flat = ic.reshape(bsz, C_KV_HEADS, SLC_QCHUNK * top_n, 1)
        shp = (bsz, C_KV_HEADS, SLC_QCHUNK, top_n, SLC_BLOCK, HEAD_DIM)
        kg = jnp.take_along_axis(ks_blk, flat, axis=2).reshape(shp)
        vg = jnp.take_along_axis(vs_blk, flat, axis=2).reshape(shp)
        sc = jnp.einsum('bqgrd,bgqnkd->bgrqnk', qc, kg) * scale
        kpos = ic[..., None] * SLC_BLOCK + jnp.arange(SLC_BLOCK)
        mask = (kpos <= tc[:, None, None])[:, :, None]
        p = masked_softmax(sc, mask, (-2, -1))
        return jnp.einsum('bgrqnk,bgqnkd->bqgrd', p.astype(vg.dtype), vg)

    o_slc = jnp.moveaxis(lax.map(slc_chunk, (q_ch, sel_ch, t_ch)), 0, 1).reshape(bsz, s, C_KV_HEADS, C_GROUP, HEAD_DIM)

    nqb = s // WIN_QBLOCK
    nband = WIN // WIN_QBLOCK + 1

    def band(z):
        zp = jnp.pad(z, ((0, 0), (WIN, 0), (0, 0), (0, 0))).reshape(bsz, nqb + nband - 1, WIN_QBLOCK, C_KV_HEADS, HEAD_DIM)
        return jnp.concatenate([zp[:, i:i + nqb] for i in range(nband)], axis=2)

    kb, vb = band(kw), band(vw)
    qb = q.reshape(bsz, nqb, WIN_QBLOCK, C_KV_HEADS, C_GROUP, HEAD_DIM)
    sc = jnp.einsum('bnqgrd,bnkgd->bngrqk', qb, kb) * scale
    tq = t.reshape(nqb, WIN_QBLOCK)[:, :, None]
    kpos = (jnp.arange(nqb)[:, None] * WIN_QBLOCK - WIN + jnp.arange(nband * WIN_QBLOCK)[None, :])[:, None, :]
    mask = (kpos >= 0) & (kpos <= tq) & (kpos > tq - WIN)
    p = masked_softmax(sc, mask[None, :, None, None], -1)
    o_win = jnp.einsum('bngrqk,bnkgd->bnqgrd', p.astype(vb.dtype), vb).reshape(bsz, s, C_KV_HEADS, C_GROUP, HEAD_DIM)

    g = jax.nn.sigmoid(gate_logits).reshape(bsz, s, 3, C_KV_HEADS, C_GROUP, 1)
    o = g[:, :, 0] * o_cmp + g[:, :, 1] * o_slc + g[:, :, 2] * o_win
    return o.reshape(bsz, s, C_WIDTH) @ w_out


def moba_attention(q, k, v, cos, sin, w_out):
    bsz, s = q.shape[:2]
    scale = HEAD_DIM ** -0.5
    heads = lambda z: z.reshape(bsz, s, D_HEADS, HEAD_DIM)
    q = apply_rope(heads(q), cos, sin)
    k = apply_rope(heads(k), cos, sin)
    v = heads(v)
    nb = -(-s // MOBA_BLOCK)
    pad = nb * MOBA_BLOCK - s
    kp = jnp.pad(k, ((0, 0), (0, pad), (0, 0), (0, 0))).reshape(bsz, nb, MOBA_BLOCK, D_HEADS, HEAD_DIM)
    vp = jnp.pad(v, ((0, 0), (0, pad), (0, 0), (0, 0))).reshape(bsz, nb, MOBA_BLOCK, D_HEADS, HEAD_DIM)
    t = jnp.arange(s)
    own = t // MOBA_BLOCK
    kmean = jnp.mean(kp.astype(jnp.float32), axis=2).astype(q.dtype)
    gate = jnp.einsum('bshd,bnhd->bhsn', q, kmean)
    past = jnp.arange(nb)[None, :] < own[:, None]
    gate = jnp.where(past, gate.astype(jnp.float32), -jnp.inf)
    n_pick = min(MOBA_TOPK, nb)
    gval, gidx = lax.top_k(gate, n_pick)
    own_idx = jnp.broadcast_to(own[None, None, :, None], (bsz, D_HEADS, s, 1)).astype(gidx.dtype)
    idx = jnp.concatenate([gidx, own_idx], axis=-1)
    valid = jnp.concatenate([jnp.isfinite(gval), jnp.ones(own_idx.shape, dtype=bool)], axis=-1)
    n_sel = n_pick + 1
    kblk = kp.transpose(0, 3, 1, 2, 4).reshape(bsz, D_HEADS, nb, MOBA_BLOCK * HEAD_DIM)
    vblk = vp.transpose(0, 3, 1, 2, 4).reshape(bsz, D_HEADS, nb, MOBA_BLOCK * HEAD_DIM)
    nq = s // MOBA_QCHUNK
    q_ch = jnp.moveaxis(q.reshape(bsz, nq, MOBA_QCHUNK, D_HEADS, HEAD_DIM), 1, 0)
    idx_ch = jnp.moveaxis(idx.reshape(bsz, D_HEADS, nq, MOBA_QCHUNK, n_sel), 2, 0)
    val_ch = jnp.moveaxis(valid.reshape(bsz, D_HEADS, nq, MOBA_QCHUNK, n_sel), 2, 0)
    t_ch = t.reshape(nq, MOBA_QCHUNK)

    def moba_chunk(args):
        qc, ic, vc, tc = args
        flat = ic.reshape(bsz, D_HEADS, MOBA_QCHUNK * n_sel, 1)
        shp = (bsz, D_HEADS, MOBA_QCHUNK, n_sel, MOBA_BLOCK, HEAD_DIM)
        kg = jnp.take_along_axis(kblk, flat, axis=2).reshape(shp)
        vg = jnp.take_along_axis(vblk, flat, axis=2).reshape(shp)
        sc = jnp.einsum('bqhd,bhqnkd->bhqnk', qc, kg) * scale
        kpos = ic[..., None] * MOBA_BLOCK + jnp.arange(MOBA_BLOCK)
        mask = vc[..., None] & (kpos <= tc[:, None, None])
        p = masked_softmax(sc, mask, (-2, -1))
        return jnp.einsum('bhqnk,bhqnkd->bqhd', p.astype(vg.dtype), vg)

    o = jnp.moveaxis(lax.map(moba_chunk, (q_ch, idx_ch, val_ch, t_ch)), 0, 1).reshape(bsz, s, D_WIDTH)
    return o @ w_out


def peer_ffn(h, wq, subkeys, u, v):
    bsz, s, d = h.shape
    hf = h.reshape(bsz * s, d)
    q = (hf @ wq).reshape(-1, PEER_HEADS, 2, PEER_QDIM // 2)
    qf = q.astype(jnp.float32)
    q = (qf * lax.rsqrt(jnp.mean(qf * qf, axis=-1, keepdims=True) + NORM_EPS)).astype(h.dtype)
    sc = jnp.einsum('thpd,hpnd->thpn', q, subkeys).astype(jnp.float32)
    v1, i1 = lax.top_k(sc[:, :, 0], PEER_TOPK)
    v2, i2 = lax.top_k(sc[:, :, 1], PEER_TOPK)
    cand = (v1[..., :, None] + v2[..., None, :]).reshape(-1, PEER_HEADS, PEER_TOPK * PEER_TOPK)
    top, ci = lax.top_k(cand, PEER_TOPK)
    expert = jnp.take_along_axis(i1, ci // PEER_TOPK, axis=-1) * PEER_NKEYS + jnp.take_along_axis(i2, ci % PEER_TOPK, axis=-1)
    g = jax.nn.softmax(top, axis=-1)
    nch = hf.shape[0] // PEER_TCHUNK

    def peer_chunk(args):
        hc, ec, gc = args
        ue = jnp.take(u, ec, axis=0)
        act = jax.nn.gelu(jnp.einsum('td,thkd->thk', hc, ue), approximate=False)
        w = (gc * act).astype(v.dtype)
        return jnp.einsum('thk,thkd->td', w, jnp.take(v, ec, axis=0))

    out = lax.map(peer_chunk, (hf.reshape(nch, PEER_TCHUNK, d), expert.reshape(nch, PEER_TCHUNK, PEER_HEADS, PEER_TOPK), g.reshape(nch, PEER_TCHUNK, PEER_HEADS, PEER_TOPK)))
    return out.reshape(bsz, s, d)


def setup_inputs(seed: int = 0) -> dict:
    key = jax.random.key(seed)
    ks = jax.random.split(key, 32)

    def nrm(k, shape, scale):
        return scale * jax.random.normal(k, shape, jnp.float32)

    L = DEPTH
    return {
        'x': nrm(ks[0], (BATCH, SEQ, D_MODEL), 1.0),
        'c': nrm(ks[1], (BATCH, D_MODEL), 1.0),
        'positions': jax.random.randint(ks[2], (BATCH, 1), 0, 4096, dtype=jnp.int32) + jnp.arange(SEQ, dtype=jnp.int32)[None, :],
        'mod_w': nrm(ks[3], (L, D_MODEL, 6 * D_MODEL), 0.5 * D_MODEL ** -0.5),
        'mod_b': nrm(ks[4], (L, 6 * D_MODEL), 0.02),
        'norm_mix_g': 1.0 + nrm(ks[5], (L, D_MODEL), 0.05),
        'norm_ffn_g': 1.0 + nrm(ks[6], (L, D_MODEL), 0.05),
        'w_in': nrm(ks[7], (L, D_MODEL, IN_COLS), D_MODEL ** -0.5),
        'a_conv_w': nrm(ks[8], (L, A_CONV, A_WIDTH), A_CONV ** -0.5),
        'a_out': nrm(ks[9], (L, A_WIDTH, D_MODEL), A_WIDTH ** -0.5),
        'b_conv_w': nrm(ks[10], (L, B_CONV, B_WIDTH), B_CONV ** -0.5),
        'b_conv_b': nrm(ks[11], (L, B_WIDTH), 0.02),
        'b_ln_g': 1.0 + nrm(ks[12], (L, B_WIDTH), 0.05),
        'b_ln_b': nrm(ks[13], (L, B_WIDTH), 0.02),
        'b_out': nrm(ks[14], (L, B_WIDTH, D_MODEL), B_WIDTH ** -0.5),
        'c_cmp_pos': nrm(ks[15], (L, 2, CMP_BLOCK, HEAD_DIM), 0.1),
        'c_cmp_w1': nrm(ks[16], (L, 2, CMP_BLOCK * HEAD_DIM, CMP_HIDDEN), (CMP_BLOCK * HEAD_DIM) ** -0.5),
        'c_cmp_w2': nrm(ks[17], (L, 2, CMP_HIDDEN, HEAD_DIM), CMP_HIDDEN ** -0.5),
        'c_out': nrm(ks[18], (L, C_WIDTH, D_MODEL), C_WIDTH ** -0.5),
        'd_out': nrm(ks[19], (L, D_WIDTH, D_MODEL), D_WIDTH ** -0.5),
        'w_o': nrm(ks[20], (L, D_MODEL, D_MODEL), D_MODEL ** -0.5),
        'peer_wq': nrm(ks[21], (L, D_MODEL, PEER_HEADS * PEER_QDIM), D_MODEL ** -0.5),
        'peer_subkeys': nrm(ks[22], (L, PEER_HEADS, 2, PEER_NKEYS, PEER_QDIM // 2), (PEER_QDIM // 2) ** -0.5),
        'peer_u': nrm(ks[23], (L, PEER_EXPERTS, D_MODEL), D_MODEL ** -0.5),
        'peer_v': nrm(ks[24], (L, PEER_EXPERTS, D_MODEL), PEER_HEADS ** -0.5),
        'final_norm_g': 1.0 + nrm(ks[25], (D_MODEL,), 0.05),
    }


def reference(x, c, positions, mod_w, mod_b, norm_mix_g, norm_ffn_g, w_in, a_conv_w, a_out, b_conv_w, b_conv_b, b_ln_g, b_ln_b, b_out, c_cmp_pos, c_cmp_w1, c_cmp_w2, c_out, d_out, w_o, peer_wq, peer_subkeys, peer_u, peer_v, final_norm_g):
    bsz, s, d = x.shape
    cos, sin = rope_tables(positions)
    cond = jax.nn.silu(c)
    for l in range(DEPTH):
        mod = (cond @ mod_w[l] + mod_b[l])[:, None, :]
        sh1, sc1, gt1, sh2, sc2, gt2 = jnp.split(mod, 6, axis=-1)
        h = rms_norm(x, norm_mix_g[l]) * (1 + sc1) + sh1
        (a_b, a_c, a_x, b_a, b_g, c_q, c_kc, c_vc, c_ks, c_vs, c_kw, c_vw, c_gate, d_q, d_k, d_v, merge) = jnp.split(h @ w_in[l], IN_SPLITS, axis=-1)
        y_a = short_conv_mixer(a_b, a_c, a_x, a_conv_w[l], a_out[l])
        y_b = conformer_conv_module(b_a, b_g, b_conv_w[l], b_conv_b[l], b_ln_g[l], b_ln_b[l], b_out[l])
        y_c = nsa_attention(c_q, c_kc, c_vc, c_ks, c_vs, c_kw, c_vw, c_gate, c_cmp_pos[l], c_cmp_w1[l], c_cmp_w2[l], cos, sin, c_out[l])
        y_d = moba_attention(d_q, d_k, d_v, cos, sin, d_out[l])
        gates = jax.nn.sigmoid(merge).reshape(bsz, s, N_BRANCH, d)
        merged = gates[:, :, 0] * y_a + gates[:, :, 1] * y_b + gates[:, :, 2] * y_c + gates[:, :, 3] * y_d
        x = x + gt1 * (merged @ w_o[l])
        h2 = rms_norm(x, norm_ffn_g[l]) * (1 + sc2) + sh2
        x = x + gt2 * peer_ffn(h2, peer_wq[l], peer_subkeys[l], peer_u[l], peer_v[l])
    return rms_norm(x, final_norm_g)
```

```python
import functools

import numpy as np
import jax
import jax.numpy as jnp
from jax import lax
from jax.experimental import pallas as pl
from jax.experimental.pallas import tpu as pltpu

F32 = jnp.float32
BF16 = jnp.bfloat16
HIGHEST = lax.Precision.HIGHEST

D_MODEL = 1024
SEQ = 2048
HEAD_DIM = 64
ROPE_THETA = 10000.0
NORM_EPS = 1e-6
A_WIDTH = 512
A_CONV = 3
B_WIDTH = 512
B_CONV = 31
C_HEADS = 8
C_KV_HEADS = 2
C_GROUP = 4
CMP_BLOCK = 32
CMP_STRIDE = 16
CMP_HIDDEN = 128
N_CMP = (SEQ - CMP_BLOCK) // CMP_STRIDE + 1
SLC_BLOCK = 64
SLC_TOPN = 16
N_SLC = SEQ // SLC_BLOCK
WIN = 512
D_HEADS = 8
MOBA_BLOCK = 256
MOBA_TOPK = 3
N_MOBA = SEQ // MOBA_BLOCK
PEER_HEADS = 8
PEER_NKEYS = 128
PEER_EXPERTS = PEER_NKEYS * PEER_NKEYS
PEER_QDIM = 256
PEER_TOPK = 16

PROJ_COLS = 9728
COL_MERGE = 0
COL_KVG = 4096
COL_GATE = COL_KVG + 768
COL_A = 5120

VMEM_LIMIT = 56 * 1024 * 1024
NEG = -1e30

ATT_T = 256
CONV_T = 256
HALO = 32


def _params(sem):
    return pltpu.CompilerParams(dimension_semantics=sem, vmem_limit_bytes=VMEM_LIMIT)


def _gelu(x):
    return 0.5 * x * (1.0 + lax.erf(x * np.float32(np.sqrt(0.5))))


def _rope_table_kernel(pos_ref, inv_ref, sign_ref, cos_ref, sin_ref):
    ang = pos_ref[...] * inv_ref[...]
    cos_ref[...] = jnp.cos(ang)
    sin_ref[...] = jnp.sin(ang) * sign_ref[...]


def _rope_tables(positions):
    bsz, s = positions.shape
    inv = 1.0 / (ROPE_THETA ** (jnp.arange(0, HEAD_DIM, 2, dtype=F32) / HEAD_DIM))
    inv128 = jnp.tile(inv, 4)[None, :]
    sign = jnp.tile(jnp.concatenate([-jnp.ones(32, F32), jnp.ones(32, F32)]), 2)[None, :]
    pos = positions.astype(F32).reshape(bsz * s, 1)
    t = bsz * s
    cos, sin = pl.pallas_call(
        _rope_table_kernel,
        grid=(t // SEQ,),
        in_specs=[pl.BlockSpec((SEQ, 1), lambda i: (i, 0)),
                  pl.BlockSpec((1, 128), lambda i: (0, 0)),
                  pl.BlockSpec((1, 128), lambda i: (0, 0))],
        out_specs=[pl.BlockSpec((SEQ, 128), lambda i: (i, 0))] * 2,
        out_shape=[jax.ShapeDtypeStruct((t, 128), F32)] * 2,
        compiler_params=_params(("parallel",)),
        name="rope_tables",
    )(pos, inv128, sign)
    return cos.reshape(bsz, s, 128), sin.reshape(bsz, s, 128)


def _rope(x, cos, sin):
    w = x.shape[-1]
    lane = lax.broadcasted_iota(jnp.int32, x.shape, 1)
    swapped = jnp.where(lane % 64 < 32, pltpu.roll(x, w - 32, 1), pltpu.roll(x, 32, 1))
    return x * cos + swapped * sin


def _mod_kernel(c_ref, w_ref, b_ref, o_ref):
    c = c_ref[...]
    cond = c * jax.nn.sigmoid(c)
    o_ref[...] = jnp.dot(cond, w_ref[...], precision=HIGHEST, preferred_element_type=F32) + b_ref[...]


def _modulation(c, mod_w, mod_b):
    nl, d, n = mod_w.shape
    bsz = c.shape[0]
    tn = 1536
    return pl.pallas_call(
        _mod_kernel,
        grid=(nl, n // tn),
        in_specs=[pl.BlockSpec((bsz, d), lambda l, j: (0, 0)),
                  pl.BlockSpec((None, d, tn), lambda l, j: (l, 0, j)),
                  pl.BlockSpec((None, 1, tn), lambda l, j: (l, 0, j))],
        out_specs=pl.BlockSpec((None, bsz, tn), lambda l, j: (l, 0, j)),
        out_shape=jax.ShapeDtypeStruct((nl, bsz, n), F32),
        compiler_params=_params(("parallel", "parallel")),
        name="adaln_mod",
    )(c, mod_w, mod_b.reshape(nl, 1, n))


def _norm_matmul_kernel(x_ref, g_ref, sc_ref, sh_ref, w_ref, o_ref, *rest, emit_h):
    h_scr = rest[-1]

    @pl.when(pl.program_id(1) == 0)
    def _():
        x = x_ref[...]
        y = x * lax.rsqrt(jnp.mean(x * x, axis=-1, keepdims=True) + NORM_EPS)
        h = (y * g_ref[...]) * (1.0 + sc_ref[...]) + sh_ref[...]
        h_scr[...] = h.astype(BF16)
        if emit_h:
            rest[0][...] = h.astype(BF16)

    o_ref[...] = jnp.dot(h_scr[...], w_ref[...], preferred_element_type=F32)


def _norm_matmul(x2, g, mod3, sh_blk, sc_blk, w, *, emit_h=False, tm=1024, tn=512):
    t, d = x2.shape
    n = w.shape[1]
    per_b = SEQ // tm
    out_shape = [jax.ShapeDtypeStruct((t, n), F32)]
    out_specs = [pl.BlockSpec((tm, tn), lambda i, j: (i, j))]
    if emit_h:
        out_shape.append(jax.ShapeDtypeStruct((t, d), BF16))
        out_specs.append(pl.BlockSpec((tm, d), lambda i, j: (i, 0)))
    res = pl.pallas_call(
        functools.partial(_norm_matmul_kernel, emit_h=emit_h),
        grid=(t // tm, n // tn),
        in_specs=[pl.BlockSpec((tm, d), lambda i, j: (i, 0)),
                  pl.BlockSpec((1, d), lambda i, j: (0, 0)),
                  pl.BlockSpec((None, 1, d), lambda i, j: (i // per_b, 0, sc_blk)),
                  pl.BlockSpec((None, 1, d), lambda i, j: (i // per_b, 0, sh_blk)),
                  pl.BlockSpec((d, tn), lambda i, j: (0, j))],
        out_specs=out_specs,
        out_shape=out_shape,
        scratch_shapes=[pltpu.VMEM((tm, d), BF16)],
        compiler_params=_params(("parallel", "arbitrary")),
        name="norm_matmul",
    )(x2, g.reshape(1, d), mod3, mod3, w)
    return res if emit_h else res[0]


def _conv_kernel(ab_ref, ac_ref, ax_ref, ba_ref, bg_ref, pac_ref, pax_ref, pba_ref, pbg_ref,
                 aw_ref, bw_ref, bb_ref, lng_ref, lnb_ref, ua_ref, ub_ref, ext_ref):
    ts = ab_ref.shape[0]
    keep = (pl.program_id(1) > 0).astype(F32)

    ext_ref[0:HALO, :] = pac_ref[...] * pax_ref[...] * keep
    ext_ref[HALO:HALO + ts, :] = ac_ref[...] * ax_ref[...]
    acc = jnp.zeros((ts, A_WIDTH), F32)
    for k in range(A_CONV):
        off = HALO - (A_CONV - 1) + k
        acc = acc + aw_ref[k:k + 1, :] * ext_ref[off:off + ts, :]
    ua_ref[...] = (ab_ref[...] * acc).astype(BF16)

    ext_ref[0:HALO, :] = pba_ref[...] * jax.nn.sigmoid(pbg_ref[...]) * keep
    ext_ref[HALO:HALO + ts, :] = ba_ref[...] * jax.nn.sigmoid(bg_ref[...])
    acc = jnp.zeros((ts, B_WIDTH), F32) + bb_ref[...]
    for k in range(B_CONV):
        off = HALO - (B_CONV - 1) + k
        acc = acc + bw_ref[k:k + 1, :] * ext_ref[off:off + ts, :]
    mu = jnp.mean(acc, axis=-1, keepdims=True)
    cen = acc - mu
    var = jnp.mean(cen * cen, axis=-1, keepdims=True)
    y = cen * lax.rsqrt(var + NORM_EPS) * lng_ref[...] + lnb_ref[...]
    ub_ref[...] = (y * jax.nn.sigmoid(y)).astype(BF16)


def _conv_mixers(proj3, a_conv_w, b_conv_w, b_conv_b, b_ln_g, b_ln_b):
    bsz = proj3.shape[0]
    ts = CONV_T
    c0 = COL_A // 512
    r = ts // HALO

    def cur(k):
        return pl.BlockSpec((None, ts, 512), lambda b, i, k=k: (b, i, c0 + k))

    def prev(k):
        return pl.BlockSpec((None, HALO, 512), lambda b, i, k=k: (b, jnp.maximum(i * r - 1, 0), c0 + k))

    def full(shape):
        return pl.BlockSpec(shape, lambda b, i: (0,) * len(shape))

    return pl.pallas_call(
        _conv_kernel,
        grid=(bsz, SEQ // ts),
        in_specs=[cur(0), cur(1), cur(2), cur(3), cur(4), prev(1), prev(2), prev(3), prev(4),
                  full((A_CONV, A_WIDTH)), full((B_CONV, B_WIDTH)), full((1, B_WIDTH)),
                  full((1, B_WIDTH)), full((1, B_WIDTH))],
        out_specs=[pl.BlockSpec((None, ts, 512), lambda b, i: (b, i, 0))] * 2,
        out_shape=[jax.ShapeDtypeStruct((bsz, SEQ, 512), BF16)] * 2,
        scratch_shapes=[pltpu.VMEM((HALO + ts, 512), F32)],
        compiler_params=_params(("parallel", "arbitrary")),
        name="conv_mixers",
    )(proj3, proj3, proj3, proj3, proj3, proj3, proj3, proj3, proj3,
      a_conv_w, b_conv_w, b_conv_b.reshape(1, -1), b_ln_g.reshape(1, -1), b_ln_b.reshape(1, -1))


def _prep_kernel(cq_ref, dq_ref, dk_ref, dv_ref, kvg_ref, cos_ref, sin_ref,
                 qc_ref, kc_ref, vc_ref, ks_ref, vs_ref, kw_ref, vw_ref,
                 qd_ref, kd_ref, vd_ref, kmean_ref):
    cos = cos_ref[...]
    sin = sin_ref[...]
    cos4 = jnp.concatenate([cos] * 4, axis=1)
    sin4 = jnp.concatenate([sin] * 4, axis=1)
    scale = np.float32(HEAD_DIM ** -0.5)
    qc_ref[...] = _rope(cq_ref[...], cos4, sin4) * scale
    qd_ref[...] = _rope(dq_ref[...], cos4, sin4) * scale
    kd = _rope(dk_ref[...], cos4, sin4)
    kd_ref[...] = kd.astype(BF16)
    kmean_ref[...] = jnp.mean(kd, axis=0, keepdims=True)
    vd_ref[...] = dv_ref[...].astype(BF16)
    kc = _rope(kvg_ref[:, 0:128], cos, sin)
    vc = kvg_ref[:, 128:256]
    for g in range(C_KV_HEADS):
        kc_ref[g] = kc[:, g * 64:(g + 1) * 64]
        vc_ref[g] = vc[:, g * 64:(g + 1) * 64]
    ks_ref[...] = _rope(kvg_ref[:, 256:384], cos, sin).astype(BF16)
    vs_ref[...] = kvg_ref[:, 384:512].astype(BF16)
    kw_ref[...] = _rope(kvg_ref[:, 512:640], cos, sin).astype(BF16)
    vw_ref[...] = kvg_ref[:, 640:768].astype(BF16)


def _prep(proj3, cos, sin):
    bsz = proj3.shape[0]
    ts = MOBA_BLOCK
    c0 = COL_A // 512

    def col512(k):
        return pl.BlockSpec((None, ts, 512), lambda b, i: (b, i, c0 + k))

    row128 = pl.BlockSpec((None, ts, 128), lambda b, i: (b, i, 0))
    row512 = pl.BlockSpec((None, ts, 512), lambda b, i: (b, i, 0))
    head64 = pl.BlockSpec((None, C_KV_HEADS, ts, 64), lambda b, i: (b, 0, i, 0))
    s128 = jax.ShapeDtypeStruct((bsz, SEQ, 128), BF16)
    return pl.pallas_call(
        _prep_kernel,
        grid=(bsz, SEQ // ts),
        in_specs=[col512(5), col512(6), col512(7), col512(8),
                  pl.BlockSpec((None, ts, 1024), lambda b, i: (b, i, COL_KVG // 1024)),
                  row128, row128],
        out_specs=[row512, head64, head64, row128, row128, row128, row128,
                   row512, row512, row512,
                   pl.BlockSpec((None, None, 1, 512), lambda b, i: (b, i, 0, 0))],
        out_shape=[jax.ShapeDtypeStruct((bsz, SEQ, 512), F32),
                   jax.ShapeDtypeStruct((bsz, C_KV_HEADS, SEQ, 64), F32),
                   jax.ShapeDtypeStruct((bsz, C_KV_HEADS, SEQ, 64), F32),
                   s128, s128, s128, s128,
                   jax.ShapeDtypeStruct((bsz, SEQ, 512), F32),
                   jax.ShapeDtypeStruct((bsz, SEQ, 512), BF16),
                   jax.ShapeDtypeStruct((bsz, SEQ, 512), BF16),
                   jax.ShapeDtypeStruct((bsz, N_MOBA, 1, 512), F32)],
        compiler_params=_params(("parallel", "parallel")),
        name="attn_prep",
    )(proj3, proj3, proj3, proj3, proj3, cos, sin)


def _compress_kernel(kc_ref, vc_ref, pos_ref, w1_ref, w2_ref, kcmp_ref, vcmp_ref):
    half = CMP_STRIDE * HEAD_DIM
    row = lax.broadcasted_iota(jnp.int32, (128, CMP_HIDDEN), 0)
    for which, (src, dst) in enumerate(((kc_ref, kcmp_ref), (vc_ref, vcmp_ref))):
        bias = jnp.dot(pos_ref[which], w1_ref[which], precision=HIGHEST, preferred_element_type=F32)[0:1]
        for g in range(C_KV_HEADS):
            chunks = src[g]
            d1 = jnp.dot(chunks, w1_ref[which, 0:half, :], precision=HIGHEST, preferred_element_type=F32)
            d2 = jnp.dot(chunks, w1_ref[which, half:2 * half, :], precision=HIGHEST, preferred_element_type=F32)
            d2 = jnp.where(row < 127, pltpu.roll(d2, 127, 0), 0.0)
            hid = _gelu(d1 + d2 + bias)
            dst[g] = jnp.dot(hid, w2_ref[which], precision=HIGHEST, preferred_element_type=F32)


def _compress(kc, vc, cmp_pos, cmp_w1, cmp_w2):
    bsz = kc.shape[0]
    kc4 = kc.reshape(bsz, C_KV_HEADS, SEQ // CMP_STRIDE, CMP_STRIDE * HEAD_DIM)
    vc4 = vc.reshape(bsz, C_KV_HEADS, SEQ // CMP_STRIDE, CMP_STRIDE * HEAD_DIM)
    pos8 = jnp.broadcast_to(cmp_pos.reshape(2, 1, CMP_BLOCK * HEAD_DIM), (2, 8, CMP_BLOCK * HEAD_DIM))
    blk = pl.BlockSpec((None, C_KV_HEADS, 128, 1024), lambda b: (b, 0, 0, 0))
    out = pl.BlockSpec((None, C_KV_HEADS, 128, 64), lambda b: (b, 0, 0, 0))
    return pl.pallas_call(
        _compress_kernel,
        grid=(bsz,),
        in_specs=[blk, blk,
                  pl.BlockSpec((2, 8, 2048), lambda b: (0, 0, 0)),
                  pl.BlockSpec((2, 2048, CMP_HIDDEN), lambda b: (0, 0, 0)),
                  pl.BlockSpec((2, CMP_HIDDEN, 64), lambda b: (0, 0, 0))],
        out_specs=[out, out],
        out_shape=[jax.ShapeDtypeStruct((bsz, C_KV_HEADS, 128, 64), F32)] * 2,
        compiler_params=_params(("parallel",)),
        name="nsa_compress",
    )(kc4, vc4, pos8, cmp_w1, cmp_w2)


def _flash_step(q, k, v, mask, m_ref, l_ref, acc_ref, groups):
    rows = q.shape[0]
    tq = rows // groups
    tk = k.shape[0]
    s = lax.dot_general(q, k, (((1,), (1,)), ((), ())), preferred_element_type=F32)
    if mask is not None:
        s = jnp.where(mask[None], s.reshape(groups, tq, tk), NEG).reshape(rows, tk)
    m_old = m_ref[...]
    m_new = jnp.maximum(m_old, jnp.max(s, axis=-1, keepdims=True))
    alpha = jnp.exp(m_old - m_new)
    p = jnp.exp(s - m_new)
    l_ref[...] = alpha * l_ref[...] + jnp.sum(p, axis=-1, keepdims=True)
    acc_ref[...] = alpha * acc_ref[...] + jnp.dot(p.astype(BF16), v, preferred_element_type=F32)
    m_ref[...] = m_new


def _flash_init(m_ref, l_ref, acc_ref):
    m_ref[...] = jnp.full(m_ref.shape, NEG, F32)
    l_ref[...] = jnp.zeros(l_ref.shape, F32)
    acc_ref[...] = jnp.zeros(acc_ref.shape, F32)


def _rank_desc(vals, n):
    lane = lax.broadcasted_iota(jnp.int32, vals.shape, 1)
    rank = jnp.zeros(vals.shape, jnp.int32)
    for i in range(n):
        vi = vals[:, i:i + 1]
        ahead = (vi > vals) | ((vi == vals) & (lane > i))
        rank = rank + ahead.astype(jnp.int32)
    return rank


def _nsa_kernel(q_ref, kcmp_ref, vcmp_ref, ks_ref, vs_ref, kw_ref, vw_ref, gate_ref, ovl_ref, exp_ref,
                o_ref, m_ref, l_ref, acc_ref):
    tq = ATT_T
    i = pl.program_id(1)
    t0 = i * tq
    rows = C_GROUP * tq
    gates = jax.nn.sigmoid(gate_ref[...])
    t_col = t0 + lax.broadcasted_iota(jnp.int32, (tq, 1), 0)
    lane = lax.broadcasted_iota(jnp.int32, (tq, 128), 1)
    rr = lax.broadcasted_iota(jnp.int32, (tq, tq), 0)
    cc = lax.broadcasted_iota(jnp.int32, (tq, tq), 1)
    causal = cc <= rr
    win_tail = cc > rr

    def gate_rows(branch, g):
        return jnp.concatenate([gates[:, branch * 8 + g * 4 + r:branch * 8 + g * 4 + r + 1]
                                for r in range(C_GROUP)], axis=0)

    for g in range(C_KV_HEADS):
        hs = slice(g * 64, (g + 1) * 64)
        qf = jnp.concatenate([q_ref[:, (g * 4 + r) * 64:(g * 4 + r + 1) * 64] for r in range(C_GROUP)], axis=0)
        qb = qf.astype(BF16)

        s = lax.dot_general(qf, kcmp_ref[g], (((1,), (1,)), ((), ())), precision=HIGHEST,
                            preferred_element_type=F32).reshape(C_GROUP, tq, 128)
        vis = ((lane * CMP_STRIDE + (CMP_BLOCK - 1)) <= t_col)[None]
        sm = jnp.where(vis, s, NEG)
        e = jnp.where(vis, jnp.exp(sm - jnp.max(sm, axis=-1, keepdims=True)), 0.0)
        p = e / jnp.maximum(jnp.sum(e, axis=-1, keepdims=True), 1e-30)
        o_cmp = jnp.dot(p.reshape(rows, 128).astype(BF16), vcmp_ref[g].astype(BF16), preferred_element_type=F32)

        imp = jnp.dot(p[0] + p[1] + p[2] + p[3], ovl_ref[...], precision=HIGHEST, preferred_element_type=F32)
        cur = t_col // SLC_BLOCK
        forced = (lane == 0) | (lane == cur) | (lane == cur - 1)
        imp = jnp.where(forced, jnp.inf, jnp.where(lane * SLC_BLOCK > t_col, -jnp.inf, imp))
        sel = (_rank_desc(imp, N_SLC) < SLC_TOPN).astype(BF16)

        def sel_mask(jb):
            start = pl.multiple_of(jb * tq, tq)
            hit = jnp.dot(sel, exp_ref[:, pl.ds(start, tq)], preferred_element_type=F32)
            return hit > 0.5

        _flash_init(m_ref, l_ref, acc_ref)
        dstart = pl.multiple_of(t0, tq)
        _flash_step(qb, ks_ref[pl.ds(dstart, tq), hs], vs_ref[pl.ds(dstart, tq), hs],
                    sel_mask(i) & causal, m_ref, l_ref, acc_ref, C_GROUP)

        def slc_body(jb, carry):
            start = pl.multiple_of(jb * tq, tq)
            _flash_step(qb, ks_ref[pl.ds(start, tq), hs], vs_ref[pl.ds(start, tq), hs],
                        sel_mask(jb), m_ref, l_ref, acc_ref, C_GROUP)
            return carry

        lax.fori_loop(0, i, slc_body, 0)
        o_slc = acc_ref[...] / l_ref[...]

        _flash_init(m_ref, l_ref, acc_ref)
        _flash_step(qb, kw_ref[pl.ds(dstart, tq), hs], vw_ref[pl.ds(dstart, tq), hs],
                    causal, m_ref, l_ref, acc_ref, C_GROUP)

        @pl.when(i >= 1)
        def _():
            start = pl.multiple_of(t0 - tq, tq)
            _flash_step(qb, kw_ref[pl.ds(start, tq), hs], vw_ref[pl.ds(start, tq), hs],
                        None, m_ref, l_ref, acc_ref, C_GROUP)

        @pl.when(i >= 2)
        def _():
            start = pl.multiple_of(t0 - 2 * tq, tq)
            _flash_step(qb, kw_ref[pl.ds(start, tq), hs], vw_ref[pl.ds(start, tq), hs],
                        win_tail, m_ref, l_ref, acc_ref, C_GROUP)

        o_win = acc_ref[...] / l_ref[...]

        o = gate_rows(0, g) * o_cmp + gate_rows(1, g) * o_slc + gate_rows(2, g) * o_win
        for r in range(C_GROUP):
            o_ref[:, (g * 4 + r) * 64:(g * 4 + r + 1) * 64] = o[r * tq:(r + 1) * tq].astype(BF16)


def _nsa_constants():
    n = np.arange(128)[:, None]
    j = np.arange(128)[None, :]
    ovl = ((n * CMP_STRIDE < j * SLC_BLOCK + SLC_BLOCK) & (n * CMP_STRIDE + CMP_BLOCK > j * SLC_BLOCK)
           & (n < N_CMP) & (j < N_SLC)).astype(np.float32)
    expand = (np.arange(SEQ)[None, :] // SLC_BLOCK == np.arange(128)[:, None]).astype(np.float32)
    return jnp.asarray(ovl), jnp.asarray(expand, dtype=BF16)


def _nsa(qc, kcmp, vcmp, ks, vs, kw, vw, proj3):
    bsz = qc.shape[0]
    tq = ATT_T
    ovl, expand = _nsa_constants()
    seq128 = pl.BlockSpec((None, SEQ, 128), lambda b, i: (b, 0, 0))
    cmp = pl.BlockSpec((None, C_KV_HEADS, 128, 64), lambda b, i: (b, 0, 0, 0))
    rows = C_GROUP * tq
    return pl.pallas_call(
        _nsa_kernel,
        grid=(bsz, SEQ // tq),
        in_specs=[pl.BlockSpec((None, tq, 512), lambda b, i: (b, i, 0)),
                  cmp, cmp, seq128, seq128, seq128, seq128,
                  pl.BlockSpec((None, tq, 128), lambda b, i: (b, i, COL_GATE // 128)),
                  pl.BlockSpec((128, 128), lambda b, i: (0, 0)),
                  pl.BlockSpec((128, SEQ), lambda b, i: (0, 0))],
        out_specs=pl.BlockSpec((None, tq, 512), lambda b, i: (b, i, 0)),
        out_shape=jax.ShapeDtypeStruct((bsz, SEQ, 512), BF16),
        scratch_shapes=[pltpu.VMEM((rows, 1), F32), pltpu.VMEM((rows, 1), F32), pltpu.VMEM((rows, 64), F32)],
        compiler_params=_params(("parallel", "arbitrary")),
        name="nsa_attention",
    )(qc, kcmp, vcmp, ks, vs, kw, vw, proj3, ovl, expand)


def _moba_kernel(q_ref, k_ref, v_ref, kmean_ref, o_ref, m_ref, l_ref, acc_ref):
    tq = ATT_T
    i = pl.program_id(1)
    t0 = pl.multiple_of(i * tq, tq)
    lane = lax.broadcasted_iota(jnp.int32, (tq, N_MOBA), 1)
    rr = lax.broadcasted_iota(jnp.int32, (tq, tq), 0)
    cc = lax.broadcasted_iota(jnp.int32, (tq, tq), 1)
    causal = cc <= rr
    past = lane < i
    for h in range(D_HEADS):
        hs = slice(h * 64, (h + 1) * 64)
        qf = q_ref[:, hs]
        qb = qf.astype(BF16)
        gate = lax.dot_general(qf, kmean_ref[:, hs], (((1,), (1,)), ((), ())), precision=HIGHEST,
                               preferred_element_type=F32)
        gate = jnp.where(past, gate, -jnp.inf)
        sel = (past & (_rank_desc(gate, N_MOBA) < MOBA_TOPK)).astype(F32)

        _flash_init(m_ref, l_ref, acc_ref)
        _flash_step(qb, k_ref[pl.ds(t0, tq), hs], v_ref[pl.ds(t0, tq), hs], causal, m_ref, l_ref, acc_ref, 1)

        def body(jb, carry):
            start = pl.multiple_of(jb * tq, tq)
            picked = jnp.sum(jnp.where(lane == jb, sel, 0.0), axis=-1, keepdims=True) > 0.5
            _flash_step(qb, k_ref[pl.ds(start, tq), hs], v_ref[pl.ds(start, tq), hs],
                        picked, m_ref, l_ref, acc_ref, 1)
            return carry

        lax.fori_loop(0, i, body, 0)
        o_ref[:, hs] = (acc_ref[...] / l_ref[...]).astype(BF16)


def _moba(qd, kd, vd, kmean):
    bsz = qd.shape[0]
    tq = ATT_T
    seq512 = pl.BlockSpec((None, SEQ, 512), lambda b, i: (b, 0, 0))
    return pl.pallas_call(
        _moba_kernel,
        grid=(bsz, SEQ // tq),
        in_specs=[pl.BlockSpec((None, tq, 512), lambda b, i: (b, i, 0)), seq512, seq512,
                  pl.BlockSpec((None, N_MOBA, 512), lambda b, i: (b, 0, 0))],
        out_specs=pl.BlockSpec((None, tq, 512), lambda b, i: (b, i, 0)),
        out_shape=jax.ShapeDtypeStruct((bsz, SEQ, 512), BF16),
        scratch_shapes=[pltpu.VMEM((tq, 1), F32), pltpu.VMEM((tq, 1), F32), pltpu.VMEM((tq, 64), F32)],
        compiler_params=_params(("parallel", "arbitrary")),
        name="moba_attention",
    )(qd, kd, vd, kmean.reshape(bsz, N_MOBA, 512))


def _merge_kernel(ua_ref, ub_ref, oc_ref, od_ref, mg_ref, x_ref, gt_ref,
                  wa_ref, wb_ref, wc_ref, wd_ref, wo_ref, o_ref):
    d = D_MODEL
    merged = jnp.zeros(x_ref.shape, F32)
    for k, (u_ref, w_ref) in enumerate(((ua_ref, wa_ref), (ub_ref, wb_ref), (oc_ref, wc_ref), (od_ref, wd_ref))):
        y = jnp.dot(u_ref[...], w_ref[...], preferred_element_type=F32)
        merged = merged + jax.nn.sigmoid(mg_ref[:, k * d:(k + 1) * d]) * y
    o_ref[...] = x_ref[...] + gt_ref[...] * jnp.dot(merged.astype(BF16), wo_ref[...], preferred_element_type=F32)


def _merge(ua, ub, oc, od, proj2, x2, mod3, wa, wb, wc, wd, wo, *, tm=512):
    t, d = x2.shape
    per_b = SEQ // tm
    act = pl.BlockSpec((tm, 512), lambda i: (i, 0))
    wspec = pl.BlockSpec((512, d), lambda i: (0, 0))
    return pl.pallas_call(
        _merge_kernel,
        grid=(t // tm,),
        in_specs=[act, act, act, act,
                  pl.BlockSpec((tm, 4 * d), lambda i: (i, COL_MERGE // (4 * d))),
                  pl.BlockSpec((tm, d), lambda i: (i, 0)),
                  pl.BlockSpec((None, 1, d), lambda i: (i // per_b, 0, 2)),
                  wspec, wspec, wspec, wspec,
                  pl.BlockSpec((d, d), lambda i: (0, 0))],
        out_specs=pl.BlockSpec((tm, d), lambda i: (i, 0)),
        out_shape=jax.ShapeDtypeStruct((t, d), F32),
        compiler_params=_params(("parallel",)),
        name="merge_out",
    )(ua, ub, oc, od, proj2, x2, mod3, wa, wb, wc, wd, wo)


def _pop_max(work, idx):
    m = jnp.max(work, axis=0, keepdims=True)
    first = jnp.min(jnp.where(work == m, idx, work.shape[0]), axis=0, keepdims=True)
    return m, jnp.where(idx == first, -jnp.inf, work)


_PEER_PAIRS = [(i, j) for i in range(PEER_TOPK) for j in range(PEER_TOPK) if (i + 1) * (j + 1) <= PEER_TOPK]


def _peer_sel_kernel(q_ref, keys_ref, s1_ref, e1_ref, s2_ref, e2_ref, tau_ref):
    tt = q_ref.shape[0]
    half = PEER_QDIM // 2
    idx = lax.broadcasted_iota(jnp.int32, (PEER_NKEYS, tt), 0)
    npair = len(_PEER_PAIRS)
    pidx = lax.broadcasted_iota(jnp.int32, (npair, tt), 0)
    for h in range(PEER_HEADS):
        scores, tops = [], []
        for p in range(2):
            qh = q_ref[:, (h * 2 + p) * half:(h * 2 + p + 1) * half]
            qn = qh * lax.rsqrt(jnp.mean(qh * qh, axis=-1, keepdims=True) + NORM_EPS)
            st = lax.dot_general(keys_ref[h, p], qn, (((1,), (1,)), ((), ())), precision=HIGHEST,
                                 preferred_element_type=F32)
            work, vals = st, []
            for _ in range(PEER_TOPK):
                m, work = _pop_max(work, idx)
                vals.append(m)
            scores.append(st)
            tops.append(vals)
        cand = jnp.concatenate([tops[0][i] + tops[1][j] for (i, j) in _PEER_PAIRS], axis=0)
        best = []
        for _ in range(PEER_TOPK):
            m, cand = _pop_max(cand, pidx)
            best.append(m)
        z = jnp.ones_like(best[0])
        for k in range(1, PEER_TOPK):
            z = z + jnp.exp(best[k] - best[0])
        s1_ref[h] = scores[0]
        s2_ref[h] = scores[1]
        e1_ref[h] = jnp.exp(scores[0] - tops[0][0]) / z
        e2_ref[h] = jnp.exp(scores[1] - tops[1][0])
        tau_ref[h] = best[PEER_TOPK - 1]


def _peer_select(q2, subkeys, *, tt=512):
    t = q2.shape[0]
    big = pl.BlockSpec((PEER_HEADS, PEER_NKEYS, tt), lambda i: (0, 0, i))
    bshape = jax.ShapeDtypeStruct((PEER_HEADS, PEER_NKEYS, t), F32)
    return pl.pallas_call(
        _peer_sel_kernel,
        grid=(t // tt,),
        in_specs=[pl.BlockSpec((tt, PEER_HEADS * PEER_QDIM), lambda i: (i, 0)),
                  pl.BlockSpec((PEER_HEADS, 2, PEER_NKEYS, PEER_QDIM // 2), lambda i: (0, 0, 0, 0))],
        out_specs=[big, big, big, big, pl.BlockSpec((PEER_HEADS, 1, tt), lambda i: (0, 0, i))],
        out_shape=[bshape, bshape, bshape, bshape, jax.ShapeDtypeStruct((PEER_HEADS, 1, t), F32)],
        compiler_params=_params(("parallel",)),
        name="peer_select",
    )(q2, subkeys)


def _peer_main_kernel(h_ref, u_ref, vt_ref, s1_ref, e1_ref, s2_ref, e2_ref, tau_ref, x_ref, gt_ref, fg_ref,
                      o_ref, acc_ref, act_ref, p_ref, *, final):
    j = pl.program_id(1)
    te, tt = act_ref.shape
    na = te // PEER_NKEYS

    @pl.when(j == 0)
    def _():
        acc_ref[...] = jnp.zeros(acc_ref.shape, F32)

    act_ref[...] = lax.dot_general(u_ref[...], h_ref[...], (((1,), (1,)), ((), ())), preferred_element_type=F32)

    a0 = pl.multiple_of(j * na, na)

    def token_chunk(c, carry):
        cs = pl.ds(pl.multiple_of(c * 128, 128), 128)
        s1 = [s1_ref[hh, pl.ds(a0, na), cs] for hh in range(PEER_HEADS)]
        e1 = [e1_ref[hh, pl.ds(a0, na), cs] for hh in range(PEER_HEADS)]
        for al in range(na):
            rs = slice(al * PEER_NKEYS, (al + 1) * PEER_NKEYS)
            w = jnp.zeros((PEER_NKEYS, 128), F32)
            for hh in range(PEER_HEADS):
                hit = (s1[hh][al:al + 1] + s2_ref[hh, :, cs]) >= tau_ref[hh, :, cs]
                w = w + jnp.where(hit, e1[hh][al:al + 1] * e2_ref[hh, :, cs], 0.0)
            p_ref[rs, cs] = (w * _gelu(act_ref[rs, cs])).astype(BF16)
        return carry

    lax.fori_loop(0, tt // 128, token_chunk, 0)
    acc_ref[...] += jnp.dot(vt_ref[...], p_ref[...], preferred_element_type=F32)

    @pl.when(j == pl.num_programs(1) - 1)
    def _():
        y = x_ref[...] + gt_ref[...] * acc_ref[...].T
        if final:
            y = y * lax.rsqrt(jnp.mean(y * y, axis=-1, keepdims=True) + NORM_EPS) * fg_ref[...]
        o_ref[...] = y


def _peer_main(h2, u_bf, vt_bf, s1, e1, s2, e2, tau, x2, mod3, final_g, *, final, tt=512, te=1024):
    t, d = x2.shape
    per_b = SEQ // tt
    big = pl.BlockSpec((PEER_HEADS, PEER_NKEYS, tt), lambda i, j: (0, 0, i))
    return pl.pallas_call(
        functools.partial(_peer_main_kernel, final=final),
        grid=(t // tt, PEER_EXPERTS // te),
        in_specs=[pl.BlockSpec((tt, d), lambda i, j: (i, 0)),
                  pl.BlockSpec((te, d), lambda i, j: (j, 0)),
                  pl.BlockSpec((d, te), lambda i, j: (0, j)),
                  big, big, big, big,
                  pl.BlockSpec((PEER_HEADS, 1, tt), lambda i, j: (0, 0, i)),
                  pl.BlockSpec((tt, d), lambda i, j: (i, 0)),
                  pl.BlockSpec((None, 1, d), lambda i, j: (i // per_b, 0, 5)),
                  pl.BlockSpec((1, d), lambda i, j: (0, 0))],
        out_specs=pl.BlockSpec((tt, d), lambda i, j: (i, 0)),
        out_shape=jax.ShapeDtypeStruct((t, d), F32),
        scratch_shapes=[pltpu.VMEM((d, tt), F32), pltpu.VMEM((te, tt), F32), pltpu.VMEM((te, tt), BF16)],
        compiler_params=_params(("parallel", "arbitrary")),
        name="peer_experts",
    )(h2, u_bf, vt_bf, s1, e1, s2, e2, tau, x2, mod3, final_g.reshape(1, d))


def _reorder_w_in(w):
    pad = jnp.zeros((w.shape[0], 1024 - 792), w.dtype)
    return jnp.concatenate([w[:, 5400:9496], w[:, 3072:3864], pad, w[:, 0:3072], w[:, 3864:5400]], axis=1)


def kernel(x, c, positions, mod_w, mod_b, norm_mix_g, norm_ffn_g, w_in, a_conv_w, a_out, b_conv_w, b_conv_b, b_ln_g, b_ln_b, b_out, c_cmp_pos, c_cmp_w1, c_cmp_w2, c_out, d_out, w_o, peer_wq, peer_subkeys, peer_u, peer_v, final_norm_g):
    bsz, s, d = x.shape
    assert s == SEQ and d == D_MODEL
    depth = mod_w.shape[0]
    t = bsz * s
    cos, sin = _rope_tables(positions)
    mod = _modulation(c, mod_w, mod_b)
    x2 = x.reshape(t, d)
    for l in range(depth):
        mod3 = mod[l].reshape(bsz, 1, 6 * d)
        proj2 = _norm_matmul(x2, norm_mix_g[l], mod3, 0, 1, _reorder_w_in(w_in[l]).astype(BF16))
        proj3 = proj2.reshape(bsz, s, PROJ_COLS)
        ua, ub = _conv_mixers(proj3, a_conv_w[l], b_conv_w[l], b_conv_b[l], b_ln_g[l], b_ln_b[l])
        qc, kc, vc, ks, vs, kw, vw, qd, kd, vd, kmean = _prep(proj3, cos, sin)
        kcmp, vcmp = _compress(kc, vc, c_cmp_pos[l], c_cmp_w1[l], c_cmp_w2[l])
        oc = _nsa(qc, kcmp, vcmp, ks, vs, kw, vw, proj3)
        od = _moba(qd, kd, vd, kmean)
        x2 = _merge(ua.reshape(t, 512), ub.reshape(t, 512), oc.reshape(t, 512), od.reshape(t, 512),
                    proj2, x2, mod3, a_out[l].astype(BF16), b_out[l].astype(BF16), c_out[l].astype(BF16),
                    d_out[l].astype(BF16), w_o[l].astype(BF16))
        q2, h2 = _norm_matmul(x2, norm_ffn_g[l], mod3, 3, 4, peer_wq[l].astype(BF16), emit_h=True)
        s1, e1, s2, e2, tau = _peer_select(q2, peer_subkeys[l])
        x2 = _peer_main(h2, peer_u[l].astype(BF16), peer_v[l].T.astype(BF16), s1, e1, s2, e2, tau,
                        x2, mod3, final_norm_g, final=(l == depth - 1))
    return x2.reshape(bsz, s, d)
```

```python
import functools

import numpy as np
import jax
import jax.numpy as jnp
from jax import lax
from jax.experimental import pallas as pl
from jax.experimental.pallas import tpu as pltpu

F32 = jnp.float32
BF16 = jnp.bfloat16
HIGHEST = lax.Precision.HIGHEST

D_MODEL = 1024
SEQ = 2048
HEAD_DIM = 64
ROPE_THETA = 10000.0
NORM_EPS = 1e-6
A_WIDTH = 512
A_CONV = 3
B_WIDTH = 512
B_CONV = 31
C_HEADS = 8
C_KV_HEADS = 2
C_GROUP = 4
CMP_BLOCK = 32
CMP_STRIDE = 16
CMP_HIDDEN = 128
N_CMP = (SEQ - CMP_BLOCK) // CMP_STRIDE + 1
SLC_BLOCK = 64
SLC_TOPN = 16
N_SLC = SEQ // SLC_BLOCK
WIN = 512
D_HEADS = 8
MOBA_BLOCK = 256
MOBA_TOPK = 3
N_MOBA = SEQ // MOBA_BLOCK
PEER_HEADS = 8
PEER_NKEYS = 128
PEER_EXPERTS = PEER_NKEYS * PEER_NKEYS
PEER_QDIM = 256
PEER_TOPK = 16

PROJ_COLS = 9728
COL_MERGE = 0
COL_KVG = 4096
COL_GATE = COL_KVG + 768
COL_A = 5120

VMEM_LIMIT = 56 * 1024 * 1024
NEG = -1e30

ATT_T = 256
CONV_T = 256
HALO = 32


def _params(sem, flags=None):
    return pltpu.CompilerParams(dimension_semantics=sem, vmem_limit_bytes=VMEM_LIMIT, flags=flags)


def _gelu(x):
    return 0.5 * x * (1.0 + lax.erf(x * np.float32(np.sqrt(0.5))))


def _rope_table_kernel(pos_ref, inv_ref, sign_ref, cos_ref, sin_ref):
    ang = pos_ref[...] * inv_ref[...]
    cos_ref[...] = jnp.cos(ang)
    sin_ref[...] = jnp.sin(ang) * sign_ref[...]


def _rope_tables(positions):
    bsz, s = positions.shape
    inv = 1.0 / (ROPE_THETA ** (jnp.arange(0, HEAD_DIM, 2, dtype=F32) / HEAD_DIM))
    inv128 = jnp.tile(inv, 4)[None, :]
    sign = jnp.tile(jnp.concatenate([-jnp.ones(32, F32), jnp.ones(32, F32)]), 2)[None, :]
    pos = positions.astype(F32).reshape(bsz * s, 1)
    t = bsz * s
    cos, sin = pl.pallas_call(
        _rope_table_kernel,
        grid=(t // SEQ,),
        in_specs=[pl.BlockSpec((SEQ, 1), lambda i: (i, 0)),
                  pl.BlockSpec((1, 128), lambda i: (0, 0)),
                  pl.BlockSpec((1, 128), lambda i: (0, 0))],
        out_specs=[pl.BlockSpec((SEQ, 128), lambda i: (i, 0))] * 2,
        out_shape=[jax.ShapeDtypeStruct((t, 128), F32)] * 2,
        compiler_params=_params(("parallel",)),
        name="rope_tables",
    )(pos, inv128, sign)
    return cos.reshape(bsz, s, 128), sin.reshape(bsz, s, 128)


def _rope(x, cos, sin):
    w = x.shape[-1]
    lane = lax.broadcasted_iota(jnp.int32, x.shape, 1)
    swapped = jnp.where(lane % 64 < 32, pltpu.roll(x, w - 32, 1), pltpu.roll(x, 32, 1))
    return x * cos + swapped * sin


def _mod_kernel(c_ref, w_ref, b_ref, o_ref):
    c = c_ref[...]
    cond = c * jax.nn.sigmoid(c)
    o_ref[...] = jnp.dot(cond, w_ref[...], precision=HIGHEST, preferred_element_type=F32) + b_ref[...]


def _modulation(c, mod_w, mod_b):
    nl, d, n = mod_w.shape
    bsz = c.shape[0]
    tn = 1536
    return pl.pallas_call(
        _mod_kernel,
        grid=(nl, n // tn),
        in_specs=[pl.BlockSpec((bsz, d), lambda l, j: (0, 0)),
                  pl.BlockSpec((None, d, tn), lambda l, j: (l, 0, j)),
                  pl.BlockSpec((None, 1, tn), lambda l, j: (l, 0, j))],
        out_specs=pl.BlockSpec((None, bsz, tn), lambda l, j: (l, 0, j)),
        out_shape=jax.ShapeDtypeStruct((nl, bsz, n), F32),
        compiler_params=_params(("parallel", "parallel")),
        name="adaln_mod",
    )(c, mod_w, mod_b.reshape(nl, 1, n))


def _norm_matmul_kernel(x_ref, g_ref, sc_ref, sh_ref, w_ref, o_ref, *rest, emit_h):
    h_scr = rest[-1]

    @pl.when(pl.program_id(1) == 0)
    def _():
        x = x_ref[...]
        y = x * lax.rsqrt(jnp.mean(x * x, axis=-1, keepdims=True) + NORM_EPS)
        h = (y * g_ref[...]) * (1.0 + sc_ref[...]) + sh_ref[...]
        h_scr[...] = h.astype(BF16)
        if emit_h:
            rest[0][...] = h.astype(BF16)

    o_ref[...] = jnp.dot(h_scr[...], w_ref[...], preferred_element_type=F32)


def _norm_matmul(x2, g, mod3, sh_blk, sc_blk, w, *, emit_h=False, tm=1024, tn=512):
    t, d = x2.shape
    n = w.shape[1]
    per_b = SEQ // tm
    out_shape = [jax.ShapeDtypeStruct((t, n), F32)]
    out_specs = [pl.BlockSpec((tm, tn), lambda i, j: (i, j))]
    if emit_h:
        out_shape.append(jax.ShapeDtypeStruct((t, d), BF16))
        out_specs.append(pl.BlockSpec((tm, d), lambda i, j: (i, 0)))
    res = pl.pallas_call(
        functools.partial(_norm_matmul_kernel, emit_h=emit_h),
        grid=(t // tm, n // tn),
        in_specs=[pl.BlockSpec((tm, d), lambda i, j: (i, 0)),
                  pl.BlockSpec((1, d), lambda i, j: (0, 0)),
                  pl.BlockSpec((None, 1, d), lambda i, j: (i // per_b, 0, sc_blk)),
                  pl.BlockSpec((None, 1, d), lambda i, j: (i // per_b, 0, sh_blk)),
                  pl.BlockSpec((d, tn), lambda i, j: (0, j))],
        out_specs=out_specs,
        out_shape=out_shape,
        scratch_shapes=[pltpu.VMEM((tm, d), BF16)],
        compiler_params=_params(("parallel", "arbitrary")),
        name="norm_matmul",
    )(x2, g.reshape(1, d), mod3, mod3, w)
    return res if emit_h else res[0]


def _conv_kernel(ab_ref, ac_ref, ax_ref, ba_ref, bg_ref, pac_ref, pax_ref, pba_ref, pbg_ref,
                 aw_ref, bw_ref, bb_ref, lng_ref, lnb_ref, ua_ref, ub_ref, ext_ref):
    ts = ab_ref.shape[0]
    keep = (pl.program_id(1) > 0).astype(F32)

    ext_ref[0:HALO, :] = pac_ref[...] * pax_ref[...] * keep
    ext_ref[HALO:HALO + ts, :] = ac_ref[...] * ax_ref[...]
    acc = jnp.zeros((ts, A_WIDTH), F32)
    for k in range(A_CONV):
        off = HALO - (A_CONV - 1) + k
        acc = acc + aw_ref[k:k + 1, :] * ext_ref[off:off + ts, :]
    ua_ref[...] = (ab_ref[...] * acc).astype(BF16)

    ext_ref[0:HALO, :] = pba_ref[...] * jax.nn.sigmoid(pbg_ref[...]) * keep
    ext_ref[HALO:HALO + ts, :] = ba_ref[...] * jax.nn.sigmoid(bg_ref[...])
    acc = jnp.zeros((ts, B_WIDTH), F32) + bb_ref[...]
    for k in range(B_CONV):
        off = HALO - (B_CONV - 1) + k
        acc = acc + bw_ref[k:k + 1, :] * ext_ref[off:off + ts, :]
    mu = jnp.mean(acc, axis=-1, keepdims=True)
    cen = acc - mu
    var = jnp.mean(cen * cen, axis=-1, keepdims=True)
    y = cen * lax.rsqrt(var + NORM_EPS) * lng_ref[...] + lnb_ref[...]
    ub_ref[...] = (y * jax.nn.sigmoid(y)).astype(BF16)


def _conv_mixers(proj3, a_conv_w, b_conv_w, b_conv_b, b_ln_g, b_ln_b):
    bsz = proj3.shape[0]
    ts = CONV_T
    c0 = COL_A // 512
    r = ts // HALO

    def cur(k):
        return pl.BlockSpec((None, ts, 512), lambda b, i, k=k: (b, i, c0 + k))

    def prev(k):
        return pl.BlockSpec((None, HALO, 512), lambda b, i, k=k: (b, jnp.maximum(i * r - 1, 0), c0 + k))

    def full(shape):
        return pl.BlockSpec(shape, lambda b, i: (0,) * len(shape))

    return pl.pallas_call(
        _conv_kernel,
        grid=(bsz, SEQ // ts),
        in_specs=[cur(0), cur(1), cur(2), cur(3), cur(4), prev(1), prev(2), prev(3), prev(4),
                  full((A_CONV, A_WIDTH)), full((B_CONV, B_WIDTH)), full((1, B_WIDTH)),
                  full((1, B_WIDTH)), full((1, B_WIDTH))],
        out_specs=[pl.BlockSpec((None, ts, 512), lambda b, i: (b, i, 0))] * 2,
        out_shape=[jax.ShapeDtypeStruct((bsz, SEQ, 512), BF16)] * 2,
        scratch_shapes=[pltpu.VMEM((HALO + ts, 512), F32)],
        compiler_params=_params(("parallel", "arbitrary")),
        name="conv_mixers",
    )(proj3, proj3, proj3, proj3, proj3, proj3, proj3, proj3, proj3,
      a_conv_w, b_conv_w, b_conv_b.reshape(1, -1), b_ln_g.reshape(1, -1), b_ln_b.reshape(1, -1))


def _prep_kernel(cq_ref, dq_ref, dk_ref, dv_ref, kvg_ref, cos_ref, sin_ref,
                 qc_ref, kc_ref, vc_ref, ks_ref, vst_ref, kw_ref, vwt_ref, gt_ref,
                 qd_ref, kd_ref, vdt_ref, kmean_ref):
    cos = cos_ref[...]
    sin = sin_ref[...]
    cos4 = jnp.concatenate([cos] * 4, axis=1)
    sin4 = jnp.concatenate([sin] * 4, axis=1)
    scale = np.float32(HEAD_DIM ** -0.5)
    qc_ref[...] = _rope(cq_ref[...], cos4, sin4) * scale
    qd_ref[...] = _rope(dq_ref[...], cos4, sin4) * scale
    kd = _rope(dk_ref[...], cos4, sin4)
    for h in range(D_HEADS):
        kd_ref[h] = kd[:, h * 64:(h + 1) * 64].astype(BF16)
    kmean_ref[...] = jnp.mean(kd, axis=0, keepdims=True)
    vdt_ref[...] = dv_ref[...].T.astype(BF16)
    kc = _rope(kvg_ref[:, 0:128], cos, sin)
    vc = kvg_ref[:, 128:256]
    ks = _rope(kvg_ref[:, 256:384], cos, sin)
    kw = _rope(kvg_ref[:, 512:640], cos, sin)
    for g in range(C_KV_HEADS):
        gs = slice(g * 64, (g + 1) * 64)
        kc_ref[g] = kc[:, gs]
        vc_ref[g] = vc[:, gs]
        ks_ref[g] = ks[:, gs].astype(BF16)
        kw_ref[g] = kw[:, gs].astype(BF16)
    vst_ref[...] = kvg_ref[:, 384:512].T.astype(BF16)
    vwt_ref[...] = kvg_ref[:, 640:768].T.astype(BF16)
    gt_ref[...] = jax.nn.sigmoid(kvg_ref[:, 768:896]).T


def _prep(proj3, cos, sin):
    bsz = proj3.shape[0]
    ts = MOBA_BLOCK
    c0 = COL_A // 512

    def col512(k):
        return pl.BlockSpec((None, ts, 512), lambda b, i: (b, i, c0 + k))

    row128 = pl.BlockSpec((None, ts, 128), lambda b, i: (b, i, 0))
    row512 = pl.BlockSpec((None, ts, 512), lambda b, i: (b, i, 0))
    col128t = pl.BlockSpec((None, 128, ts), lambda b, i: (b, 0, i))
    head64 = pl.BlockSpec((None, C_KV_HEADS, ts, 64), lambda b, i: (b, 0, i, 0))
    k64 = jax.ShapeDtypeStruct((bsz, C_KV_HEADS, SEQ, 64), BF16)
    t128 = jax.ShapeDtypeStruct((bsz, 128, SEQ), BF16)
    return pl.pallas_call(
        _prep_kernel,
        grid=(bsz, SEQ // ts),
        in_specs=[col512(5), col512(6), col512(7), col512(8),
                  pl.BlockSpec((None, ts, 1024), lambda b, i: (b, i, COL_KVG // 1024)),
                  row128, row128],
        out_specs=[row512, head64, head64, head64, col128t, head64, col128t, col128t,
                   row512,
                   pl.BlockSpec((None, D_HEADS, ts, 64), lambda b, i: (b, 0, i, 0)),
                   pl.BlockSpec((None, 512, ts), lambda b, i: (b, 0, i)),
                   pl.BlockSpec((None, None, 1, 512), lambda b, i: (b, i, 0, 0))],
        out_shape=[jax.ShapeDtypeStruct((bsz, SEQ, 512), F32),
                   jax.ShapeDtypeStruct((bsz, C_KV_HEADS, SEQ, 64), F32),
                   jax.ShapeDtypeStruct((bsz, C_KV_HEADS, SEQ, 64), F32),
                   k64, t128, k64, t128,
                   jax.ShapeDtypeStruct((bsz, 128, SEQ), F32),
                   jax.ShapeDtypeStruct((bsz, SEQ, 512), F32),
                   jax.ShapeDtypeStruct((bsz, D_HEADS, SEQ, 64), BF16),
                   jax.ShapeDtypeStruct((bsz, 512, SEQ), BF16),
                   jax.ShapeDtypeStruct((bsz, N_MOBA, 1, 512), F32)],
        compiler_params=_params(("parallel", "parallel")),
        name="attn_prep",
    )(proj3, proj3, proj3, proj3, proj3, cos, sin)


def _compress_kernel(kc_ref, vc_ref, pos_ref, w1_ref, w2_ref, kcmp_ref, vcmp_ref):
    half = CMP_STRIDE * HEAD_DIM
    row = lax.broadcasted_iota(jnp.int32, (128, CMP_HIDDEN), 0)
    for which, (src, dst) in enumerate(((kc_ref, kcmp_ref), (vc_ref, vcmp_ref))):
        bias = jnp.dot(pos_ref[which], w1_ref[which], precision=HIGHEST, preferred_element_type=F32)[0:1]
        for g in range(C_KV_HEADS):
            chunks = src[g]
            d1 = jnp.dot(chunks, w1_ref[which, 0:half, :], precision=HIGHEST, preferred_element_type=F32)
            d2 = jnp.dot(chunks, w1_ref[which, half:2 * half, :], precision=HIGHEST, preferred_element_type=F32)
            d2 = jnp.where(row < 127, pltpu.roll(d2, 127, 0), 0.0)
            hid = _gelu(d1 + d2 + bias)
            out = jnp.dot(hid, w2_ref[which], precision=HIGHEST, preferred_element_type=F32)
            dst[g] = out if which == 0 else out.T


def _compress(kc, vc, cmp_pos, cmp_w1, cmp_w2):
    bsz = kc.shape[0]
    kc4 = kc.reshape(bsz, C_KV_HEADS, SEQ // CMP_STRIDE, CMP_STRIDE * HEAD_DIM)
    vc4 = vc.reshape(bsz, C_KV_HEADS, SEQ // CMP_STRIDE, CMP_STRIDE * HEAD_DIM)
    pos8 = jnp.broadcast_to(cmp_pos.reshape(2, 1, CMP_BLOCK * HEAD_DIM), (2, 8, CMP_BLOCK * HEAD_DIM))
    blk = pl.BlockSpec((None, C_KV_HEADS, 128, 1024), lambda b: (b, 0, 0, 0))
    out = pl.BlockSpec((None, C_KV_HEADS, 128, 64), lambda b: (b, 0, 0, 0))
    return pl.pallas_call(
        _compress_kernel,
        grid=(bsz,),
        in_specs=[blk, blk,
                  pl.BlockSpec((2, 8, 2048), lambda b: (0, 0, 0)),
                  pl.BlockSpec((2, 2048, CMP_HIDDEN), lambda b: (0, 0, 0)),
                  pl.BlockSpec((2, CMP_HIDDEN, 64), lambda b: (0, 0, 0))],
        out_specs=[out, pl.BlockSpec((None, C_KV_HEADS, 64, 128), lambda b: (b, 0, 0, 0))],
        out_shape=[jax.ShapeDtypeStruct((bsz, C_KV_HEADS, 128, 64), F32),
                   jax.ShapeDtypeStruct((bsz, C_KV_HEADS, 64, 128), F32)],
        compiler_params=_params(("parallel",)),
        name="nsa_compress",
    )(kc4, vc4, pos8, cmp_w1, cmp_w2)


def _flash_steps_t(heads, m_ref, l_ref, acc_ref, groups):
    scores = [lax.dot_general(k, qb, (((1,), (1,)), ((), ())), preferred_element_type=F32)
              for (qb, k, _, _) in heads]
    probs = []
    for n, (qb, _, _, mask) in enumerate(heads):
        s = scores[n]
        tq = qb.shape[0] // groups
        if mask is not None:
            s = jnp.concatenate([jnp.where(mask, s[:, r * tq:(r + 1) * tq], NEG) for r in range(groups)], axis=1)
        sr = _stat_row(n)
        m_old = m_ref[sr, :]
        m_new = jnp.maximum(m_old, jnp.max(s, axis=0, keepdims=True))
        alpha = jnp.exp(m_old - m_new)
        p = jnp.exp(s - m_new)
        l_ref[sr, :] = alpha * l_ref[sr, :] + jnp.sum(p, axis=0, keepdims=True)
        m_ref[sr, :] = m_new
        probs.append((alpha, p.astype(BF16)))
    for n, (_, _, vt, _) in enumerate(heads):
        alpha, p = probs[n]
        a = slice(n * HEAD_DIM, (n + 1) * HEAD_DIM)
        acc_ref[a, :] = alpha * acc_ref[a, :] + jnp.dot(vt, p, preferred_element_type=F32)


def _stat_row(row):
    return slice(8 * row, 8 * row + 1)


def _flash_init(m_ref, l_ref, acc_ref):
    m_ref[...] = jnp.full(m_ref.shape, NEG, F32)
    l_ref[...] = jnp.zeros(l_ref.shape, F32)
    acc_ref[...] = jnp.zeros(acc_ref.shape, F32)


def _rank_desc_rows(vals):
    n = vals.shape[0]
    row = lax.broadcasted_iota(jnp.int32, vals.shape, 0)
    rank = jnp.zeros(vals.shape, jnp.int32)
    for i in range(n):
        vi = vals[i:i + 1, :]
        ahead = (vi > vals) | ((vi == vals) & (row > i))
        rank = rank + ahead.astype(jnp.int32)
    return rank


def _nsa_kernel(q_ref, kcmp_ref, vcmpt_ref, ks_ref, vst_ref, kw_ref, vwt_ref, gt_ref, ovlt_ref, expt_ref,
                o_ref, qb_ref, sel_ref, m_ref, l_ref, acc_ref, out_ref):
    tq = ATT_T
    i = pl.program_id(1)
    t0 = i * tq
    dstart = pl.multiple_of(t0, tq)
    t_row = t0 + lax.broadcasted_iota(jnp.int32, (1, tq), 1)
    t_rows = jnp.concatenate([t_row] * C_GROUP, axis=1)
    blk = lax.broadcasted_iota(jnp.int32, (128, 1), 0)
    kk = lax.broadcasted_iota(jnp.int32, (tq, tq), 0)
    qq = lax.broadcasted_iota(jnp.int32, (tq, tq), 1)
    causal = kk <= qq
    win_tail = kk > qq

    def gate_row(branch, g):
        base = branch * 8 + g * C_GROUP
        return jnp.concatenate([gt_ref[base + r:base + r + 1, :] for r in range(C_GROUP)], axis=1)

    def hrows(g):
        return slice(g * HEAD_DIM, (g + 1) * HEAD_DIM)

    for g in range(C_KV_HEADS):
        qf = jnp.concatenate([q_ref[:, (g * 4 + r) * 64:(g * 4 + r + 1) * 64] for r in range(C_GROUP)], axis=0)
        qb_ref[g] = qf.astype(BF16)

        s = lax.dot_general(kcmp_ref[g], qf, (((1,), (1,)), ((), ())), precision=HIGHEST,
                            preferred_element_type=F32)
        vis = (blk * CMP_STRIDE + (CMP_BLOCK - 1)) <= t_rows
        sm = jnp.where(vis, s, NEG)
        e = jnp.where(vis, jnp.exp(sm - jnp.max(sm, axis=0, keepdims=True)), 0.0)
        p = e / jnp.maximum(jnp.sum(e, axis=0, keepdims=True), 1e-30)
        o_cmp = jnp.dot(vcmpt_ref[g].astype(BF16), p.astype(BF16), preferred_element_type=F32)
        out_ref[hrows(g), :] = gate_row(0, g) * o_cmp

        psum = p[:, 0:tq] + p[:, tq:2 * tq] + p[:, 2 * tq:3 * tq] + p[:, 3 * tq:4 * tq]
        imp = jnp.dot(ovlt_ref[...], psum, precision=HIGHEST, preferred_element_type=F32)[0:N_SLC]
        b32 = blk[0:N_SLC]
        cur = t_row // SLC_BLOCK
        forced = (b32 == 0) | (b32 == cur) | (b32 == cur - 1)
        imp = jnp.where(forced, jnp.inf, jnp.where(b32 * SLC_BLOCK > t_row, -jnp.inf, imp))
        sel = (_rank_desc_rows(imp) < SLC_TOPN).astype(F32)
        sel_ref[g] = jnp.concatenate([sel, jnp.zeros((128 - N_SLC, tq), F32)], axis=0).astype(BF16)

    def slc_tile(jb, extra):
        start = pl.multiple_of(jb * tq, tq)
        heads = []
        for g in range(C_KV_HEADS):
            hit = jnp.dot(expt_ref[pl.ds(start, tq), :], sel_ref[g], preferred_element_type=F32) > 0.5
            mask = hit if extra is None else hit & extra
            heads.append((qb_ref[g], ks_ref[g, pl.ds(start, tq), :], vst_ref[hrows(g), pl.ds(start, tq)], mask))
        _flash_steps_t(heads, m_ref, l_ref, acc_ref, C_GROUP)

    def win_tile(start, mask):
        heads = [(qb_ref[g], kw_ref[g, pl.ds(start, tq), :], vwt_ref[hrows(g), pl.ds(start, tq)], mask)
                 for g in range(C_KV_HEADS)]
        _flash_steps_t(heads, m_ref, l_ref, acc_ref, C_GROUP)

    def add_branch(branch):
        for g in range(C_KV_HEADS):
            out_ref[hrows(g), :] += gate_row(branch, g) * (acc_ref[hrows(g), :] / l_ref[_stat_row(g), :])

    _flash_init(m_ref, l_ref, acc_ref)
    slc_tile(i, causal)

    def slc_body(jb, carry):
        slc_tile(jb, None)
        return carry

    lax.fori_loop(0, i, slc_body, 0)
    add_branch(1)

    _flash_init(m_ref, l_ref, acc_ref)
    win_tile(dstart, causal)

    @pl.when(i >= 1)
    def _():
        win_tile(pl.multiple_of(t0 - tq, tq), None)

    @pl.when(i >= 2)
    def _():
        win_tile(pl.multiple_of(t0 - 2 * tq, tq), win_tail)

    add_branch(2)

    o_t = jnp.concatenate([out_ref[hrows(g), r * tq:(r + 1) * tq]
                           for g in range(C_KV_HEADS) for r in range(C_GROUP)], axis=0)
    o_ref[...] = o_t.T.astype(BF16)


def _nsa_constants():
    j = np.arange(128)[:, None]
    n = np.arange(128)[None, :]
    ovl_t = ((n * CMP_STRIDE < j * SLC_BLOCK + SLC_BLOCK) & (n * CMP_STRIDE + CMP_BLOCK > j * SLC_BLOCK)
             & (n < N_CMP) & (j < N_SLC)).astype(np.float32)
    expand_t = (np.arange(SEQ)[:, None] // SLC_BLOCK == np.arange(128)[None, :]).astype(np.float32)
    return jnp.asarray(ovl_t), jnp.asarray(expand_t, dtype=BF16)


def _nsa(qc, kcmp, vcmpt, ks, vst, kw, vwt, gt):
    bsz = qc.shape[0]
    tq = ATT_T
    ovl_t, expand_t = _nsa_constants()
    keys = pl.BlockSpec((None, C_KV_HEADS, SEQ, 64), lambda b, i: (b, 0, 0, 0))
    vals = pl.BlockSpec((None, 128, SEQ), lambda b, i: (b, 0, 0))
    rows = C_GROUP * tq
    return pl.pallas_call(
        _nsa_kernel,
        grid=(bsz, SEQ // tq),
        in_specs=[pl.BlockSpec((None, tq, 512), lambda b, i: (b, i, 0)),
                  pl.BlockSpec((None, C_KV_HEADS, 128, 64), lambda b, i: (b, 0, 0, 0)),
                  pl.BlockSpec((None, C_KV_HEADS, 64, 128), lambda b, i: (b, 0, 0, 0)),
                  keys, vals, keys, vals,
                  pl.BlockSpec((None, 128, tq), lambda b, i: (b, 0, i)),
                  pl.BlockSpec((128, 128), lambda b, i: (0, 0)),
                  pl.BlockSpec((SEQ, 128), lambda b, i: (0, 0))],
        out_specs=pl.BlockSpec((None, tq, 512), lambda b, i: (b, i, 0)),
        out_shape=jax.ShapeDtypeStruct((bsz, SEQ, 512), BF16),
        scratch_shapes=[pltpu.VMEM((C_KV_HEADS, rows, 64), BF16), pltpu.VMEM((C_KV_HEADS, 128, tq), BF16),
                        pltpu.VMEM((8 * C_KV_HEADS, rows), F32), pltpu.VMEM((8 * C_KV_HEADS, rows), F32),
                        pltpu.VMEM((C_KV_HEADS * 64, rows), F32), pltpu.VMEM((C_KV_HEADS * 64, rows), F32)],
        compiler_params=_params(("parallel", "arbitrary")),
        name="nsa_attention",
    )(qc, kcmp, vcmpt, ks, vst, kw, vwt, gt, ovl_t, expand_t)


def _moba_kernel(q_ref, k_ref, vt_ref, kmean_ref, o_ref, qb_ref, sel_ref, m_ref, l_ref, acc_ref):
    tq = ATT_T
    i = pl.program_id(1)
    blk = lax.broadcasted_iota(jnp.int32, (N_MOBA, 1), 0)
    kk = lax.broadcasted_iota(jnp.int32, (tq, tq), 0)
    qq = lax.broadcasted_iota(jnp.int32, (tq, tq), 1)
    causal = kk <= qq
    past = blk < i
    for h in range(D_HEADS):
        hs = slice(h * 64, (h + 1) * 64)
        qf = q_ref[:, hs]
        qb_ref[h] = qf.astype(BF16)
        gate = lax.dot_general(kmean_ref[:, hs], qf, (((1,), (1,)), ((), ())), precision=HIGHEST,
                               preferred_element_type=F32)
        gate = jnp.where(past, gate, -jnp.inf)
        sel_ref[h] = (past & (_rank_desc_rows(gate) < MOBA_TOPK)).astype(F32)

    def tile(jb, diag):
        start = pl.multiple_of(jb * tq, tq)
        heads = []
        for h in range(D_HEADS):
            if diag:
                mask = causal
            else:
                mask = jnp.sum(jnp.where(blk == jb, sel_ref[h], 0.0), axis=0, keepdims=True) > 0.5
            heads.append((qb_ref[h], k_ref[h, pl.ds(start, tq), :], vt_ref[h * 64:(h + 1) * 64, pl.ds(start, tq)],
                          mask))
        _flash_steps_t(heads, m_ref, l_ref, acc_ref, 1)

    _flash_init(m_ref, l_ref, acc_ref)
    tile(i, True)

    def body(jb, carry):
        tile(jb, False)
        return carry

    lax.fori_loop(0, i, body, 0)
    o_t = jnp.concatenate([acc_ref[h * 64:(h + 1) * 64, :] / l_ref[_stat_row(h), :] for h in range(D_HEADS)], axis=0)
    o_ref[...] = o_t.T.astype(BF16)


def _moba(qd, kd, vdt, kmean):
    bsz = qd.shape[0]
    tq = ATT_T
    return pl.pallas_call(
        _moba_kernel,
        grid=(bsz, SEQ // tq),
        in_specs=[pl.BlockSpec((None, tq, 512), lambda b, i: (b, i, 0)),
                  pl.BlockSpec((None, D_HEADS, SEQ, 64), lambda b, i: (b, 0, 0, 0)),
                  pl.BlockSpec((None, 512, SEQ), lambda b, i: (b, 0, 0)),
                  pl.BlockSpec((None, N_MOBA, 512), lambda b, i: (b, 0, 0))],
        out_specs=pl.BlockSpec((None, tq, 512), lambda b, i: (b, i, 0)),
        out_shape=jax.ShapeDtypeStruct((bsz, SEQ, 512), BF16),
        scratch_shapes=[pltpu.VMEM((D_HEADS, tq, 64), BF16), pltpu.VMEM((D_HEADS, N_MOBA, tq), F32),
                        pltpu.VMEM((8 * D_HEADS, tq), F32), pltpu.VMEM((8 * D_HEADS, tq), F32),
                        pltpu.VMEM((512, tq), F32)],
        compiler_params=_params(("parallel", "arbitrary")),
        name="moba_attention",
    )(qd, kd, vdt, kmean.reshape(bsz, N_MOBA, 512))


def _merge_kernel(ua_ref, ub_ref, oc_ref, od_ref, mg_ref, x_ref, gt_ref,
                  wa_ref, wb_ref, wc_ref, wd_ref, wo_ref, o_ref):
    d = D_MODEL
    merged = jnp.zeros(x_ref.shape, F32)
    for k, (u_ref, w_ref) in enumerate(((ua_ref, wa_ref), (ub_ref, wb_ref), (oc_ref, wc_ref), (od_ref, wd_ref))):
        y = jnp.dot(u_ref[...], w_ref[...], preferred_element_type=F32)
        merged = merged + jax.nn.sigmoid(mg_ref[:, k * d:(k + 1) * d]) * y
    o_ref[...] = x_ref[...] + gt_ref[...] * jnp.dot(merged.astype(BF16), wo_ref[...], preferred_element_type=F32)


def _merge(ua, ub, oc, od, proj2, x2, mod3, wa, wb, wc, wd, wo, *, tm=512):
    t, d = x2.shape
    per_b = SEQ // tm
    act = pl.BlockSpec((tm, 512), lambda i: (i, 0))
    wspec = pl.BlockSpec((512, d), lambda i: (0, 0))
    return pl.pallas_call(
        _merge_kernel,
        grid=(t // tm,),
        in_specs=[act, act, act, act,
                  pl.BlockSpec((tm, 4 * d), lambda i: (i, COL_MERGE // (4 * d))),
                  pl.BlockSpec((tm, d), lambda i: (i, 0)),
                  pl.BlockSpec((None, 1, d), lambda i: (i // per_b, 0, 2)),
                  wspec, wspec, wspec, wspec,
                  pl.BlockSpec((d, d), lambda i: (0, 0))],
        out_specs=pl.BlockSpec((tm, d), lambda i: (i, 0)),
        out_shape=jax.ShapeDtypeStruct((t, d), F32),
        compiler_params=_params(("parallel",)),
        name="merge_out",
    )(ua, ub, oc, od, proj2, x2, mod3, wa, wb, wc, wd, wo)


def _pop_max(work, idx):
    m = jnp.max(work, axis=0, keepdims=True)
    first = jnp.min(jnp.where(work == m, idx, work.shape[0]), axis=0, keepdims=True)
    return m, idx == first


_PEER_PAIRS = [(i, j) for i in range(PEER_TOPK) for j in range(PEER_TOPK) if (i + 1) * (j + 1) <= PEER_TOPK]
_PEER_PAIR_ROWS = 56
NOT_TOP = 99.0


def _peer_sel_kernel(q_ref, keys_ref, grp_ref, n1_ref, e1_ref, r2_ref, e2_ref, st_ref):
    tt = q_ref.shape[0]
    half = PEER_QDIM // 2
    for hp in range(2 * PEER_HEADS):
        qh = q_ref[:, hp * half:(hp + 1) * half]
        qn = qh * lax.rsqrt(jnp.mean(qh * qh, axis=-1, keepdims=True) + NORM_EPS)
        st_ref[hp] = lax.dot_general(keys_ref[hp], qn, (((1,), (1,)), ((), ())), precision=HIGHEST,
                                     preferred_element_type=F32)

    idx = lax.broadcasted_iota(jnp.int32, (PEER_NKEYS, 128), 0)
    pidx = lax.broadcasted_iota(jnp.int32, (_PEER_PAIR_ROWS, 128), 0)
    pad_rows = jnp.full((_PEER_PAIR_ROWS - len(_PEER_PAIRS), 128), -jnp.inf, F32)

    def token_chunk(c, carry):
        cs = pl.ds(pl.multiple_of(c * 128, 128), 128)
        for h in range(PEER_HEADS):
            scores, tops, ranks = [], [], []
            for p in range(2):
                st = st_ref[2 * h + p, :, cs]
                work, rank, vals = st, jnp.full(st.shape, NOT_TOP, F32), []
                for it in range(PEER_TOPK):
                    m, hit = _pop_max(work, idx)
                    work = jnp.where(hit, -jnp.inf, work)
                    rank = jnp.where(hit, float(it), rank)
                    vals.append(m)
                scores.append(st)
                tops.append(vals)
                ranks.append(rank)
            cand = jnp.concatenate([tops[0][i] + tops[1][j] for (i, j) in _PEER_PAIRS] + [pad_rows], axis=0)
            picked = jnp.zeros(cand.shape, F32)
            best = []
            for _ in range(PEER_TOPK):
                m, hit = _pop_max(cand, pidx)
                cand = jnp.where(hit, -jnp.inf, cand)
                picked = jnp.where(hit, 1.0, picked)
                best.append(m)
            z = jnp.ones_like(best[0])
            for k in range(1, PEER_TOPK):
                z = z + jnp.exp(best[k] - best[0])
            count = jnp.dot(grp_ref[...], picked.astype(BF16), preferred_element_type=F32)
            n1 = jnp.zeros(scores[0].shape, F32)
            for i in range(PEER_TOPK):
                n1 = jnp.where(ranks[0] == float(i), count[i:i + 1], n1)
            n1_ref[h, :, cs] = n1
            r2_ref[h, :, cs] = ranks[1]
            e1_ref[h, :, cs] = jnp.exp(scores[0] - tops[0][0]) / z
            e2_ref[h, :, cs] = jnp.exp(scores[1] - tops[1][0])
        return carry

    lax.fori_loop(0, tt // 128, token_chunk, 0)


def _peer_select(q2, subkeys, *, tt=512):
    t = q2.shape[0]
    grp = np.zeros((PEER_TOPK, _PEER_PAIR_ROWS), np.float32)
    for row, (i, _) in enumerate(_PEER_PAIRS):
        grp[i, row] = 1.0
    big = pl.BlockSpec((PEER_HEADS, PEER_NKEYS, tt), lambda i: (0, 0, i))
    bshape = jax.ShapeDtypeStruct((PEER_HEADS, PEER_NKEYS, t), F32)
    half = PEER_QDIM // 2
    return pl.pallas_call(
        _peer_sel_kernel,
        grid=(t // tt,),
        in_specs=[pl.BlockSpec((tt, PEER_HEADS * PEER_QDIM), lambda i: (i, 0)),
                  pl.BlockSpec((2 * PEER_HEADS, PEER_NKEYS, half), lambda i: (0, 0, 0)),
                  pl.BlockSpec((PEER_TOPK, _PEER_PAIR_ROWS), lambda i: (0, 0))],
        out_specs=[big, big, big, big],
        out_shape=[bshape, bshape, bshape, bshape],
        scratch_shapes=[pltpu.VMEM((2 * PEER_HEADS, PEER_NKEYS, tt), F32)],
        compiler_params=_params(("parallel",)),
        name="peer_select",
    )(q2, subkeys.reshape(2 * PEER_HEADS, PEER_NKEYS, half), jnp.asarray(grp, dtype=BF16))


PEER_SLAB = 256


def _peer_main_kernel(h_ref, u_ref, vt_ref, n1_ref, e1_ref, r2_ref, e2_ref, x_ref, gt_ref, fg_ref,
                      o_ref, acc_ref, act_ref, p_ref, *, final):
    j = pl.program_id(1)
    te, tt = act_ref.shape
    na = te // PEER_NKEYS

    @pl.when(j == 0)
    def _():
        acc_ref[...] = jnp.zeros(acc_ref.shape, F32)

    a0 = pl.multiple_of(j * na, na)
    hb = h_ref[...]
    for s0 in range(0, te, PEER_SLAB):
        ss = slice(s0, s0 + PEER_SLAB)
        act_ref[ss, :] = lax.dot_general(u_ref[ss, :], hb, (((1,), (1,)), ((), ())), preferred_element_type=F32)
    part = None
    for s0 in range(0, te, PEER_SLAB):
        ss = slice(s0, s0 + PEER_SLAB)
        for al in range(s0 // PEER_NKEYS, (s0 + PEER_SLAB) // PEER_NKEYS):
            rs = slice(al * PEER_NKEYS, (al + 1) * PEER_NKEYS)
            for c in range(tt // 128):
                cs = slice(c * 128, (c + 1) * 128)
                w = jnp.zeros((PEER_NKEYS, 128), F32)
                for hh in range(PEER_HEADS):
                    n1 = n1_ref[hh, pl.ds(a0, na), cs][al:al + 1]
                    e1 = e1_ref[hh, pl.ds(a0, na), cs][al:al + 1]
                    w = w + jnp.where(r2_ref[hh, :, cs] < n1, e1 * e2_ref[hh, :, cs], 0.0)
                p_ref[rs, cs] = (w * _gelu(act_ref[rs, cs])).astype(BF16)
        d = jnp.dot(vt_ref[:, ss], p_ref[ss, :], preferred_element_type=F32)
        part = d if part is None else part + d
    acc_ref[...] += part

    @pl.when(j == pl.num_programs(1) - 1)
    def _():
        y = x_ref[...] + gt_ref[...] * acc_ref[...].T
        if final:
            y = y * lax.rsqrt(jnp.mean(y * y, axis=-1, keepdims=True) + NORM_EPS) * fg_ref[...]
        o_ref[...] = y


def _peer_main(h2, u_bf, vt_bf, n1, e1, r2, e2, x2, mod3, final_g, *, final, tt=512, te=1024):
    t, d = x2.shape
    assert te // PEER_NKEYS == 8 and te % PEER_SLAB == 0
    per_b = SEQ // tt
    big = pl.BlockSpec((PEER_HEADS, PEER_NKEYS, tt), lambda i, j: (0, 0, i))
    return pl.pallas_call(
        functools.partial(_peer_main_kernel, final=final),
        grid=(t // tt, PEER_EXPERTS // te),
        in_specs=[pl.BlockSpec((tt, d), lambda i, j: (i, 0)),
                  pl.BlockSpec((te, d), lambda i, j: (j, 0)),
                  pl.BlockSpec((d, te), lambda i, j: (0, j)),
                  big, big, big, big,
                  pl.BlockSpec((tt, d), lambda i, j: (i, 0)),
                  pl.BlockSpec((None, 1, d), lambda i, j: (i // per_b, 0, 5)),
                  pl.BlockSpec((1, d), lambda i, j: (0, 0))],
        out_specs=pl.BlockSpec((tt, d), lambda i, j: (i, 0)),
        out_shape=jax.ShapeDtypeStruct((t, d), F32),
        scratch_shapes=[pltpu.VMEM((d, tt), F32), pltpu.VMEM((te, tt), F32), pltpu.VMEM((te, tt), BF16)],
        compiler_params=_params(("parallel", "arbitrary")),
        name="peer_experts",
    )(h2, u_bf, vt_bf, n1, e1, r2, e2, x2, mod3, final_g.reshape(1, d))


def _reorder_w_in(w):
    pad = jnp.zeros((w.shape[0], 1024 - 792), w.dtype)
    return jnp.concatenate([w[:, 5400:9496], w[:, 3072:3864], pad, w[:, 0:3072], w[:, 3864:5400]], axis=1)


def kernel(x, c, positions, mod_w, mod_b, norm_mix_g, norm_ffn_g, w_in, a_conv_w, a_out, b_conv_w, b_conv_b, b_ln_g, b_ln_b, b_out, c_cmp_pos, c_cmp_w1, c_cmp_w2, c_out, d_out, w_o, peer_wq, peer_subkeys, peer_u, peer_v, final_norm_g):
    bsz, s, d = x.shape
    assert s == SEQ and d == D_MODEL
    depth = mod_w.shape[0]
    t = bsz * s
    cos, sin = _rope_tables(positions)
    mod = _modulation(c, mod_w, mod_b)
    x2 = x.reshape(t, d)
    for l in range(depth):
        mod3 = mod[l].reshape(bsz, 1, 6 * d)
        proj2 = _norm_matmul(x2, norm_mix_g[l], mod3, 0, 1, _reorder_w_in(w_in[l]).astype(BF16))
        proj3 = proj2.reshape(bsz, s, PROJ_COLS)
        ua, ub = _conv_mixers(proj3, a_conv_w[l], b_conv_w[l], b_conv_b[l], b_ln_g[l], b_ln_b[l])
        qc, kc, vc, ks, vst, kw, vwt, gt, qd, kd, vdt, kmean = _prep(proj3, cos, sin)
        kcmp, vcmpt = _compress(kc, vc, c_cmp_pos[l], c_cmp_w1[l], c_cmp_w2[l])
        oc = _nsa(qc, kcmp, vcmpt, ks, vst, kw, vwt, gt)
        od = _moba(qd, kd, vdt, kmean)
        x2 = _merge(ua.reshape(t, 512), ub.reshape(t, 512), oc.reshape(t, 512), od.reshape(t, 512),
                    proj2, x2, mod3, a_out[l].astype(BF16), b_out[l].astype(BF16), c_out[l].astype(BF16),
                    d_out[l].astype(BF16), w_o[l].astype(BF16))
        q2, h2 = _norm_matmul(x2, norm_ffn_g[l], mod3, 3, 4, peer_wq[l].astype(BF16), emit_h=True)
        n1, e1, r2, e2 = _peer_select(q2, peer_subkeys[l])
        x2 = _peer_main(h2, peer_u[l].astype(BF16), peer_v[l].T.astype(BF16), n1, e1, r2, e2,
                        x2, mod3, final_norm_g, final=(l == depth - 1))
    return x2.reshape(bsz, s, d)
```

```python
import functools

import numpy as np
import jax
import jax.numpy as jnp
from jax import lax
from jax.experimental import pallas as pl
from jax.experimental.pallas import tpu as pltpu

F32 = jnp.float32
BF16 = jnp.bfloat16
HIGHEST = lax.Precision.HIGHEST

D_MODEL = 1024
SEQ = 2048
HEAD_DIM = 64
ROPE_THETA = 10000.0
NORM_EPS = 1e-6
A_WIDTH = 512
A_CONV = 3
B_WIDTH = 512
B_CONV = 31
C_HEADS = 8
C_KV_HEADS = 2
C_GROUP = 4
CMP_BLOCK = 32
CMP_STRIDE = 16
CMP_HIDDEN = 128
N_CMP = (SEQ - CMP_BLOCK) // CMP_STRIDE + 1
SLC_BLOCK = 64
SLC_TOPN = 16
N_SLC = SEQ // SLC_BLOCK
WIN = 512
D_HEADS = 8
MOBA_BLOCK = 256
MOBA_TOPK = 3
N_MOBA = SEQ // MOBA_BLOCK
PEER_HEADS = 8
PEER_NKEYS = 128
PEER_EXPERTS = PEER_NKEYS * PEER_NKEYS
PEER_QDIM = 256
PEER_TOPK = 16

PROJ_COLS = 9728
COL_MERGE = 0
COL_KVG = 4096
COL_GATE = COL_KVG + 768
COL_A = 5120

VMEM_LIMIT = 56 * 1024 * 1024
NEG = -1e30

ATT_T = 256
CONV_T = 256
HALO = 32


def _params(sem, flags=None):
    return pltpu.CompilerParams(dimension_semantics=sem, vmem_limit_bytes=VMEM_LIMIT, flags=flags)


def _gelu(x):
    return 0.5 * x * (1.0 + lax.erf(x * np.float32(np.sqrt(0.5))))


def _rope_table_kernel(pos_ref, inv_ref, sign_ref, cos_ref, sin_ref):
    ang = pos_ref[...] * inv_ref[...]
    cos_ref[...] = jnp.cos(ang)
    sin_ref[...] = jnp.sin(ang) * sign_ref[...]


def _rope_tables(positions):
    bsz, s = positions.shape
    inv = 1.0 / (ROPE_THETA ** (jnp.arange(0, HEAD_DIM, 2, dtype=F32) / HEAD_DIM))
    inv128 = jnp.tile(inv, 4)[None, :]
    sign = jnp.tile(jnp.concatenate([-jnp.ones(32, F32), jnp.ones(32, F32)]), 2)[None, :]
    pos = positions.astype(F32).reshape(bsz * s, 1)
    t = bsz * s
    cos, sin = pl.pallas_call(
        _rope_table_kernel,
        grid=(t // SEQ,),
        in_specs=[pl.BlockSpec((SEQ, 1), lambda i: (i, 0)),
                  pl.BlockSpec((1, 128), lambda i: (0, 0)),
                  pl.BlockSpec((1, 128), lambda i: (0, 0))],
        out_specs=[pl.BlockSpec((SEQ, 128), lambda i: (i, 0))] * 2,
        out_shape=[jax.ShapeDtypeStruct((t, 128), F32)] * 2,
        compiler_params=_params(("parallel",)),
        name="rope_tables",
    )(pos, inv128, sign)
    return cos.reshape(bsz, s, 128), sin.reshape(bsz, s, 128)


def _rope(x, cos, sin):
    w = x.shape[-1]
    lane = lax.broadcasted_iota(jnp.int32, x.shape, 1)
    swapped = jnp.where(lane % 64 < 32, pltpu.roll(x, w - 32, 1), pltpu.roll(x, 32, 1))
    return x * cos + swapped * sin


def _mod_kernel(c_ref, w_ref, b_ref, o_ref):
    c = c_ref[...]
    cond = c * jax.nn.sigmoid(c)
    o_ref[...] = jnp.dot(cond, w_ref[...], precision=HIGHEST, preferred_element_type=F32) + b_ref[...]


def _modulation(c, mod_w, mod_b):
    nl, d, n = mod_w.shape
    bsz = c.shape[0]
    tn = 1536
    return pl.pallas_call(
        _mod_kernel,
        grid=(nl, n // tn),
        in_specs=[pl.BlockSpec((bsz, d), lambda l, j: (0, 0)),
                  pl.BlockSpec((None, d, tn), lambda l, j: (l, 0, j)),
                  pl.BlockSpec((None, 1, tn), lambda l, j: (l, 0, j))],
        out_specs=pl.BlockSpec((None, bsz, tn), lambda l, j: (l, 0, j)),
        out_shape=jax.ShapeDtypeStruct((nl, bsz, n), F32),
        compiler_params=_params(("parallel", "parallel")),
        name="adaln_mod",
    )(c, mod_w, mod_b.reshape(nl, 1, n))


def _norm_matmul_kernel(x_ref, g_ref, sc_ref, sh_ref, w_ref, o_ref, *rest, emit_h):
    h_scr = rest[-1]

    @pl.when(pl.program_id(1) == 0)
    def _():
        x = x_ref[...]
        y = x * lax.rsqrt(jnp.mean(x * x, axis=-1, keepdims=True) + NORM_EPS)
        h = (y * g_ref[...]) * (1.0 + sc_ref[...]) + sh_ref[...]
        h_scr[...] = h.astype(BF16)
        if emit_h:
            rest[0][...] = h.astype(BF16)

    o_ref[...] = jnp.dot(h_scr[...], w_ref[...], preferred_element_type=F32).astype(o_ref.dtype)


def _norm_matmul(x2, g, mod3, sh_blk, sc_blk, w, *, out_dtype=F32, emit_h=False, tm=1024, tn=512):
    t, d = x2.shape
    n = w.shape[1]
    per_b = SEQ // tm
    out_shape = [jax.ShapeDtypeStruct((t, n), out_dtype)]
    out_specs = [pl.BlockSpec((tm, tn), lambda i, j: (i, j))]
    if emit_h:
        out_shape.append(jax.ShapeDtypeStruct((t, d), BF16))
        out_specs.append(pl.BlockSpec((tm, d), lambda i, j: (i, 0)))
    res = pl.pallas_call(
        functools.partial(_norm_matmul_kernel, emit_h=emit_h),
        grid=(t // tm, n // tn),
        in_specs=[pl.BlockSpec((tm, d), lambda i, j: (i, 0)),
                  pl.BlockSpec((1, d), lambda i, j: (0, 0)),
                  pl.BlockSpec((None, 1, d), lambda i, j: (i // per_b, 0, sc_blk)),
                  pl.BlockSpec((None, 1, d), lambda i, j: (i // per_b, 0, sh_blk)),
                  pl.BlockSpec((d, tn), lambda i, j: (0, j))],
        out_specs=out_specs,
        out_shape=out_shape,
        scratch_shapes=[pltpu.VMEM((tm, d), BF16)],
        compiler_params=_params(("parallel", "arbitrary")),
        name="norm_matmul",
    )(x2, g.reshape(1, d), mod3, mod3, w)
    return res if emit_h else res[0]


def _conv_kernel(ab_ref, ac_ref, ax_ref, ba_ref, bg_ref, pac_ref, pax_ref, pba_ref, pbg_ref,
                 aw_ref, bw_ref, bb_ref, lng_ref, lnb_ref, ua_ref, ub_ref, ext_ref):
    ts = ab_ref.shape[0]
    keep = (pl.program_id(1) > 0).astype(F32)

    def f32(ref):
        return ref[...].astype(F32)

    ext_ref[0:HALO, :] = f32(pac_ref) * f32(pax_ref) * keep
    ext_ref[HALO:HALO + ts, :] = f32(ac_ref) * f32(ax_ref)
    acc = jnp.zeros((ts, A_WIDTH), F32)
    for k in range(A_CONV):
        off = HALO - (A_CONV - 1) + k
        acc = acc + aw_ref[k:k + 1, :] * ext_ref[off:off + ts, :]
    ua_ref[...] = (f32(ab_ref) * acc).astype(BF16)

    ext_ref[0:HALO, :] = f32(pba_ref) * jax.nn.sigmoid(f32(pbg_ref)) * keep
    ext_ref[HALO:HALO + ts, :] = f32(ba_ref) * jax.nn.sigmoid(f32(bg_ref))
    acc = jnp.zeros((ts, B_WIDTH), F32) + bb_ref[...]
    for k in range(B_CONV):
        off = HALO - (B_CONV - 1) + k
        acc = acc + bw_ref[k:k + 1, :] * ext_ref[off:off + ts, :]
    mu = jnp.mean(acc, axis=-1, keepdims=True)
    cen = acc - mu
    var = jnp.mean(cen * cen, axis=-1, keepdims=True)
    y = cen * lax.rsqrt(var + NORM_EPS) * lng_ref[...] + lnb_ref[...]
    ub_ref[...] = (y * jax.nn.sigmoid(y)).astype(BF16)


def _conv_mixers(proj3, a_conv_w, b_conv_w, b_conv_b, b_ln_g, b_ln_b):
    bsz = proj3.shape[0]
    ts = CONV_T
    c0 = COL_A // 512
    r = ts // HALO

    def cur(k):
        return pl.BlockSpec((None, ts, 512), lambda b, i, k=k: (b, i, c0 + k))

    def prev(k):
        return pl.BlockSpec((None, HALO, 512), lambda b, i, k=k: (b, jnp.maximum(i * r - 1, 0), c0 + k))

    def full(shape):
        return pl.BlockSpec(shape, lambda b, i: (0,) * len(shape))

    return pl.pallas_call(
        _conv_kernel,
        grid=(bsz, SEQ // ts),
        in_specs=[cur(0), cur(1), cur(2), cur(3), cur(4), prev(1), prev(2), prev(3), prev(4),
                  full((A_CONV, A_WIDTH)), full((B_CONV, B_WIDTH)), full((1, B_WIDTH)),
                  full((1, B_WIDTH)), full((1, B_WIDTH))],
        out_specs=[pl.BlockSpec((None, ts, 512), lambda b, i: (b, i, 0))] * 2,
        out_shape=[jax.ShapeDtypeStruct((bsz, SEQ, 512), BF16)] * 2,
        scratch_shapes=[pltpu.VMEM((HALO + ts, 512), F32)],
        compiler_params=_params(("parallel", "arbitrary")),
        name="conv_mixers",
    )(proj3, proj3, proj3, proj3, proj3, proj3, proj3, proj3, proj3,
      a_conv_w, b_conv_w, b_conv_b.reshape(1, -1), b_ln_g.reshape(1, -1), b_ln_b.reshape(1, -1))


def _prep_kernel(cq_ref, dq_ref, dk_ref, dv_ref, kvg_ref, cos_ref, sin_ref,
                 qc_ref, kc_ref, vc_ref, ks_ref, vst_ref, kw_ref, vwt_ref, gt_ref,
                 qd_ref, kd_ref, vdt_ref, kmean_ref):
    cos = cos_ref[...]
    sin = sin_ref[...]
    cos4 = jnp.concatenate([cos] * 4, axis=1)
    sin4 = jnp.concatenate([sin] * 4, axis=1)
    scale = np.float32(HEAD_DIM ** -0.5)
    qc_ref[...] = _rope(cq_ref[...].astype(F32), cos4, sin4) * scale
    qd_ref[...] = _rope(dq_ref[...].astype(F32), cos4, sin4) * scale
    kd = _rope(dk_ref[...].astype(F32), cos4, sin4)
    for h in range(D_HEADS):
        kd_ref[h] = kd[:, h * 64:(h + 1) * 64].astype(BF16)
    kmean_ref[...] = jnp.mean(kd, axis=0, keepdims=True)
    vdt_ref[...] = dv_ref[...].astype(F32).T.astype(BF16)

    def kvg(k):
        return kvg_ref[:, k * 128:(k + 1) * 128].astype(F32)

    kc = _rope(kvg(0), cos, sin)
    vc = kvg(1)
    ks = _rope(kvg(2), cos, sin)
    kw = _rope(kvg(4), cos, sin)
    for g in range(C_KV_HEADS):
        gs = slice(g * 64, (g + 1) * 64)
        kc_ref[g] = kc[:, gs]
        vc_ref[g] = vc[:, gs]
        ks_ref[g] = ks[:, gs].astype(BF16)
        kw_ref[g] = kw[:, gs].astype(BF16)
    vst_ref[...] = kvg(3).T.astype(BF16)
    vwt_ref[...] = kvg(5).T.astype(BF16)
    gt_ref[...] = jax.nn.sigmoid(kvg(6)).T


def _prep(proj3, cos, sin):
    bsz = proj3.shape[0]
    ts = MOBA_BLOCK
    c0 = COL_A // 512

    def col512(k):
        return pl.BlockSpec((None, ts, 512), lambda b, i: (b, i, c0 + k))

    row128 = pl.BlockSpec((None, ts, 128), lambda b, i: (b, i, 0))
    row512 = pl.BlockSpec((None, ts, 512), lambda b, i: (b, i, 0))
    col128t = pl.BlockSpec((None, 128, ts), lambda b, i: (b, 0, i))
    head64 = pl.BlockSpec((None, C_KV_HEADS, ts, 64), lambda b, i: (b, 0, i, 0))
    k64 = jax.ShapeDtypeStruct((bsz, C_KV_HEADS, SEQ, 64), BF16)
    t128 = jax.ShapeDtypeStruct((bsz, 128, SEQ), BF16)
    return pl.pallas_call(
        _prep_kernel,
        grid=(bsz, SEQ // ts),
        in_specs=[col512(5), col512(6), col512(7), col512(8),
                  pl.BlockSpec((None, ts, 1024), lambda b, i: (b, i, COL_KVG // 1024)),
                  row128, row128],
        out_specs=[row512, head64, head64, head64, col128t, head64, col128t, col128t,
                   row512,
                   pl.BlockSpec((None, D_HEADS, ts, 64), lambda b, i: (b, 0, i, 0)),
                   pl.BlockSpec((None, 512, ts), lambda b, i: (b, 0, i)),
                   pl.BlockSpec((None, None, 1, 512), lambda b, i: (b, i, 0, 0))],
        out_shape=[jax.ShapeDtypeStruct((bsz, SEQ, 512), F32),
                   jax.ShapeDtypeStruct((bsz, C_KV_HEADS, SEQ, 64), F32),
                   jax.ShapeDtypeStruct((bsz, C_KV_HEADS, SEQ, 64), F32),
                   k64, t128, k64, t128,
                   jax.ShapeDtypeStruct((bsz, 128, SEQ), F32),
                   jax.ShapeDtypeStruct((bsz, SEQ, 512), F32),
                   jax.ShapeDtypeStruct((bsz, D_HEADS, SEQ, 64), BF16),
                   jax.ShapeDtypeStruct((bsz, 512, SEQ), BF16),
                   jax.ShapeDtypeStruct((bsz, N_MOBA, 1, 512), F32)],
        compiler_params=_params(("parallel", "parallel")),
        name="attn_prep",
    )(proj3, proj3, proj3, proj3, proj3, cos, sin)


def _compress_kernel(kc_ref, vc_ref, pos_ref, w1_ref, w2_ref, kcmp_ref, vcmp_ref):
    half = CMP_STRIDE * HEAD_DIM
    row = lax.broadcasted_iota(jnp.int32, (128, CMP_HIDDEN), 0)
    for which, (src, dst) in enumerate(((kc_ref, kcmp_ref), (vc_ref, vcmp_ref))):
        bias = jnp.dot(pos_ref[which], w1_ref[which], precision=HIGHEST, preferred_element_type=F32)[0:1]
        for g in range(C_KV_HEADS):
            chunks = src[g]
            d1 = jnp.dot(chunks, w1_ref[which, 0:half, :], precision=HIGHEST, preferred_element_type=F32)
            d2 = jnp.dot(chunks, w1_ref[which, half:2 * half, :], precision=HIGHEST, preferred_element_type=F32)
            d2 = jnp.where(row < 127, pltpu.roll(d2, 127, 0), 0.0)
            hid = _gelu(d1 + d2 + bias)
            out = jnp.dot(hid, w2_ref[which], precision=HIGHEST, preferred_element_type=F32)
            dst[g] = out if which == 0 else out.T


def _compress(kc, vc, cmp_pos, cmp_w1, cmp_w2):
    bsz = kc.shape[0]
    kc4 = kc.reshape(bsz, C_KV_HEADS, SEQ // CMP_STRIDE, CMP_STRIDE * HEAD_DIM)
    vc4 = vc.reshape(bsz, C_KV_HEADS, SEQ // CMP_STRIDE, CMP_STRIDE * HEAD_DIM)
    pos8 = jnp.broadcast_to(cmp_pos.reshape(2, 1, CMP_BLOCK * HEAD_DIM), (2, 8, CMP_BLOCK * HEAD_DIM))
    blk = pl.BlockSpec((None, C_KV_HEADS, 128, 1024), lambda b: (b, 0, 0, 0))
    out = pl.BlockSpec((None, C_KV_HEADS, 128, 64), lambda b: (b, 0, 0, 0))
    return pl.pallas_call(
        _compress_kernel,
        grid=(bsz,),
        in_specs=[blk, blk,
                  pl.BlockSpec((2, 8, 2048), lambda b: (0, 0, 0)),
                  pl.BlockSpec((2, 2048, CMP_HIDDEN), lambda b: (0, 0, 0)),
                  pl.BlockSpec((2, CMP_HIDDEN, 64), lambda b: (0, 0, 0))],
        out_specs=[out, pl.BlockSpec((None, C_KV_HEADS, 64, 128), lambda b: (b, 0, 0, 0))],
        out_shape=[jax.ShapeDtypeStruct((bsz, C_KV_HEADS, 128, 64), F32),
                   jax.ShapeDtypeStruct((bsz, C_KV_HEADS, 64, 128), F32)],
        compiler_params=_params(("parallel",)),
        name="nsa_compress",
    )(kc4, vc4, pos8, cmp_w1, cmp_w2)


def _flash_steps_t(heads, m_ref, l_ref, acc_ref, groups):
    scores = [lax.dot_general(k, qb, (((1,), (1,)), ((), ())), preferred_element_type=F32)
              for (qb, k, _, _) in heads]
    probs = []
    for n, (qb, _, _, mask) in enumerate(heads):
        s = scores[n]
        tq = qb.shape[0] // groups
        if mask is not None:
            s = jnp.concatenate([jnp.where(mask, s[:, r * tq:(r + 1) * tq], NEG) for r in range(groups)], axis=1)
        sr = _stat_row(n)
        m_old = m_ref[sr, :]
        m_new = jnp.maximum(m_old, jnp.max(s, axis=0, keepdims=True))
        alpha = jnp.exp(m_old - m_new)
        p = jnp.exp(s - m_new)
        l_ref[sr, :] = alpha * l_ref[sr, :] + jnp.sum(p, axis=0, keepdims=True)
        m_ref[sr, :] = m_new
        probs.append((alpha, p.astype(BF16)))
    for n, (_, _, vt, _) in enumerate(heads):
        alpha, p = probs[n]
        a = slice(n * HEAD_DIM, (n + 1) * HEAD_DIM)
        acc_ref[a, :] = alpha * acc_ref[a, :] + jnp.dot(vt, p, preferred_element_type=F32)


def _stat_row(row):
    return slice(8 * row, 8 * row + 1)


def _flash_init(m_ref, l_ref, acc_ref):
    m_ref[...] = jnp.full(m_ref.shape, NEG, F32)
    l_ref[...] = jnp.zeros(l_ref.shape, F32)
    acc_ref[...] = jnp.zeros(acc_ref.shape, F32)


def _rank_desc_rows(vals):
    n = vals.shape[0]
    row = lax.broadcasted_iota(jnp.int32, vals.shape, 0)
    rank = jnp.zeros(vals.shape, jnp.int32)
    for i in range(n):
        vi = vals[i:i + 1, :]
        ahead = (vi > vals) | ((vi == vals) & (row > i))
        rank = rank + ahead.astype(jnp.int32)
    return rank


def _nsa_kernel(q_ref, kcmp_ref, vcmpt_ref, ks_ref, vst_ref, kw_ref, vwt_ref, gt_ref, ovlt_ref, expt_ref,
                o_ref, qb_ref, sel_ref, m_ref, l_ref, acc_ref, out_ref):
    tq = ATT_T
    i = pl.program_id(1)
    t0 = i * tq
    dstart = pl.multiple_of(t0, tq)
    t_row = t0 + lax.broadcasted_iota(jnp.int32, (1, tq), 1)
    t_rows = jnp.concatenate([t_row] * C_GROUP, axis=1)
    blk = lax.broadcasted_iota(jnp.int32, (128, 1), 0)
    kk = lax.broadcasted_iota(jnp.int32, (tq, tq), 0)
    qq = lax.broadcasted_iota(jnp.int32, (tq, tq), 1)
    causal = kk <= qq
    win_tail = kk > qq

    def gate_row(branch, g):
        base = branch * 8 + g * C_GROUP
        return jnp.concatenate([gt_ref[base + r:base + r + 1, :] for r in range(C_GROUP)], axis=1)

    def hrows(g):
        return slice(g * HEAD_DIM, (g + 1) * HEAD_DIM)

    for g in range(C_KV_HEADS):
        qf = jnp.concatenate([q_ref[:, (g * 4 + r) * 64:(g * 4 + r + 1) * 64] for r in range(C_GROUP)], axis=0)
        qb_ref[g] = qf.astype(BF16)

        s = lax.dot_general(kcmp_ref[g], qf, (((1,), (1,)), ((), ())), precision=HIGHEST,
                            preferred_element_type=F32)
        vis = (blk * CMP_STRIDE + (CMP_BLOCK - 1)) <= t_rows
        sm = jnp.where(vis, s, NEG)
        e = jnp.where(vis, jnp.exp(sm - jnp.max(sm, axis=0, keepdims=True)), 0.0)
        p = e / jnp.maximum(jnp.sum(e, axis=0, keepdims=True), 1e-30)
        o_cmp = jnp.dot(vcmpt_ref[g].astype(BF16), p.astype(BF16), preferred_element_type=F32)
        out_ref[hrows(g), :] = gate_row(0, g) * o_cmp

        psum = p[:, 0:tq] + p[:, tq:2 * tq] + p[:, 2 * tq:3 * tq] + p[:, 3 * tq:4 * tq]
        imp = jnp.dot(ovlt_ref[...], psum, precision=HIGHEST, preferred_element_type=F32)[0:N_SLC]
        b32 = blk[0:N_SLC]
        cur = t_row // SLC_BLOCK
        forced = (b32 == 0) | (b32 == cur) | (b32 == cur - 1)
        imp = jnp.where(forced, jnp.inf, jnp.where(b32 * SLC_BLOCK > t_row, -jnp.inf, imp))
        sel = (_rank_desc_rows(imp) < SLC_TOPN).astype(F32)
        sel_ref[g] = jnp.concatenate([sel, jnp.zeros((128 - N_SLC, tq), F32)], axis=0).astype(BF16)

    def slc_tile(jb, extra):
        start = pl.multiple_of(jb * tq, tq)
        heads = []
        for g in range(C_KV_HEADS):
            hit = jnp.dot(expt_ref[pl.ds(start, tq), :], sel_ref[g], preferred_element_type=F32) > 0.5
            mask = hit if extra is None else hit & extra
            heads.append((qb_ref[g], ks_ref[g, pl.ds(start, tq), :], vst_ref[hrows(g), pl.ds(start, tq)], mask))
        _flash_steps_t(heads, m_ref, l_ref, acc_ref, C_GROUP)

    def win_tile(start, mask):
        heads = [(qb_ref[g], kw_ref[g, pl.ds(start, tq), :], vwt_ref[hrows(g), pl.ds(start, tq)], mask)
                 for g in range(C_KV_HEADS)]
        _flash_steps_t(heads, m_ref, l_ref, acc_ref, C_GROUP)

    def add_branch(branch):
        for g in range(C_KV_HEADS):
            out_ref[hrows(g), :] += gate_row(branch, g) * (acc_ref[hrows(g), :] / l_ref[_stat_row(g), :])

    _flash_init(m_ref, l_ref, acc_ref)
    slc_tile(i, causal)

    def slc_body(jb, carry):
        slc_tile(jb, None)
        return carry

    lax.fori_loop(0, i, slc_body, 0)
    add_branch(1)

    _flash_init(m_ref, l_ref, acc_ref)
    win_tile(dstart, causal)

    @pl.when(i >= 1)
    def _():
        win_tile(pl.multiple_of(t0 - tq, tq), None)

    @pl.when(i >= 2)
    def _():
        win_tile(pl.multiple_of(t0 - 2 * tq, tq), win_tail)

    add_branch(2)

    o_t = jnp.concatenate([out_ref[hrows(g), r * tq:(r + 1) * tq]
                           for g in range(C_KV_HEADS) for r in range(C_GROUP)], axis=0)
    o_ref[...] = o_t.T.astype(BF16)


def _nsa_constants():
    j = np.arange(128)[:, None]
    n = np.arange(128)[None, :]
    ovl_t = ((n * CMP_STRIDE < j * SLC_BLOCK + SLC_BLOCK) & (n * CMP_STRIDE + CMP_BLOCK > j * SLC_BLOCK)
             & (n < N_CMP) & (j < N_SLC)).astype(np.float32)
    expand_t = (np.arange(SEQ)[:, None] // SLC_BLOCK == np.arange(128)[None, :]).astype(np.float32)
    return jnp.asarray(ovl_t), jnp.asarray(expand_t, dtype=BF16)


def _nsa(qc, kcmp, vcmpt, ks, vst, kw, vwt, gt):
    bsz = qc.shape[0]
    tq = ATT_T
    ovl_t, expand_t = _nsa_constants()
    keys = pl.BlockSpec((None, C_KV_HEADS, SEQ, 64), lambda b, i: (b, 0, 0, 0))
    vals = pl.BlockSpec((None, 128, SEQ), lambda b, i: (b, 0, 0))
    rows = C_GROUP * tq
    return pl.pallas_call(
        _nsa_kernel,
        grid=(bsz, SEQ // tq),
        in_specs=[pl.BlockSpec((None, tq, 512), lambda b, i: (b, i, 0)),
                  pl.BlockSpec((None, C_KV_HEADS, 128, 64), lambda b, i: (b, 0, 0, 0)),
                  pl.BlockSpec((None, C_KV_HEADS, 64, 128), lambda b, i: (b, 0, 0, 0)),
                  keys, vals, keys, vals,
                  pl.BlockSpec((None, 128, tq), lambda b, i: (b, 0, i)),
                  pl.BlockSpec((128, 128), lambda b, i: (0, 0)),
                  pl.BlockSpec((SEQ, 128), lambda b, i: (0, 0))],
        out_specs=pl.BlockSpec((None, tq, 512), lambda b, i: (b, i, 0)),
        out_shape=jax.ShapeDtypeStruct((bsz, SEQ, 512), BF16),
        scratch_shapes=[pltpu.VMEM((C_KV_HEADS, rows, 64), BF16), pltpu.VMEM((C_KV_HEADS, 128, tq), BF16),
                        pltpu.VMEM((8 * C_KV_HEADS, rows), F32), pltpu.VMEM((8 * C_KV_HEADS, rows), F32),
                        pltpu.VMEM((C_KV_HEADS * 64, rows), F32), pltpu.VMEM((C_KV_HEADS * 64, rows), F32)],
        compiler_params=_params(("parallel", "arbitrary")),
        name="nsa_attention",
    )(qc, kcmp, vcmpt, ks, vst, kw, vwt, gt, ovl_t, expand_t)


def _moba_kernel(q_ref, k_ref, vt_ref, kmean_ref, o_ref, qb_ref, sel_ref, m_ref, l_ref, acc_ref):
    tq = ATT_T
    i = pl.program_id(1)
    blk = lax.broadcasted_iota(jnp.int32, (N_MOBA, 1), 0)
    kk = lax.broadcasted_iota(jnp.int32, (tq, tq), 0)
    qq = lax.broadcasted_iota(jnp.int32, (tq, tq), 1)
    causal = kk <= qq
    past = blk < i
    for h in range(D_HEADS):
        hs = slice(h * 64, (h + 1) * 64)
        qf = q_ref[:, hs]
        qb_ref[h] = qf.astype(BF16)
        gate = lax.dot_general(kmean_ref[:, hs], qf, (((1,), (1,)), ((), ())), precision=HIGHEST,
                               preferred_element_type=F32)
        gate = jnp.where(past, gate, -jnp.inf)
        sel_ref[h] = (past & (_rank_desc_rows(gate) < MOBA_TOPK)).astype(F32)

    def tile(jb, diag):
        start = pl.multiple_of(jb * tq, tq)
        heads = []
        for h in range(D_HEADS):
            if diag:
                mask = causal
            else:
                mask = jnp.sum(jnp.where(blk == jb, sel_ref[h], 0.0), axis=0, keepdims=True) > 0.5
            heads.append((qb_ref[h], k_ref[h, pl.ds(start, tq), :], vt_ref[h * 64:(h + 1) * 64, pl.ds(start, tq)],
                          mask))
        _flash_steps_t(heads, m_ref, l_ref, acc_ref, 1)

    _flash_init(m_ref, l_ref, acc_ref)
    tile(i, True)

    def body(jb, carry):
        tile(jb, False)
        return carry

    lax.fori_loop(0, i, body, 0)
    o_t = jnp.concatenate([acc_ref[h * 64:(h + 1) * 64, :] / l_ref[_stat_row(h), :] for h in range(D_HEADS)], axis=0)
    o_ref[...] = o_t.T.astype(BF16)


def _moba(qd, kd, vdt, kmean):
    bsz = qd.shape[0]
    tq = ATT_T
    return pl.pallas_call(
        _moba_kernel,
        grid=(bsz, SEQ // tq),
        in_specs=[pl.BlockSpec((None, tq, 512), lambda b, i: (b, i, 0)),
                  pl.BlockSpec((None, D_HEADS, SEQ, 64), lambda b, i: (b, 0, 0, 0)),
                  pl.BlockSpec((None, 512, SEQ), lambda b, i: (b, 0, 0)),
                  pl.BlockSpec((None, N_MOBA, 512), lambda b, i: (b, 0, 0))],
        out_specs=pl.BlockSpec((None, tq, 512), lambda b, i: (b, i, 0)),
        out_shape=jax.ShapeDtypeStruct((bsz, SEQ, 512), BF16),
        scratch_shapes=[pltpu.VMEM((D_HEADS, tq, 64), BF16), pltpu.VMEM((D_HEADS, N_MOBA, tq), F32),
                        pltpu.VMEM((8 * D_HEADS, tq), F32), pltpu.VMEM((8 * D_HEADS, tq), F32),
                        pltpu.VMEM((512, tq), F32)],
        compiler_params=_params(("parallel", "arbitrary")),
        name="moba_attention",
    )(qd, kd, vdt, kmean.reshape(bsz, N_MOBA, 512))


def _merge_kernel(ua_ref, ub_ref, oc_ref, od_ref, mg_ref, x_ref, gt_ref,
                  wa_ref, wb_ref, wc_ref, wd_ref, wo_ref, o_ref):
    d = D_MODEL
    merged = jnp.zeros(x_ref.shape, F32)
    for k, (u_ref, w_ref) in enumerate(((ua_ref, wa_ref), (ub_ref, wb_ref), (oc_ref, wc_ref), (od_ref, wd_ref))):
        y = jnp.dot(u_ref[...], w_ref[...], preferred_element_type=F32)
        merged = merged + jax.nn.sigmoid(mg_ref[:, k * d:(k + 1) * d].astype(F32)) * y
    o_ref[...] = x_ref[...] + gt_ref[...] * jnp.dot(merged.astype(BF16), wo_ref[...], preferred_element_type=F32)


def _merge(ua, ub, oc, od, proj2, x2, mod3, wa, wb, wc, wd, wo, *, tm=512):
    t, d = x2.shape
    per_b = SEQ // tm
    act = pl.BlockSpec((tm, 512), lambda i: (i, 0))
    wspec = pl.BlockSpec((512, d), lambda i: (0, 0))
    return pl.pallas_call(
        _merge_kernel,
        grid=(t // tm,),
        in_specs=[act, act, act, act,
                  pl.BlockSpec((tm, 4 * d), lambda i: (i, COL_MERGE // (4 * d))),
                  pl.BlockSpec((tm, d), lambda i: (i, 0)),
                  pl.BlockSpec((None, 1, d), lambda i: (i // per_b, 0, 2)),
                  wspec, wspec, wspec, wspec,
                  pl.BlockSpec((d, d), lambda i: (0, 0))],
        out_specs=pl.BlockSpec((tm, d), lambda i: (i, 0)),
        out_shape=jax.ShapeDtypeStruct((t, d), F32),
        compiler_params=_params(("parallel",)),
        name="merge_out",
    )(ua, ub, oc, od, proj2, x2, mod3, wa, wb, wc, wd, wo)


def _pop_max(work, idx):
    m = jnp.max(work, axis=0, keepdims=True)
    first = jnp.min(jnp.where(work == m, idx, work.shape[0]), axis=0, keepdims=True)
    return m, idx == first


_PEER_PAIRS = [(i, j) for i in range(PEER_TOPK) for j in range(PEER_TOPK) if (i + 1) * (j + 1) <= PEER_TOPK]
_PEER_PAIR_ROWS = 56
NOT_TOP = 99.0


def _peer_sel_kernel(q_ref, keys_ref, grp_ref, n1_ref, e1_ref, r2_ref, e2_ref, st_ref):
    tt = q_ref.shape[0]
    half = PEER_QDIM // 2
    for hp in range(2 * PEER_HEADS):
        qh = q_ref[:, hp * half:(hp + 1) * half]
        qn = qh * lax.rsqrt(jnp.mean(qh * qh, axis=-1, keepdims=True) + NORM_EPS)
        st_ref[hp] = lax.dot_general(keys_ref[hp], qn, (((1,), (1,)), ((), ())), precision=HIGHEST,
                                     preferred_element_type=F32)

    idx = lax.broadcasted_iota(jnp.int32, (PEER_NKEYS, 128), 0)
    pidx = lax.broadcasted_iota(jnp.int32, (_PEER_PAIR_ROWS, 128), 0)
    pad_rows = jnp.full((_PEER_PAIR_ROWS - len(_PEER_PAIRS), 128), -jnp.inf, F32)

    def pop16(work, index, tie_safe):
        vals, order = [], jnp.full(work.shape, NOT_TOP, F32)
        for it in range(PEER_TOPK):
            if tie_safe:
                m, hit = _pop_max(work, index)
            else:
                m = jnp.max(work, axis=0, keepdims=True)
                hit = work == m
            work = jnp.where(hit, -jnp.inf, work)
            order = jnp.where(hit, float(it), order)
            vals.append(m)
        removed = jnp.sum(jnp.where(work == -jnp.inf, 1.0, 0.0), axis=0, keepdims=True)
        return vals, order, removed

    def select(cs, tie_safe):
        wrong = jnp.zeros((1, 128), F32)
        n_pad = float(_PEER_PAIR_ROWS - len(_PEER_PAIRS))
        for h in range(PEER_HEADS):
            scores, tops, ranks = [], [], []
            for p in range(2):
                st = st_ref[2 * h + p, :, cs]
                vals, rank, removed = pop16(st, idx, tie_safe)
                wrong = wrong + jnp.abs(removed - float(PEER_TOPK))
                scores.append(st)
                tops.append(vals)
                ranks.append(rank)
            cand = jnp.concatenate([tops[0][i] + tops[1][j] for (i, j) in _PEER_PAIRS] + [pad_rows], axis=0)
            best, order, removed = pop16(cand, pidx, tie_safe)
            wrong = wrong + jnp.abs(removed - (float(PEER_TOPK) + n_pad))
            picked = jnp.where(order < float(PEER_TOPK), 1.0, 0.0)
            z = jnp.ones_like(best[0])
            for k in range(1, PEER_TOPK):
                z = z + jnp.exp(best[k] - best[0])
            count = jnp.dot(grp_ref[...], picked.astype(BF16), preferred_element_type=F32)
            n1 = jnp.zeros(scores[0].shape, F32)
            for i in range(PEER_TOPK):
                n1 = jnp.where(ranks[0] == float(i), count[i:i + 1], n1)
            n1_ref[h, :, cs] = n1
            r2_ref[h, :, cs] = ranks[1]
            e1_ref[h, :, cs] = jnp.exp(scores[0] - tops[0][0]) / z
            e2_ref[h, :, cs] = jnp.exp(scores[1] - tops[1][0])
        return wrong

    def token_chunk(c, carry):
        cs = pl.ds(pl.multiple_of(c * 128, 128), 128)
        wrong = select(cs, tie_safe=False)

        @pl.when(jnp.max(wrong) > 0.0)
        def _():
            select(cs, tie_safe=True)

        return carry

    lax.fori_loop(0, tt // 128, token_chunk, 0)


def _peer_select(q2, subkeys, *, tt=512):
    t = q2.shape[0]
    grp = np.zeros((PEER_TOPK, _PEER_PAIR_ROWS), np.float32)
    for row, (i, _) in enumerate(_PEER_PAIRS):
        grp[i, row] = 1.0
    big = pl.BlockSpec((PEER_HEADS, PEER_NKEYS, tt), lambda i: (0, 0, i))
    bshape = jax.ShapeDtypeStruct((PEER_HEADS, PEER_NKEYS, t), F32)
    half = PEER_QDIM // 2
    return pl.pallas_call(
        _peer_sel_kernel,
        grid=(t // tt,),
        in_specs=[pl.BlockSpec((tt, PEER_HEADS * PEER_QDIM), lambda i: (i, 0)),
                  pl.BlockSpec((2 * PEER_HEADS, PEER_NKEYS, half), lambda i: (0, 0, 0)),
                  pl.BlockSpec((PEER_TOPK, _PEER_PAIR_ROWS), lambda i: (0, 0))],
        out_specs=[big, big, big, big],
        out_shape=[bshape, bshape, bshape, bshape],
        scratch_shapes=[pltpu.VMEM((2 * PEER_HEADS, PEER_NKEYS, tt), F32)],
        compiler_params=_params(("parallel",)),
        name="peer_select",
    )(q2, subkeys.reshape(2 * PEER_HEADS, PEER_NKEYS, half), jnp.asarray(grp, dtype=BF16))


PEER_SLAB = 256

def _peer_main_kernel(h_ref, u_ref, vt_ref, n1_ref, e1_ref, r2_ref, e2_ref, x_ref, gt_ref, fg_ref,
                      o_ref, acc_ref, act_ref, p_ref, *, final):
    j = pl.program_id(1)
    te, tt = act_ref.shape
    na = te // PEER_NKEYS

    @pl.when(j == 0)
    def _():
        acc_ref[...] = jnp.zeros(acc_ref.shape, F32)

    a0 = pl.multiple_of(j * na, na)
    hb = h_ref[...]
    nslab = te // PEER_SLAB
    parts = []

    def act(s):
        ss = slice(s * PEER_SLAB, (s + 1) * PEER_SLAB)
        act_ref[ss, :] = lax.dot_general(u_ref[ss, :], hb, (((1,), (1,)), ((), ())), preferred_element_type=F32)

    def val(s):
        ss = slice(s * PEER_SLAB, (s + 1) * PEER_SLAB)
        parts.append(jnp.dot(vt_ref[:, ss], p_ref[ss, :], preferred_element_type=F32))

    def mask(al):
        rs = slice(al * PEER_NKEYS, (al + 1) * PEER_NKEYS)
        for c in range(tt // 128):
            cs = slice(c * 128, (c + 1) * 128)
            w = jnp.zeros((PEER_NKEYS, 128), F32)
            for hh in range(PEER_HEADS):
                n1 = n1_ref[hh, pl.ds(a0, na), cs][al:al + 1]
                e1 = e1_ref[hh, pl.ds(a0, na), cs][al:al + 1]
                w = w + jnp.where(r2_ref[hh, :, cs] < n1, e1 * e2_ref[hh, :, cs], 0.0)
            p_ref[rs, cs] = (w * _gelu(act_ref[rs, cs])).astype(BF16)

    per = PEER_SLAB // PEER_NKEYS
    act(0)
    act(1)
    for s in range(nslab):
        for k in range(per):
            mask(s * per + k)
            if k == 0 and s + 2 < nslab:
                act(s + 2)
        val(s)
    acc_ref[...] += functools.reduce(lambda x, y: x + y, parts)

    @pl.when(j == pl.num_programs(1) - 1)
    def _():
        y = x_ref[...] + gt_ref[...] * acc_ref[...].T
        if final:
            y = y * lax.rsqrt(jnp.mean(y * y, axis=-1, keepdims=True) + NORM_EPS) * fg_ref[...]
        o_ref[...] = y


def _peer_main(h2, u_bf, vt_bf, n1, e1, r2, e2, x2, mod3, final_g, *, final, tt=512, te=1024):
    t, d = x2.shape
    assert te // PEER_NKEYS == 8 and te % PEER_SLAB == 0
    per_b = SEQ // tt
    big = pl.BlockSpec((PEER_HEADS, PEER_NKEYS, tt), lambda i, j: (0, 0, i))
    return pl.pallas_call(
        functools.partial(_peer_main_kernel, final=final),
        grid=(t // tt, PEER_EXPERTS // te),
        in_specs=[pl.BlockSpec((tt, d), lambda i, j: (i, 0)),
                  pl.BlockSpec((te, d), lambda i, j: (j, 0)),
                  pl.BlockSpec((d, te), lambda i, j: (0, j)),
                  big, big, big, big,
                  pl.BlockSpec((tt, d), lambda i, j: (i, 0)),
                  pl.BlockSpec((None, 1, d), lambda i, j: (i // per_b, 0, 5)),
                  pl.BlockSpec((1, d), lambda i, j: (0, 0))],
        out_specs=pl.BlockSpec((tt, d), lambda i, j: (i, 0)),
        out_shape=jax.ShapeDtypeStruct((t, d), F32),
        scratch_shapes=[pltpu.VMEM((d, tt), F32), pltpu.VMEM((te, tt), F32), pltpu.VMEM((te, tt), BF16)],
        compiler_params=_params(("parallel", "arbitrary")),
        name="peer_experts",
    )(h2, u_bf, vt_bf, n1, e1, r2, e2, x2, mod3, final_g.reshape(1, d))


def _reorder_w_in(w):
    pad = jnp.zeros((w.shape[0], 1024 - 792), w.dtype)
    return jnp.concatenate([w[:, 5400:9496], w[:, 3072:3864], pad, w[:, 0:3072], w[:, 3864:5400]], axis=1)


def kernel(x, c, positions, mod_w, mod_b, norm_mix_g, norm_ffn_g, w_in, a_conv_w, a_out, b_conv_w, b_conv_b, b_ln_g, b_ln_b, b_out, c_cmp_pos, c_cmp_w1, c_cmp_w2, c_out, d_out, w_o, peer_wq, peer_subkeys, peer_u, peer_v, final_norm_g):
    bsz, s, d = x.shape
    assert s == SEQ and d == D_MODEL
    depth = mod_w.shape[0]
    t = bsz * s
    cos, sin = _rope_tables(positions)
    mod = _modulation(c, mod_w, mod_b)
    x2 = x.reshape(t, d)
    for l in range(depth):
        mod3 = mod[l].reshape(bsz, 1, 6 * d)
        proj2 = _norm_matmul(x2, norm_mix_g[l], mod3, 0, 1, _reorder_w_in(w_in[l]).astype(BF16), out_dtype=BF16)
        proj3 = proj2.reshape(bsz, s, PROJ_COLS)
        ua, ub = _conv_mixers(proj3, a_conv_w[l], b_conv_w[l], b_conv_b[l], b_ln_g[l], b_ln_b[l])
        qc, kc, vc, ks, vst, kw, vwt, gt, qd, kd, vdt, kmean = _prep(proj3, cos, sin)
        kcmp, vcmpt = _compress(kc, vc, c_cmp_pos[l], c_cmp_w1[l], c_cmp_w2[l])
        oc = _nsa(qc, kcmp, vcmpt, ks, vst, kw, vwt, gt)
        od = _moba(qd, kd, vdt, kmean)
        x2 = _merge(ua.reshape(t, 512), ub.reshape(t, 512), oc.reshape(t, 512), od.reshape(t, 512),
                    proj2, x2, mod3, a_out[l].astype(BF16), b_out[l].astype(BF16), c_out[l].astype(BF16),
                    d_out[l].astype(BF16), w_o[l].astype(BF16))
        q2, h2 = _norm_matmul(x2, norm_ffn_g[l], mod3, 3, 4, peer_wq[l].astype(BF16), emit_h=True)
        n1, e1, r2, e2 = _peer_select(q2, peer_subkeys[l])
        x2 = _peer_main(h2, peer_u[l].astype(BF16), peer_v[l].T.astype(BF16), n1, e1, r2, e2,
                        x2, mod3, final_norm_g, final=(l == depth - 1))
    return x2.reshape(bsz, s, d)
```

```python
import functools

import numpy as np
import jax
import jax.numpy as jnp
from jax import lax
from jax.experimental import pallas as pl
from jax.experimental.pallas import tpu as pltpu

F32 = jnp.float32
BF16 = jnp.bfloat16
HIGHEST = lax.Precision.HIGHEST

D_MODEL = 1024
SEQ = 2048
HEAD_DIM = 64
ROPE_THETA = 10000.0
NORM_EPS = 1e-6
A_WIDTH = 512
A_CONV = 3
B_WIDTH = 512
B_CONV = 31
C_HEADS = 8
C_KV_HEADS = 2
C_GROUP = 4
CMP_BLOCK = 32
CMP_STRIDE = 16
CMP_HIDDEN = 128
N_CMP = (SEQ - CMP_BLOCK) // CMP_STRIDE + 1
SLC_BLOCK = 64
SLC_TOPN = 16
N_SLC = SEQ // SLC_BLOCK
WIN = 512
D_HEADS = 8
MOBA_BLOCK = 256
MOBA_TOPK = 3
N_MOBA = SEQ // MOBA_BLOCK
PEER_HEADS = 8
PEER_NKEYS = 128
PEER_EXPERTS = PEER_NKEYS * PEER_NKEYS
PEER_QDIM = 256
PEER_TOPK = 16

PROJ_COLS = 9728
COL_MERGE = 0
COL_KVG = 4096
COL_GATE = COL_KVG + 768
COL_A = 5120

VMEM_LIMIT = 56 * 1024 * 1024
NEG = -1e30

ATT_T = 256
CONV_T = 256
HALO = 32


def _params(sem, flags=None):
    return pltpu.CompilerParams(dimension_semantics=sem, vmem_limit_bytes=VMEM_LIMIT, flags=flags)


def _gelu(x):
    return 0.5 * x * (1.0 + lax.erf(x * np.float32(np.sqrt(0.5))))


def _rope_table_kernel(pos_ref, inv_ref, sign_ref, cos_ref, sin_ref):
    ang = pos_ref[...] * inv_ref[...]
    cos_ref[...] = jnp.cos(ang)
    sin_ref[...] = jnp.sin(ang) * sign_ref[...]


def _rope_tables(positions):
    bsz, s = positions.shape
    inv = 1.0 / (ROPE_THETA ** (jnp.arange(0, HEAD_DIM, 2, dtype=F32) / HEAD_DIM))
    inv128 = jnp.tile(inv, 4)[None, :]
    sign = jnp.tile(jnp.concatenate([-jnp.ones(32, F32), jnp.ones(32, F32)]), 2)[None, :]
    pos = positions.astype(F32).reshape(bsz * s, 1)
    t = bsz * s
    cos, sin = pl.pallas_call(
        _rope_table_kernel,
        grid=(t // SEQ,),
        in_specs=[pl.BlockSpec((SEQ, 1), lambda i: (i, 0)),
                  pl.BlockSpec((1, 128), lambda i: (0, 0)),
                  pl.BlockSpec((1, 128), lambda i: (0, 0))],
        out_specs=[pl.BlockSpec((SEQ, 128), lambda i: (i, 0))] * 2,
        out_shape=[jax.ShapeDtypeStruct((t, 128), F32)] * 2,
        compiler_params=_params(("parallel",)),
        name="rope_tables",
    )(pos, inv128, sign)
    return cos.reshape(bsz, s, 128), sin.reshape(bsz, s, 128)


def _rope(x, cos, sin):
    w = x.shape[-1]
    lane = lax.broadcasted_iota(jnp.int32, x.shape, 1)
    swapped = jnp.where(lane % 64 < 32, pltpu.roll(x, w - 32, 1), pltpu.roll(x, 32, 1))
    return x * cos + swapped * sin


def _mod_kernel(c_ref, w_ref, b_ref, o_ref):
    c = c_ref[...]
    cond = c * jax.nn.sigmoid(c)
    o_ref[...] = jnp.dot(cond, w_ref[...], precision=HIGHEST, preferred_element_type=F32) + b_ref[...]


def _modulation(c, mod_w, mod_b):
    nl, d, n = mod_w.shape
    bsz = c.shape[0]
    tn = 1536
    return pl.pallas_call(
        _mod_kernel,
        grid=(nl, n // tn),
        in_specs=[pl.BlockSpec((bsz, d), lambda l, j: (0, 0)),
                  pl.BlockSpec((None, d, tn), lambda l, j: (l, 0, j)),
                  pl.BlockSpec((None, 1, tn), lambda l, j: (l, 0, j))],
        out_specs=pl.BlockSpec((None, bsz, tn), lambda l, j: (l, 0, j)),
        out_shape=jax.ShapeDtypeStruct((nl, bsz, n), F32),
        compiler_params=_params(("parallel", "parallel")),
        name="adaln_mod",
    )(c, mod_w, mod_b.reshape(nl, 1, n))


def _norm_matmul_kernel(x_ref, g_ref, sc_ref, sh_ref, w_ref, o_ref, *rest, emit_h):
    h_scr = rest[-1]

    @pl.when(pl.program_id(1) == 0)
    def _():
        x = x_ref[...]
        y = x * lax.rsqrt(jnp.mean(x * x, axis=-1, keepdims=True) + NORM_EPS)
        h = (y * g_ref[...]) * (1.0 + sc_ref[...]) + sh_ref[...]
        h_scr[...] = h.astype(BF16)
        if emit_h:
            rest[0][...] = h.astype(BF16)

    o_ref[...] = jnp.dot(h_scr[...], w_ref[...], preferred_element_type=F32).astype(o_ref.dtype)


def _norm_matmul(x2, g, mod3, sh_blk, sc_blk, w, *, out_dtype=F32, emit_h=False, tm=1024, tn=512):
    t, d = x2.shape
    n = w.shape[1]
    per_b = SEQ // tm
    out_shape = [jax.ShapeDtypeStruct((t, n), out_dtype)]
    out_specs = [pl.BlockSpec((tm, tn), lambda i, j: (i, j))]
    if emit_h:
        out_shape.append(jax.ShapeDtypeStruct((t, d), BF16))
        out_specs.append(pl.BlockSpec((tm, d), lambda i, j: (i, 0)))
    res = pl.pallas_call(
        functools.partial(_norm_matmul_kernel, emit_h=emit_h),
        grid=(t // tm, n // tn),
        in_specs=[pl.BlockSpec((tm, d), lambda i, j: (i, 0)),
                  pl.BlockSpec((1, d), lambda i, j: (0, 0)),
                  pl.BlockSpec((None, 1, d), lambda i, j: (i // per_b, 0, sc_blk)),
                  pl.BlockSpec((None, 1, d), lambda i, j: (i // per_b, 0, sh_blk)),
                  pl.BlockSpec((d, tn), lambda i, j: (0, j))],
        out_specs=out_specs,
        out_shape=out_shape,
        scratch_shapes=[pltpu.VMEM((tm, d), BF16)],
        compiler_params=_params(("parallel", "arbitrary")),
        name="norm_matmul",
    )(x2, g.reshape(1, d), mod3, mod3, w)
    return res if emit_h else res[0]


def _conv_kernel(ab_ref, ac_ref, ax_ref, ba_ref, bg_ref, pac_ref, pax_ref, pba_ref, pbg_ref,
                 aw_ref, bw_ref, bb_ref, lng_ref, lnb_ref, ua_ref, ub_ref, ext_ref):
    ts = ab_ref.shape[0]
    keep = (pl.program_id(1) > 0).astype(F32)

    def f32(ref):
        return ref[...].astype(F32)

    ext_ref[0:HALO, :] = f32(pac_ref) * f32(pax_ref) * keep
    ext_ref[HALO:HALO + ts, :] = f32(ac_ref) * f32(ax_ref)
    acc = jnp.zeros((ts, A_WIDTH), F32)
    for k in range(A_CONV):
        off = HALO - (A_CONV - 1) + k
        acc = acc + aw_ref[k:k + 1, :] * ext_ref[off:off + ts, :]
    ua_ref[...] = (f32(ab_ref) * acc).astype(BF16)

    ext_ref[0:HALO, :] = f32(pba_ref) * jax.nn.sigmoid(f32(pbg_ref)) * keep
    ext_ref[HALO:HALO + ts, :] = f32(ba_ref) * jax.nn.sigmoid(f32(bg_ref))
    acc = jnp.zeros((ts, B_WIDTH), F32) + bb_ref[...]
    for k in range(B_CONV):
        off = HALO - (B_CONV - 1) + k
        acc = acc + bw_ref[k:k + 1, :] * ext_ref[off:off + ts, :]
    mu = jnp.mean(acc, axis=-1, keepdims=True)
    cen = acc - mu
    var = jnp.mean(cen * cen, axis=-1, keepdims=True)
    y = cen * lax.rsqrt(var + NORM_EPS) * lng_ref[...] + lnb_ref[...]
    ub_ref[...] = (y * jax.nn.sigmoid(y)).astype(BF16)


def _conv_mixers(proj3, a_conv_w, b_conv_w, b_conv_b, b_ln_g, b_ln_b):
    bsz = proj3.shape[0]
    ts = CONV_T
    c0 = COL_A // 512
    r = ts // HALO

    def cur(k):
        return pl.BlockSpec((None, ts, 512), lambda b, i, k=k: (b, i, c0 + k))

    def prev(k):
        return pl.BlockSpec((None, HALO, 512), lambda b, i, k=k: (b, jnp.maximum(i * r - 1, 0), c0 + k))

    def full(shape):
        return pl.BlockSpec(shape, lambda b, i: (0,) * len(shape))

    return pl.pallas_call(
        _conv_kernel,
        grid=(bsz, SEQ // ts),
        in_specs=[cur(0), cur(1), cur(2), cur(3), cur(4), prev(1), prev(2), prev(3), prev(4),
                  full((A_CONV, A_WIDTH)), full((B_CONV, B_WIDTH)), full((1, B_WIDTH)),
                  full((1, B_WIDTH)), full((1, B_WIDTH))],
        out_specs=[pl.BlockSpec((None, ts, 512), lambda b, i: (b, i, 0))] * 2,
        out_shape=[jax.ShapeDtypeStruct((bsz, SEQ, 512), BF16)] * 2,
        scratch_shapes=[pltpu.VMEM((HALO + ts, 512), F32)],
        compiler_params=_params(("parallel", "arbitrary")),
        name="conv_mixers",
    )(proj3, proj3, proj3, proj3, proj3, proj3, proj3, proj3, proj3,
      a_conv_w, b_conv_w, b_conv_b.reshape(1, -1), b_ln_g.reshape(1, -1), b_ln_b.reshape(1, -1))


def _prep_kernel(cq_ref, dq_ref, dk_ref, dv_ref, kvg_ref, cos_ref, sin_ref,
                 qc_ref, kc_ref, vc_ref, ks_ref, vst_ref, kw_ref, vwt_ref, gt_ref,
                 qd_ref, kd_ref, vdt_ref, kmean_ref):
    cos = cos_ref[...]
    sin = sin_ref[...]
    cos4 = jnp.concatenate([cos] * 4, axis=1)
    sin4 = jnp.concatenate([sin] * 4, axis=1)
    scale = np.float32(HEAD_DIM ** -0.5)
    qc_ref[...] = _rope(cq_ref[...].astype(F32), cos4, sin4) * scale
    qd_ref[...] = _rope(dq_ref[...].astype(F32), cos4, sin4) * scale
    kd = _rope(dk_ref[...].astype(F32), cos4, sin4)
    for h in range(D_HEADS):
        kd_ref[h] = kd[:, h * 64:(h + 1) * 64].astype(BF16)
    kmean_ref[...] = jnp.mean(kd, axis=0, keepdims=True)
    vdt_ref[...] = dv_ref[...].astype(F32).T.astype(BF16)

    def kvg(k):
        return kvg_ref[:, k * 128:(k + 1) * 128].astype(F32)

    kc = _rope(kvg(0), cos, sin)
    vc = kvg(1)
    ks = _rope(kvg(2), cos, sin)
    kw = _rope(kvg(4), cos, sin)
    for g in range(C_KV_HEADS):
        gs = slice(g * 64, (g + 1) * 64)
        kc_ref[g] = kc[:, gs]
        vc_ref[g] = vc[:, gs]
        ks_ref[g] = ks[:, gs].astype(BF16)
        kw_ref[g] = kw[:, gs].astype(BF16)
    vst_ref[...] = kvg(3).T.astype(BF16)
    vwt_ref[...] = kvg(5).T.astype(BF16)
    gt_ref[...] = jax.nn.sigmoid(kvg(6)).T


def _prep(proj3, cos, sin):
    bsz = proj3.shape[0]
    ts = MOBA_BLOCK
    c0 = COL_A // 512

    def col512(k):
        return pl.BlockSpec((None, ts, 512), lambda b, i: (b, i, c0 + k))

    row128 = pl.BlockSpec((None, ts, 128), lambda b, i: (b, i, 0))
    row512 = pl.BlockSpec((None, ts, 512), lambda b, i: (b, i, 0))
    col128t = pl.BlockSpec((None, 128, ts), lambda b, i: (b, 0, i))
    head64 = pl.BlockSpec((None, C_KV_HEADS, ts, 64), lambda b, i: (b, 0, i, 0))
    k64 = jax.ShapeDtypeStruct((bsz, C_KV_HEADS, SEQ, 64), BF16)
    t128 = jax.ShapeDtypeStruct((bsz, 128, SEQ), BF16)
    return pl.pallas_call(
        _prep_kernel,
        grid=(bsz, SEQ // ts),
        in_specs=[col512(5), col512(6), col512(7), col512(8),
                  pl.BlockSpec((None, ts, 1024), lambda b, i: (b, i, COL_KVG // 1024)),
                  row128, row128],
        out_specs=[row512, head64, head64, head64, col128t, head64, col128t, col128t,
                   row512,
                   pl.BlockSpec((None, D_HEADS, ts, 64), lambda b, i: (b, 0, i, 0)),
                   pl.BlockSpec((None, 512, ts), lambda b, i: (b, 0, i)),
                   pl.BlockSpec((None, None, 1, 512), lambda b, i: (b, i, 0, 0))],
        out_shape=[jax.ShapeDtypeStruct((bsz, SEQ, 512), F32),
                   jax.ShapeDtypeStruct((bsz, C_KV_HEADS, SEQ, 64), F32),
                   jax.ShapeDtypeStruct((bsz, C_KV_HEADS, SEQ, 64), F32),
                   k64, t128, k64, t128,
                   jax.ShapeDtypeStruct((bsz, 128, SEQ), F32),
                   jax.ShapeDtypeStruct((bsz, SEQ, 512), F32),
                   jax.ShapeDtypeStruct((bsz, D_HEADS, SEQ, 64), BF16),
                   jax.ShapeDtypeStruct((bsz, 512, SEQ), BF16),
                   jax.ShapeDtypeStruct((bsz, N_MOBA, 1, 512), F32)],
        compiler_params=_params(("parallel", "parallel")),
        name="attn_prep",
    )(proj3, proj3, proj3, proj3, proj3, cos, sin)


def _compress_kernel(kc_ref, vc_ref, pos_ref, w1_ref, w2_ref, kcmp_ref, vcmp_ref):
    half = CMP_STRIDE * HEAD_DIM
    row = lax.broadcasted_iota(jnp.int32, (128, CMP_HIDDEN), 0)
    for which, (src, dst) in enumerate(((kc_ref, kcmp_ref), (vc_ref, vcmp_ref))):
        bias = jnp.dot(pos_ref[which], w1_ref[which], precision=HIGHEST, preferred_element_type=F32)[0:1]
        for g in range(C_KV_HEADS):
            chunks = src[g]
            d1 = jnp.dot(chunks, w1_ref[which, 0:half, :], precision=HIGHEST, preferred_element_type=F32)
            d2 = jnp.dot(chunks, w1_ref[which, half:2 * half, :], precision=HIGHEST, preferred_element_type=F32)
            d2 = jnp.where(row < 127, pltpu.roll(d2, 127, 0), 0.0)
            hid = _gelu(d1 + d2 + bias)
            out = jnp.dot(hid, w2_ref[which], precision=HIGHEST, preferred_element_type=F32)
            dst[g] = out if which == 0 else out.T


def _compress(kc, vc, cmp_pos, cmp_w1, cmp_w2):
    bsz = kc.shape[0]
    kc4 = kc.reshape(bsz, C_KV_HEADS, SEQ // CMP_STRIDE, CMP_STRIDE * HEAD_DIM)
    vc4 = vc.reshape(bsz, C_KV_HEADS, SEQ // CMP_STRIDE, CMP_STRIDE * HEAD_DIM)
    pos8 = jnp.broadcast_to(cmp_pos.reshape(2, 1, CMP_BLOCK * HEAD_DIM), (2, 8, CMP_BLOCK * HEAD_DIM))
    blk = pl.BlockSpec((None, C_KV_HEADS, 128, 1024), lambda b: (b, 0, 0, 0))
    out = pl.BlockSpec((None, C_KV_HEADS, 128, 64), lambda b: (b, 0, 0, 0))
    return pl.pallas_call(
        _compress_kernel,
        grid=(bsz,),
        in_specs=[blk, blk,
                  pl.BlockSpec((2, 8, 2048), lambda b: (0, 0, 0)),
                  pl.BlockSpec((2, 2048, CMP_HIDDEN), lambda b: (0, 0, 0)),
                  pl.BlockSpec((2, CMP_HIDDEN, 64), lambda b: (0, 0, 0))],
        out_specs=[out, pl.BlockSpec((None, C_KV_HEADS, 64, 128), lambda b: (b, 0, 0, 0))],
        out_shape=[jax.ShapeDtypeStruct((bsz, C_KV_HEADS, 128, 64), F32),
                   jax.ShapeDtypeStruct((bsz, C_KV_HEADS, 64, 128), F32)],
        compiler_params=_params(("parallel",)),
        name="nsa_compress",
    )(kc4, vc4, pos8, cmp_w1, cmp_w2)


def _flash_steps_t(heads, m_ref, l_ref, acc_ref, groups):
    scores = [lax.dot_general(k, qb, (((1,), (1,)), ((), ())), preferred_element_type=F32)
              for (qb, k, _, _) in heads]
    probs = []
    for n, (qb, _, _, mask) in enumerate(heads):
        s = scores[n]
        tq = qb.shape[0] // groups
        if mask is not None:
            s = jnp.concatenate([jnp.where(mask, s[:, r * tq:(r + 1) * tq], NEG) for r in range(groups)], axis=1)
        sr = _stat_row(n)
        m_old = m_ref[sr, :]
        m_new = jnp.maximum(m_old, jnp.max(s, axis=0, keepdims=True))
        alpha = jnp.exp(m_old - m_new)
        p = jnp.exp(s - m_new)
        l_ref[sr, :] = alpha * l_ref[sr, :] + jnp.sum(p, axis=0, keepdims=True)
        m_ref[sr, :] = m_new
        probs.append((alpha, p.astype(BF16)))
    for n, (_, _, vt, _) in enumerate(heads):
        alpha, p = probs[n]
        a = slice(n * HEAD_DIM, (n + 1) * HEAD_DIM)
        acc_ref[a, :] = alpha * acc_ref[a, :] + jnp.dot(vt, p, preferred_element_type=F32)


def _stat_row(row):
    return slice(8 * row, 8 * row + 1)


def _flash_init(m_ref, l_ref, acc_ref):
    m_ref[...] = jnp.full(m_ref.shape, NEG, F32)
    l_ref[...] = jnp.zeros(l_ref.shape, F32)
    acc_ref[...] = jnp.zeros(acc_ref.shape, F32)


def _rank_desc_rows(vals):
    n = vals.shape[0]
    row = lax.broadcasted_iota(jnp.int32, vals.shape, 0)
    rank = jnp.zeros(vals.shape, jnp.int32)
    for i in range(n):
        vi = vals[i:i + 1, :]
        ahead = (vi > vals) | ((vi == vals) & (row > i))
        rank = rank + ahead.astype(jnp.int32)
    return rank


def _nsa_kernel(q_ref, kcmp_ref, vcmpt_ref, ks_ref, vst_ref, kw_ref, vwt_ref, gt_ref, ovlt_ref, expt_ref,
                o_ref, qb_ref, sel_ref, m_ref, l_ref, acc_ref, out_ref):
    tq = ATT_T
    i = pl.program_id(1)
    t0 = i * tq
    dstart = pl.multiple_of(t0, tq)
    t_row = t0 + lax.broadcasted_iota(jnp.int32, (1, tq), 1)
    t_rows = jnp.concatenate([t_row] * C_GROUP, axis=1)
    blk = lax.broadcasted_iota(jnp.int32, (128, 1), 0)
    kk = lax.broadcasted_iota(jnp.int32, (tq, tq), 0)
    qq = lax.broadcasted_iota(jnp.int32, (tq, tq), 1)
    causal = kk <= qq
    win_tail = kk > qq

    def gate_row(branch, g):
        base = branch * 8 + g * C_GROUP
        return jnp.concatenate([gt_ref[base + r:base + r + 1, :] for r in range(C_GROUP)], axis=1)

    def hrows(g):
        return slice(g * HEAD_DIM, (g + 1) * HEAD_DIM)

    for g in range(C_KV_HEADS):
        qf = jnp.concatenate([q_ref[:, (g * 4 + r) * 64:(g * 4 + r + 1) * 64] for r in range(C_GROUP)], axis=0)
        qb_ref[g] = qf.astype(BF16)

        s = lax.dot_general(kcmp_ref[g], qf, (((1,), (1,)), ((), ())), precision=HIGHEST,
                            preferred_element_type=F32)
        vis = (blk * CMP_STRIDE + (CMP_BLOCK - 1)) <= t_rows
        sm = jnp.where(vis, s, NEG)
        e = jnp.where(vis, jnp.exp(sm - jnp.max(sm, axis=0, keepdims=True)), 0.0)
        p = e / jnp.maximum(jnp.sum(e, axis=0, keepdims=True), 1e-30)
        o_cmp = jnp.dot(vcmpt_ref[g].astype(BF16), p.astype(BF16), preferred_element_type=F32)
        out_ref[hrows(g), :] = gate_row(0, g) * o_cmp

        psum = p[:, 0:tq] + p[:, tq:2 * tq] + p[:, 2 * tq:3 * tq] + p[:, 3 * tq:4 * tq]
        imp = jnp.dot(ovlt_ref[...], psum, precision=HIGHEST, preferred_element_type=F32)[0:N_SLC]
        b32 = blk[0:N_SLC]
        cur = t_row // SLC_BLOCK
        forced = (b32 == 0) | (b32 == cur) | (b32 == cur - 1)
        imp = jnp.where(forced, jnp.inf, jnp.where(b32 * SLC_BLOCK > t_row, -jnp.inf, imp))
        sel = (_rank_desc_rows(imp) < SLC_TOPN).astype(F32)
        sel_ref[g] = jnp.concatenate([sel, jnp.zeros((128 - N_SLC, tq), F32)], axis=0).astype(BF16)

    def slc_tile(jb, extra):
        start = pl.multiple_of(jb * tq, tq)
        heads = []
        for g in range(C_KV_HEADS):
            hit = jnp.dot(expt_ref[pl.ds(start, tq), :], sel_ref[g], preferred_element_type=F32) > 0.5
            mask = hit if extra is None else hit & extra
            heads.append((qb_ref[g], ks_ref[g, pl.ds(start, tq), :], vst_ref[hrows(g), pl.ds(start, tq)], mask))
        _flash_steps_t(heads, m_ref, l_ref, acc_ref, C_GROUP)

    def win_tile(start, mask):
        heads = [(qb_ref[g], kw_ref[g, pl.ds(start, tq), :], vwt_ref[hrows(g), pl.ds(start, tq)], mask)
                 for g in range(C_KV_HEADS)]
        _flash_steps_t(heads, m_ref, l_ref, acc_ref, C_GROUP)

    def add_branch(branch):
        for g in range(C_KV_HEADS):
            out_ref[hrows(g), :] += gate_row(branch, g) * (acc_ref[hrows(g), :] / l_ref[_stat_row(g), :])

    _flash_init(m_ref, l_ref, acc_ref)
    slc_tile(i, causal)

    def slc_body(jb, carry):
        slc_tile(jb, None)
        return carry

    lax.fori_loop(0, i, slc_body, 0)
    add_branch(1)

    _flash_init(m_ref, l_ref, acc_ref)
    win_tile(dstart, causal)

    @pl.when(i >= 1)
    def _():
        win_tile(pl.multiple_of(t0 - tq, tq), None)

    @pl.when(i >= 2)
    def _():
        win_tile(pl.multiple_of(t0 - 2 * tq, tq), win_tail)

    add_branch(2)

    o_t = jnp.concatenate([out_ref[hrows(g), r * tq:(r + 1) * tq]
                           for g in range(C_KV_HEADS) for r in range(C_GROUP)], axis=0)
    o_ref[...] = o_t.T.astype(BF16)


def _nsa_constants():
    j = np.arange(128)[:, None]
    n = np.arange(128)[None, :]
    ovl_t = ((n * CMP_STRIDE < j * SLC_BLOCK + SLC_BLOCK) & (n * CMP_STRIDE + CMP_BLOCK > j * SLC_BLOCK)
             & (n < N_CMP) & (j < N_SLC)).astype(np.float32)
    expand_t = (np.arange(SEQ)[:, None] // SLC_BLOCK == np.arange(128)[None, :]).astype(np.float32)
    return jnp.asarray(ovl_t), jnp.asarray(expand_t, dtype=BF16)


def _nsa(qc, kcmp, vcmpt, ks, vst, kw, vwt, gt):
    bsz = qc.shape[0]
    tq = ATT_T
    ovl_t, expand_t = _nsa_constants()
    keys = pl.BlockSpec((None, C_KV_HEADS, SEQ, 64), lambda b, i: (b, 0, 0, 0))
    vals = pl.BlockSpec((None, 128, SEQ), lambda b, i: (b, 0, 0))
    rows = C_GROUP * tq
    return pl.pallas_call(
        _nsa_kernel,
        grid=(bsz, SEQ // tq),
        in_specs=[pl.BlockSpec((None, tq, 512), lambda b, i: (b, i, 0)),
                  pl.BlockSpec((None, C_KV_HEADS, 128, 64), lambda b, i: (b, 0, 0, 0)),
                  pl.BlockSpec((None, C_KV_HEADS, 64, 128), lambda b, i: (b, 0, 0, 0)),
                  keys, vals, keys, vals,
                  pl.BlockSpec((None, 128, tq), lambda b, i: (b, 0, i)),
                  pl.BlockSpec((128, 128), lambda b, i: (0, 0)),
                  pl.BlockSpec((SEQ, 128), lambda b, i: (0, 0))],
        out_specs=pl.BlockSpec((None, tq, 512), lambda b, i: (b, i, 0)),
        out_shape=jax.ShapeDtypeStruct((bsz, SEQ, 512), BF16),
        scratch_shapes=[pltpu.VMEM((C_KV_HEADS, rows, 64), BF16), pltpu.VMEM((C_KV_HEADS, 128, tq), BF16),
                        pltpu.VMEM((8 * C_KV_HEADS, rows), F32), pltpu.VMEM((8 * C_KV_HEADS, rows), F32),
                        pltpu.VMEM((C_KV_HEADS * 64, rows), F32), pltpu.VMEM((C_KV_HEADS * 64, rows), F32)],
        compiler_params=_params(("parallel", "arbitrary")),
        name="nsa_attention",
    )(qc, kcmp, vcmpt, ks, vst, kw, vwt, gt, ovl_t, expand_t)


def _moba_kernel(q_ref, k_ref, vt_ref, kmean_ref, o_ref, qb_ref, sel_ref, m_ref, l_ref, acc_ref):
    tq = ATT_T
    i = pl.program_id(1)
    blk = lax.broadcasted_iota(jnp.int32, (N_MOBA, 1), 0)
    kk = lax.broadcasted_iota(jnp.int32, (tq, tq), 0)
    qq = lax.broadcasted_iota(jnp.int32, (tq, tq), 1)
    causal = kk <= qq
    past = blk < i
    for h in range(D_HEADS):
        hs = slice(h * 64, (h + 1) * 64)
        qf = q_ref[:, hs]
        qb_ref[h] = qf.astype(BF16)
        gate = lax.dot_general(kmean_ref[:, hs], qf, (((1,), (1,)), ((), ())), precision=HIGHEST,
                               preferred_element_type=F32)
        gate = jnp.where(past, gate, -jnp.inf)
        sel_ref[h] = (past & (_rank_desc_rows(gate) < MOBA_TOPK)).astype(F32)

    def tile(jb, diag):
        start = pl.multiple_of(jb * tq, tq)
        heads = []
        for h in range(D_HEADS):
            if diag:
                mask = causal
            else:
                mask = jnp.sum(jnp.where(blk == jb, sel_ref[h], 0.0), axis=0, keepdims=True) > 0.5
            heads.append((qb_ref[h], k_ref[h, pl.ds(start, tq), :], vt_ref[h * 64:(h + 1) * 64, pl.ds(start, tq)],
                          mask))
        _flash_steps_t(heads, m_ref, l_ref, acc_ref, 1)

    _flash_init(m_ref, l_ref, acc_ref)
    tile(i, True)

    def body(jb, carry):
        tile(jb, False)
        return carry

    lax.fori_loop(0, i, body, 0)
    o_t = jnp.concatenate([acc_ref[h * 64:(h + 1) * 64, :] / l_ref[_stat_row(h), :] for h in range(D_HEADS)], axis=0)
    o_ref[...] = o_t.T.astype(BF16)


def _moba(qd, kd, vdt, kmean):
    bsz = qd.shape[0]
    tq = ATT_T
    return pl.pallas_call(
        _moba_kernel,
        grid=(bsz, SEQ // tq),
        in_specs=[pl.BlockSpec((None, tq, 512), lambda b, i: (b, i, 0)),
                  pl.BlockSpec((None, D_HEADS, SEQ, 64), lambda b, i: (b, 0, 0, 0)),
                  pl.BlockSpec((None, 512, SEQ), lambda b, i: (b, 0, 0)),
                  pl.BlockSpec((None, N_MOBA, 512), lambda b, i: (b, 0, 0))],
        out_specs=pl.BlockSpec((None, tq, 512), lambda b, i: (b, i, 0)),
        out_shape=jax.ShapeDtypeStruct((bsz, SEQ, 512), BF16),
        scratch_shapes=[pltpu.VMEM((D_HEADS, tq, 64), BF16), pltpu.VMEM((D_HEADS, N_MOBA, tq), F32),
                        pltpu.VMEM((8 * D_HEADS, tq), F32), pltpu.VMEM((8 * D_HEADS, tq), F32),
                        pltpu.VMEM((512, tq), F32)],
        compiler_params=_params(("parallel", "arbitrary")),
        name="moba_attention",
    )(qd, kd, vdt, kmean.reshape(bsz, N_MOBA, 512))


def _merge_kernel(ua_ref, ub_ref, oc_ref, od_ref, mg_ref, x_ref, gt_ref,
                  wa_ref, wb_ref, wc_ref, wd_ref, wo_ref, o_ref):
    d = D_MODEL
    merged = jnp.zeros(x_ref.shape, F32)
    for k, (u_ref, w_ref) in enumerate(((ua_ref, wa_ref), (ub_ref, wb_ref), (oc_ref, wc_ref), (od_ref, wd_ref))):
        y = jnp.dot(u_ref[...], w_ref[...], preferred_element_type=F32)
        merged = merged + jax.nn.sigmoid(mg_ref[:, k * d:(k + 1) * d].astype(F32)) * y
    o_ref[...] = x_ref[...] + gt_ref[...] * jnp.dot(merged.astype(BF16), wo_ref[...], preferred_element_type=F32)


def _merge(ua, ub, oc, od, proj2, x2, mod3, wa, wb, wc, wd, wo, *, tm=512):
    t, d = x2.shape
    per_b = SEQ // tm
    act = pl.BlockSpec((tm, 512), lambda i: (i, 0))
    wspec = pl.BlockSpec((512, d), lambda i: (0, 0))
    return pl.pallas_call(
        _merge_kernel,
        grid=(t // tm,),
        in_specs=[act, act, act, act,
                  pl.BlockSpec((tm, 4 * d), lambda i: (i, COL_MERGE // (4 * d))),
                  pl.BlockSpec((tm, d), lambda i: (i, 0)),
                  pl.BlockSpec((None, 1, d), lambda i: (i // per_b, 0, 2)),
                  wspec, wspec, wspec, wspec,
                  pl.BlockSpec((d, d), lambda i: (0, 0))],
        out_specs=pl.BlockSpec((tm, d), lambda i: (i, 0)),
        out_shape=jax.ShapeDtypeStruct((t, d), F32),
        compiler_params=_params(("parallel",)),
        name="merge_out",
    )(ua, ub, oc, od, proj2, x2, mod3, wa, wb, wc, wd, wo)


def _pop_max(work, idx):
    m = jnp.max(work, axis=0, keepdims=True)
    first = jnp.min(jnp.where(work == m, idx, work.shape[0]), axis=0, keepdims=True)
    return m, idx == first


_PEER_PAIRS = [(i, j) for i in range(PEER_TOPK) for j in range(PEER_TOPK) if (i + 1) * (j + 1) <= PEER_TOPK]
_PEER_PAIR_ROWS = 56
NOT_TOP = 99.0


def _peer_sel_kernel(q_ref, keys_ref, grp_ref, n1_ref, e1_ref, r2_ref, e2_ref, st_ref):
    tt = q_ref.shape[0]
    half = PEER_QDIM // 2
    for hp in range(2 * PEER_HEADS):
        qh = q_ref[:, hp * half:(hp + 1) * half]
        qn = qh * lax.rsqrt(jnp.mean(qh * qh, axis=-1, keepdims=True) + NORM_EPS)
        st_ref[hp] = lax.dot_general(keys_ref[hp], qn, (((1,), (1,)), ((), ())), precision=HIGHEST,
                                     preferred_element_type=F32)

    idx = lax.broadcasted_iota(jnp.int32, (PEER_NKEYS, 128), 0)
    pidx = lax.broadcasted_iota(jnp.int32, (_PEER_PAIR_ROWS, 128), 0)
    pad_rows = jnp.full((_PEER_PAIR_ROWS - len(_PEER_PAIRS), 128), -jnp.inf, F32)

    def pop16(work, index, tie_safe):
        vals, order = [], jnp.full(work.shape, NOT_TOP, F32)
        for it in range(PEER_TOPK):
            if tie_safe:
                m, hit = _pop_max(work, index)
            else:
                m = jnp.max(work, axis=0, keepdims=True)
                hit = work == m
            work = jnp.where(hit, -jnp.inf, work)
            order = jnp.where(hit, float(it), order)
            vals.append(m)
        removed = jnp.sum(jnp.where(work == -jnp.inf, 1.0, 0.0), axis=0, keepdims=True)
        return vals, order, removed

    def select(cs, tie_safe):
        wrong = jnp.zeros((1, 128), F32)
        n_pad = float(_PEER_PAIR_ROWS - len(_PEER_PAIRS))
        for h in range(PEER_HEADS):
            scores, tops, ranks = [], [], []
            for p in range(2):
                st = st_ref[2 * h + p, :, cs]
                vals, rank, removed = pop16(st, idx, tie_safe)
                wrong = wrong + jnp.abs(removed - float(PEER_TOPK))
                scores.append(st)
                tops.append(vals)
                ranks.append(rank)
            cand = jnp.concatenate([tops[0][i] + tops[1][j] for (i, j) in _PEER_PAIRS] + [pad_rows], axis=0)
            best, order, removed = pop16(cand, pidx, tie_safe)
            wrong = wrong + jnp.abs(removed - (float(PEER_TOPK) + n_pad))
            picked = jnp.where(order < float(PEER_TOPK), 1.0, 0.0)
            z = jnp.ones_like(best[0])
            for k in range(1, PEER_TOPK):
                z = z + jnp.exp(best[k] - best[0])
            count = jnp.dot(grp_ref[...], picked.astype(BF16), preferred_element_type=F32)
            n1 = jnp.zeros(scores[0].shape, F32)
            for i in range(PEER_TOPK):
                n1 = jnp.where(ranks[0] == float(i), count[i:i + 1], n1)
            n1_ref[h, :, cs] = n1
            r2_ref[h, :, cs] = ranks[1]
            e1_ref[h, :, cs] = jnp.exp(scores[0] - tops[0][0]) / z
            e2_ref[h, :, cs] = jnp.exp(scores[1] - tops[1][0])
        return wrong

    def token_chunk(c, carry):
        cs = pl.ds(pl.multiple_of(c * 128, 128), 128)
        wrong = select(cs, tie_safe=False)

        @pl.when(jnp.max(wrong) > 0.0)
        def _():
            select(cs, tie_safe=True)

        return carry

    lax.fori_loop(0, tt // 128, token_chunk, 0)


def _peer_select(q2, subkeys, *, tt=512):
    t = q2.shape[0]
    grp = np.zeros((PEER_TOPK, _PEER_PAIR_ROWS), np.float32)
    for row, (i, _) in enumerate(_PEER_PAIRS):
        grp[i, row] = 1.0
    big = pl.BlockSpec((PEER_HEADS, PEER_NKEYS, tt), lambda i: (0, 0, i))
    bshape = jax.ShapeDtypeStruct((PEER_HEADS, PEER_NKEYS, t), F32)
    half = PEER_QDIM // 2
    return pl.pallas_call(
        _peer_sel_kernel,
        grid=(t // tt,),
        in_specs=[pl.BlockSpec((tt, PEER_HEADS * PEER_QDIM), lambda i: (i, 0)),
                  pl.BlockSpec((2 * PEER_HEADS, PEER_NKEYS, half), lambda i: (0, 0, 0)),
                  pl.BlockSpec((PEER_TOPK, _PEER_PAIR_ROWS), lambda i: (0, 0))],
        out_specs=[big, big, big, big],
        out_shape=[bshape, bshape, bshape, bshape],
        scratch_shapes=[pltpu.VMEM((2 * PEER_HEADS, PEER_NKEYS, tt), F32)],
        compiler_params=_params(("parallel",)),
        name="peer_select",
    )(q2, subkeys.reshape(2 * PEER_HEADS, PEER_NKEYS, half), jnp.asarray(grp, dtype=BF16))


PEER_SLAB = 256

def _peer_main_kernel(h_ref, u_ref, vt_ref, n1_ref, e1_ref, r2_ref, e2_ref, x_ref, gt_ref, fg_ref,
                      o_ref, acc_ref, act_ref, p_ref, ht_ref, *, final):
    j = pl.program_id(1)
    te, tt = act_ref.shape
    na = te // PEER_NKEYS

    @pl.when(j == 0)
    def _():
        acc_ref[...] = jnp.zeros(acc_ref.shape, F32)
        ht_ref[...] = h_ref[...].astype(F32).T.astype(BF16)

    a0 = pl.multiple_of(j * na, na)
    nslab = te // PEER_SLAB
    parts = []

    def act(s):
        ss = slice(s * PEER_SLAB, (s + 1) * PEER_SLAB)
        act_ref[ss, :] = jnp.dot(u_ref[ss, :], ht_ref[...], preferred_element_type=F32)

    def val(s):
        ss = slice(s * PEER_SLAB, (s + 1) * PEER_SLAB)
        parts.append(jnp.dot(vt_ref[:, ss], p_ref[ss, :], preferred_element_type=F32))

    def mask(al):
        rs = slice(al * PEER_NKEYS, (al + 1) * PEER_NKEYS)
        for c in range(tt // 128):
            cs = slice(c * 128, (c + 1) * 128)
            w = jnp.zeros((PEER_NKEYS, 128), F32)
            for hh in range(PEER_HEADS):
                n1 = n1_ref[hh, pl.ds(a0, na), cs][al:al + 1]
                e1 = e1_ref[hh, pl.ds(a0, na), cs][al:al + 1]
                w = w + jnp.where(r2_ref[hh, :, cs] < n1, e1 * e2_ref[hh, :, cs], 0.0)
            p_ref[rs, cs] = (w * _gelu(act_ref[rs, cs])).astype(BF16)

    per = PEER_SLAB // PEER_NKEYS
    act(0)
    act(1)
    for s in range(nslab):
        for k in range(per):
            mask(s * per + k)
            if k == 0 and s + 2 < nslab:
                act(s + 2)
        val(s)
    acc_ref[...] += functools.reduce(lambda x, y: x + y, parts)

    @pl.when(j == pl.num_programs(1) - 1)
    def _():
        y = x_ref[...] + gt_ref[...] * acc_ref[...].T
        if final:
            y = y * lax.rsqrt(jnp.mean(y * y, axis=-1, keepdims=True) + NORM_EPS) * fg_ref[...]
        o_ref[...] = y


def _peer_main(h2, u_bf, vt_bf, n1, e1, r2, e2, x2, mod3, final_g, *, final, tt=512, te=2048):
    t, d = x2.shape
    assert (te // PEER_NKEYS) % 8 == 0 and te % PEER_SLAB == 0
    per_b = SEQ // tt
    big = pl.BlockSpec((PEER_HEADS, PEER_NKEYS, tt), lambda i, j: (0, 0, i))
    return pl.pallas_call(
        functools.partial(_peer_main_kernel, final=final),
        grid=(t // tt, PEER_EXPERTS // te),
        in_specs=[pl.BlockSpec((tt, d), lambda i, j: (i, 0)),
                  pl.BlockSpec((te, d), lambda i, j: (j, 0)),
                  pl.BlockSpec((d, te), lambda i, j: (0, j)),
                  big, big, big, big,
                  pl.BlockSpec((tt, d), lambda i, j: (i, 0)),
                  pl.BlockSpec((None, 1, d), lambda i, j: (i // per_b, 0, 5)),
                  pl.BlockSpec((1, d), lambda i, j: (0, 0))],
        out_specs=pl.BlockSpec((tt, d), lambda i, j: (i, 0)),
        out_shape=jax.ShapeDtypeStruct((t, d), F32),
        scratch_shapes=[pltpu.VMEM((d, tt), F32), pltpu.VMEM((te, tt), F32), pltpu.VMEM((te, tt), BF16),
                        pltpu.VMEM((d, tt), BF16)],
        compiler_params=_params(("parallel", "arbitrary")),
        name="peer_experts",
    )(h2, u_bf, vt_bf, n1, e1, r2, e2, x2, mod3, final_g.reshape(1, d))


def _reorder_w_in(w):
    pad = jnp.zeros((w.shape[0], 1024 - 792), w.dtype)
    return jnp.concatenate([w[:, 5400:9496], w[:, 3072:3864], pad, w[:, 0:3072], w[:, 3864:5400]], axis=1)


def kernel(x, c, positions, mod_w, mod_b, norm_mix_g, norm_ffn_g, w_in, a_conv_w, a_out, b_conv_w, b_conv_b, b_ln_g, b_ln_b, b_out, c_cmp_pos, c_cmp_w1, c_cmp_w2, c_out, d_out, w_o, peer_wq, peer_subkeys, peer_u, peer_v, final_norm_g):
    bsz, s, d = x.shape
    assert s == SEQ and d == D_MODEL
    depth = mod_w.shape[0]
    t = bsz * s
    cos, sin = _rope_tables(positions)
    mod = _modulation(c, mod_w, mod_b)
    x2 = x.reshape(t, d)
    for l in range(depth):
        mod3 = mod[l].reshape(bsz, 1, 6 * d)
        proj2 = _norm_matmul(x2, norm_mix_g[l], mod3, 0, 1, _reorder_w_in(w_in[l]).astype(BF16), out_dtype=BF16)
        proj3 = proj2.reshape(bsz, s, PROJ_COLS)
        ua, ub = _conv_mixers(proj3, a_conv_w[l], b_conv_w[l], b_conv_b[l], b_ln_g[l], b_ln_b[l])
        qc, kc, vc, ks, vst, kw, vwt, gt, qd, kd, vdt, kmean = _prep(proj3, cos, sin)
        kcmp, vcmpt = _compress(kc, vc, c_cmp_pos[l], c_cmp_w1[l], c_cmp_w2[l])
        oc = _nsa(qc, kcmp, vcmpt, ks, vst, kw, vwt, gt)
        od = _moba(qd, kd, vdt, kmean)
        x2 = _merge(ua.reshape(t, 512), ub.reshape(t, 512), oc.reshape(t, 512), od.reshape(t, 512),
                    proj2, x2, mod3, a_out[l].astype(BF16), b_out[l].astype(BF16), c_out[l].astype(BF16),
                    d_out[l].astype(BF16), w_o[l].astype(BF16))
        q2, h2 = _norm_matmul(x2, norm_ffn_g[l], mod3, 3, 4, peer_wq[l].astype(BF16), emit_h=True)
        n1, e1, r2, e2 = _peer_select(q2, peer_subkeys[l])
        x2 = _peer_main(h2, peer_u[l].astype(BF16), peer_v[l].T.astype(BF16), n1, e1, r2, e2,
                        x2, mod3, final_norm_g, final=(l == depth - 1))
    return x2.reshape(bsz, s, d)
```

```python
import functools

import numpy as np
import jax
import jax.numpy as jnp
from jax import lax
from jax.experimental import pallas as pl
from jax.experimental.pallas import tpu as pltpu

F32 = jnp.float32
BF16 = jnp.bfloat16
HIGHEST = lax.Precision.HIGHEST

D_MODEL = 1024
SEQ = 2048
HEAD_DIM = 64
ROPE_THETA = 10000.0
NORM_EPS = 1e-6
A_WIDTH = 512
A_CONV = 3
B_WIDTH = 512
B_CONV = 31
C_HEADS = 8
C_KV_HEADS = 2
C_GROUP = 4
CMP_BLOCK = 32
CMP_STRIDE = 16
CMP_HIDDEN = 128
N_CMP = (SEQ - CMP_BLOCK) // CMP_STRIDE + 1
SLC_BLOCK = 64
SLC_TOPN = 16
N_SLC = SEQ // SLC_BLOCK
WIN = 512
D_HEADS = 8
MOBA_BLOCK = 256
MOBA_TOPK = 3
N_MOBA = SEQ // MOBA_BLOCK
PEER_HEADS = 8
PEER_NKEYS = 128
PEER_EXPERTS = PEER_NKEYS * PEER_NKEYS
PEER_QDIM = 256
PEER_TOPK = 16

PROJ_COLS = 9728
COL_MERGE = 0
COL_KVG = 4096
COL_GATE = COL_KVG + 768
COL_A = 5120

VMEM_LIMIT = 56 * 1024 * 1024
NEG = -1e30

ATT_T = 256
CONV_T = 256
HALO = 32


def _params(sem, flags=None):
    return pltpu.CompilerParams(dimension_semantics=sem, vmem_limit_bytes=VMEM_LIMIT, flags=flags)


def _gelu(x):
    return 0.5 * x * (1.0 + lax.erf(x * np.float32(np.sqrt(0.5))))


def _rope_table_kernel(pos_ref, inv_ref, sign_ref, cos_ref, sin_ref):
    ang = pos_ref[...] * inv_ref[...]
    cos_ref[...] = jnp.cos(ang)
    sin_ref[...] = jnp.sin(ang) * sign_ref[...]


def _rope_tables(positions):
    bsz, s = positions.shape
    inv = 1.0 / (ROPE_THETA ** (jnp.arange(0, HEAD_DIM, 2, dtype=F32) / HEAD_DIM))
    inv128 = jnp.tile(inv, 4)[None, :]
    sign = jnp.tile(jnp.concatenate([-jnp.ones(32, F32), jnp.ones(32, F32)]), 2)[None, :]
    pos = positions.astype(F32).reshape(bsz * s, 1)
    t = bsz * s
    cos, sin = pl.pallas_call(
        _rope_table_kernel,
        grid=(t // SEQ,),
        in_specs=[pl.BlockSpec((SEQ, 1), lambda i: (i, 0)),
                  pl.BlockSpec((1, 128), lambda i: (0, 0)),
                  pl.BlockSpec((1, 128), lambda i: (0, 0))],
        out_specs=[pl.BlockSpec((SEQ, 128), lambda i: (i, 0))] * 2,
        out_shape=[jax.ShapeDtypeStruct((t, 128), F32)] * 2,
        compiler_params=_params(("parallel",)),
        name="rope_tables",
    )(pos, inv128, sign)
    return cos.reshape(bsz, s, 128), sin.reshape(bsz, s, 128)


def _rope(x, cos, sin):
    w = x.shape[-1]
    lane = lax.broadcasted_iota(jnp.int32, x.shape, 1)
    swapped = jnp.where(lane % 64 < 32, pltpu.roll(x, w - 32, 1), pltpu.roll(x, 32, 1))
    return x * cos + swapped * sin


def _mod_kernel(c_ref, w_ref, b_ref, o_ref):
    c = c_ref[...]
    cond = c * jax.nn.sigmoid(c)
    o_ref[...] = jnp.dot(cond, w_ref[...], precision=HIGHEST, preferred_element_type=F32) + b_ref[...]


def _modulation(c, mod_w, mod_b):
    nl, d, n = mod_w.shape
    bsz = c.shape[0]
    tn = 1536
    return pl.pallas_call(
        _mod_kernel,
        grid=(nl, n // tn),
        in_specs=[pl.BlockSpec((bsz, d), lambda l, j: (0, 0)),
                  pl.BlockSpec((None, d, tn), lambda l, j: (l, 0, j)),
                  pl.BlockSpec((None, 1, tn), lambda l, j: (l, 0, j))],
        out_specs=pl.BlockSpec((None, bsz, tn), lambda l, j: (l, 0, j)),
        out_shape=jax.ShapeDtypeStruct((nl, bsz, n), F32),
        compiler_params=_params(("parallel", "parallel")),
        name="adaln_mod",
    )(c, mod_w, mod_b.reshape(nl, 1, n))


def _norm_matmul_kernel(x_ref, g_ref, sc_ref, sh_ref, w_ref, o_ref, *rest, emit_h):
    h_scr = rest[-1]

    @pl.when(pl.program_id(1) == 0)
    def _():
        x = x_ref[...]
        y = x * lax.rsqrt(jnp.mean(x * x, axis=-1, keepdims=True) + NORM_EPS)
        h = (y * g_ref[...]) * (1.0 + sc_ref[...]) + sh_ref[...]
        h_scr[...] = h.astype(BF16)
        if emit_h:
            rest[0][...] = h.astype(BF16)

    o_ref[...] = jnp.dot(h_scr[...], w_ref[...], preferred_element_type=F32).astype(o_ref.dtype)


def _norm_matmul(x2, g, mod3, sh_blk, sc_blk, w, *, out_dtype=F32, emit_h=False, tm=1024, tn=512):
    t, d = x2.shape
    n = w.shape[1]
    per_b = SEQ // tm
    out_shape = [jax.ShapeDtypeStruct((t, n), out_dtype)]
    out_specs = [pl.BlockSpec((tm, tn), lambda i, j: (i, j))]
    if emit_h:
        out_shape.append(jax.ShapeDtypeStruct((t, d), BF16))
        out_specs.append(pl.BlockSpec((tm, d), lambda i, j: (i, 0)))
    res = pl.pallas_call(
        functools.partial(_norm_matmul_kernel, emit_h=emit_h),
        grid=(t // tm, n // tn),
        in_specs=[pl.BlockSpec((tm, d), lambda i, j: (i, 0)),
                  pl.BlockSpec((1, d), lambda i, j: (0, 0)),
                  pl.BlockSpec((None, 1, d), lambda i, j: (i // per_b, 0, sc_blk)),
                  pl.BlockSpec((None, 1, d), lambda i, j: (i // per_b, 0, sh_blk)),
                  pl.BlockSpec((d, tn), lambda i, j: (0, j))],
        out_specs=out_specs,
        out_shape=out_shape,
        scratch_shapes=[pltpu.VMEM((tm, d), BF16)],
        compiler_params=_params(("parallel", "arbitrary")),
        name="norm_matmul",
    )(x2, g.reshape(1, d), mod3, mod3, w)
    return res if emit_h else res[0]


def _conv_kernel(ab_ref, ac_ref, ax_ref, ba_ref, bg_ref, pac_ref, pax_ref, pba_ref, pbg_ref,
                 aw_ref, bw_ref, bb_ref, lng_ref, lnb_ref, ua_ref, ub_ref, ext_ref):
    ts = ab_ref.shape[0]
    keep = (pl.program_id(1) > 0).astype(F32)

    def f32(ref):
        return ref[...].astype(F32)

    ext_ref[0:HALO, :] = f32(pac_ref) * f32(pax_ref) * keep
    ext_ref[HALO:HALO + ts, :] = f32(ac_ref) * f32(ax_ref)
    acc = jnp.zeros((ts, A_WIDTH), F32)
    for k in range(A_CONV):
        off = HALO - (A_CONV - 1) + k
        acc = acc + aw_ref[k:k + 1, :] * ext_ref[off:off + ts, :]
    ua_ref[...] = (f32(ab_ref) * acc).astype(BF16)

    ext_ref[0:HALO, :] = f32(pba_ref) * jax.nn.sigmoid(f32(pbg_ref)) * keep
    ext_ref[HALO:HALO + ts, :] = f32(ba_ref) * jax.nn.sigmoid(f32(bg_ref))
    acc = jnp.zeros((ts, B_WIDTH), F32) + bb_ref[...]
    for k in range(B_CONV):
        off = HALO - (B_CONV - 1) + k
        acc = acc + bw_ref[k:k + 1, :] * ext_ref[off:off + ts, :]
    mu = jnp.mean(acc, axis=-1, keepdims=True)
    cen = acc - mu
    var = jnp.mean(cen * cen, axis=-1, keepdims=True)
    y = cen * lax.rsqrt(var + NORM_EPS) * lng_ref[...] + lnb_ref[...]
    ub_ref[...] = (y * jax.nn.sigmoid(y)).astype(BF16)


def _conv_mixers(proj3, a_conv_w, b_conv_w, b_conv_b, b_ln_g, b_ln_b):
    bsz = proj3.shape[0]
    ts = CONV_T
    c0 = COL_A // 512
    r = ts // HALO

    def cur(k):
        return pl.BlockSpec((None, ts, 512), lambda b, i, k=k: (b, i, c0 + k))

    def prev(k):
        return pl.BlockSpec((None, HALO, 512), lambda b, i, k=k: (b, jnp.maximum(i * r - 1, 0), c0 + k))

    def full(shape):
        return pl.BlockSpec(shape, lambda b, i: (0,) * len(shape))

    return pl.pallas_call(
        _conv_kernel,
        grid=(bsz, SEQ // ts),
        in_specs=[cur(0), cur(1), cur(2), cur(3), cur(4), prev(1), prev(2), prev(3), prev(4),
                  full((A_CONV, A_WIDTH)), full((B_CONV, B_WIDTH)), full((1, B_WIDTH)),
                  full((1, B_WIDTH)), full((1, B_WIDTH))],
        out_specs=[pl.BlockSpec((None, ts, 512), lambda b, i: (b, i, 0))] * 2,
        out_shape=[jax.ShapeDtypeStruct((bsz, SEQ, 512), BF16)] * 2,
        scratch_shapes=[pltpu.VMEM((HALO + ts, 512), F32)],
        compiler_params=_params(("parallel", "arbitrary")),
        name="conv_mixers",
    )(proj3, proj3, proj3, proj3, proj3, proj3, proj3, proj3, proj3,
      a_conv_w, b_conv_w, b_conv_b.reshape(1, -1), b_ln_g.reshape(1, -1), b_ln_b.reshape(1, -1))


VT_ROWS = 80


def _values_t(v, n_heads):
    rows = v.shape[0]
    vt = v.T
    tail = (lax.broadcasted_iota(jnp.int32, (VT_ROWS - HEAD_DIM, rows), 0) == 0).astype(F32)
    blocks = []
    for h in range(n_heads):
        blocks += [vt[h * HEAD_DIM:(h + 1) * HEAD_DIM], tail]
    return jnp.concatenate(blocks, axis=0).astype(BF16)


def _prep_kernel(cq_ref, dq_ref, dk_ref, dv_ref, kvg_ref, cos_ref, sin_ref,
                 qc_ref, kc_ref, vc_ref, ks_ref, vst_ref, kw_ref, vwt_ref, gt_ref,
                 qd_ref, kd_ref, vdt_ref, kmean_ref):
    cos = cos_ref[...]
    sin = sin_ref[...]
    cos4 = jnp.concatenate([cos] * 4, axis=1)
    sin4 = jnp.concatenate([sin] * 4, axis=1)
    scale = np.float32(HEAD_DIM ** -0.5 * np.log2(np.e))
    qc_ref[...] = _rope(cq_ref[...].astype(F32), cos4, sin4) * scale
    qd_ref[...] = _rope(dq_ref[...].astype(F32), cos4, sin4) * scale
    kd = _rope(dk_ref[...].astype(F32), cos4, sin4)
    for h in range(D_HEADS):
        kd_ref[h] = kd[:, h * 64:(h + 1) * 64].astype(BF16)
    kmean_ref[...] = jnp.mean(kd, axis=0, keepdims=True)
    vdt_ref[...] = _values_t(dv_ref[...].astype(F32), D_HEADS)

    def kvg(k):
        return kvg_ref[:, k * 128:(k + 1) * 128].astype(F32)

    kc = _rope(kvg(0), cos, sin)
    vc = kvg(1)
    ks = _rope(kvg(2), cos, sin)
    kw = _rope(kvg(4), cos, sin)
    for g in range(C_KV_HEADS):
        gs = slice(g * 64, (g + 1) * 64)
        kc_ref[g] = kc[:, gs]
        vc_ref[g] = vc[:, gs]
        ks_ref[g] = ks[:, gs].astype(BF16)
        kw_ref[g] = kw[:, gs].astype(BF16)
    vst_ref[...] = _values_t(kvg(3), C_KV_HEADS)
    vwt_ref[...] = _values_t(kvg(5), C_KV_HEADS)
    gt_ref[...] = jax.nn.sigmoid(kvg(6)).T


def _prep(proj3, cos, sin):
    bsz = proj3.shape[0]
    ts = MOBA_BLOCK
    c0 = COL_A // 512

    def col512(k):
        return pl.BlockSpec((None, ts, 512), lambda b, i: (b, i, c0 + k))

    row128 = pl.BlockSpec((None, ts, 128), lambda b, i: (b, i, 0))
    row512 = pl.BlockSpec((None, ts, 512), lambda b, i: (b, i, 0))
    col128t = pl.BlockSpec((None, 128, ts), lambda b, i: (b, 0, i))
    val2t = pl.BlockSpec((None, C_KV_HEADS * VT_ROWS, ts), lambda b, i: (b, 0, i))
    head64 = pl.BlockSpec((None, C_KV_HEADS, ts, 64), lambda b, i: (b, 0, i, 0))
    k64 = jax.ShapeDtypeStruct((bsz, C_KV_HEADS, SEQ, 64), BF16)
    t128 = jax.ShapeDtypeStruct((bsz, C_KV_HEADS * VT_ROWS, SEQ), BF16)
    return pl.pallas_call(
        _prep_kernel,
        grid=(bsz, SEQ // ts),
        in_specs=[col512(5), col512(6), col512(7), col512(8),
                  pl.BlockSpec((None, ts, 1024), lambda b, i: (b, i, COL_KVG // 1024)),
                  row128, row128],
        out_specs=[row512, head64, head64, head64, val2t, head64, val2t, col128t,
                   row512,
                   pl.BlockSpec((None, D_HEADS, ts, 64), lambda b, i: (b, 0, i, 0)),
                   pl.BlockSpec((None, D_HEADS * VT_ROWS, ts), lambda b, i: (b, 0, i)),
                   pl.BlockSpec((None, None, 1, 512), lambda b, i: (b, i, 0, 0))],
        out_shape=[jax.ShapeDtypeStruct((bsz, SEQ, 512), F32),
                   jax.ShapeDtypeStruct((bsz, C_KV_HEADS, SEQ, 64), F32),
                   jax.ShapeDtypeStruct((bsz, C_KV_HEADS, SEQ, 64), F32),
                   k64, t128, k64, t128,
                   jax.ShapeDtypeStruct((bsz, 128, SEQ), F32),
                   jax.ShapeDtypeStruct((bsz, SEQ, 512), F32),
                   jax.ShapeDtypeStruct((bsz, D_HEADS, SEQ, 64), BF16),
                   jax.ShapeDtypeStruct((bsz, D_HEADS * VT_ROWS, SEQ), BF16),
                   jax.ShapeDtypeStruct((bsz, N_MOBA, 1, 512), F32)],
        compiler_params=_params(("parallel", "parallel")),
        name="attn_prep",
    )(proj3, proj3, proj3, proj3, proj3, cos, sin)


def _compress_kernel(kc_ref, vc_ref, pos_ref, w1_ref, w2_ref, kcmp_ref, vcmp_ref):
    half = CMP_STRIDE * HEAD_DIM
    row = lax.broadcasted_iota(jnp.int32, (128, CMP_HIDDEN), 0)
    for which, (src, dst) in enumerate(((kc_ref, kcmp_ref), (vc_ref, vcmp_ref))):
        bias = jnp.dot(pos_ref[which], w1_ref[which], precision=HIGHEST, preferred_element_type=F32)[0:1]
        for g in range(C_KV_HEADS):
            chunks = src[g]
            d1 = jnp.dot(chunks, w1_ref[which, 0:half, :], precision=HIGHEST, preferred_element_type=F32)
            d2 = jnp.dot(chunks, w1_ref[which, half:2 * half, :], precision=HIGHEST, preferred_element_type=F32)
            d2 = jnp.where(row < 127, pltpu.roll(d2, 127, 0), 0.0)
            hid = _gelu(d1 + d2 + bias)
            out = jnp.dot(hid, w2_ref[which], precision=HIGHEST, preferred_element_type=F32)
            dst[g] = out if which == 0 else out.T


def _compress(kc, vc, cmp_pos, cmp_w1, cmp_w2):
    bsz = kc.shape[0]
    kc4 = kc.reshape(bsz, C_KV_HEADS, SEQ // CMP_STRIDE, CMP_STRIDE * HEAD_DIM)
    vc4 = vc.reshape(bsz, C_KV_HEADS, SEQ // CMP_STRIDE, CMP_STRIDE * HEAD_DIM)
    pos8 = jnp.broadcast_to(cmp_pos.reshape(2, 1, CMP_BLOCK * HEAD_DIM), (2, 8, CMP_BLOCK * HEAD_DIM))
    blk = pl.BlockSpec((None, C_KV_HEADS, 128, 1024), lambda b: (b, 0, 0, 0))
    out = pl.BlockSpec((None, C_KV_HEADS, 128, 64), lambda b: (b, 0, 0, 0))
    return pl.pallas_call(
        _compress_kernel,
        grid=(bsz,),
        in_specs=[blk, blk,
                  pl.BlockSpec((2, 8, 2048), lambda b: (0, 0, 0)),
                  pl.BlockSpec((2, 2048, CMP_HIDDEN), lambda b: (0, 0, 0)),
                  pl.BlockSpec((2, CMP_HIDDEN, 64), lambda b: (0, 0, 0))],
        out_specs=[out, pl.BlockSpec((None, C_KV_HEADS, 64, 128), lambda b: (b, 0, 0, 0))],
        out_shape=[jax.ShapeDtypeStruct((bsz, C_KV_HEADS, 128, 64), F32),
                   jax.ShapeDtypeStruct((bsz, C_KV_HEADS, 64, 128), F32)],
        compiler_params=_params(("parallel",)),
        name="nsa_compress",
    )(kc4, vc4, pos8, cmp_w1, cmp_w2)


def _flash_steps_t(heads, m_ref, acc_ref, groups):
    scores = [lax.dot_general(k, qb, (((1,), (1,)), ((), ())), preferred_element_type=F32)
              for (qb, k, _, _) in heads]
    probs = []
    for n, (qb, _, _, mask) in enumerate(heads):
        s = scores[n]
        tq = qb.shape[0] // groups
        if mask is not None:
            s = jnp.concatenate([jnp.where(mask, s[:, r * tq:(r + 1) * tq], NEG) for r in range(groups)], axis=1)
        sr = _stat_row(n)
        m_old = m_ref[sr, :]
        m_new = jnp.maximum(m_old, jnp.max(s, axis=0, keepdims=True))
        m_ref[sr, :] = m_new
        probs.append((jnp.exp2(m_old - m_new), jnp.exp2(s - m_new).astype(BF16)))
    for n, (_, _, vt, _) in enumerate(heads):
        alpha, p = probs[n]
        a = slice(n * VT_ROWS, (n + 1) * VT_ROWS)
        acc_ref[a, :] = alpha * acc_ref[a, :] + jnp.dot(vt, p, preferred_element_type=F32)


def _flash_out(acc_ref, n):
    return acc_ref[n * VT_ROWS:n * VT_ROWS + HEAD_DIM, :] / acc_ref[n * VT_ROWS + HEAD_DIM:n * VT_ROWS + HEAD_DIM + 1, :]


def _stat_row(row):
    return slice(8 * row, 8 * row + 1)


def _flash_init(m_ref, acc_ref):
    m_ref[...] = jnp.full(m_ref.shape, NEG, F32)
    acc_ref[...] = jnp.zeros(acc_ref.shape, F32)


def _rank_desc_rows(vals):
    n = vals.shape[0]
    row = lax.broadcasted_iota(jnp.int32, vals.shape, 0)
    rank = jnp.zeros(vals.shape, jnp.int32)
    for i in range(n):
        vi = vals[i:i + 1, :]
        ahead = (vi > vals) | ((vi == vals) & (row > i))
        rank = rank + ahead.astype(jnp.int32)
    return rank


def _nsa_kernel(q_ref, kcmp_ref, vcmpt_ref, ks_ref, vst_ref, kw_ref, vwt_ref, gt_ref, ovlt_ref, expt_ref,
                o_ref, qb_ref, sel_ref, m_ref, acc_ref, out_ref):
    tq = ATT_T
    i = pl.program_id(1)
    t0 = i * tq
    dstart = pl.multiple_of(t0, tq)
    t_row = t0 + lax.broadcasted_iota(jnp.int32, (1, tq), 1)
    t_rows = jnp.concatenate([t_row] * C_GROUP, axis=1)
    blk = lax.broadcasted_iota(jnp.int32, (128, 1), 0)
    kk = lax.broadcasted_iota(jnp.int32, (tq, tq), 0)
    qq = lax.broadcasted_iota(jnp.int32, (tq, tq), 1)
    causal = kk <= qq
    win_tail = kk > qq

    def gate_row(branch, g):
        base = branch * 8 + g * C_GROUP
        return jnp.concatenate([gt_ref[base + r:base + r + 1, :] for r in range(C_GROUP)], axis=1)

    def hrows(g):
        return slice(g * HEAD_DIM, (g + 1) * HEAD_DIM)

    def vrows(g):
        return slice(g * VT_ROWS, (g + 1) * VT_ROWS)

    for g in range(C_KV_HEADS):
        qf = jnp.concatenate([q_ref[:, (g * 4 + r) * 64:(g * 4 + r + 1) * 64] for r in range(C_GROUP)], axis=0)
        qb_ref[g] = qf.astype(BF16)

        s = lax.dot_general(kcmp_ref[g], qf, (((1,), (1,)), ((), ())), precision=HIGHEST,
                            preferred_element_type=F32)
        vis = (blk * CMP_STRIDE + (CMP_BLOCK - 1)) <= t_rows
        sm = jnp.where(vis, s, NEG)
        e = jnp.where(vis, jnp.exp2(sm - jnp.max(sm, axis=0, keepdims=True)), 0.0)
        p = e / jnp.maximum(jnp.sum(e, axis=0, keepdims=True), 1e-30)
        o_cmp = jnp.dot(vcmpt_ref[g].astype(BF16), p.astype(BF16), preferred_element_type=F32)
        out_ref[hrows(g), :] = gate_row(0, g) * o_cmp

        psum = p[:, 0:tq] + p[:, tq:2 * tq] + p[:, 2 * tq:3 * tq] + p[:, 3 * tq:4 * tq]
        imp = jnp.dot(ovlt_ref[...], psum, precision=HIGHEST, preferred_element_type=F32)[0:N_SLC]
        b32 = blk[0:N_SLC]
        cur = t_row // SLC_BLOCK
        forced = (b32 == 0) | (b32 == cur) | (b32 == cur - 1)
        imp = jnp.where(forced, jnp.inf, jnp.where(b32 * SLC_BLOCK > t_row, -jnp.inf, imp))
        sel = (_rank_desc_rows(imp) < SLC_TOPN).astype(F32)
        sel_ref[g] = jnp.concatenate([sel, jnp.zeros((128 - N_SLC, tq), F32)], axis=0).astype(BF16)

    def slc_tile(jb, extra):
        start = pl.multiple_of(jb * tq, tq)
        heads = []
        for g in range(C_KV_HEADS):
            hit = jnp.dot(expt_ref[pl.ds(start, tq), :], sel_ref[g], preferred_element_type=F32) > 0.5
            mask = hit if extra is None else hit & extra
            heads.append((qb_ref[g], ks_ref[g, pl.ds(start, tq), :], vst_ref[vrows(g), pl.ds(start, tq)], mask))
        _flash_steps_t(heads, m_ref, acc_ref, C_GROUP)

    def win_tile(start, mask):
        heads = [(qb_ref[g], kw_ref[g, pl.ds(start, tq), :], vwt_ref[vrows(g), pl.ds(start, tq)], mask)
                 for g in range(C_KV_HEADS)]
        _flash_steps_t(heads, m_ref, acc_ref, C_GROUP)

    def add_branch(branch):
        for g in range(C_KV_HEADS):
            out_ref[hrows(g), :] += gate_row(branch, g) * _flash_out(acc_ref, g)

    _flash_init(m_ref, acc_ref)
    slc_tile(i, causal)

    def slc_body(jb, carry):
        slc_tile(jb, None)
        return carry

    lax.fori_loop(0, i, slc_body, 0)
    add_branch(1)

    _flash_init(m_ref, acc_ref)
    win_tile(dstart, causal)

    @pl.when(i >= 1)
    def _():
        win_tile(pl.multiple_of(t0 - tq, tq), None)

    @pl.when(i >= 2)
    def _():
        win_tile(pl.multiple_of(t0 - 2 * tq, tq), win_tail)

    add_branch(2)

    o_t = jnp.concatenate([out_ref[hrows(g), r * tq:(r + 1) * tq]
                           for g in range(C_KV_HEADS) for r in range(C_GROUP)], axis=0)
    o_ref[...] = o_t.T.astype(BF16)


def _nsa_constants():
    j = np.arange(128)[:, None]
    n = np.arange(128)[None, :]
    ovl_t = ((n * CMP_STRIDE < j * SLC_BLOCK + SLC_BLOCK) & (n * CMP_STRIDE + CMP_BLOCK > j * SLC_BLOCK)
             & (n < N_CMP) & (j < N_SLC)).astype(np.float32)
    expand_t = (np.arange(SEQ)[:, None] // SLC_BLOCK == np.arange(128)[None, :]).astype(np.float32)
    return jnp.asarray(ovl_t), jnp.asarray(expand_t, dtype=BF16)


def _nsa(qc, kcmp, vcmpt, ks, vst, kw, vwt, gt):
    bsz = qc.shape[0]
    tq = ATT_T
    ovl_t, expand_t = _nsa_constants()
    keys = pl.BlockSpec((None, C_KV_HEADS, SEQ, 64), lambda b, i: (b, 0, 0, 0))
    vals = pl.BlockSpec((None, C_KV_HEADS * VT_ROWS, SEQ), lambda b, i: (b, 0, 0))
    rows = C_GROUP * tq
    return pl.pallas_call(
        _nsa_kernel,
        grid=(bsz, SEQ // tq),
        in_specs=[pl.BlockSpec((None, tq, 512), lambda b, i: (b, i, 0)),
                  pl.BlockSpec((None, C_KV_HEADS, 128, 64), lambda b, i: (b, 0, 0, 0)),
                  pl.BlockSpec((None, C_KV_HEADS, 64, 128), lambda b, i: (b, 0, 0, 0)),
                  keys, vals, keys, vals,
                  pl.BlockSpec((None, 128, tq), lambda b, i: (b, 0, i)),
                  pl.BlockSpec((128, 128), lambda b, i: (0, 0)),
                  pl.BlockSpec((SEQ, 128), lambda b, i: (0, 0))],
        out_specs=pl.BlockSpec((None, tq, 512), lambda b, i: (b, i, 0)),
        out_shape=jax.ShapeDtypeStruct((bsz, SEQ, 512), BF16),
        scratch_shapes=[pltpu.VMEM((C_KV_HEADS, rows, 64), BF16), pltpu.VMEM((C_KV_HEADS, 128, tq), BF16),
                        pltpu.VMEM((8 * C_KV_HEADS, rows), F32),
                        pltpu.VMEM((C_KV_HEADS * VT_ROWS, rows), F32), pltpu.VMEM((C_KV_HEADS * 64, rows), F32)],
        compiler_params=_params(("parallel", "arbitrary")),
        name="nsa_attention",
    )(qc, kcmp, vcmpt, ks, vst, kw, vwt, gt, ovl_t, expand_t)


def _moba_kernel(q_ref, k_ref, vt_ref, kmean_ref, o_ref, qb_ref, sel_ref, m_ref, acc_ref):
    tq = ATT_T
    i = pl.program_id(1)
    blk = lax.broadcasted_iota(jnp.int32, (N_MOBA, 1), 0)
    kk = lax.broadcasted_iota(jnp.int32, (tq, tq), 0)
    qq = lax.broadcasted_iota(jnp.int32, (tq, tq), 1)
    causal = kk <= qq
    past = blk < i
    for h in range(D_HEADS):
        hs = slice(h * 64, (h + 1) * 64)
        qf = q_ref[:, hs]
        qb_ref[h] = qf.astype(BF16)
        gate = lax.dot_general(kmean_ref[:, hs], qf, (((1,), (1,)), ((), ())), precision=HIGHEST,
                               preferred_element_type=F32)
        gate = jnp.where(past, gate, -jnp.inf)
        sel_ref[h] = (past & (_rank_desc_rows(gate) < MOBA_TOPK)).astype(F32)

    def tile(jb, diag):
        start = pl.multiple_of(jb * tq, tq)
        heads = []
        for h in range(D_HEADS):
            if diag:
                mask = causal
            else:
                mask = jnp.sum(jnp.where(blk == jb, sel_ref[h], 0.0), axis=0, keepdims=True) > 0.5
            heads.append((qb_ref[h], k_ref[h, pl.ds(start, tq), :],
                          vt_ref[h * VT_ROWS:(h + 1) * VT_ROWS, pl.ds(start, tq)], mask))
        _flash_steps_t(heads, m_ref, acc_ref, 1)

    _flash_init(m_ref, acc_ref)
    tile(i, True)

    def body(jb, carry):
        tile(jb, False)
        return carry

    lax.fori_loop(0, i, body, 0)
    o_t = jnp.concatenate([_flash_out(acc_ref, h) for h in range(D_HEADS)], axis=0)
    o_ref[...] = o_t.T.astype(BF16)


def _moba(qd, kd, vdt, kmean):
    bsz = qd.shape[0]
    tq = ATT_T
    return pl.pallas_call(
        _moba_kernel,
        grid=(bsz, SEQ // tq),
        in_specs=[pl.BlockSpec((None, tq, 512), lambda b, i: (b, i, 0)),
                  pl.BlockSpec((None, D_HEADS, SEQ, 64), lambda b, i: (b, 0, 0, 0)),
                  pl.BlockSpec((None, D_HEADS * VT_ROWS, SEQ), lambda b, i: (b, 0, 0)),
                  pl.BlockSpec((None, N_MOBA, 512), lambda b, i: (b, 0, 0))],
        out_specs=pl.BlockSpec((None, tq, 512), lambda b, i: (b, i, 0)),
        out_shape=jax.ShapeDtypeStruct((bsz, SEQ, 512), BF16),
        scratch_shapes=[pltpu.VMEM((D_HEADS, tq, 64), BF16), pltpu.VMEM((D_HEADS, N_MOBA, tq), F32),
                        pltpu.VMEM((8 * D_HEADS, tq), F32), pltpu.VMEM((D_HEADS * VT_ROWS, tq), F32)],
        compiler_params=_params(("parallel", "arbitrary")),
        name="moba_attention",
    )(qd, kd, vdt, kmean.reshape(bsz, N_MOBA, 512))


def _merge_kernel(ua_ref, ub_ref, oc_ref, od_ref, mg_ref, x_ref, gt_ref,
                  wa_ref, wb_ref, wc_ref, wd_ref, wo_ref, o_ref):
    d = D_MODEL
    merged = jnp.zeros(x_ref.shape, F32)
    for k, (u_ref, w_ref) in enumerate(((ua_ref, wa_ref), (ub_ref, wb_ref), (oc_ref, wc_ref), (od_ref, wd_ref))):
        y = jnp.dot(u_ref[...], w_ref[...], preferred_element_type=F32)
        merged = merged + jax.nn.sigmoid(mg_ref[:, k * d:(k + 1) * d].astype(F32)) * y
    o_ref[...] = x_ref[...] + gt_ref[...] * jnp.dot(merged.astype(BF16), wo_ref[...], preferred_element_type=F32)


def _merge(ua, ub, oc, od, proj2, x2, mod3, wa, wb, wc, wd, wo, *, tm=512):
    t, d = x2.shape
    per_b = SEQ // tm
    act = pl.BlockSpec((tm, 512), lambda i: (i, 0))
    wspec = pl.BlockSpec((512, d), lambda i: (0, 0))
    return pl.pallas_call(
        _merge_kernel,
        grid=(t // tm,),
        in_specs=[act, act, act, act,
                  pl.BlockSpec((tm, 4 * d), lambda i: (i, COL_MERGE // (4 * d))),
                  pl.BlockSpec((tm, d), lambda i: (i, 0)),
                  pl.BlockSpec((None, 1, d), lambda i: (i // per_b, 0, 2)),
                  wspec, wspec, wspec, wspec,
                  pl.BlockSpec((d, d), lambda i: (0, 0))],
        out_specs=pl.BlockSpec((tm, d), lambda i: (i, 0)),
        out_shape=jax.ShapeDtypeStruct((t, d), F32),
        compiler_params=_params(("parallel",)),
        name="merge_out",
    )(ua, ub, oc, od, proj2, x2, mod3, wa, wb, wc, wd, wo)


def _pop_max(work, idx):
    m = jnp.max(work, axis=0, keepdims=True)
    first = jnp.min(jnp.where(work == m, idx, work.shape[0]), axis=0, keepdims=True)
    return m, idx == first


_PEER_PAIRS = [(i, j) for i in range(PEER_TOPK) for j in range(PEER_TOPK) if (i + 1) * (j + 1) <= PEER_TOPK]
_PEER_PAIR_ROWS = 56
NOT_TOP = 99.0


def _peer_sel_kernel(q_ref, keys_ref, grp_ref, n1_ref, e1_ref, r2_ref, e2_ref, st_ref):
    tt = q_ref.shape[0]
    half = PEER_QDIM // 2
    for hp in range(2 * PEER_HEADS):
        qh = q_ref[:, hp * half:(hp + 1) * half]
        qn = qh * lax.rsqrt(jnp.mean(qh * qh, axis=-1, keepdims=True) + NORM_EPS)
        st_ref[hp] = lax.dot_general(keys_ref[hp], qn, (((1,), (1,)), ((), ())), precision=HIGHEST,
                                     preferred_element_type=F32)

    idx = lax.broadcasted_iota(jnp.int32, (PEER_NKEYS, 128), 0)
    pidx = lax.broadcasted_iota(jnp.int32, (_PEER_PAIR_ROWS, 128), 0)
    pad_rows = jnp.full((_PEER_PAIR_ROWS - len(_PEER_PAIRS), 128), -jnp.inf, F32)

    def pop16(work, index, tie_safe, track):
        vals, order = [], jnp.full(work.shape, NOT_TOP, F32)
        for it in range(PEER_TOPK):
            if tie_safe:
                m, hit = _pop_max(work, index)
            else:
                m = jnp.max(work, axis=0, keepdims=True)
                hit = work == m
            work = jnp.where(hit, -jnp.inf, work)
            if track:
                order = jnp.where(hit, float(it), order)
            vals.append(m)
        removed = jnp.sum(jnp.where(work == -jnp.inf, 1.0, 0.0), axis=0, keepdims=True)
        return vals, order, removed

    def select(cs, tie_safe):
        wrong = jnp.zeros((1, 128), F32)
        n_pad = float(_PEER_PAIR_ROWS - len(_PEER_PAIRS))
        for h in range(PEER_HEADS):
            scores, tops, ranks = [], [], []
            for p in range(2):
                st = st_ref[2 * h + p, :, cs]
                vals, rank, removed = pop16(st, idx, tie_safe, track=tie_safe or p == 1)
                wrong = wrong + jnp.abs(removed - float(PEER_TOPK))
                scores.append(st)
                tops.append(vals)
                ranks.append(rank)
            cand = jnp.concatenate([tops[0][i] + tops[1][j] for (i, j) in _PEER_PAIRS] + [pad_rows], axis=0)
            best, order, removed = pop16(cand, pidx, tie_safe, track=tie_safe)
            wrong = wrong + jnp.abs(removed - (float(PEER_TOPK) + n_pad))
            picked = jnp.where((order < float(PEER_TOPK)) if tie_safe else (cand >= best[-1]), 1.0, 0.0)
            z = jnp.ones_like(best[0])
            for k in range(1, PEER_TOPK):
                z = z + jnp.exp(best[k] - best[0])
            count = jnp.dot(grp_ref[...], picked.astype(BF16), preferred_element_type=F32)
            n1 = jnp.zeros(scores[0].shape, F32)
            for i in reversed(range(PEER_TOPK)):
                at_i = (ranks[0] == float(i)) if tie_safe else (scores[0] >= tops[0][i])
                n1 = jnp.where(at_i, count[i:i + 1], n1)
            n1_ref[h, :, cs] = n1
            r2_ref[h, :, cs] = ranks[1]
            e1_ref[h, :, cs] = jnp.exp(scores[0] - tops[0][0]) / z
            e2_ref[h, :, cs] = jnp.exp(scores[1] - tops[1][0])
        return wrong

    def token_chunk(c, carry):
        cs = pl.ds(pl.multiple_of(c * 128, 128), 128)
        wrong = select(cs, tie_safe=False)

        @pl.when(jnp.max(wrong) > 0.0)
        def _():
            select(cs, tie_safe=True)

        return carry

    lax.fori_loop(0, tt // 128, token_chunk, 0)


def _peer_select(q2, subkeys, *, tt=512):
    t = q2.shape[0]
    grp = np.zeros((PEER_TOPK, _PEER_PAIR_ROWS), np.float32)
    for row, (i, _) in enumerate(_PEER_PAIRS):
        grp[i, row] = 1.0
    big = pl.BlockSpec((PEER_HEADS, PEER_NKEYS, tt), lambda i: (0, 0, i))
    bshape = jax.ShapeDtypeStruct((PEER_HEADS, PEER_NKEYS, t), F32)
    half = PEER_QDIM // 2
    return pl.pallas_call(
        _peer_sel_kernel,
        grid=(t // tt,),
        in_specs=[pl.BlockSpec((tt, PEER_HEADS * PEER_QDIM), lambda i: (i, 0)),
                  pl.BlockSpec((2 * PEER_HEADS, PEER_NKEYS, half), lambda i: (0, 0, 0)),
                  pl.BlockSpec((PEER_TOPK, _PEER_PAIR_ROWS), lambda i: (0, 0))],
        out_specs=[big, big, big, big],
        out_shape=[bshape, bshape, bshape, bshape],
        scratch_shapes=[pltpu.VMEM((2 * PEER_HEADS, PEER_NKEYS, tt), F32)],
        compiler_params=_params(("parallel",)),
        name="peer_select",
    )(q2, subkeys.reshape(2 * PEER_HEADS, PEER_NKEYS, half), jnp.asarray(grp, dtype=BF16))


PEER_SLAB = 256

def _peer_main_kernel(h_ref, u_ref, vt_ref, n1_ref, e1_ref, r2_ref, e2_ref, x_ref, gt_ref, fg_ref,
                      o_ref, acc_ref, act_ref, p_ref, ht_ref, *, final):
    j = pl.program_id(1)
    te, tt = act_ref.shape
    na = te // PEER_NKEYS

    @pl.when(j == 0)
    def _():
        acc_ref[...] = jnp.zeros(acc_ref.shape, F32)
        ht_ref[...] = h_ref[...].astype(F32).T.astype(BF16)

    a0 = pl.multiple_of(j * na, na)
    nslab = te // PEER_SLAB
    parts = []

    def act(s):
        ss = slice(s * PEER_SLAB, (s + 1) * PEER_SLAB)
        act_ref[ss, :] = jnp.dot(u_ref[ss, :], ht_ref[...], preferred_element_type=F32)

    def val(s):
        ss = slice(s * PEER_SLAB, (s + 1) * PEER_SLAB)
        parts.append(jnp.dot(vt_ref[:, ss], p_ref[ss, :], preferred_element_type=F32))

    def mask(al):
        rs = slice(al * PEER_NKEYS, (al + 1) * PEER_NKEYS)
        for c in range(tt // 128):
            cs = slice(c * 128, (c + 1) * 128)
            w = jnp.zeros((PEER_NKEYS, 128), F32)
            for hh in range(PEER_HEADS):
                n1 = n1_ref[hh, pl.ds(a0, na), cs][al:al + 1]
                e1 = e1_ref[hh, pl.ds(a0, na), cs][al:al + 1]
                w = w + jnp.where(r2_ref[hh, :, cs] < n1, e1 * e2_ref[hh, :, cs], 0.0)
            p_ref[rs, cs] = (w * _gelu(act_ref[rs, cs])).astype(BF16)

    per = PEER_SLAB // PEER_NKEYS
    act(0)
    act(1)
    for s in range(nslab):
        for k in range(per):
            mask(s * per + k)
            if k == 0 and s + 2 < nslab:
                act(s + 2)
        val(s)
    acc_ref[...] += functools.reduce(lambda x, y: x + y, parts)

    @pl.when(j == pl.num_programs(1) - 1)
    def _():
        y = x_ref[...] + gt_ref[...] * acc_ref[...].T
        if final:
            y = y * lax.rsqrt(jnp.mean(y * y, axis=-1, keepdims=True) + NORM_EPS) * fg_ref[...]
        o_ref[...] = y


def _peer_main(h2, u_bf, vt_bf, n1, e1, r2, e2, x2, mod3, final_g, *, final, tt=512, te=2048):
    t, d = x2.shape
    assert (te // PEER_NKEYS) % 8 == 0 and te % PEER_SLAB == 0
    per_b = SEQ // tt
    big = pl.BlockSpec((PEER_HEADS, PEER_NKEYS, tt), lambda i, j: (0, 0, i))
    return pl.pallas_call(
        functools.partial(_peer_main_kernel, final=final),
        grid=(t // tt, PEER_EXPERTS // te),
        in_specs=[pl.BlockSpec((tt, d), lambda i, j: (i, 0)),
                  pl.BlockSpec((te, d), lambda i, j: (j, 0)),
                  pl.BlockSpec((d, te), lambda i, j: (0, j)),
                  big, big, big, big,
                  pl.BlockSpec((tt, d), lambda i, j: (i, 0)),
                  pl.BlockSpec((None, 1, d), lambda i, j: (i // per_b, 0, 5)),
                  pl.BlockSpec((1, d), lambda i, j: (0, 0))],
        out_specs=pl.BlockSpec((tt, d), lambda i, j: (i, 0)),
        out_shape=jax.ShapeDtypeStruct((t, d), F32),
        scratch_shapes=[pltpu.VMEM((d, tt), F32), pltpu.VMEM((te, tt), F32), pltpu.VMEM((te, tt), BF16),
                        pltpu.VMEM((d, tt), BF16)],
        compiler_params=_params(("parallel", "arbitrary")),
        name="peer_experts",
    )(h2, u_bf, vt_bf, n1, e1, r2, e2, x2, mod3, final_g.reshape(1, d))


def _reorder_w_in(w):
    pad = jnp.zeros((w.shape[0], 1024 - 792), w.dtype)
    return jnp.concatenate([w[:, 5400:9496], w[:, 3072:3864], pad, w[:, 0:3072], w[:, 3864:5400]], axis=1)


def kernel(x, c, positions, mod_w, mod_b, norm_mix_g, norm_ffn_g, w_in, a_conv_w, a_out, b_conv_w, b_conv_b, b_ln_g, b_ln_b, b_out, c_cmp_pos, c_cmp_w1, c_cmp_w2, c_out, d_out, w_o, peer_wq, peer_subkeys, peer_u, peer_v, final_norm_g):
    bsz, s, d = x.shape
    assert s == SEQ and d == D_MODEL
    depth = mod_w.shape[0]
    t = bsz * s
    cos, sin = _rope_tables(positions)
    mod = _modulation(c, mod_w, mod_b)
    x2 = x.reshape(t, d)
    for l in range(depth):
        mod3 = mod[l].reshape(bsz, 1, 6 * d)
        proj2 = _norm_matmul(x2, norm_mix_g[l], mod3, 0, 1, _reorder_w_in(w_in[l]).astype(BF16), out_dtype=BF16)
        proj3 = proj2.reshape(bsz, s, PROJ_COLS)
        ua, ub = _conv_mixers(proj3, a_conv_w[l], b_conv_w[l], b_conv_b[l], b_ln_g[l], b_ln_b[l])
        qc, kc, vc, ks, vst, kw, vwt, gt, qd, kd, vdt, kmean = _prep(proj3, cos, sin)
        kcmp, vcmpt = _compress(kc, vc, c_cmp_pos[l], c_cmp_w1[l], c_cmp_w2[l])
        oc = _nsa(qc, kcmp, vcmpt, ks, vst, kw, vwt, gt)
        od = _moba(qd, kd, vdt, kmean)
        x2 = _merge(ua.reshape(t, 512), ub.reshape(t, 512), oc.reshape(t, 512), od.reshape(t, 512),
                    proj2, x2, mod3, a_out[l].astype(BF16), b_out[l].astype(BF16), c_out[l].astype(BF16),
                    d_out[l].astype(BF16), w_o[l].astype(BF16))
        q2, h2 = _norm_matmul(x2, norm_ffn_g[l], mod3, 3, 4, peer_wq[l].astype(BF16), emit_h=True)
        n1, e1, r2, e2 = _peer_select(q2, peer_subkeys[l])
        x2 = _peer_main(h2, peer_u[l].astype(BF16), peer_v[l].T.astype(BF16), n1, e1, r2, e2,
                        x2, mod3, final_norm_g, final=(l == depth - 1))
    return x2.reshape(bsz, s, d)
```

```python
import functools

import numpy as np
import jax
import jax.numpy as jnp
from jax import lax
from jax.experimental import pallas as pl
from jax.experimental.pallas import tpu as pltpu

F32 = jnp.float32
BF16 = jnp.bfloat16
HIGHEST = lax.Precision.HIGHEST

D_MODEL = 1024
SEQ = 2048
HEAD_DIM = 64
ROPE_THETA = 10000.0
NORM_EPS = 1e-6
A_WIDTH = 512
A_CONV = 3
B_WIDTH = 512
B_CONV = 31
C_HEADS = 8
C_KV_HEADS = 2
C_GROUP = 4
CMP_BLOCK = 32
CMP_STRIDE = 16
CMP_HIDDEN = 128
N_CMP = (SEQ - CMP_BLOCK) // CMP_STRIDE + 1
SLC_BLOCK = 64
SLC_TOPN = 16
N_SLC = SEQ // SLC_BLOCK
WIN = 512
D_HEADS = 8
MOBA_BLOCK = 256
MOBA_TOPK = 3
N_MOBA = SEQ // MOBA_BLOCK
PEER_HEADS = 8
PEER_NKEYS = 128
PEER_EXPERTS = PEER_NKEYS * PEER_NKEYS
PEER_QDIM = 256
PEER_TOPK = 16

PROJ_COLS = 9728
COL_MERGE = 0
COL_KVG = 4096
COL_GATE = COL_KVG + 768
COL_A = 5120

VMEM_LIMIT = 56 * 1024 * 1024
NEG = -1e30

ATT_T = 256
CONV_T = 256
HALO = 32


def _params(sem, flags=None):
    return pltpu.CompilerParams(dimension_semantics=sem, vmem_limit_bytes=VMEM_LIMIT, flags=flags)


def _gelu(x):
    return 0.5 * x * (1.0 + lax.erf(x * np.float32(np.sqrt(0.5))))


def _rope_table_kernel(pos_ref, inv_ref, sign_ref, cos_ref, sin_ref):
    ang = pos_ref[...] * inv_ref[...]
    cos_ref[...] = jnp.cos(ang)
    sin_ref[...] = jnp.sin(ang) * sign_ref[...]


def _rope_tables(positions):
    bsz, s = positions.shape
    inv = 1.0 / (ROPE_THETA ** (jnp.arange(0, HEAD_DIM, 2, dtype=F32) / HEAD_DIM))
    inv128 = jnp.tile(inv, 4)[None, :]
    sign = jnp.tile(jnp.concatenate([-jnp.ones(32, F32), jnp.ones(32, F32)]), 2)[None, :]
    pos = positions.astype(F32).reshape(bsz * s, 1)
    t = bsz * s
    cos, sin = pl.pallas_call(
        _rope_table_kernel,
        grid=(t // SEQ,),
        in_specs=[pl.BlockSpec((SEQ, 1), lambda i: (i, 0)),
                  pl.BlockSpec((1, 128), lambda i: (0, 0)),
                  pl.BlockSpec((1, 128), lambda i: (0, 0))],
        out_specs=[pl.BlockSpec((SEQ, 128), lambda i: (i, 0))] * 2,
        out_shape=[jax.ShapeDtypeStruct((t, 128), F32)] * 2,
        compiler_params=_params(("parallel",)),
        name="rope_tables",
    )(pos, inv128, sign)
    return cos.reshape(bsz, s, 128), sin.reshape(bsz, s, 128)


def _rope(x, cos, sin):
    w = x.shape[-1]
    lane = lax.broadcasted_iota(jnp.int32, x.shape, 1)
    swapped = jnp.where(lane % 64 < 32, pltpu.roll(x, w - 32, 1), pltpu.roll(x, 32, 1))
    return x * cos + swapped * sin


def _mod_kernel(c_ref, w_ref, b_ref, o_ref):
    c = c_ref[...]
    cond = c * jax.nn.sigmoid(c)
    o_ref[...] = jnp.dot(cond, w_ref[...], precision=HIGHEST, preferred_element_type=F32) + b_ref[...]


def _modulation(c, mod_w, mod_b):
    nl, d, n = mod_w.shape
    bsz = c.shape[0]
    tn = 1536
    return pl.pallas_call(
        _mod_kernel,
        grid=(nl, n // tn),
        in_specs=[pl.BlockSpec((bsz, d), lambda l, j: (0, 0)),
                  pl.BlockSpec((None, d, tn), lambda l, j: (l, 0, j)),
                  pl.BlockSpec((None, 1, tn), lambda l, j: (l, 0, j))],
        out_specs=pl.BlockSpec((None, bsz, tn), lambda l, j: (l, 0, j)),
        out_shape=jax.ShapeDtypeStruct((nl, bsz, n), F32),
        compiler_params=_params(("parallel", "parallel")),
        name="adaln_mod",
    )(c, mod_w, mod_b.reshape(nl, 1, n))


def _norm_matmul_kernel(x_ref, g_ref, sc_ref, sh_ref, w_ref, o_ref, *rest, emit_h):
    h_scr = rest[-1]

    @pl.when(pl.program_id(1) == 0)
    def _():
        x = x_ref[...]
        y = x * lax.rsqrt(jnp.mean(x * x, axis=-1, keepdims=True) + NORM_EPS)
        h = (y * g_ref[...]) * (1.0 + sc_ref[...]) + sh_ref[...]
        h_scr[...] = h.astype(BF16)
        if emit_h:
            rest[0][...] = h.astype(BF16)

    o_ref[...] = jnp.dot(h_scr[...], w_ref[...], preferred_element_type=F32).astype(o_ref.dtype)


def _norm_matmul(x2, g, mod3, sh_blk, sc_blk, w, *, out_dtype=F32, emit_h=False, tm=1024, tn=512):
    t, d = x2.shape
    n = w.shape[1]
    per_b = SEQ // tm
    out_shape = [jax.ShapeDtypeStruct((t, n), out_dtype)]
    out_specs = [pl.BlockSpec((tm, tn), lambda i, j: (i, j))]
    if emit_h:
        out_shape.append(jax.ShapeDtypeStruct((t, d), BF16))
        out_specs.append(pl.BlockSpec((tm, d), lambda i, j: (i, 0)))
    res = pl.pallas_call(
        functools.partial(_norm_matmul_kernel, emit_h=emit_h),
        grid=(t // tm, n // tn),
        in_specs=[pl.BlockSpec((tm, d), lambda i, j: (i, 0)),
                  pl.BlockSpec((1, d), lambda i, j: (0, 0)),
                  pl.BlockSpec((None, 1, d), lambda i, j: (i // per_b, 0, sc_blk)),
                  pl.BlockSpec((None, 1, d), lambda i, j: (i // per_b, 0, sh_blk)),
                  pl.BlockSpec((d, tn), lambda i, j: (0, j))],
        out_specs=out_specs,
        out_shape=out_shape,
        scratch_shapes=[pltpu.VMEM((tm, d), BF16)],
        compiler_params=_params(("parallel", "arbitrary")),
        name="norm_matmul",
    )(x2, g.reshape(1, d), mod3, mod3, w)
    return res if emit_h else res[0]


def _conv_kernel(ab_ref, ac_ref, ax_ref, ba_ref, bg_ref, pac_ref, pax_ref, pba_ref, pbg_ref,
                 aw_ref, bw_ref, bb_ref, lng_ref, lnb_ref, ua_ref, ub_ref, ext_ref, y_ref):
    ts = ab_ref.shape[0]
    keep = (pl.program_id(1) > 0).astype(F32)

    def f32(ref):
        return ref[...].astype(F32)

    def causal_conv(w_ref, taps, bias_ref, out_ref):
        blk = 128
        reach = ((taps - 1) // 8) * 8
        for t0 in range(0, ts, blk):
            for c0 in range(0, w_ref.shape[1], 128):
                cs = slice(c0, c0 + 128)
                acc = jnp.zeros((blk, 128), F32) if bias_ref is None else jnp.zeros((blk, 128), F32) + bias_ref[:, cs]
                for r in range(min(8, taps)):
                    base = HALO + t0 - reach - r
                    ur = ext_ref[base:base + blk + reach, cs]
                    for q in range(reach // 8 + 1):
                        shift = 8 * q + r
                        if shift < taps:
                            k = taps - 1 - shift
                            acc = acc + w_ref[k:k + 1, cs] * ur[reach - 8 * q:reach - 8 * q + blk]
                out_ref[t0:t0 + blk, cs] = acc

    ext_ref[0:HALO, :] = f32(pac_ref) * f32(pax_ref) * keep
    ext_ref[HALO:HALO + ts, :] = f32(ac_ref) * f32(ax_ref)
    causal_conv(aw_ref, A_CONV, None, y_ref)
    ua_ref[...] = (f32(ab_ref) * y_ref[...]).astype(BF16)

    ext_ref[0:HALO, :] = f32(pba_ref) * jax.nn.sigmoid(f32(pbg_ref)) * keep
    ext_ref[HALO:HALO + ts, :] = f32(ba_ref) * jax.nn.sigmoid(f32(bg_ref))
    causal_conv(bw_ref, B_CONV, bb_ref, y_ref)
    acc = y_ref[...]
    mu = jnp.mean(acc, axis=-1, keepdims=True)
    cen = acc - mu
    var = jnp.mean(cen * cen, axis=-1, keepdims=True)
    y = cen * lax.rsqrt(var + NORM_EPS) * lng_ref[...] + lnb_ref[...]
    ub_ref[...] = (y * jax.nn.sigmoid(y)).astype(BF16)


def _conv_mixers(proj3, a_conv_w, b_conv_w, b_conv_b, b_ln_g, b_ln_b):
    bsz = proj3.shape[0]
    ts = CONV_T
    c0 = COL_A // 512
    r = ts // HALO

    def cur(k):
        return pl.BlockSpec((None, ts, 512), lambda b, i, k=k: (b, i, c0 + k))

    def prev(k):
        return pl.BlockSpec((None, HALO, 512), lambda b, i, k=k: (b, jnp.maximum(i * r - 1, 0), c0 + k))

    def full(shape):
        return pl.BlockSpec(shape, lambda b, i: (0,) * len(shape))

    return pl.pallas_call(
        _conv_kernel,
        grid=(bsz, SEQ // ts),
        in_specs=[cur(0), cur(1), cur(2), cur(3), cur(4), prev(1), prev(2), prev(3), prev(4),
                  full((A_CONV, A_WIDTH)), full((B_CONV, B_WIDTH)), full((1, B_WIDTH)),
                  full((1, B_WIDTH)), full((1, B_WIDTH))],
        out_specs=[pl.BlockSpec((None, ts, 512), lambda b, i: (b, i, 0))] * 2,
        out_shape=[jax.ShapeDtypeStruct((bsz, SEQ, 512), BF16)] * 2,
        scratch_shapes=[pltpu.VMEM((HALO + ts, 512), F32), pltpu.VMEM((ts, 512), F32)],
        compiler_params=_params(("parallel", "arbitrary")),
        name="conv_mixers",
    )(proj3, proj3, proj3, proj3, proj3, proj3, proj3, proj3, proj3,
      a_conv_w, b_conv_w, b_conv_b.reshape(1, -1), b_ln_g.reshape(1, -1), b_ln_b.reshape(1, -1))


VT_ROWS = 80


def _values_t(v, n_heads):
    rows = v.shape[0]
    vt = v.T
    tail = (lax.broadcasted_iota(jnp.int32, (VT_ROWS - HEAD_DIM, rows), 0) == 0).astype(F32)
    blocks = []
    for h in range(n_heads):
        blocks += [vt[h * HEAD_DIM:(h + 1) * HEAD_DIM], tail]
    return jnp.concatenate(blocks, axis=0).astype(BF16)


def _prep_kernel(cq_ref, dq_ref, dk_ref, dv_ref, kvg_ref, cos_ref, sin_ref,
                 qc_ref, kc_ref, vc_ref, ks_ref, vst_ref, kw_ref, vwt_ref, gt_ref,
                 qd_ref, kd_ref, vdt_ref, kmean_ref):
    cos = cos_ref[...]
    sin = sin_ref[...]
    cos4 = jnp.concatenate([cos] * 4, axis=1)
    sin4 = jnp.concatenate([sin] * 4, axis=1)
    scale = np.float32(HEAD_DIM ** -0.5 * np.log2(np.e))
    qc_ref[...] = _rope(cq_ref[...].astype(F32), cos4, sin4) * scale
    qd_ref[...] = _rope(dq_ref[...].astype(F32), cos4, sin4) * scale
    kd = _rope(dk_ref[...].astype(F32), cos4, sin4)
    for h in range(D_HEADS):
        kd_ref[h] = kd[:, h * 64:(h + 1) * 64].astype(BF16)
    kmean_ref[...] = jnp.mean(kd, axis=0, keepdims=True)
    vdt_ref[...] = _values_t(dv_ref[...].astype(F32), D_HEADS)

    def kvg(k):
        return kvg_ref[:, k * 128:(k + 1) * 128].astype(F32)

    kc = _rope(kvg(0), cos, sin)
    vc = kvg(1)
    ks = _rope(kvg(2), cos, sin)
    kw = _rope(kvg(4), cos, sin)
    for g in range(C_KV_HEADS):
        gs = slice(g * 64, (g + 1) * 64)
        kc_ref[g] = kc[:, gs]
        vc_ref[g] = vc[:, gs]
        ks_ref[g] = ks[:, gs].astype(BF16)
        kw_ref[g] = kw[:, gs].astype(BF16)
    vst_ref[...] = _values_t(kvg(3), C_KV_HEADS)
    vwt_ref[...] = _values_t(kvg(5), C_KV_HEADS)
    gt_ref[...] = jax.nn.sigmoid(kvg(6)).T


def _prep(proj3, cos, sin):
    bsz = proj3.shape[0]
    ts = MOBA_BLOCK
    c0 = COL_A // 512

    def col512(k):
        return pl.BlockSpec((None, ts, 512), lambda b, i: (b, i, c0 + k))

    row128 = pl.BlockSpec((None, ts, 128), lambda b, i: (b, i, 0))
    row512 = pl.BlockSpec((None, ts, 512), lambda b, i: (b, i, 0))
    col128t = pl.BlockSpec((None, 128, ts), lambda b, i: (b, 0, i))
    val2t = pl.BlockSpec((None, C_KV_HEADS * VT_ROWS, ts), lambda b, i: (b, 0, i))
    head64 = pl.BlockSpec((None, C_KV_HEADS, ts, 64), lambda b, i: (b, 0, i, 0))
    k64 = jax.ShapeDtypeStruct((bsz, C_KV_HEADS, SEQ, 64), BF16)
    t128 = jax.ShapeDtypeStruct((bsz, C_KV_HEADS * VT_ROWS, SEQ), BF16)
    return pl.pallas_call(
        _prep_kernel,
        grid=(bsz, SEQ // ts),
        in_specs=[col512(5), col512(6), col512(7), col512(8),
                  pl.BlockSpec((None, ts, 1024), lambda b, i: (b, i, COL_KVG // 1024)),
                  row128, row128],
        out_specs=[row512, head64, head64, head64, val2t, head64, val2t, col128t,
                   row512,
                   pl.BlockSpec((None, D_HEADS, ts, 64), lambda b, i: (b, 0, i, 0)),
                   pl.BlockSpec((None, D_HEADS * VT_ROWS, ts), lambda b, i: (b, 0, i)),
                   pl.BlockSpec((None, None, 1, 512), lambda b, i: (b, i, 0, 0))],
        out_shape=[jax.ShapeDtypeStruct((bsz, SEQ, 512), F32),
                   jax.ShapeDtypeStruct((bsz, C_KV_HEADS, SEQ, 64), F32),
                   jax.ShapeDtypeStruct((bsz, C_KV_HEADS, SEQ, 64), F32),
                   k64, t128, k64, t128,
                   jax.ShapeDtypeStruct((bsz, 128, SEQ), F32),
                   jax.ShapeDtypeStruct((bsz, SEQ, 512), F32),
                   jax.ShapeDtypeStruct((bsz, D_HEADS, SEQ, 64), BF16),
                   jax.ShapeDtypeStruct((bsz, D_HEADS * VT_ROWS, SEQ), BF16),
                   jax.ShapeDtypeStruct((bsz, N_MOBA, 1, 512), F32)],
        compiler_params=_params(("parallel", "parallel")),
        name="attn_prep",
    )(proj3, proj3, proj3, proj3, proj3, cos, sin)


def _compress_kernel(kc_ref, vc_ref, pos_ref, w1_ref, w2_ref, kcmp_ref, vcmp_ref):
    half = CMP_STRIDE * HEAD_DIM
    row = lax.broadcasted_iota(jnp.int32, (128, CMP_HIDDEN), 0)
    for which, (src, dst) in enumerate(((kc_ref, kcmp_ref), (vc_ref, vcmp_ref))):
        bias = jnp.dot(pos_ref[which], w1_ref[which], precision=HIGHEST, preferred_element_type=F32)[0:1]
        for g in range(C_KV_HEADS):
            chunks = src[g]
            d1 = jnp.dot(chunks, w1_ref[which, 0:half, :], precision=HIGHEST, preferred_element_type=F32)
            d2 = jnp.dot(chunks, w1_ref[which, half:2 * half, :], precision=HIGHEST, preferred_element_type=F32)
            d2 = jnp.where(row < 127, pltpu.roll(d2, 127, 0), 0.0)
            hid = _gelu(d1 + d2 + bias)
            out = jnp.dot(hid, w2_ref[which], precision=HIGHEST, preferred_element_type=F32)
            dst[g] = out if which == 0 else out.T


def _compress(kc, vc, cmp_pos, cmp_w1, cmp_w2):
    bsz = kc.shape[0]
    kc4 = kc.reshape(bsz, C_KV_HEADS, SEQ // CMP_STRIDE, CMP_STRIDE * HEAD_DIM)
    vc4 = vc.reshape(bsz, C_KV_HEADS, SEQ // CMP_STRIDE, CMP_STRIDE * HEAD_DIM)
    pos8 = jnp.broadcast_to(cmp_pos.reshape(2, 1, CMP_BLOCK * HEAD_DIM), (2, 8, CMP_BLOCK * HEAD_DIM))
    blk = pl.BlockSpec((None, C_KV_HEADS, 128, 1024), lambda b: (b, 0, 0, 0))
    out = pl.BlockSpec((None, C_KV_HEADS, 128, 64), lambda b: (b, 0, 0, 0))
    return pl.pallas_call(
        _compress_kernel,
        grid=(bsz,),
        in_specs=[blk, blk,
                  pl.BlockSpec((2, 8, 2048), lambda b: (0, 0, 0)),
                  pl.BlockSpec((2, 2048, CMP_HIDDEN), lambda b: (0, 0, 0)),
                  pl.BlockSpec((2, CMP_HIDDEN, 64), lambda b: (0, 0, 0))],
        out_specs=[out, pl.BlockSpec((None, C_KV_HEADS, 64, 128), lambda b: (b, 0, 0, 0))],
        out_shape=[jax.ShapeDtypeStruct((bsz, C_KV_HEADS, 128, 64), F32),
                   jax.ShapeDtypeStruct((bsz, C_KV_HEADS, 64, 128), F32)],
        compiler_params=_params(("parallel",)),
        name="nsa_compress",
    )(kc4, vc4, pos8, cmp_w1, cmp_w2)


def _flash_steps_t(heads, m_ref, acc_ref, groups):
    scores = [lax.dot_general(k, qb, (((1,), (1,)), ((), ())), preferred_element_type=F32)
              for (qb, k, _, _) in heads]
    probs = []
    for n, (qb, _, _, mask) in enumerate(heads):
        s = scores[n]
        tq = qb.shape[0] // groups
        if mask is not None:
            s = jnp.concatenate([jnp.where(mask, s[:, r * tq:(r + 1) * tq], NEG) for r in range(groups)], axis=1)
        sr = _stat_row(n)
        m_old = m_ref[sr, :]
        m_new = jnp.maximum(m_old, jnp.max(s, axis=0, keepdims=True))
        m_ref[sr, :] = m_new
        probs.append((jnp.exp2(m_old - m_new), jnp.exp2(s - m_new).astype(BF16)))
    for n, (_, _, vt, _) in enumerate(heads):
        alpha, p = probs[n]
        a = slice(n * VT_ROWS, (n + 1) * VT_ROWS)
        acc_ref[a, :] = alpha * acc_ref[a, :] + jnp.dot(vt, p, preferred_element_type=F32)


def _flash_out(acc_ref, n):
    return acc_ref[n * VT_ROWS:n * VT_ROWS + HEAD_DIM, :] / acc_ref[n * VT_ROWS + HEAD_DIM:n * VT_ROWS + HEAD_DIM + 1, :]


def _stat_row(row):
    return slice(8 * row, 8 * row + 1)


def _flash_init(m_ref, acc_ref):
    m_ref[...] = jnp.full(m_ref.shape, NEG, F32)
    acc_ref[...] = jnp.zeros(acc_ref.shape, F32)


def _rank_desc_rows(vals):
    n = vals.shape[0]
    row = lax.broadcasted_iota(jnp.int32, vals.shape, 0)
    rank = jnp.zeros(vals.shape, jnp.int32)
    for i in range(n):
        vi = vals[i:i + 1, :]
        ahead = (vi > vals) | ((vi == vals) & (row > i))
        rank = rank + ahead.astype(jnp.int32)
    return rank


def _nsa_kernel(q_ref, kcmp_ref, vcmpt_ref, ks_ref, vst_ref, kw_ref, vwt_ref, gt_ref, ovlt_ref, expt_ref,
                o_ref, qb_ref, sel_ref, m_ref, acc_ref, out_ref):
    tq = ATT_T
    i = pl.program_id(1)
    t0 = i * tq
    dstart = pl.multiple_of(t0, tq)
    t_row = t0 + lax.broadcasted_iota(jnp.int32, (1, tq), 1)
    t_rows = jnp.concatenate([t_row] * C_GROUP, axis=1)
    blk = lax.broadcasted_iota(jnp.int32, (128, 1), 0)
    kk = lax.broadcasted_iota(jnp.int32, (tq, tq), 0)
    qq = lax.broadcasted_iota(jnp.int32, (tq, tq), 1)
    causal = kk <= qq
    win_tail = kk > qq

    def gate_row(branch, g):
        base = branch * 8 + g * C_GROUP
        return jnp.concatenate([gt_ref[base + r:base + r + 1, :] for r in range(C_GROUP)], axis=1)

    def hrows(g):
        return slice(g * HEAD_DIM, (g + 1) * HEAD_DIM)

    def vrows(g):
        return slice(g * VT_ROWS, (g + 1) * VT_ROWS)

    for g in range(C_KV_HEADS):
        qf = jnp.concatenate([q_ref[:, (g * 4 + r) * 64:(g * 4 + r + 1) * 64] for r in range(C_GROUP)], axis=0)
        qb_ref[g] = qf.astype(BF16)

        s = lax.dot_general(kcmp_ref[g], qf, (((1,), (1,)), ((), ())), precision=HIGHEST,
                            preferred_element_type=F32)
        vis = (blk * CMP_STRIDE + (CMP_BLOCK - 1)) <= t_rows
        sm = jnp.where(vis, s, NEG)
        e = jnp.where(vis, jnp.exp2(sm - jnp.max(sm, axis=0, keepdims=True)), 0.0)
        p = e / jnp.maximum(jnp.sum(e, axis=0, keepdims=True), 1e-30)
        o_cmp = jnp.dot(vcmpt_ref[g].astype(BF16), p.astype(BF16), preferred_element_type=F32)
        out_ref[hrows(g), :] = gate_row(0, g) * o_cmp

        psum = p[:, 0:tq] + p[:, tq:2 * tq] + p[:, 2 * tq:3 * tq] + p[:, 3 * tq:4 * tq]
        imp = jnp.dot(ovlt_ref[...], psum, precision=HIGHEST, preferred_element_type=F32)[0:N_SLC]
        b32 = blk[0:N_SLC]
        cur = t_row // SLC_BLOCK
        forced = (b32 == 0) | (b32 == cur) | (b32 == cur - 1)
        imp = jnp.where(forced, jnp.inf, jnp.where(b32 * SLC_BLOCK > t_row, -jnp.inf, imp))
        sel = (_rank_desc_rows(imp) < SLC_TOPN).astype(F32)
        sel_ref[g] = jnp.concatenate([sel, jnp.zeros((128 - N_SLC, tq), F32)], axis=0).astype(BF16)

    def slc_tile(jb, extra):
        start = pl.multiple_of(jb * tq, tq)
        heads = []
        for g in range(C_KV_HEADS):
            hit = jnp.dot(expt_ref[pl.ds(start, tq), :], sel_ref[g], preferred_element_type=F32) > 0.5
            mask = hit if extra is None else hit & extra
            heads.append((qb_ref[g], ks_ref[g, pl.ds(start, tq), :], vst_ref[vrows(g), pl.ds(start, tq)], mask))
        _flash_steps_t(heads, m_ref, acc_ref, C_GROUP)

    def win_tile(start, mask):
        heads = [(qb_ref[g], kw_ref[g, pl.ds(start, tq), :], vwt_ref[vrows(g), pl.ds(start, tq)], mask)
                 for g in range(C_KV_HEADS)]
        _flash_steps_t(heads, m_ref, acc_ref, C_GROUP)

    def add_branch(branch):
        for g in range(C_KV_HEADS):
            out_ref[hrows(g), :] += gate_row(branch, g) * _flash_out(acc_ref, g)

    _flash_init(m_ref, acc_ref)
    slc_tile(i, causal)

    def slc_body(jb, carry):
        slc_tile(jb, None)
        return carry

    lax.fori_loop(0, i, slc_body, 0)
    add_branch(1)

    _flash_init(m_ref, acc_ref)
    win_tile(dstart, causal)

    @pl.when(i >= 1)
    def _():
        win_tile(pl.multiple_of(t0 - tq, tq), None)

    @pl.when(i >= 2)
    def _():
        win_tile(pl.multiple_of(t0 - 2 * tq, tq), win_tail)

    add_branch(2)

    o_t = jnp.concatenate([out_ref[hrows(g), r * tq:(r + 1) * tq]
                           for g in range(C_KV_HEADS) for r in range(C_GROUP)], axis=0)
    o_ref[...] = o_t.T.astype(BF16)


def _nsa_constants():
    j = np.arange(128)[:, None]
    n = np.arange(128)[None, :]
    ovl_t = ((n * CMP_STRIDE < j * SLC_BLOCK + SLC_BLOCK) & (n * CMP_STRIDE + CMP_BLOCK > j * SLC_BLOCK)
             & (n < N_CMP) & (j < N_SLC)).astype(np.float32)
    expand_t = (np.arange(SEQ)[:, None] // SLC_BLOCK == np.arange(128)[None, :]).astype(np.float32)
    return jnp.asarray(ovl_t), jnp.asarray(expand_t, dtype=BF16)


def _nsa(qc, kcmp, vcmpt, ks, vst, kw, vwt, gt):
    bsz = qc.shape[0]
    tq = ATT_T
    ovl_t, expand_t = _nsa_constants()
    keys = pl.BlockSpec((None, C_KV_HEADS, SEQ, 64), lambda b, i: (b, 0, 0, 0))
    vals = pl.BlockSpec((None, C_KV_HEADS * VT_ROWS, SEQ), lambda b, i: (b, 0, 0))
    rows = C_GROUP * tq
    return pl.pallas_call(
        _nsa_kernel,
        grid=(bsz, SEQ // tq),
        in_specs=[pl.BlockSpec((None, tq, 512), lambda b, i: (b, i, 0)),
                  pl.BlockSpec((None, C_KV_HEADS, 128, 64), lambda b, i: (b, 0, 0, 0)),
                  pl.BlockSpec((None, C_KV_HEADS, 64, 128), lambda b, i: (b, 0, 0, 0)),
                  keys, vals, keys, vals,
                  pl.BlockSpec((None, 128, tq), lambda b, i: (b, 0, i)),
                  pl.BlockSpec((128, 128), lambda b, i: (0, 0)),
                  pl.BlockSpec((SEQ, 128), lambda b, i: (0, 0))],
        out_specs=pl.BlockSpec((None, tq, 512), lambda b, i: (b, i, 0)),
        out_shape=jax.ShapeDtypeStruct((bsz, SEQ, 512), BF16),
        scratch_shapes=[pltpu.VMEM((C_KV_HEADS, rows, 64), BF16), pltpu.VMEM((C_KV_HEADS, 128, tq), BF16),
                        pltpu.VMEM((8 * C_KV_HEADS, rows), F32),
                        pltpu.VMEM((C_KV_HEADS * VT_ROWS, rows), F32), pltpu.VMEM((C_KV_HEADS * 64, rows), F32)],
        compiler_params=_params(("parallel", "arbitrary")),
        name="nsa_attention",
    )(qc, kcmp, vcmpt, ks, vst, kw, vwt, gt, ovl_t, expand_t)


def _moba_kernel(q_ref, k_ref, vt_ref, kmean_ref, o_ref, qb_ref, sel_ref, m_ref, acc_ref):
    tq = ATT_T
    i = pl.program_id(1)
    blk = lax.broadcasted_iota(jnp.int32, (N_MOBA, 1), 0)
    kk = lax.broadcasted_iota(jnp.int32, (tq, tq), 0)
    qq = lax.broadcasted_iota(jnp.int32, (tq, tq), 1)
    causal = kk <= qq
    past = blk < i
    for h in range(D_HEADS):
        hs = slice(h * 64, (h + 1) * 64)
        qf = q_ref[:, hs]
        qb_ref[h] = qf.astype(BF16)
        gate = lax.dot_general(kmean_ref[:, hs], qf, (((1,), (1,)), ((), ())), precision=HIGHEST,
                               preferred_element_type=F32)
        gate = jnp.where(past, gate, -jnp.inf)
        sel_ref[h] = (past & (_rank_desc_rows(gate) < MOBA_TOPK)).astype(F32)

    def tile(jb, diag):
        start = pl.multiple_of(jb * tq, tq)
        heads = []
        for h in range(D_HEADS):
            if diag:
                mask = causal
            else:
                mask = jnp.sum(jnp.where(blk == jb, sel_ref[h], 0.0), axis=0, keepdims=True) > 0.5
            heads.append((qb_ref[h], k_ref[h, pl.ds(start, tq), :],
                          vt_ref[h * VT_ROWS:(h + 1) * VT_ROWS, pl.ds(start, tq)], mask))
        _flash_steps_t(heads, m_ref, acc_ref, 1)

    _flash_init(m_ref, acc_ref)
    tile(i, True)

    def body(jb, carry):
        tile(jb, False)
        return carry

    lax.fori_loop(0, i, body, 0)
    o_t = jnp.concatenate([_flash_out(acc_ref, h) for h in range(D_HEADS)], axis=0)
    o_ref[...] = o_t.T.astype(BF16)


def _moba(qd, kd, vdt, kmean):
    bsz = qd.shape[0]
    tq = ATT_T
    return pl.pallas_call(
        _moba_kernel,
        grid=(bsz, SEQ // tq),
        in_specs=[pl.BlockSpec((None, tq, 512), lambda b, i: (b, i, 0)),
                  pl.BlockSpec((None, D_HEADS, SEQ, 64), lambda b, i: (b, 0, 0, 0)),
                  pl.BlockSpec((None, D_HEADS * VT_ROWS, SEQ), lambda b, i: (b, 0, 0)),
                  pl.BlockSpec((None, N_MOBA, 512), lambda b, i: (b, 0, 0))],
        out_specs=pl.BlockSpec((None, tq, 512), lambda b, i: (b, i, 0)),
        out_shape=jax.ShapeDtypeStruct((bsz, SEQ, 512), BF16),
        scratch_shapes=[pltpu.VMEM((D_HEADS, tq, 64), BF16), pltpu.VMEM((D_HEADS, N_MOBA, tq), F32),
                        pltpu.VMEM((8 * D_HEADS, tq), F32), pltpu.VMEM((D_HEADS * VT_ROWS, tq), F32)],
        compiler_params=_params(("parallel", "arbitrary")),
        name="moba_attention",
    )(qd, kd, vdt, kmean.reshape(bsz, N_MOBA, 512))


def _merge_kernel(ua_ref, ub_ref, oc_ref, od_ref, mg_ref, x_ref, gt_ref,
                  wa_ref, wb_ref, wc_ref, wd_ref, wo_ref, o_ref):
    d = D_MODEL
    merged = jnp.zeros(x_ref.shape, F32)
    for k, (u_ref, w_ref) in enumerate(((ua_ref, wa_ref), (ub_ref, wb_ref), (oc_ref, wc_ref), (od_ref, wd_ref))):
        y = jnp.dot(u_ref[...], w_ref[...], preferred_element_type=F32)
        merged = merged + jax.nn.sigmoid(mg_ref[:, k * d:(k + 1) * d].astype(F32)) * y
    o_ref[...] = x_ref[...] + gt_ref[...] * jnp.dot(merged.astype(BF16), wo_ref[...], preferred_element_type=F32)


def _merge(ua, ub, oc, od, proj2, x2, mod3, wa, wb, wc, wd, wo, *, tm=512):
    t, d = x2.shape
    per_b = SEQ // tm
    act = pl.BlockSpec((tm, 512), lambda i: (i, 0))
    wspec = pl.BlockSpec((512, d), lambda i: (0, 0))
    return pl.pallas_call(
        _merge_kernel,
        grid=(t // tm,),
        in_specs=[act, act, act, act,
                  pl.BlockSpec((tm, 4 * d), lambda i: (i, COL_MERGE // (4 * d))),
                  pl.BlockSpec((tm, d), lambda i: (i, 0)),
                  pl.BlockSpec((None, 1, d), lambda i: (i // per_b, 0, 2)),
                  wspec, wspec, wspec, wspec,
                  pl.BlockSpec((d, d), lambda i: (0, 0))],
        out_specs=pl.BlockSpec((tm, d), lambda i: (i, 0)),
        out_shape=jax.ShapeDtypeStruct((t, d), F32),
        compiler_params=_params(("parallel",)),
        name="merge_out",
    )(ua, ub, oc, od, proj2, x2, mod3, wa, wb, wc, wd, wo)


def _pop_max(work, idx):
    m = jnp.max(work, axis=0, keepdims=True)
    first = jnp.min(jnp.where(work == m, idx, work.shape[0]), axis=0, keepdims=True)
    return m, idx == first


_PEER_PAIRS = [(i, j) for i in range(PEER_TOPK) for j in range(PEER_TOPK) if (i + 1) * (j + 1) <= PEER_TOPK]
_PEER_PAIR_ROWS = 56
NOT_TOP = 99.0


def _peer_sel_kernel(q_ref, keys_ref, grp_ref, n1_ref, e1_ref, r2_ref, e2_ref, st_ref):
    tt = q_ref.shape[0]
    half = PEER_QDIM // 2
    for hp in range(2 * PEER_HEADS):
        qh = q_ref[:, hp * half:(hp + 1) * half]
        qn = qh * lax.rsqrt(jnp.mean(qh * qh, axis=-1, keepdims=True) + NORM_EPS)
        st_ref[hp] = lax.dot_general(keys_ref[hp], qn, (((1,), (1,)), ((), ())), precision=HIGHEST,
                                     preferred_element_type=F32)

    idx = lax.broadcasted_iota(jnp.int32, (PEER_NKEYS, 128), 0)
    pidx = lax.broadcasted_iota(jnp.int32, (_PEER_PAIR_ROWS, 128), 0)
    pad_rows = jnp.full((_PEER_PAIR_ROWS - len(_PEER_PAIRS), 128), -jnp.inf, F32)

    def pop16(work, index, tie_safe, track):
        vals, order = [], jnp.full(work.shape, NOT_TOP, F32)
        for it in range(PEER_TOPK):
            if tie_safe:
                m, hit = _pop_max(work, index)
            else:
                m = jnp.max(work, axis=0, keepdims=True)
                hit = work == m
            work = jnp.where(hit, -jnp.inf, work)
            if track:
                order = jnp.where(hit, float(it), order)
            vals.append(m)
        removed = jnp.sum(jnp.where(work == -jnp.inf, 1.0, 0.0), axis=0, keepdims=True)
        return vals, order, removed

    def select(cs, tie_safe):
        wrong = jnp.zeros((1, 128), F32)
        n_pad = float(_PEER_PAIR_ROWS - len(_PEER_PAIRS))
        for h in range(PEER_HEADS):
            scores, tops, ranks = [], [], []
            for p in range(2):
                st = st_ref[2 * h + p, :, cs]
                vals, rank, removed = pop16(st, idx, tie_safe, track=tie_safe or p == 1)
                wrong = wrong + jnp.abs(removed - float(PEER_TOPK))
                scores.append(st)
                tops.append(vals)
                ranks.append(rank)
            cand = jnp.concatenate([tops[0][i] + tops[1][j] for (i, j) in _PEER_PAIRS] + [pad_rows], axis=0)
            best, order, removed = pop16(cand, pidx, tie_safe, track=tie_safe)
            wrong = wrong + jnp.abs(removed - (float(PEER_TOPK) + n_pad))
            picked = jnp.where((order < float(PEER_TOPK)) if tie_safe else (cand >= best[-1]), 1.0, 0.0)
            z = jnp.ones_like(best[0])
            for k in range(1, PEER_TOPK):
                z = z + jnp.exp(best[k] - best[0])
            count = jnp.dot(grp_ref[...], picked.astype(BF16), preferred_element_type=F32)
            n1 = jnp.zeros(scores[0].shape, F32)
            for i in reversed(range(PEER_TOPK)):
                at_i = (ranks[0] == float(i)) if tie_safe else (scores[0] >= tops[0][i])
                n1 = jnp.where(at_i, count[i:i + 1], n1)
            n1_ref[h, :, cs] = n1
            r2_ref[h, :, cs] = ranks[1]
            e1_ref[h, :, cs] = jnp.exp(scores[0] - tops[0][0]) / z
            e2_ref[h, :, cs] = jnp.exp(scores[1] - tops[1][0])
        return wrong

    def token_chunk(c, carry):
        cs = pl.ds(pl.multiple_of(c * 128, 128), 128)
        wrong = select(cs, tie_safe=False)

        @pl.when(jnp.max(wrong) > 0.0)
        def _():
            select(cs, tie_safe=True)

        return carry

    lax.fori_loop(0, tt // 128, token_chunk, 0)


def _peer_select(q2, subkeys, *, tt=512):
    t = q2.shape[0]
    grp = np.zeros((PEER_TOPK, _PEER_PAIR_ROWS), np.float32)
    for row, (i, _) in enumerate(_PEER_PAIRS):
        grp[i, row] = 1.0
    big = pl.BlockSpec((PEER_HEADS, PEER_NKEYS, tt), lambda i: (0, 0, i))
    bshape = jax.ShapeDtypeStruct((PEER_HEADS, PEER_NKEYS, t), F32)
    half = PEER_QDIM // 2
    return pl.pallas_call(
        _peer_sel_kernel,
        grid=(t // tt,),
        in_specs=[pl.BlockSpec((tt, PEER_HEADS * PEER_QDIM), lambda i: (i, 0)),
                  pl.BlockSpec((2 * PEER_HEADS, PEER_NKEYS, half), lambda i: (0, 0, 0)),
                  pl.BlockSpec((PEER_TOPK, _PEER_PAIR_ROWS), lambda i: (0, 0))],
        out_specs=[big, big, big, big],
        out_shape=[bshape, bshape, bshape, bshape],
        scratch_shapes=[pltpu.VMEM((2 * PEER_HEADS, PEER_NKEYS, tt), F32)],
        compiler_params=_params(("parallel",)),
        name="peer_select",
    )(q2, subkeys.reshape(2 * PEER_HEADS, PEER_NKEYS, half), jnp.asarray(grp, dtype=BF16))


PEER_SLAB = 256

def _peer_main_kernel(h_ref, u_ref, vt_ref, n1_ref, e1_ref, r2_ref, e2_ref, x_ref, gt_ref, fg_ref,
                      o_ref, acc_ref, act_ref, p_ref, ht_ref, *, final):
    j = pl.program_id(1)
    te, tt = act_ref.shape
    na = te // PEER_NKEYS

    @pl.when(j == 0)
    def _():
        acc_ref[...] = jnp.zeros(acc_ref.shape, F32)
        ht_ref[...] = h_ref[...].astype(F32).T.astype(BF16)

    a0 = pl.multiple_of(j * na, na)
    nslab = te // PEER_SLAB
    parts = []

    def act(s):
        ss = slice(s * PEER_SLAB, (s + 1) * PEER_SLAB)
        act_ref[ss, :] = jnp.dot(u_ref[ss, :], ht_ref[...], preferred_element_type=F32)

    def val(s):
        ss = slice(s * PEER_SLAB, (s + 1) * PEER_SLAB)
        parts.append(jnp.dot(vt_ref[:, ss], p_ref[ss, :], preferred_element_type=F32))

    def mask(al):
        rs = slice(al * PEER_NKEYS, (al + 1) * PEER_NKEYS)
        for c in range(tt // 128):
            cs = slice(c * 128, (c + 1) * 128)
            w = jnp.zeros((PEER_NKEYS, 128), F32)
            for hh in range(PEER_HEADS):
                n1 = n1_ref[hh, pl.ds(a0, na), cs][al:al + 1]
                e1 = e1_ref[hh, pl.ds(a0, na), cs][al:al + 1]
                w = w + jnp.where(r2_ref[hh, :, cs] < n1, e1 * e2_ref[hh, :, cs], 0.0)
            p_ref[rs, cs] = (w * _gelu(act_ref[rs, cs])).astype(BF16)

    per = PEER_SLAB // PEER_NKEYS
    act(0)
    act(1)
    for s in range(nslab):
        for k in range(per):
            mask(s * per + k)
            if k == 0 and s + 2 < nslab:
                act(s + 2)
        val(s)
    acc_ref[...] += functools.reduce(lambda x, y: x + y, parts)

    @pl.when(j == pl.num_programs(1) - 1)
    def _():
        y = x_ref[...] + gt_ref[...] * acc_ref[...].T
        if final:
            y = y * lax.rsqrt(jnp.mean(y * y, axis=-1, keepdims=True) + NORM_EPS) * fg_ref[...]
        o_ref[...] = y


def _peer_main(h2, u_bf, vt_bf, n1, e1, r2, e2, x2, mod3, final_g, *, final, tt=512, te=2048):
    t, d = x2.shape
    assert (te // PEER_NKEYS) % 8 == 0 and te % PEER_SLAB == 0
    per_b = SEQ // tt
    big = pl.BlockSpec((PEER_HEADS, PEER_NKEYS, tt), lambda i, j: (0, 0, i))
    return pl.pallas_call(
        functools.partial(_peer_main_kernel, final=final),
        grid=(t // tt, PEER_EXPERTS // te),
        in_specs=[pl.BlockSpec((tt, d), lambda i, j: (i, 0)),
                  pl.BlockSpec((te, d), lambda i, j: (j, 0)),
                  pl.BlockSpec((d, te), lambda i, j: (0, j)),
                  big, big, big, big,
                  pl.BlockSpec((tt, d), lambda i, j: (i, 0)),
                  pl.BlockSpec((None, 1, d), lambda i, j: (i // per_b, 0, 5)),
                  pl.BlockSpec((1, d), lambda i, j: (0, 0))],
        out_specs=pl.BlockSpec((tt, d), lambda i, j: (i, 0)),
        out_shape=jax.ShapeDtypeStruct((t, d), F32),
        scratch_shapes=[pltpu.VMEM((d, tt), F32), pltpu.VMEM((te, tt), F32), pltpu.VMEM((te, tt), BF16),
                        pltpu.VMEM((d, tt), BF16)],
        compiler_params=_params(("parallel", "arbitrary")),
        name="peer_experts",
    )(h2, u_bf, vt_bf, n1, e1, r2, e2, x2, mod3, final_g.reshape(1, d))


def _reorder_w_in(w):
    pad = jnp.zeros((w.shape[0], 1024 - 792), w.dtype)
    return jnp.concatenate([w[:, 5400:9496], w[:, 3072:3864], pad, w[:, 0:3072], w[:, 3864:5400]], axis=1)


def kernel(x, c, positions, mod_w, mod_b, norm_mix_g, norm_ffn_g, w_in, a_conv_w, a_out, b_conv_w, b_conv_b, b_ln_g, b_ln_b, b_out, c_cmp_pos, c_cmp_w1, c_cmp_w2, c_out, d_out, w_o, peer_wq, peer_subkeys, peer_u, peer_v, final_norm_g):
    bsz, s, d = x.shape
    assert s == SEQ and d == D_MODEL
    depth = mod_w.shape[0]
    t = bsz * s
    cos, sin = _rope_tables(positions)
    mod = _modulation(c, mod_w, mod_b)
    x2 = x.reshape(t, d)
    for l in range(depth):
        mod3 = mod[l].reshape(bsz, 1, 6 * d)
        proj2 = _norm_matmul(x2, norm_mix_g[l], mod3, 0, 1, _reorder_w_in(w_in[l]).astype(BF16), out_dtype=BF16,
                             tn=PROJ_COLS // 4)
        proj3 = proj2.reshape(bsz, s, PROJ_COLS)
        ua, ub = _conv_mixers(proj3, a_conv_w[l], b_conv_w[l], b_conv_b[l], b_ln_g[l], b_ln_b[l])
        qc, kc, vc, ks, vst, kw, vwt, gt, qd, kd, vdt, kmean = _prep(proj3, cos, sin)
        kcmp, vcmpt = _compress(kc, vc, c_cmp_pos[l], c_cmp_w1[l], c_cmp_w2[l])
        oc = _nsa(qc, kcmp, vcmpt, ks, vst, kw, vwt, gt)
        od = _moba(qd, kd, vdt, kmean)
        x2 = _merge(ua.reshape(t, 512), ub.reshape(t, 512), oc.reshape(t, 512), od.reshape(t, 512),
                    proj2, x2, mod3, a_out[l].astype(BF16), b_out[l].astype(BF16), c_out[l].astype(BF16),
                    d_out[l].astype(BF16), w_o[l].astype(BF16))
        q2, h2 = _norm_matmul(x2, norm_ffn_g[l], mod3, 3, 4, peer_wq[l].astype(BF16), emit_h=True,
                              tn=PEER_HEADS * PEER_QDIM)
        n1, e1, r2, e2 = _peer_select(q2, peer_subkeys[l])
        x2 = _peer_main(h2, peer_u[l].astype(BF16), peer_v[l].T.astype(BF16), n1, e1, r2, e2,
                        x2, mod3, final_norm_g, final=(l == depth - 1))
    return x2.reshape(bsz, s, d)
```

```python
import functools

import numpy as np
import jax
import jax.numpy as jnp
from jax import lax
from jax.experimental import pallas as pl
from jax.experimental.pallas import tpu as pltpu

F32 = jnp.float32
BF16 = jnp.bfloat16
HIGHEST = lax.Precision.HIGHEST

D_MODEL = 1024
SEQ = 2048
HEAD_DIM = 64
ROPE_THETA = 10000.0
NORM_EPS = 1e-6
A_WIDTH = 512
A_CONV = 3
B_WIDTH = 512
B_CONV = 31
C_HEADS = 8
C_KV_HEADS = 2
C_GROUP = 4
CMP_BLOCK = 32
CMP_STRIDE = 16
CMP_HIDDEN = 128
N_CMP = (SEQ - CMP_BLOCK) // CMP_STRIDE + 1
SLC_BLOCK = 64
SLC_TOPN = 16
N_SLC = SEQ // SLC_BLOCK
WIN = 512
D_HEADS = 8
MOBA_BLOCK = 256
MOBA_TOPK = 3
N_MOBA = SEQ // MOBA_BLOCK
PEER_HEADS = 8
PEER_NKEYS = 128
PEER_EXPERTS = PEER_NKEYS * PEER_NKEYS
PEER_QDIM = 256
PEER_TOPK = 16

PROJ_COLS = 9728
COL_MERGE = 0
COL_KVG = 4096
COL_GATE = COL_KVG + 768
COL_A = 5120

VMEM_LIMIT = 56 * 1024 * 1024
NEG = -1e30

ATT_T = 256
CONV_T = 256
HALO = 32


def _params(sem, flags=None):
    return pltpu.CompilerParams(dimension_semantics=sem, vmem_limit_bytes=VMEM_LIMIT, flags=flags)


def _gelu(x):
    return 0.5 * x * (1.0 + lax.erf(x * np.float32(np.sqrt(0.5))))


def _rope_table_kernel(pos_ref, inv_ref, sign_ref, cos_ref, sin_ref):
    ang = pos_ref[...] * inv_ref[...]
    cos_ref[...] = jnp.cos(ang)
    sin_ref[...] = jnp.sin(ang) * sign_ref[...]


def _rope_tables(positions):
    bsz, s = positions.shape
    inv = 1.0 / (ROPE_THETA ** (jnp.arange(0, HEAD_DIM, 2, dtype=F32) / HEAD_DIM))
    inv128 = jnp.tile(inv, 4)[None, :]
    sign = jnp.tile(jnp.concatenate([-jnp.ones(32, F32), jnp.ones(32, F32)]), 2)[None, :]
    pos = positions.astype(F32).reshape(bsz * s, 1)
    t = bsz * s
    cos, sin = pl.pallas_call(
        _rope_table_kernel,
        grid=(t // SEQ,),
        in_specs=[pl.BlockSpec((SEQ, 1), lambda i: (i, 0)),
                  pl.BlockSpec((1, 128), lambda i: (0, 0)),
                  pl.BlockSpec((1, 128), lambda i: (0, 0))],
        out_specs=[pl.BlockSpec((SEQ, 128), lambda i: (i, 0))] * 2,
        out_shape=[jax.ShapeDtypeStruct((t, 128), F32)] * 2,
        compiler_params=_params(("parallel",)),
        name="rope_tables",
    )(pos, inv128, sign)
    return cos.reshape(bsz, s, 128), sin.reshape(bsz, s, 128)


def _rope(x, cos, sin):
    w = x.shape[-1]
    lane = lax.broadcasted_iota(jnp.int32, x.shape, 1)
    swapped = jnp.where(lane % 64 < 32, pltpu.roll(x, w - 32, 1), pltpu.roll(x, 32, 1))
    return x * cos + swapped * sin


def _mod_kernel(c_ref, w_ref, b_ref, o_ref):
    c = c_ref[...]
    cond = c * jax.nn.sigmoid(c)
    o_ref[...] = jnp.dot(cond, w_ref[...], precision=HIGHEST, preferred_element_type=F32) + b_ref[...]


def _modulation(c, mod_w, mod_b):
    nl, d, n = mod_w.shape
    bsz = c.shape[0]
    tn = 1536
    return pl.pallas_call(
        _mod_kernel,
        grid=(nl, n // tn),
        in_specs=[pl.BlockSpec((bsz, d), lambda l, j: (0, 0)),
                  pl.BlockSpec((None, d, tn), lambda l, j: (l, 0, j)),
                  pl.BlockSpec((None, 1, tn), lambda l, j: (l, 0, j))],
        out_specs=pl.BlockSpec((None, bsz, tn), lambda l, j: (l, 0, j)),
        out_shape=jax.ShapeDtypeStruct((nl, bsz, n), F32),
        compiler_params=_params(("parallel", "parallel")),
        name="adaln_mod",
    )(c, mod_w, mod_b.reshape(nl, 1, n))


def _norm_matmul_kernel(x_ref, g_ref, sc_ref, sh_ref, w_ref, o_ref, *rest, emit_h):
    h_scr = rest[-1]

    @pl.when(pl.program_id(1) == 0)
    def _():
        x = x_ref[...]
        y = x * lax.rsqrt(jnp.mean(x * x, axis=-1, keepdims=True) + NORM_EPS)
        h = (y * g_ref[...]) * (1.0 + sc_ref[...]) + sh_ref[...]
        h_scr[...] = h.astype(BF16)
        if emit_h:
            rest[0][...] = h.astype(BF16)

    o_ref[...] = jnp.dot(h_scr[...], w_ref[...], preferred_element_type=F32).astype(o_ref.dtype)


def _norm_matmul(x2, g, mod3, sh_blk, sc_blk, w, *, out_dtype=F32, emit_h=False, tm=1024, tn=512):
    t, d = x2.shape
    n = w.shape[1]
    per_b = SEQ // tm
    out_shape = [jax.ShapeDtypeStruct((t, n), out_dtype)]
    out_specs = [pl.BlockSpec((tm, tn), lambda i, j: (i, j))]
    if emit_h:
        out_shape.append(jax.ShapeDtypeStruct((t, d), BF16))
        out_specs.append(pl.BlockSpec((tm, d), lambda i, j: (i, 0)))
    res = pl.pallas_call(
        functools.partial(_norm_matmul_kernel, emit_h=emit_h),
        grid=(t // tm, n // tn),
        in_specs=[pl.BlockSpec((tm, d), lambda i, j: (i, 0)),
                  pl.BlockSpec((1, d), lambda i, j: (0, 0)),
                  pl.BlockSpec((None, 1, d), lambda i, j: (i // per_b, 0, sc_blk)),
                  pl.BlockSpec((None, 1, d), lambda i, j: (i // per_b, 0, sh_blk)),
                  pl.BlockSpec((d, tn), lambda i, j: (0, j))],
        out_specs=out_specs,
        out_shape=out_shape,
        scratch_shapes=[pltpu.VMEM((tm, d), BF16)],
        compiler_params=_params(("parallel", "arbitrary")),
        name="norm_matmul",
    )(x2, g.reshape(1, d), mod3, mod3, w)
    return res if emit_h else res[0]


def _conv_kernel(ab_ref, ac_ref, ax_ref, ba_ref, bg_ref, pac_ref, pax_ref, pba_ref, pbg_ref,
                 aw_ref, bw_ref, bb_ref, lng_ref, lnb_ref, ua_ref, ub_ref, ext_ref, y_ref):
    ts = ab_ref.shape[0]
    keep = (pl.program_id(1) > 0).astype(F32)

    def f32(ref):
        return ref[...].astype(F32)

    def causal_conv(w_ref, taps, bias_ref, out_ref):
        blk = 128
        reach = ((taps - 1) // 8) * 8
        for t0 in range(0, ts, blk):
            for c0 in range(0, w_ref.shape[1], 128):
                cs = slice(c0, c0 + 128)
                acc = jnp.zeros((blk, 128), F32) if bias_ref is None else jnp.zeros((blk, 128), F32) + bias_ref[:, cs]
                for r in range(min(8, taps)):
                    base = HALO + t0 - reach - r
                    ur = ext_ref[base:base + blk + reach, cs]
                    for q in range(reach // 8 + 1):
                        shift = 8 * q + r
                        if shift < taps:
                            k = taps - 1 - shift
                            acc = acc + w_ref[k:k + 1, cs] * ur[reach - 8 * q:reach - 8 * q + blk]
                out_ref[t0:t0 + blk, cs] = acc

    ext_ref[0:HALO, :] = f32(pac_ref) * f32(pax_ref) * keep
    ext_ref[HALO:HALO + ts, :] = f32(ac_ref) * f32(ax_ref)
    causal_conv(aw_ref, A_CONV, None, y_ref)
    ua_ref[...] = (f32(ab_ref) * y_ref[...]).astype(BF16)

    ext_ref[0:HALO, :] = f32(pba_ref) * jax.nn.sigmoid(f32(pbg_ref)) * keep
    ext_ref[HALO:HALO + ts, :] = f32(ba_ref) * jax.nn.sigmoid(f32(bg_ref))
    causal_conv(bw_ref, B_CONV, bb_ref, y_ref)
    acc = y_ref[...]
    mu = jnp.mean(acc, axis=-1, keepdims=True)
    cen = acc - mu
    var = jnp.mean(cen * cen, axis=-1, keepdims=True)
    y = cen * lax.rsqrt(var + NORM_EPS) * lng_ref[...] + lnb_ref[...]
    ub_ref[...] = (y * jax.nn.sigmoid(y)).astype(BF16)


def _conv_mixers(proj3, a_conv_w, b_conv_w, b_conv_b, b_ln_g, b_ln_b):
    bsz = proj3.shape[0]
    ts = CONV_T
    c0 = COL_A // 512
    r = ts // HALO

    def cur(k):
        return pl.BlockSpec((None, ts, 512), lambda b, i, k=k: (b, i, c0 + k))

    def prev(k):
        return pl.BlockSpec((None, HALO, 512), lambda b, i, k=k: (b, jnp.maximum(i * r - 1, 0), c0 + k))

    def full(shape):
        return pl.BlockSpec(shape, lambda b, i: (0,) * len(shape))

    return pl.pallas_call(
        _conv_kernel,
        grid=(bsz, SEQ // ts),
        in_specs=[cur(0), cur(1), cur(2), cur(3), cur(4), prev(1), prev(2), prev(3), prev(4),
                  full((A_CONV, A_WIDTH)), full((B_CONV, B_WIDTH)), full((1, B_WIDTH)),
                  full((1, B_WIDTH)), full((1, B_WIDTH))],
        out_specs=[pl.BlockSpec((None, ts, 512), lambda b, i: (b, i, 0))] * 2,
        out_shape=[jax.ShapeDtypeStruct((bsz, SEQ, 512), BF16)] * 2,
        scratch_shapes=[pltpu.VMEM((HALO + ts, 512), F32), pltpu.VMEM((ts, 512), F32)],
        compiler_params=_params(("parallel", "arbitrary")),
        name="conv_mixers",
    )(proj3, proj3, proj3, proj3, proj3, proj3, proj3, proj3, proj3,
      a_conv_w, b_conv_w, b_conv_b.reshape(1, -1), b_ln_g.reshape(1, -1), b_ln_b.reshape(1, -1))


VT_ROWS = 80


def _values_t(v, n_heads):
    rows = v.shape[0]
    vt = v.T
    tail = (lax.broadcasted_iota(jnp.int32, (VT_ROWS - HEAD_DIM, rows), 0) == 0).astype(F32)
    blocks = []
    for h in range(n_heads):
        blocks += [vt[h * HEAD_DIM:(h + 1) * HEAD_DIM], tail]
    return jnp.concatenate(blocks, axis=0).astype(BF16)


def _prep_kernel(cq_ref, dq_ref, dk_ref, dv_ref, kvg_ref, cos_ref, sin_ref,
                 qc_ref, kc_ref, vc_ref, ks_ref, vst_ref, kw_ref, vwt_ref, gt_ref,
                 qd_ref, kd_ref, vdt_ref, kmean_ref):
    cos = cos_ref[...]
    sin = sin_ref[...]
    cos4 = jnp.concatenate([cos] * 4, axis=1)
    sin4 = jnp.concatenate([sin] * 4, axis=1)
    scale = np.float32(HEAD_DIM ** -0.5 * np.log2(np.e))
    qc_ref[...] = _rope(cq_ref[...].astype(F32), cos4, sin4) * scale
    qd_ref[...] = _rope(dq_ref[...].astype(F32), cos4, sin4) * scale
    kd = _rope(dk_ref[...].astype(F32), cos4, sin4)
    for h in range(D_HEADS):
        kd_ref[h] = kd[:, h * 64:(h + 1) * 64].astype(BF16)
    kmean_ref[...] = jnp.mean(kd, axis=0, keepdims=True)
    vdt_ref[...] = _values_t(dv_ref[...].astype(F32), D_HEADS)

    def kvg(k):
        return kvg_ref[:, k * 128:(k + 1) * 128].astype(F32)

    kc = _rope(kvg(0), cos, sin)
    vc = kvg(1)
    ks = _rope(kvg(2), cos, sin)
    kw = _rope(kvg(4), cos, sin)
    for g in range(C_KV_HEADS):
        gs = slice(g * 64, (g + 1) * 64)
        kc_ref[g] = kc[:, gs]
        vc_ref[g] = vc[:, gs]
        ks_ref[g] = ks[:, gs].astype(BF16)
        kw_ref[g] = kw[:, gs].astype(BF16)
    vst_ref[...] = _values_t(kvg(3), C_KV_HEADS)
    vwt_ref[...] = _values_t(kvg(5), C_KV_HEADS)
    gt_ref[...] = jax.nn.sigmoid(kvg(6)).T


def _prep(proj3, cos, sin):
    bsz = proj3.shape[0]
    ts = MOBA_BLOCK
    c0 = COL_A // 512

    def col512(k):
        return pl.BlockSpec((None, ts, 512), lambda b, i: (b, i, c0 + k))

    row128 = pl.BlockSpec((None, ts, 128), lambda b, i: (b, i, 0))
    row512 = pl.BlockSpec((None, ts, 512), lambda b, i: (b, i, 0))
    col128t = pl.BlockSpec((None, 128, ts), lambda b, i: (b, 0, i))
    val2t = pl.BlockSpec((None, C_KV_HEADS * VT_ROWS, ts), lambda b, i: (b, 0, i))
    head64 = pl.BlockSpec((None, C_KV_HEADS, ts, 64), lambda b, i: (b, 0, i, 0))
    k64 = jax.ShapeDtypeStruct((bsz, C_KV_HEADS, SEQ, 64), BF16)
    t128 = jax.ShapeDtypeStruct((bsz, C_KV_HEADS * VT_ROWS, SEQ), BF16)
    return pl.pallas_call(
        _prep_kernel,
        grid=(bsz, SEQ // ts),
        in_specs=[col512(5), col512(6), col512(7), col512(8),
                  pl.BlockSpec((None, ts, 1024), lambda b, i: (b, i, COL_KVG // 1024)),
                  row128, row128],
        out_specs=[row512, head64, head64, head64, val2t, head64, val2t, col128t,
                   row512,
                   pl.BlockSpec((None, D_HEADS, ts, 64), lambda b, i: (b, 0, i, 0)),
                   pl.BlockSpec((None, D_HEADS * VT_ROWS, ts), lambda b, i: (b, 0, i)),
                   pl.BlockSpec((None, None, 1, 512), lambda b, i: (b, i, 0, 0))],
        out_shape=[jax.ShapeDtypeStruct((bsz, SEQ, 512), F32),
                   jax.ShapeDtypeStruct((bsz, C_KV_HEADS, SEQ, 64), F32),
                   jax.ShapeDtypeStruct((bsz, C_KV_HEADS, SEQ, 64), F32),
                   k64, t128, k64, t128,
                   jax.ShapeDtypeStruct((bsz, 128, SEQ), F32),
                   jax.ShapeDtypeStruct((bsz, SEQ, 512), F32),
                   jax.ShapeDtypeStruct((bsz, D_HEADS, SEQ, 64), BF16),
                   jax.ShapeDtypeStruct((bsz, D_HEADS * VT_ROWS, SEQ), BF16),
                   jax.ShapeDtypeStruct((bsz, N_MOBA, 1, 512), F32)],
        compiler_params=_params(("parallel", "parallel")),
        name="attn_prep",
    )(proj3, proj3, proj3, proj3, proj3, cos, sin)


def _compress_kernel(kc_ref, vc_ref, pos_ref, w1_ref, w2_ref, kcmp_ref, vcmp_ref):
    half = CMP_STRIDE * HEAD_DIM
    row = lax.broadcasted_iota(jnp.int32, (128, CMP_HIDDEN), 0)
    for which, (src, dst) in enumerate(((kc_ref, kcmp_ref), (vc_ref, vcmp_ref))):
        bias = jnp.dot(pos_ref[which], w1_ref[which], precision=HIGHEST, preferred_element_type=F32)[0:1]
        for g in range(C_KV_HEADS):
            chunks = src[g]
            d1 = jnp.dot(chunks, w1_ref[which, 0:half, :], precision=HIGHEST, preferred_element_type=F32)
            d2 = jnp.dot(chunks, w1_ref[which, half:2 * half, :], precision=HIGHEST, preferred_element_type=F32)
            d2 = jnp.where(row < 127, pltpu.roll(d2, 127, 0), 0.0)
            hid = _gelu(d1 + d2 + bias)
            out = jnp.dot(hid, w2_ref[which], precision=HIGHEST, preferred_element_type=F32)
            dst[g] = out if which == 0 else out.T


def _compress(kc, vc, cmp_pos, cmp_w1, cmp_w2):
    bsz = kc.shape[0]
    kc4 = kc.reshape(bsz, C_KV_HEADS, SEQ // CMP_STRIDE, CMP_STRIDE * HEAD_DIM)
    vc4 = vc.reshape(bsz, C_KV_HEADS, SEQ // CMP_STRIDE, CMP_STRIDE * HEAD_DIM)
    pos8 = jnp.broadcast_to(cmp_pos.reshape(2, 1, CMP_BLOCK * HEAD_DIM), (2, 8, CMP_BLOCK * HEAD_DIM))
    blk = pl.BlockSpec((None, C_KV_HEADS, 128, 1024), lambda b: (b, 0, 0, 0))
    out = pl.BlockSpec((None, C_KV_HEADS, 128, 64), lambda b: (b, 0, 0, 0))
    return pl.pallas_call(
        _compress_kernel,
        grid=(bsz,),
        in_specs=[blk, blk,
                  pl.BlockSpec((2, 8, 2048), lambda b: (0, 0, 0)),
                  pl.BlockSpec((2, 2048, CMP_HIDDEN), lambda b: (0, 0, 0)),
                  pl.BlockSpec((2, CMP_HIDDEN, 64), lambda b: (0, 0, 0))],
        out_specs=[out, pl.BlockSpec((None, C_KV_HEADS, 64, 128), lambda b: (b, 0, 0, 0))],
        out_shape=[jax.ShapeDtypeStruct((bsz, C_KV_HEADS, 128, 64), F32),
                   jax.ShapeDtypeStruct((bsz, C_KV_HEADS, 64, 128), F32)],
        compiler_params=_params(("parallel",)),
        name="nsa_compress",
    )(kc4, vc4, pos8, cmp_w1, cmp_w2)


def _flash_steps_t(heads, m_ref, acc_ref, groups):
    scores = [lax.dot_general(k, qb, (((1,), (1,)), ((), ())), preferred_element_type=F32)
              for (qb, k, _, _) in heads]
    probs = []
    for n, (qb, _, _, mask) in enumerate(heads):
        s = scores[n]
        tq = qb.shape[0] // groups
        if mask is not None:
            s = jnp.concatenate([jnp.where(mask, s[:, r * tq:(r + 1) * tq], NEG) for r in range(groups)], axis=1)
        sr = _stat_row(n)
        m_old = m_ref[sr, :]
        m_new = jnp.maximum(m_old, jnp.max(s, axis=0, keepdims=True))
        m_ref[sr, :] = m_new
        probs.append((jnp.exp2(m_old - m_new), jnp.exp2(s - m_new).astype(BF16)))
    for n, (_, _, vt, _) in enumerate(heads):
        alpha, p = probs[n]
        a = slice(n * VT_ROWS, (n + 1) * VT_ROWS)
        acc_ref[a, :] = alpha * acc_ref[a, :] + jnp.dot(vt, p, preferred_element_type=F32)


def _flash_out(acc_ref, n):
    return acc_ref[n * VT_ROWS:n * VT_ROWS + HEAD_DIM, :] / acc_ref[n * VT_ROWS + HEAD_DIM:n * VT_ROWS + HEAD_DIM + 1, :]


def _stat_row(row):
    return slice(8 * row, 8 * row + 1)


def _flash_init(m_ref, acc_ref):
    m_ref[...] = jnp.full(m_ref.shape, NEG, F32)
    acc_ref[...] = jnp.zeros(acc_ref.shape, F32)


def _rank_desc_rows(vals):
    n = vals.shape[0]
    row = lax.broadcasted_iota(jnp.int32, vals.shape, 0)
    rank = jnp.zeros(vals.shape, jnp.int32)
    for i in range(n):
        vi = vals[i:i + 1, :]
        ahead = (vi > vals) | ((vi == vals) & (row > i))
        rank = rank + ahead.astype(jnp.int32)
    return rank


def _nsa_kernel(q_ref, kcmp_ref, vcmpt_ref, ks_ref, vst_ref, kw_ref, vwt_ref, gt_ref, ovlt_ref, expt_ref,
                o_ref, qb_ref, sel_ref, m_ref, acc_ref, out_ref):
    tq = ATT_T
    i = pl.program_id(1)
    t0 = i * tq
    dstart = pl.multiple_of(t0, tq)
    t_row = t0 + lax.broadcasted_iota(jnp.int32, (1, tq), 1)
    t_rows = jnp.concatenate([t_row] * C_GROUP, axis=1)
    blk = lax.broadcasted_iota(jnp.int32, (128, 1), 0)
    kk = lax.broadcasted_iota(jnp.int32, (tq, tq), 0)
    qq = lax.broadcasted_iota(jnp.int32, (tq, tq), 1)
    causal = kk <= qq
    win_tail = kk > qq

    def gate_row(branch, g):
        base = branch * 8 + g * C_GROUP
        return jnp.concatenate([gt_ref[base + r:base + r + 1, :] for r in range(C_GROUP)], axis=1)

    def hrows(g):
        return slice(g * HEAD_DIM, (g + 1) * HEAD_DIM)

    def vrows(g):
        return slice(g * VT_ROWS, (g + 1) * VT_ROWS)

    for g in range(C_KV_HEADS):
        qf = jnp.concatenate([q_ref[:, (g * 4 + r) * 64:(g * 4 + r + 1) * 64] for r in range(C_GROUP)], axis=0)
        qb_ref[g] = qf.astype(BF16)

        s = lax.dot_general(kcmp_ref[g], qf, (((1,), (1,)), ((), ())), precision=HIGHEST,
                            preferred_element_type=F32)
        vis = (blk * CMP_STRIDE + (CMP_BLOCK - 1)) <= t_rows
        sm = jnp.where(vis, s, NEG)
        e = jnp.where(vis, jnp.exp2(sm - jnp.max(sm, axis=0, keepdims=True)), 0.0)
        p = e / jnp.maximum(jnp.sum(e, axis=0, keepdims=True), 1e-30)
        o_cmp = jnp.dot(vcmpt_ref[g].astype(BF16), p.astype(BF16), preferred_element_type=F32)
        out_ref[hrows(g), :] = gate_row(0, g) * o_cmp

        psum = p[:, 0:tq] + p[:, tq:2 * tq] + p[:, 2 * tq:3 * tq] + p[:, 3 * tq:4 * tq]
        imp = jnp.dot(ovlt_ref[...], psum, precision=HIGHEST, preferred_element_type=F32)[0:N_SLC]
        b32 = blk[0:N_SLC]
        cur = t_row // SLC_BLOCK
        forced = (b32 == 0) | (b32 == cur) | (b32 == cur - 1)
        imp = jnp.where(forced, jnp.inf, jnp.where(b32 * SLC_BLOCK > t_row, -jnp.inf, imp))
        sel = (_rank_desc_rows(imp) < SLC_TOPN).astype(F32)
        sel_ref[g] = jnp.concatenate([sel, jnp.zeros((128 - N_SLC, tq), F32)], axis=0).astype(BF16)

    def slc_tile(jb, extra):
        start = pl.multiple_of(jb * tq, tq)
        heads = []
        for g in range(C_KV_HEADS):
            hit = jnp.dot(expt_ref[pl.ds(start, tq), :], sel_ref[g], preferred_element_type=F32) > 0.5
            mask = hit if extra is None else hit & extra
            heads.append((qb_ref[g], ks_ref[g, pl.ds(start, tq), :], vst_ref[vrows(g), pl.ds(start, tq)], mask))
        _flash_steps_t(heads, m_ref, acc_ref, C_GROUP)

    def win_tile(start, mask):
        heads = [(qb_ref[g], kw_ref[g, pl.ds(start, tq), :], vwt_ref[vrows(g), pl.ds(start, tq)], mask)
                 for g in range(C_KV_HEADS)]
        _flash_steps_t(heads, m_ref, acc_ref, C_GROUP)

    def add_branch(branch):
        for g in range(C_KV_HEADS):
            out_ref[hrows(g), :] += gate_row(branch, g) * _flash_out(acc_ref, g)

    _flash_init(m_ref, acc_ref)
    slc_tile(i, causal)

    def slc_body(jb, carry):
        slc_tile(jb, None)
        return carry

    lax.fori_loop(0, i, slc_body, 0)
    add_branch(1)

    _flash_init(m_ref, acc_ref)
    win_tile(dstart, causal)

    @pl.when(i >= 1)
    def _():
        win_tile(pl.multiple_of(t0 - tq, tq), None)

    @pl.when(i >= 2)
    def _():
        win_tile(pl.multiple_of(t0 - 2 * tq, tq), win_tail)

    add_branch(2)

    o_t = jnp.concatenate([out_ref[hrows(g), r * tq:(r + 1) * tq]
                           for g in range(C_KV_HEADS) for r in range(C_GROUP)], axis=0)
    o_ref[...] = o_t.T.astype(BF16)


def _nsa_constants():
    j = np.arange(128)[:, None]
    n = np.arange(128)[None, :]
    ovl_t = ((n * CMP_STRIDE < j * SLC_BLOCK + SLC_BLOCK) & (n * CMP_STRIDE + CMP_BLOCK > j * SLC_BLOCK)
             & (n < N_CMP) & (j < N_SLC)).astype(np.float32)
    expand_t = (np.arange(SEQ)[:, None] // SLC_BLOCK == np.arange(128)[None, :]).astype(np.float32)
    return jnp.asarray(ovl_t), jnp.asarray(expand_t, dtype=BF16)


def _nsa(qc, kcmp, vcmpt, ks, vst, kw, vwt, gt):
    bsz = qc.shape[0]
    tq = ATT_T
    ovl_t, expand_t = _nsa_constants()
    keys = pl.BlockSpec((None, C_KV_HEADS, SEQ, 64), lambda b, i: (b, 0, 0, 0))
    vals = pl.BlockSpec((None, C_KV_HEADS * VT_ROWS, SEQ), lambda b, i: (b, 0, 0))
    rows = C_GROUP * tq
    return pl.pallas_call(
        _nsa_kernel,
        grid=(bsz, SEQ // tq),
        in_specs=[pl.BlockSpec((None, tq, 512), lambda b, i: (b, i, 0)),
                  pl.BlockSpec((None, C_KV_HEADS, 128, 64), lambda b, i: (b, 0, 0, 0)),
                  pl.BlockSpec((None, C_KV_HEADS, 64, 128), lambda b, i: (b, 0, 0, 0)),
                  keys, vals, keys, vals,
                  pl.BlockSpec((None, 128, tq), lambda b, i: (b, 0, i)),
                  pl.BlockSpec((128, 128), lambda b, i: (0, 0)),
                  pl.BlockSpec((SEQ, 128), lambda b, i: (0, 0))],
        out_specs=pl.BlockSpec((None, tq, 512), lambda b, i: (b, i, 0)),
        out_shape=jax.ShapeDtypeStruct((bsz, SEQ, 512), BF16),
        scratch_shapes=[pltpu.VMEM((C_KV_HEADS, rows, 64), BF16), pltpu.VMEM((C_KV_HEADS, 128, tq), BF16),
                        pltpu.VMEM((8 * C_KV_HEADS, rows), F32),
                        pltpu.VMEM((C_KV_HEADS * VT_ROWS, rows), F32), pltpu.VMEM((C_KV_HEADS * 64, rows), F32)],
        compiler_params=_params(("parallel", "arbitrary")),
        name="nsa_attention",
    )(qc, kcmp, vcmpt, ks, vst, kw, vwt, gt, ovl_t, expand_t)


def _moba_kernel(q_ref, k_ref, vt_ref, kmean_ref, o_ref, qb_ref, sel_ref, m_ref, acc_ref):
    tq = ATT_T
    i = pl.program_id(1)
    blk = lax.broadcasted_iota(jnp.int32, (N_MOBA, 1), 0)
    kk = lax.broadcasted_iota(jnp.int32, (tq, tq), 0)
    qq = lax.broadcasted_iota(jnp.int32, (tq, tq), 1)
    causal = kk <= qq
    past = blk < i
    for h in range(D_HEADS):
        hs = slice(h * 64, (h + 1) * 64)
        qf = q_ref[:, hs]
        qb_ref[h] = qf.astype(BF16)
        gate = lax.dot_general(kmean_ref[:, hs], qf, (((1,), (1,)), ((), ())), precision=HIGHEST,
                               preferred_element_type=F32)
        gate = jnp.where(past, gate, -jnp.inf)
        sel_ref[h] = (past & (_rank_desc_rows(gate) < MOBA_TOPK)).astype(F32)

    def tile(jb, diag):
        start = pl.multiple_of(jb * tq, tq)
        heads = []
        for h in range(D_HEADS):
            if diag:
                mask = causal
            else:
                mask = jnp.sum(jnp.where(blk == jb, sel_ref[h], 0.0), axis=0, keepdims=True) > 0.5
            heads.append((qb_ref[h], k_ref[h, pl.ds(start, tq), :],
                          vt_ref[h * VT_ROWS:(h + 1) * VT_ROWS, pl.ds(start, tq)], mask))
        _flash_steps_t(heads, m_ref, acc_ref, 1)

    _flash_init(m_ref, acc_ref)
    tile(i, True)

    def body(jb, carry):
        tile(jb, False)
        return carry

    lax.fori_loop(0, i, body, 0)
    o_t = jnp.concatenate([_flash_out(acc_ref, h) for h in range(D_HEADS)], axis=0)
    o_ref[...] = o_t.T.astype(BF16)


def _moba(qd, kd, vdt, kmean):
    bsz = qd.shape[0]
    tq = ATT_T
    return pl.pallas_call(
        _moba_kernel,
        grid=(bsz, SEQ // tq),
        in_specs=[pl.BlockSpec((None, tq, 512), lambda b, i: (b, i, 0)),
                  pl.BlockSpec((None, D_HEADS, SEQ, 64), lambda b, i: (b, 0, 0, 0)),
                  pl.BlockSpec((None, D_HEADS * VT_ROWS, SEQ), lambda b, i: (b, 0, 0)),
                  pl.BlockSpec((None, N_MOBA, 512), lambda b, i: (b, 0, 0))],
        out_specs=pl.BlockSpec((None, tq, 512), lambda b, i: (b, i, 0)),
        out_shape=jax.ShapeDtypeStruct((bsz, SEQ, 512), BF16),
        scratch_shapes=[pltpu.VMEM((D_HEADS, tq, 64), BF16), pltpu.VMEM((D_HEADS, N_MOBA, tq), F32),
                        pltpu.VMEM((8 * D_HEADS, tq), F32), pltpu.VMEM((D_HEADS * VT_ROWS, tq), F32)],
        compiler_params=_params(("parallel", "arbitrary")),
        name="moba_attention",
    )(qd, kd, vdt, kmean.reshape(bsz, N_MOBA, 512))


def _merge_kernel(ua_ref, ub_ref, oc_ref, od_ref, mg_ref, x_ref, gt_ref,
                  wa_ref, wb_ref, wc_ref, wd_ref, wo_ref, o_ref):
    d = D_MODEL
    merged = jnp.zeros(x_ref.shape, F32)
    for k, (u_ref, w_ref) in enumerate(((ua_ref, wa_ref), (ub_ref, wb_ref), (oc_ref, wc_ref), (od_ref, wd_ref))):
        y = jnp.dot(u_ref[...], w_ref[...], preferred_element_type=F32)
        merged = merged + jax.nn.sigmoid(mg_ref[:, k * d:(k + 1) * d].astype(F32)) * y
    o_ref[...] = x_ref[...] + gt_ref[...] * jnp.dot(merged.astype(BF16), wo_ref[...], preferred_element_type=F32)


def _merge(ua, ub, oc, od, proj2, x2, mod3, wa, wb, wc, wd, wo, *, tm=512):
    t, d = x2.shape
    per_b = SEQ // tm
    act = pl.BlockSpec((tm, 512), lambda i: (i, 0))
    wspec = pl.BlockSpec((512, d), lambda i: (0, 0))
    return pl.pallas_call(
        _merge_kernel,
        grid=(t // tm,),
        in_specs=[act, act, act, act,
                  pl.BlockSpec((tm, 4 * d), lambda i: (i, COL_MERGE // (4 * d))),
                  pl.BlockSpec((tm, d), lambda i: (i, 0)),
                  pl.BlockSpec((None, 1, d), lambda i: (i // per_b, 0, 2)),
                  wspec, wspec, wspec, wspec,
                  pl.BlockSpec((d, d), lambda i: (0, 0))],
        out_specs=pl.BlockSpec((tm, d), lambda i: (i, 0)),
        out_shape=jax.ShapeDtypeStruct((t, d), F32),
        compiler_params=_params(("parallel",)),
        name="merge_out",
    )(ua, ub, oc, od, proj2, x2, mod3, wa, wb, wc, wd, wo)


def _pop_max(work, idx):
    m = jnp.max(work, axis=0, keepdims=True)
    first = jnp.min(jnp.where(work == m, idx, work.shape[0]), axis=0, keepdims=True)
    return m, idx == first


_PEER_PAIRS = [(i, j) for i in range(PEER_TOPK) for j in range(PEER_TOPK) if (i + 1) * (j + 1) <= PEER_TOPK]
_PEER_PAIR_ROWS = 56
NOT_TOP = 99.0


def _peer_sel_kernel(q_ref, keys_ref, grp_ref, n1_ref, e1_ref, r2_ref, e2_ref, st_ref):
    tt = q_ref.shape[0]
    half = PEER_QDIM // 2
    for hp in range(2 * PEER_HEADS):
        qh = q_ref[:, hp * half:(hp + 1) * half]
        qn = qh * lax.rsqrt(jnp.mean(qh * qh, axis=-1, keepdims=True) + NORM_EPS)
        st_ref[hp] = lax.dot_general(keys_ref[hp], qn, (((1,), (1,)), ((), ())), precision=HIGHEST,
                                     preferred_element_type=F32)

    idx = lax.broadcasted_iota(jnp.int32, (PEER_NKEYS, 128), 0)
    pidx = lax.broadcasted_iota(jnp.int32, (_PEER_PAIR_ROWS, 128), 0)
    pad_rows = jnp.full((_PEER_PAIR_ROWS - len(_PEER_PAIRS), 128), -jnp.inf, F32)

    def pop16(work, index, tie_safe, track):
        vals, order = [], jnp.full(work.shape, NOT_TOP, F32)
        for it in range(PEER_TOPK):
            if tie_safe:
                m, hit = _pop_max(work, index)
            else:
                m = jnp.max(work, axis=0, keepdims=True)
                hit = work == m
            work = jnp.where(hit, -jnp.inf, work)
            if track:
                order = jnp.where(hit, float(it), order)
            vals.append(m)
        removed = jnp.sum(jnp.where(work == -jnp.inf, 1.0, 0.0), axis=0, keepdims=True)
        return vals, order, removed

    def select(cs, tie_safe):
        wrong = jnp.zeros((1, 128), F32)
        n_pad = float(_PEER_PAIR_ROWS - len(_PEER_PAIRS))
        for h in range(PEER_HEADS):
            scores, tops, ranks = [], [], []
            for p in range(2):
                st = st_ref[2 * h + p, :, cs]
                vals, rank, removed = pop16(st, idx, tie_safe, track=tie_safe or p == 1)
                wrong = wrong + jnp.abs(removed - float(PEER_TOPK))
                scores.append(st)
                tops.append(vals)
                ranks.append(rank)
            cand = jnp.concatenate([tops[0][i] + tops[1][j] for (i, j) in _PEER_PAIRS] + [pad_rows], axis=0)
            best, order, removed = pop16(cand, pidx, tie_safe, track=tie_safe)
            wrong = wrong + jnp.abs(removed - (float(PEER_TOPK) + n_pad))
            picked = jnp.where((order < float(PEER_TOPK)) if tie_safe else (cand >= best[-1]), 1.0, 0.0)
            z = jnp.ones_like(best[0])
            for k in range(1, PEER_TOPK):
                z = z + jnp.exp(best[k] - best[0])
            count = jnp.dot(grp_ref[...], picked.astype(BF16), preferred_element_type=F32)
            n1 = jnp.zeros(scores[0].shape, F32)
            for i in reversed(range(PEER_TOPK)):
                at_i = (ranks[0] == float(i)) if tie_safe else (scores[0] >= tops[0][i])
                n1 = jnp.where(at_i, count[i:i + 1], n1)
            n1_ref[h, :, cs] = n1
            r2_ref[h, :, cs] = ranks[1]
            e1_ref[h, :, cs] = jnp.exp(scores[0] - tops[0][0]) / z
            e2_ref[h, :, cs] = jnp.exp(scores[1] - tops[1][0])
        return wrong

    def token_chunk(c, carry):
        cs = pl.ds(pl.multiple_of(c * 128, 128), 128)
        wrong = select(cs, tie_safe=False)

        @pl.when(jnp.max(wrong) > 0.0)
        def _():
            select(cs, tie_safe=True)

        return carry

    lax.fori_loop(0, tt // 128, token_chunk, 0)


def _peer_select(q2, subkeys, *, tt=512):
    t = q2.shape[0]
    grp = np.zeros((PEER_TOPK, _PEER_PAIR_ROWS), np.float32)
    for row, (i, _) in enumerate(_PEER_PAIRS):
        grp[i, row] = 1.0
    big = pl.BlockSpec((PEER_HEADS, PEER_NKEYS, tt), lambda i: (0, 0, i))
    bshape = jax.ShapeDtypeStruct((PEER_HEADS, PEER_NKEYS, t), F32)
    half = PEER_QDIM // 2
    return pl.pallas_call(
        _peer_sel_kernel,
        grid=(t // tt,),
        in_specs=[pl.BlockSpec((tt, PEER_HEADS * PEER_QDIM), lambda i: (i, 0)),
                  pl.BlockSpec((2 * PEER_HEADS, PEER_NKEYS, half), lambda i: (0, 0, 0)),
                  pl.BlockSpec((PEER_TOPK, _PEER_PAIR_ROWS), lambda i: (0, 0))],
        out_specs=[big, big, big, big],
        out_shape=[bshape, bshape, bshape, bshape],
        scratch_shapes=[pltpu.VMEM((2 * PEER_HEADS, PEER_NKEYS, tt), F32)],
        compiler_params=_params(("parallel",)),
        name="peer_select",
    )(q2, subkeys.reshape(2 * PEER_HEADS, PEER_NKEYS, half), jnp.asarray(grp, dtype=BF16))


PEER_SLAB = 256

def _peer_main_kernel(h_ref, u_ref, vt_ref, n1_ref, e1_ref, r2_ref, e2_ref, x_ref, gt_ref, fg_ref,
                      o_ref, acc_ref, act_ref, p_ref, ht_ref, *, final):
    j = pl.program_id(1)
    te, tt = act_ref.shape
    na = te // PEER_NKEYS

    @pl.when(j == 0)
    def _():
        acc_ref[...] = jnp.zeros(acc_ref.shape, F32)
        ht_ref[...] = h_ref[...].astype(F32).T.astype(BF16)

    a0 = pl.multiple_of(j * na, na)
    nslab = te // PEER_SLAB
    parts = []

    def act(s):
        ss = slice(s * PEER_SLAB, (s + 1) * PEER_SLAB)
        act_ref[ss, :] = jnp.dot(u_ref[ss, :], ht_ref[...], preferred_element_type=F32)

    def val(s):
        ss = slice(s * PEER_SLAB, (s + 1) * PEER_SLAB)
        parts.append(jnp.dot(vt_ref[:, ss], p_ref[ss, :], preferred_element_type=F32))

    def mask(al):
        rs = slice(al * PEER_NKEYS, (al + 1) * PEER_NKEYS)
        for c in range(tt // 128):
            cs = slice(c * 128, (c + 1) * 128)
            w = jnp.zeros((PEER_NKEYS, 128), F32)
            for hh in range(PEER_HEADS):
                n1 = n1_ref[hh, pl.ds(a0, na), cs][al:al + 1]
                e1 = e1_ref[hh, pl.ds(a0, na), cs][al:al + 1]
                w = w + jnp.where(r2_ref[hh, :, cs] < n1, e1 * e2_ref[hh, :, cs], 0.0)
            p_ref[rs, cs] = (w * _gelu(act_ref[rs, cs])).astype(BF16)

    per = PEER_SLAB // PEER_NKEYS
    act(0)
    act(1)
    for s in range(nslab):
        for k in range(per):
            mask(s * per + k)
            if k == 0 and s + 2 < nslab:
                act(s + 2)
        val(s)
    acc_ref[...] += functools.reduce(lambda x, y: x + y, parts)

    @pl.when(j == pl.num_programs(1) - 1)
    def _():
        y = x_ref[...] + gt_ref[...] * acc_ref[...].T
        if final:
            y = y * lax.rsqrt(jnp.mean(y * y, axis=-1, keepdims=True) + NORM_EPS) * fg_ref[...]
        o_ref[...] = y


def _peer_main(h2, u_bf, vt_bf, n1, e1, r2, e2, x2, mod3, final_g, *, final, tt=256, te=4096):
    t, d = x2.shape
    assert (te // PEER_NKEYS) % 8 == 0 and te % PEER_SLAB == 0
    per_b = SEQ // tt
    big = pl.BlockSpec((PEER_HEADS, PEER_NKEYS, tt), lambda i, j: (0, 0, i))
    return pl.pallas_call(
        functools.partial(_peer_main_kernel, final=final),
        grid=(t // tt, PEER_EXPERTS // te),
        in_specs=[pl.BlockSpec((tt, d), lambda i, j: (i, 0)),
                  pl.BlockSpec((te, d), lambda i, j: (j, 0)),
                  pl.BlockSpec((d, te), lambda i, j: (0, j)),
                  big, big, big, big,
                  pl.BlockSpec((tt, d), lambda i, j: (i, 0)),
                  pl.BlockSpec((None, 1, d), lambda i, j: (i // per_b, 0, 5)),
                  pl.BlockSpec((1, d), lambda i, j: (0, 0))],
        out_specs=pl.BlockSpec((tt, d), lambda i, j: (i, 0)),
        out_shape=jax.ShapeDtypeStruct((t, d), F32),
        scratch_shapes=[pltpu.VMEM((d, tt), F32), pltpu.VMEM((te, tt), F32), pltpu.VMEM((te, tt), BF16),
                        pltpu.VMEM((d, tt), BF16)],
        compiler_params=_params(("parallel", "arbitrary")),
        name="peer_experts",
    )(h2, u_bf, vt_bf, n1, e1, r2, e2, x2, mod3, final_g.reshape(1, d))


def _reorder_w_in(w):
    pad = jnp.zeros((w.shape[0], 1024 - 792), w.dtype)
    return jnp.concatenate([w[:, 5400:9496], w[:, 3072:3864], pad, w[:, 0:3072], w[:, 3864:5400]], axis=1)


def kernel(x, c, positions, mod_w, mod_b, norm_mix_g, norm_ffn_g, w_in, a_conv_w, a_out, b_conv_w, b_conv_b, b_ln_g, b_ln_b, b_out, c_cmp_pos, c_cmp_w1, c_cmp_w2, c_out, d_out, w_o, peer_wq, peer_subkeys, peer_u, peer_v, final_norm_g):
    bsz, s, d = x.shape
    assert s == SEQ and d == D_MODEL
    depth = mod_w.shape[0]
    t = bsz * s
    cos, sin = _rope_tables(positions)
    mod = _modulation(c, mod_w, mod_b)
    x2 = x.reshape(t, d)
    for l in range(depth):
        mod3 = mod[l].reshape(bsz, 1, 6 * d)
        proj2 = _norm_matmul(x2, norm_mix_g[l], mod3, 0, 1, _reorder_w_in(w_in[l].astype(BF16)), out_dtype=BF16,
                             tn=PROJ_COLS // 4)
        proj3 = proj2.reshape(bsz, s, PROJ_COLS)
        ua, ub = _conv_mixers(proj3, a_conv_w[l], b_conv_w[l], b_conv_b[l], b_ln_g[l], b_ln_b[l])
        qc, kc, vc, ks, vst, kw, vwt, gt, qd, kd, vdt, kmean = _prep(proj3, cos, sin)
        kcmp, vcmpt = _compress(kc, vc, c_cmp_pos[l], c_cmp_w1[l], c_cmp_w2[l])
        oc = _nsa(qc, kcmp, vcmpt, ks, vst, kw, vwt, gt)
        od = _moba(qd, kd, vdt, kmean)
        x2 = _merge(ua.reshape(t, 512), ub.reshape(t, 512), oc.reshape(t, 512), od.reshape(t, 512),
                    proj2, x2, mod3, a_out[l].astype(BF16), b_out[l].astype(BF16), c_out[l].astype(BF16),
                    d_out[l].astype(BF16), w_o[l].astype(BF16))
        q2, h2 = _norm_matmul(x2, norm_ffn_g[l], mod3, 3, 4, peer_wq[l].astype(BF16), emit_h=True,
                              tn=PEER_HEADS * PEER_QDIM)
        n1, e1, r2, e2 = _peer_select(q2, peer_subkeys[l])
        x2 = _peer_main(h2, peer_u[l].astype(BF16), peer_v[l].T.astype(BF16), n1, e1, r2, e2,
                        x2, mod3, final_norm_g, final=(l == depth - 1))
    return x2.reshape(bsz, s, d)
```

```python
import functools

import numpy as np
import jax
import jax.numpy as jnp
from jax import lax
from jax.experimental import pallas as pl
from jax.experimental.pallas import tpu as pltpu

F32 = jnp.float32
BF16 = jnp.bfloat16
HIGHEST = lax.Precision.HIGHEST

D_MODEL = 1024
SEQ = 2048
HEAD_DIM = 64
ROPE_THETA = 10000.0
NORM_EPS = 1e-6
A_WIDTH = 512
A_CONV = 3
B_WIDTH = 512
B_CONV = 31
C_HEADS = 8
C_KV_HEADS = 2
C_GROUP = 4
CMP_BLOCK = 32
CMP_STRIDE = 16
CMP_HIDDEN = 128
N_CMP = (SEQ - CMP_BLOCK) // CMP_STRIDE + 1
SLC_BLOCK = 64
SLC_TOPN = 16
N_SLC = SEQ // SLC_BLOCK
WIN = 512
D_HEADS = 8
MOBA_BLOCK = 256
MOBA_TOPK = 3
N_MOBA = SEQ // MOBA_BLOCK
PEER_HEADS = 8
PEER_NKEYS = 128
PEER_EXPERTS = PEER_NKEYS * PEER_NKEYS
PEER_QDIM = 256
PEER_TOPK = 16

PROJ_COLS = 9728
COL_MERGE = 0
COL_KVG = 4096
COL_GATE = COL_KVG + 768
COL_A = 5120

VMEM_LIMIT = 56 * 1024 * 1024
NEG = -1e30

ATT_T = 256
CONV_T = 256
HALO = 32


def _params(sem, flags=None):
    return pltpu.CompilerParams(dimension_semantics=sem, vmem_limit_bytes=VMEM_LIMIT, flags=flags)


def _gelu(x):
    return 0.5 * x * (1.0 + lax.erf(x * np.float32(np.sqrt(0.5))))


def _rope_table_kernel(pos_ref, inv_ref, sign_ref, cos_ref, sin_ref):
    ang = pos_ref[...] * inv_ref[...]
    cos_ref[...] = jnp.cos(ang)
    sin_ref[...] = jnp.sin(ang) * sign_ref[...]


def _rope_tables(positions):
    bsz, s = positions.shape
    inv = 1.0 / (ROPE_THETA ** (jnp.arange(0, HEAD_DIM, 2, dtype=F32) / HEAD_DIM))
    inv128 = jnp.tile(inv, 4)[None, :]
    sign = jnp.tile(jnp.concatenate([-jnp.ones(32, F32), jnp.ones(32, F32)]), 2)[None, :]
    pos = positions.astype(F32).reshape(bsz * s, 1)
    t = bsz * s
    cos, sin = pl.pallas_call(
        _rope_table_kernel,
        grid=(t // SEQ,),
        in_specs=[pl.BlockSpec((SEQ, 1), lambda i: (i, 0)),
                  pl.BlockSpec((1, 128), lambda i: (0, 0)),
                  pl.BlockSpec((1, 128), lambda i: (0, 0))],
        out_specs=[pl.BlockSpec((SEQ, 128), lambda i: (i, 0))] * 2,
        out_shape=[jax.ShapeDtypeStruct((t, 128), F32)] * 2,
        compiler_params=_params(("parallel",)),
        name="rope_tables",
    )(pos, inv128, sign)
    return cos.reshape(bsz, s, 128), sin.reshape(bsz, s, 128)


def _rope(x, cos, sin):
    w = x.shape[-1]
    lane = lax.broadcasted_iota(jnp.int32, x.shape, 1)
    swapped = jnp.where(lane % 64 < 32, pltpu.roll(x, w - 32, 1), pltpu.roll(x, 32, 1))
    return x * cos + swapped * sin


def _mod_kernel(c_ref, w_ref, b_ref, o_ref):
    c = c_ref[...]
    cond = c * jax.nn.sigmoid(c)
    o_ref[...] = jnp.dot(cond, w_ref[...], precision=HIGHEST, preferred_element_type=F32) + b_ref[...]


def _modulation(c, mod_w, mod_b):
    nl, d, n = mod_w.shape
    bsz = c.shape[0]
    tn = 1536
    return pl.pallas_call(
        _mod_kernel,
        grid=(nl, n // tn),
        in_specs=[pl.BlockSpec((bsz, d), lambda l, j: (0, 0)),
                  pl.BlockSpec((None, d, tn), lambda l, j: (l, 0, j)),
                  pl.BlockSpec((None, 1, tn), lambda l, j: (l, 0, j))],
        out_specs=pl.BlockSpec((None, bsz, tn), lambda l, j: (l, 0, j)),
        out_shape=jax.ShapeDtypeStruct((nl, bsz, n), F32),
        compiler_params=_params(("parallel", "parallel")),
        name="adaln_mod",
    )(c, mod_w, mod_b.reshape(nl, 1, n))


def _norm_matmul_kernel(x_ref, g_ref, sc_ref, sh_ref, w_ref, o_ref, *rest, emit_h):
    h_scr = rest[-1]

    @pl.when(pl.program_id(1) == 0)
    def _():
        x = x_ref[...]
        y = x * lax.rsqrt(jnp.mean(x * x, axis=-1, keepdims=True) + NORM_EPS)
        h = (y * g_ref[...]) * (1.0 + sc_ref[...]) + sh_ref[...]
        h_scr[...] = h.astype(BF16)
        if emit_h:
            rest[0][...] = h.astype(BF16)

    o_ref[...] = jnp.dot(h_scr[...], w_ref[...], preferred_element_type=F32).astype(o_ref.dtype)


def _norm_matmul(x2, g, mod3, sh_blk, sc_blk, w, *, out_dtype=F32, emit_h=False, tm=1024, tn=512):
    t, d = x2.shape
    n = w.shape[1]
    per_b = SEQ // tm
    out_shape = [jax.ShapeDtypeStruct((t, n), out_dtype)]
    out_specs = [pl.BlockSpec((tm, tn), lambda i, j: (i, j))]
    if emit_h:
        out_shape.append(jax.ShapeDtypeStruct((t, d), BF16))
        out_specs.append(pl.BlockSpec((tm, d), lambda i, j: (i, 0)))
    res = pl.pallas_call(
        functools.partial(_norm_matmul_kernel, emit_h=emit_h),
        grid=(t // tm, n // tn),
        in_specs=[pl.BlockSpec((tm, d), lambda i, j: (i, 0)),
                  pl.BlockSpec((1, d), lambda i, j: (0, 0)),
                  pl.BlockSpec((None, 1, d), lambda i, j: (i // per_b, 0, sc_blk)),
                  pl.BlockSpec((None, 1, d), lambda i, j: (i // per_b, 0, sh_blk)),
                  pl.BlockSpec((d, tn), lambda i, j: (0, j))],
        out_specs=out_specs,
        out_shape=out_shape,
        scratch_shapes=[pltpu.VMEM((tm, d), BF16)],
        compiler_params=_params(("parallel", "arbitrary")),
        name="norm_matmul",
    )(x2, g.reshape(1, d), mod3, mod3, w)
    return res if emit_h else res[0]


def _conv_kernel(ab_ref, ac_ref, ax_ref, ba_ref, bg_ref, pac_ref, pax_ref, pba_ref, pbg_ref,
                 aw_ref, bw_ref, bb_ref, lng_ref, lnb_ref, ua_ref, ub_ref, ext_ref, y_ref):
    ts = ab_ref.shape[0]
    keep = (pl.program_id(1) > 0).astype(F32)

    def f32(ref):
        return ref[...].astype(F32)

    def causal_conv(w_ref, taps, bias_ref, out_ref):
        blk = 128
        reach = ((taps - 1) // 8) * 8
        for t0 in range(0, ts, blk):
            for c0 in range(0, w_ref.shape[1], 128):
                cs = slice(c0, c0 + 128)
                acc = jnp.zeros((blk, 128), F32) if bias_ref is None else jnp.zeros((blk, 128), F32) + bias_ref[:, cs]
                for r in range(min(8, taps)):
                    base = HALO + t0 - reach - r
                    ur = ext_ref[base:base + blk + reach, cs]
                    for q in range(reach // 8 + 1):
                        shift = 8 * q + r
                        if shift < taps:
                            k = taps - 1 - shift
                            acc = acc + w_ref[k:k + 1, cs] * ur[reach - 8 * q:reach - 8 * q + blk]
                out_ref[t0:t0 + blk, cs] = acc

    ext_ref[0:HALO, :] = f32(pac_ref) * f32(pax_ref) * keep
    ext_ref[HALO:HALO + ts, :] = f32(ac_ref) * f32(ax_ref)
    causal_conv(aw_ref, A_CONV, None, y_ref)
    ua_ref[...] = (f32(ab_ref) * y_ref[...]).astype(BF16)

    ext_ref[0:HALO, :] = f32(pba_ref) * jax.nn.sigmoid(f32(pbg_ref)) * keep
    ext_ref[HALO:HALO + ts, :] = f32(ba_ref) * jax.nn.sigmoid(f32(bg_ref))
    causal_conv(bw_ref, B_CONV, bb_ref, y_ref)
    acc = y_ref[...]
    mu = jnp.mean(acc, axis=-1, keepdims=True)
    cen = acc - mu
    var = jnp.mean(cen * cen, axis=-1, keepdims=True)
    y = cen * lax.rsqrt(var + NORM_EPS) * lng_ref[...] + lnb_ref[...]
    ub_ref[...] = (y * jax.nn.sigmoid(y)).astype(BF16)


def _conv_mixers(proj3, a_conv_w, b_conv_w, b_conv_b, b_ln_g, b_ln_b):
    bsz = proj3.shape[0]
    ts = CONV_T
    c0 = COL_A // 512
    r = ts // HALO

    def cur(k):
        return pl.BlockSpec((None, ts, 512), lambda b, i, k=k: (b, i, c0 + k))

    def prev(k):
        return pl.BlockSpec((None, HALO, 512), lambda b, i, k=k: (b, jnp.maximum(i * r - 1, 0), c0 + k))

    def full(shape):
        return pl.BlockSpec(shape, lambda b, i: (0,) * len(shape))

    return pl.pallas_call(
        _conv_kernel,
        grid=(bsz, SEQ // ts),
        in_specs=[cur(0), cur(1), cur(2), cur(3), cur(4), prev(1), prev(2), prev(3), prev(4),
                  full((A_CONV, A_WIDTH)), full((B_CONV, B_WIDTH)), full((1, B_WIDTH)),
                  full((1, B_WIDTH)), full((1, B_WIDTH))],
        out_specs=[pl.BlockSpec((None, ts, 512), lambda b, i: (b, i, 0))] * 2,
        out_shape=[jax.ShapeDtypeStruct((bsz, SEQ, 512), BF16)] * 2,
        scratch_shapes=[pltpu.VMEM((HALO + ts, 512), F32), pltpu.VMEM((ts, 512), F32)],
        compiler_params=_params(("parallel", "arbitrary")),
        name="conv_mixers",
    )(proj3, proj3, proj3, proj3, proj3, proj3, proj3, proj3, proj3,
      a_conv_w, b_conv_w, b_conv_b.reshape(1, -1), b_ln_g.reshape(1, -1), b_ln_b.reshape(1, -1))


VT_ROWS = 80


def _values_t(v, n_heads):
    rows = v.shape[0]
    vt = v.T
    tail = (lax.broadcasted_iota(jnp.int32, (VT_ROWS - HEAD_DIM, rows), 0) == 0).astype(F32)
    blocks = []
    for h in range(n_heads):
        blocks += [vt[h * HEAD_DIM:(h + 1) * HEAD_DIM], tail]
    return jnp.concatenate(blocks, axis=0).astype(BF16)


def _prep_kernel(cq_ref, dq_ref, dk_ref, dv_ref, kvg_ref, cos_ref, sin_ref,
                 qc_ref, kc_ref, vc_ref, ks_ref, vst_ref, kw_ref, vwt_ref, gt_ref,
                 qd_ref, kd_ref, vdt_ref, kmean_ref):
    cos = cos_ref[...]
    sin = sin_ref[...]
    cos4 = jnp.concatenate([cos] * 4, axis=1)
    sin4 = jnp.concatenate([sin] * 4, axis=1)
    scale = np.float32(HEAD_DIM ** -0.5 * np.log2(np.e))
    qc_ref[...] = _rope(cq_ref[...].astype(F32), cos4, sin4) * scale
    qd_ref[...] = _rope(dq_ref[...].astype(F32), cos4, sin4) * scale
    kd = _rope(dk_ref[...].astype(F32), cos4, sin4)
    for h in range(D_HEADS):
        kd_ref[h] = kd[:, h * 64:(h + 1) * 64].astype(BF16)
    kmean_ref[...] = jnp.mean(kd, axis=0, keepdims=True)
    vdt_ref[...] = _values_t(dv_ref[...].astype(F32), D_HEADS)

    def kvg(k):
        return kvg_ref[:, k * 128:(k + 1) * 128].astype(F32)

    kc = _rope(kvg(0), cos, sin)
    vc = kvg(1)
    ks = _rope(kvg(2), cos, sin)
    kw = _rope(kvg(4), cos, sin)
    for g in range(C_KV_HEADS):
        gs = slice(g * 64, (g + 1) * 64)
        kc_ref[g] = kc[:, gs]
        vc_ref[g] = vc[:, gs]
        ks_ref[g] = ks[:, gs].astype(BF16)
        kw_ref[g] = kw[:, gs].astype(BF16)
    vst_ref[...] = _values_t(kvg(3), C_KV_HEADS)
    vwt_ref[...] = _values_t(kvg(5), C_KV_HEADS)
    gt_ref[...] = jax.nn.sigmoid(kvg(6)).T


def _prep(proj3, cos, sin):
    bsz = proj3.shape[0]
    ts = MOBA_BLOCK
    c0 = COL_A // 512

    def col512(k):
        return pl.BlockSpec((None, ts, 512), lambda b, i: (b, i, c0 + k))

    row128 = pl.BlockSpec((None, ts, 128), lambda b, i: (b, i, 0))
    row512 = pl.BlockSpec((None, ts, 512), lambda b, i: (b, i, 0))
    col128t = pl.BlockSpec((None, 128, ts), lambda b, i: (b, 0, i))
    val2t = pl.BlockSpec((None, C_KV_HEADS * VT_ROWS, ts), lambda b, i: (b, 0, i))
    head64 = pl.BlockSpec((None, C_KV_HEADS, ts, 64), lambda b, i: (b, 0, i, 0))
    k64 = jax.ShapeDtypeStruct((bsz, C_KV_HEADS, SEQ, 64), BF16)
    t128 = jax.ShapeDtypeStruct((bsz, C_KV_HEADS * VT_ROWS, SEQ), BF16)
    return pl.pallas_call(
        _prep_kernel,
        grid=(bsz, SEQ // ts),
        in_specs=[col512(5), col512(6), col512(7), col512(8),
                  pl.BlockSpec((None, ts, 1024), lambda b, i: (b, i, COL_KVG // 1024)),
                  row128, row128],
        out_specs=[row512, head64, head64, head64, val2t, head64, val2t, col128t,
                   row512,
                   pl.BlockSpec((None, D_HEADS, ts, 64), lambda b, i: (b, 0, i, 0)),
                   pl.BlockSpec((None, D_HEADS * VT_ROWS, ts), lambda b, i: (b, 0, i)),
                   pl.BlockSpec((None, None, 1, 512), lambda b, i: (b, i, 0, 0))],
        out_shape=[jax.ShapeDtypeStruct((bsz, SEQ, 512), F32),
                   jax.ShapeDtypeStruct((bsz, C_KV_HEADS, SEQ, 64), F32),
                   jax.ShapeDtypeStruct((bsz, C_KV_HEADS, SEQ, 64), F32),
                   k64, t128, k64, t128,
                   jax.ShapeDtypeStruct((bsz, 128, SEQ), F32),
                   jax.ShapeDtypeStruct((bsz, SEQ, 512), F32),
                   jax.ShapeDtypeStruct((bsz, D_HEADS, SEQ, 64), BF16),
                   jax.ShapeDtypeStruct((bsz, D_HEADS * VT_ROWS, SEQ), BF16),
                   jax.ShapeDtypeStruct((bsz, N_MOBA, 1, 512), F32)],
        compiler_params=_params(("parallel", "parallel")),
        name="attn_prep",
    )(proj3, proj3, proj3, proj3, proj3, cos, sin)


def _compress_kernel(kc_ref, vc_ref, pos_ref, w1_ref, w2_ref, kcmp_ref, vcmp_ref):
    half = CMP_STRIDE * HEAD_DIM
    row = lax.broadcasted_iota(jnp.int32, (128, CMP_HIDDEN), 0)
    for which, (src, dst) in enumerate(((kc_ref, kcmp_ref), (vc_ref, vcmp_ref))):
        bias = jnp.dot(pos_ref[which], w1_ref[which], precision=HIGHEST, preferred_element_type=F32)[0:1]
        for g in range(C_KV_HEADS):
            chunks = src[g]
            d1 = jnp.dot(chunks, w1_ref[which, 0:half, :], precision=HIGHEST, preferred_element_type=F32)
            d2 = jnp.dot(chunks, w1_ref[which, half:2 * half, :], precision=HIGHEST, preferred_element_type=F32)
            d2 = jnp.where(row < 127, pltpu.roll(d2, 127, 0), 0.0)
            hid = _gelu(d1 + d2 + bias)
            out = jnp.dot(hid, w2_ref[which], precision=HIGHEST, preferred_element_type=F32)
            dst[g] = out if which == 0 else out.T


def _compress(kc, vc, cmp_pos, cmp_w1, cmp_w2):
    bsz = kc.shape[0]
    kc4 = kc.reshape(bsz, C_KV_HEADS, SEQ // CMP_STRIDE, CMP_STRIDE * HEAD_DIM)
    vc4 = vc.reshape(bsz, C_KV_HEADS, SEQ // CMP_STRIDE, CMP_STRIDE * HEAD_DIM)
    pos8 = jnp.broadcast_to(cmp_pos.reshape(2, 1, CMP_BLOCK * HEAD_DIM), (2, 8, CMP_BLOCK * HEAD_DIM))
    blk = pl.BlockSpec((None, C_KV_HEADS, 128, 1024), lambda b: (b, 0, 0, 0))
    out = pl.BlockSpec((None, C_KV_HEADS, 128, 64), lambda b: (b, 0, 0, 0))
    return pl.pallas_call(
        _compress_kernel,
        grid=(bsz,),
        in_specs=[blk, blk,
                  pl.BlockSpec((2, 8, 2048), lambda b: (0, 0, 0)),
                  pl.BlockSpec((2, 2048, CMP_HIDDEN), lambda b: (0, 0, 0)),
                  pl.BlockSpec((2, CMP_HIDDEN, 64), lambda b: (0, 0, 0))],
        out_specs=[out, pl.BlockSpec((None, C_KV_HEADS, 64, 128), lambda b: (b, 0, 0, 0))],
        out_shape=[jax.ShapeDtypeStruct((bsz, C_KV_HEADS, 128, 64), F32),
                   jax.ShapeDtypeStruct((bsz, C_KV_HEADS, 64, 128), F32)],
        compiler_params=_params(("parallel",)),
        name="nsa_compress",
    )(kc4, vc4, pos8, cmp_w1, cmp_w2)


def _flash_steps_t(heads, m_ref, acc_ref, groups):
    scores = [lax.dot_general(k, qb, (((1,), (1,)), ((), ())), preferred_element_type=F32)
              for (qb, k, _, _) in heads]
    probs = []
    for n, (qb, _, _, mask) in enumerate(heads):
        s = scores[n]
        tq = qb.shape[0] // groups
        if mask is not None:
            s = jnp.concatenate([jnp.where(mask, s[:, r * tq:(r + 1) * tq], NEG) for r in range(groups)], axis=1)
        sr = _stat_row(n)
        m_old = m_ref[sr, :]
        m_new = jnp.maximum(m_old, jnp.max(s, axis=0, keepdims=True))
        m_ref[sr, :] = m_new
        probs.append((jnp.exp2(m_old - m_new), jnp.exp2(s - m_new).astype(BF16)))
    for n, (_, _, vt, _) in enumerate(heads):
        alpha, p = probs[n]
        a = slice(n * VT_ROWS, (n + 1) * VT_ROWS)
        acc_ref[a, :] = alpha * acc_ref[a, :] + jnp.dot(vt, p, preferred_element_type=F32)


def _flash_out(acc_ref, n):
    return acc_ref[n * VT_ROWS:n * VT_ROWS + HEAD_DIM, :] / acc_ref[n * VT_ROWS + HEAD_DIM:n * VT_ROWS + HEAD_DIM + 1, :]


def _stat_row(row):
    return slice(8 * row, 8 * row + 1)


def _flash_init(m_ref, acc_ref):
    m_ref[...] = jnp.full(m_ref.shape, NEG, F32)
    acc_ref[...] = jnp.zeros(acc_ref.shape, F32)


def _rank_desc_rows(vals):
    n = vals.shape[0]
    row = lax.broadcasted_iota(jnp.int32, vals.shape, 0)
    rank = jnp.zeros(vals.shape, jnp.int32)
    for i in range(n):
        vi = vals[i:i + 1, :]
        ahead = (vi > vals) | ((vi == vals) & (row > i))
        rank = rank + ahead.astype(jnp.int32)
    return rank


def _nsa_kernel(q_ref, kcmp_ref, vcmpt_ref, ks_ref, vst_ref, kw_ref, vwt_ref, gt_ref, ovlt_ref, expt_ref,
                o_ref, qb_ref, sel_ref, m_ref, acc_ref, out_ref):
    tq = ATT_T
    i = pl.program_id(1)
    t0 = i * tq
    t_row = t0 + lax.broadcasted_iota(jnp.int32, (1, tq), 1)
    t_rows = jnp.concatenate([t_row] * C_GROUP, axis=1)
    blk = lax.broadcasted_iota(jnp.int32, (128, 1), 0)
    kk = lax.broadcasted_iota(jnp.int32, (tq, tq), 0)
    qq = lax.broadcasted_iota(jnp.int32, (tq, tq), 1)
    causal = kk <= qq

    def gate_row(branch, g):
        base = branch * 8 + g * C_GROUP
        return jnp.concatenate([gt_ref[base + r:base + r + 1, :] for r in range(C_GROUP)], axis=1)

    def hrows(g):
        return slice(g * HEAD_DIM, (g + 1) * HEAD_DIM)

    def vrows(g):
        return slice(g * VT_ROWS, (g + 1) * VT_ROWS)

    for g in range(C_KV_HEADS):
        qf = jnp.concatenate([q_ref[:, (g * 4 + r) * 64:(g * 4 + r + 1) * 64] for r in range(C_GROUP)], axis=0)
        qb_ref[g] = qf.astype(BF16)

        s = lax.dot_general(kcmp_ref[g], qf, (((1,), (1,)), ((), ())), precision=HIGHEST,
                            preferred_element_type=F32)
        vis = (blk * CMP_STRIDE + (CMP_BLOCK - 1)) <= t_rows
        sm = jnp.where(vis, s, NEG)
        e = jnp.where(vis, jnp.exp2(sm - jnp.max(sm, axis=0, keepdims=True)), 0.0)
        p = e / jnp.maximum(jnp.sum(e, axis=0, keepdims=True), 1e-30)
        o_cmp = jnp.dot(vcmpt_ref[g].astype(BF16), p.astype(BF16), preferred_element_type=F32)
        out_ref[hrows(g), :] = gate_row(0, g) * o_cmp

        psum = p[:, 0:tq] + p[:, tq:2 * tq] + p[:, 2 * tq:3 * tq] + p[:, 3 * tq:4 * tq]
        imp = jnp.dot(ovlt_ref[...], psum, precision=HIGHEST, preferred_element_type=F32)[0:N_SLC]
        b32 = blk[0:N_SLC]
        cur = t_row // SLC_BLOCK
        forced = (b32 == 0) | (b32 == cur) | (b32 == cur - 1)
        imp = jnp.where(forced, jnp.inf, jnp.where(b32 * SLC_BLOCK > t_row, -jnp.inf, imp))
        sel = (_rank_desc_rows(imp) < SLC_TOPN).astype(F32)
        sel_ref[g] = jnp.concatenate([sel, jnp.zeros((128 - N_SLC, tq), F32)], axis=0).astype(BF16)

    def slc_heads(jb, extra):
        start = pl.multiple_of(jb * tq, tq)
        heads = []
        for g in range(C_KV_HEADS):
            hit = jnp.dot(expt_ref[pl.ds(start, tq), :], sel_ref[g], preferred_element_type=F32) > 0.5
            mask = hit if extra is None else hit & extra
            heads.append((qb_ref[g], ks_ref[g, pl.ds(start, tq), :], vst_ref[vrows(g), pl.ds(start, tq)], mask))
        return heads

    def win_heads():
        wk = WIN + tq
        wstart = pl.multiple_of(jnp.maximum(t0 - WIN, 0), tq)
        kpos = wstart + lax.broadcasted_iota(jnp.int32, (wk, 1), 0)
        mask = (kpos <= t_row) & (kpos > t_row - WIN)
        return [(qb_ref[g], kw_ref[g, pl.ds(wstart, wk), :], vwt_ref[vrows(g), pl.ds(wstart, wk)], mask)
                for g in range(C_KV_HEADS)]

    _flash_init(m_ref, acc_ref)
    _flash_steps_t(slc_heads(i, causal) + win_heads(), m_ref, acc_ref, C_GROUP)

    def slc_body(jb, carry):
        _flash_steps_t(slc_heads(jb, None), m_ref, acc_ref, C_GROUP)
        return carry

    lax.fori_loop(0, i, slc_body, 0)
    for g in range(C_KV_HEADS):
        out_ref[hrows(g), :] += (gate_row(1, g) * _flash_out(acc_ref, g)
                                 + gate_row(2, g) * _flash_out(acc_ref, C_KV_HEADS + g))

    o_t = jnp.concatenate([out_ref[hrows(g), r * tq:(r + 1) * tq]
                           for g in range(C_KV_HEADS) for r in range(C_GROUP)], axis=0)
    o_ref[...] = o_t.T.astype(BF16)


def _nsa_constants():
    j = np.arange(128)[:, None]
    n = np.arange(128)[None, :]
    ovl_t = ((n * CMP_STRIDE < j * SLC_BLOCK + SLC_BLOCK) & (n * CMP_STRIDE + CMP_BLOCK > j * SLC_BLOCK)
             & (n < N_CMP) & (j < N_SLC)).astype(np.float32)
    expand_t = (np.arange(SEQ)[:, None] // SLC_BLOCK == np.arange(128)[None, :]).astype(np.float32)
    return jnp.asarray(ovl_t), jnp.asarray(expand_t, dtype=BF16)


def _nsa(qc, kcmp, vcmpt, ks, vst, kw, vwt, gt):
    bsz = qc.shape[0]
    tq = ATT_T
    ovl_t, expand_t = _nsa_constants()
    keys = pl.BlockSpec((None, C_KV_HEADS, SEQ, 64), lambda b, i: (b, 0, 0, 0))
    vals = pl.BlockSpec((None, C_KV_HEADS * VT_ROWS, SEQ), lambda b, i: (b, 0, 0))
    rows = C_GROUP * tq
    return pl.pallas_call(
        _nsa_kernel,
        grid=(bsz, SEQ // tq),
        in_specs=[pl.BlockSpec((None, tq, 512), lambda b, i: (b, i, 0)),
                  pl.BlockSpec((None, C_KV_HEADS, 128, 64), lambda b, i: (b, 0, 0, 0)),
                  pl.BlockSpec((None, C_KV_HEADS, 64, 128), lambda b, i: (b, 0, 0, 0)),
                  keys, vals, keys, vals,
                  pl.BlockSpec((None, 128, tq), lambda b, i: (b, 0, i)),
                  pl.BlockSpec((128, 128), lambda b, i: (0, 0)),
                  pl.BlockSpec((SEQ, 128), lambda b, i: (0, 0))],
        out_specs=pl.BlockSpec((None, tq, 512), lambda b, i: (b, i, 0)),
        out_shape=jax.ShapeDtypeStruct((bsz, SEQ, 512), BF16),
        scratch_shapes=[pltpu.VMEM((C_KV_HEADS, rows, 64), BF16), pltpu.VMEM((C_KV_HEADS, 128, tq), BF16),
                        pltpu.VMEM((8 * 2 * C_KV_HEADS, rows), F32),
                        pltpu.VMEM((2 * C_KV_HEADS * VT_ROWS, rows), F32), pltpu.VMEM((C_KV_HEADS * 64, rows), F32)],
        compiler_params=_params(("parallel", "arbitrary")),
        name="nsa_attention",
    )(qc, kcmp, vcmpt, ks, vst, kw, vwt, gt, ovl_t, expand_t)


def _moba_kernel(q_ref, k_ref, vt_ref, kmean_ref, o_ref, qb_ref, sel_ref, m_ref, acc_ref):
    tq = ATT_T
    i = pl.program_id(1)
    blk = lax.broadcasted_iota(jnp.int32, (N_MOBA, 1), 0)
    kk = lax.broadcasted_iota(jnp.int32, (tq, tq), 0)
    qq = lax.broadcasted_iota(jnp.int32, (tq, tq), 1)
    causal = kk <= qq
    past = blk < i
    for h in range(D_HEADS):
        hs = slice(h * 64, (h + 1) * 64)
        qf = q_ref[:, hs]
        qb_ref[h] = qf.astype(BF16)
        gate = lax.dot_general(kmean_ref[:, hs], qf, (((1,), (1,)), ((), ())), precision=HIGHEST,
                               preferred_element_type=F32)
        gate = jnp.where(past, gate, -jnp.inf)
        sel_ref[h] = (past & (_rank_desc_rows(gate) < MOBA_TOPK)).astype(F32)

    def tile(jb, diag):
        start = pl.multiple_of(jb * tq, tq)
        heads = []
        for h in range(D_HEADS):
            if diag:
                mask = causal
            else:
                mask = jnp.sum(jnp.where(blk == jb, sel_ref[h], 0.0), axis=0, keepdims=True) > 0.5
            heads.append((qb_ref[h], k_ref[h, pl.ds(start, tq), :],
                          vt_ref[h * VT_ROWS:(h + 1) * VT_ROWS, pl.ds(start, tq)], mask))
        _flash_steps_t(heads, m_ref, acc_ref, 1)

    _flash_init(m_ref, acc_ref)
    tile(i, True)

    def body(jb, carry):
        tile(jb, False)
        return carry

    lax.fori_loop(0, i, body, 0)
    o_t = jnp.concatenate([_flash_out(acc_ref, h) for h in range(D_HEADS)], axis=0)
    o_ref[...] = o_t.T.astype(BF16)


def _moba(qd, kd, vdt, kmean):
    bsz = qd.shape[0]
    tq = ATT_T
    return pl.pallas_call(
        _moba_kernel,
        grid=(bsz, SEQ // tq),
        in_specs=[pl.BlockSpec((None, tq, 512), lambda b, i: (b, i, 0)),
                  pl.BlockSpec((None, D_HEADS, SEQ, 64), lambda b, i: (b, 0, 0, 0)),
                  pl.BlockSpec((None, D_HEADS * VT_ROWS, SEQ), lambda b, i: (b, 0, 0)),
                  pl.BlockSpec((None, N_MOBA, 512), lambda b, i: (b, 0, 0))],
        out_specs=pl.BlockSpec((None, tq, 512), lambda b, i: (b, i, 0)),
        out_shape=jax.ShapeDtypeStruct((bsz, SEQ, 512), BF16),
        scratch_shapes=[pltpu.VMEM((D_HEADS, tq, 64), BF16), pltpu.VMEM((D_HEADS, N_MOBA, tq), F32),
                        pltpu.VMEM((8 * D_HEADS, tq), F32), pltpu.VMEM((D_HEADS * VT_ROWS, tq), F32)],
        compiler_params=_params(("parallel", "arbitrary")),
        name="moba_attention",
    )(qd, kd, vdt, kmean.reshape(bsz, N_MOBA, 512))


def _merge_kernel(ua_ref, ub_ref, oc_ref, od_ref, mg_ref, x_ref, gt_ref,
                  wa_ref, wb_ref, wc_ref, wd_ref, wo_ref, o_ref):
    d = D_MODEL
    merged = jnp.zeros(x_ref.shape, F32)
    for k, (u_ref, w_ref) in enumerate(((ua_ref, wa_ref), (ub_ref, wb_ref), (oc_ref, wc_ref), (od_ref, wd_ref))):
        y = jnp.dot(u_ref[...], w_ref[...], preferred_element_type=F32)
        merged = merged + jax.nn.sigmoid(mg_ref[:, k * d:(k + 1) * d].astype(F32)) * y
    o_ref[...] = x_ref[...] + gt_ref[...] * jnp.dot(merged.astype(BF16), wo_ref[...], preferred_element_type=F32)


def _merge(ua, ub, oc, od, proj2, x2, mod3, wa, wb, wc, wd, wo, *, tm=512):
    t, d = x2.shape
    per_b = SEQ // tm
    act = pl.BlockSpec((tm, 512), lambda i: (i, 0))
    wspec = pl.BlockSpec((512, d), lambda i: (0, 0))
    return pl.pallas_call(
        _merge_kernel,
        grid=(t // tm,),
        in_specs=[act, act, act, act,
                  pl.BlockSpec((tm, 4 * d), lambda i: (i, COL_MERGE // (4 * d))),
                  pl.BlockSpec((tm, d), lambda i: (i, 0)),
                  pl.BlockSpec((None, 1, d), lambda i: (i // per_b, 0, 2)),
                  wspec, wspec, wspec, wspec,
                  pl.BlockSpec((d, d), lambda i: (0, 0))],
        out_specs=pl.BlockSpec((tm, d), lambda i: (i, 0)),
        out_shape=jax.ShapeDtypeStruct((t, d), F32),
        compiler_params=_params(("parallel",)),
        name="merge_out",
    )(ua, ub, oc, od, proj2, x2, mod3, wa, wb, wc, wd, wo)


def _pop_max(work, idx):
    m = jnp.max(work, axis=0, keepdims=True)
    first = jnp.min(jnp.where(work == m, idx, work.shape[0]), axis=0, keepdims=True)
    return m, idx == first


_PEER_PAIRS = [(i, j) for i in range(PEER_TOPK) for j in range(PEER_TOPK) if (i + 1) * (j + 1) <= PEER_TOPK]
_PEER_PAIR_ROWS = 56
NOT_TOP = 99.0


def _peer_sel_kernel(q_ref, keys_ref, grp_ref, n1_ref, e1_ref, r2_ref, e2_ref, st_ref):
    tt = q_ref.shape[0]
    half = PEER_QDIM // 2
    for hp in range(2 * PEER_HEADS):
        qh = q_ref[:, hp * half:(hp + 1) * half]
        qn = qh * lax.rsqrt(jnp.mean(qh * qh, axis=-1, keepdims=True) + NORM_EPS)
        st_ref[hp] = lax.dot_general(keys_ref[hp], qn, (((1,), (1,)), ((), ())), precision=HIGHEST,
                                     preferred_element_type=F32)

    idx = lax.broadcasted_iota(jnp.int32, (PEER_NKEYS, 128), 0)
    pidx = lax.broadcasted_iota(jnp.int32, (_PEER_PAIR_ROWS, 128), 0)
    pad_rows = jnp.full((_PEER_PAIR_ROWS - len(_PEER_PAIRS), 128), -jnp.inf, F32)

    def pop16(work, index, tie_safe, track):
        vals, order = [], jnp.full(work.shape, NOT_TOP, F32)
        for it in range(PEER_TOPK):
            if tie_safe:
                m, hit = _pop_max(work, index)
            else:
                m = jnp.max(work, axis=0, keepdims=True)
                hit = work == m
            work = jnp.where(hit, -jnp.inf, work)
            if track:
                order = jnp.where(hit, float(it), order)
            vals.append(m)
        removed = jnp.sum(jnp.where(work == -jnp.inf, 1.0, 0.0), axis=0, keepdims=True)
        return vals, order, removed

    def select(cs, tie_safe):
        wrong = jnp.zeros((1, 128), F32)
        n_pad = float(_PEER_PAIR_ROWS - len(_PEER_PAIRS))
        for h in range(PEER_HEADS):
            scores, tops, ranks = [], [], []
            for p in range(2):
                st = st_ref[2 * h + p, :, cs]
                vals, rank, removed = pop16(st, idx, tie_safe, track=tie_safe or p == 1)
                wrong = wrong + jnp.abs(removed - float(PEER_TOPK))
                scores.append(st)
                tops.append(vals)
                ranks.append(rank)
            cand = jnp.concatenate([tops[0][i] + tops[1][j] for (i, j) in _PEER_PAIRS] + [pad_rows], axis=0)
            best, order, removed = pop16(cand, pidx, tie_safe, track=tie_safe)
            wrong = wrong + jnp.abs(removed - (float(PEER_TOPK) + n_pad))
            picked = jnp.where((order < float(PEER_TOPK)) if tie_safe else (cand >= best[-1]), 1.0, 0.0)
            z = jnp.ones_like(best[0])
            for k in range(1, PEER_TOPK):
                z = z + jnp.exp(best[k] - best[0])
            count = jnp.dot(grp_ref[...], picked.astype(BF16), preferred_element_type=F32)
            n1 = jnp.zeros(scores[0].shape, F32)
            for i in reversed(range(PEER_TOPK)):
                at_i = (ranks[0] == float(i)) if tie_safe else (scores[0] >= tops[0][i])
                n1 = jnp.where(at_i, count[i:i + 1], n1)
            n1_ref[h, :, cs] = n1
            r2_ref[h, :, cs] = ranks[1]
            e1_ref[h, :, cs] = jnp.exp(scores[0] - tops[0][0]) / z
            e2_ref[h, :, cs] = jnp.exp(scores[1] - tops[1][0])
        return wrong

    def token_chunk(c, carry):
        cs = pl.ds(pl.multiple_of(c * 128, 128), 128)
        wrong = select(cs, tie_safe=False)

        @pl.when(jnp.max(wrong) > 0.0)
        def _():
            select(cs, tie_safe=True)

        return carry

    lax.fori_loop(0, tt // 128, token_chunk, 0)


def _peer_select(q2, subkeys, *, tt=512):
    t = q2.shape[0]
    grp = np.zeros((PEER_TOPK, _PEER_PAIR_ROWS), np.float32)
    for row, (i, _) in enumerate(_PEER_PAIRS):
        grp[i, row] = 1.0
    big = pl.BlockSpec((PEER_HEADS, PEER_NKEYS, tt), lambda i: (0, 0, i))
    bshape = jax.ShapeDtypeStruct((PEER_HEADS, PEER_NKEYS, t), F32)
    half = PEER_QDIM // 2
    return pl.pallas_call(
        _peer_sel_kernel,
        grid=(t // tt,),
        in_specs=[pl.BlockSpec((tt, PEER_HEADS * PEER_QDIM), lambda i: (i, 0)),
                  pl.BlockSpec((2 * PEER_HEADS, PEER_NKEYS, half), lambda i: (0, 0, 0)),
                  pl.BlockSpec((PEER_TOPK, _PEER_PAIR_ROWS), lambda i: (0, 0))],
        out_specs=[big, big, big, big],
        out_shape=[bshape, bshape, bshape, bshape],
        scratch_shapes=[pltpu.VMEM((2 * PEER_HEADS, PEER_NKEYS, tt), F32)],
        compiler_params=_params(("parallel",)),
        name="peer_select",
    )(q2, subkeys.reshape(2 * PEER_HEADS, PEER_NKEYS, half), jnp.asarray(grp, dtype=BF16))


PEER_SLAB = 256

def _peer_main_kernel(h_ref, u_ref, vt_ref, n1_ref, e1_ref, r2_ref, e2_ref, x_ref, gt_ref, fg_ref,
                      o_ref, acc_ref, act_ref, p_ref, ht_ref, *, final):
    j = pl.program_id(1)
    te, tt = act_ref.shape
    na = te // PEER_NKEYS

    @pl.when(j == 0)
    def _():
        acc_ref[...] = jnp.zeros(acc_ref.shape, F32)
        ht_ref[...] = h_ref[...].astype(F32).T.astype(BF16)

    a0 = pl.multiple_of(j * na, na)
    nslab = te // PEER_SLAB
    parts = []

    def act(s):
        ss = slice(s * PEER_SLAB, (s + 1) * PEER_SLAB)
        act_ref[ss, :] = jnp.dot(u_ref[ss, :], ht_ref[...], preferred_element_type=F32)

    def val(s):
        ss = slice(s * PEER_SLAB, (s + 1) * PEER_SLAB)
        parts.append(jnp.dot(vt_ref[:, ss], p_ref[ss, :], preferred_element_type=F32))

    def mask(al):
        rs = slice(al * PEER_NKEYS, (al + 1) * PEER_NKEYS)
        for c in range(tt // 128):
            cs = slice(c * 128, (c + 1) * 128)
            w = jnp.zeros((PEER_NKEYS, 128), F32)
            for hh in range(PEER_HEADS):
                n1 = n1_ref[hh, pl.ds(a0, na), cs][al:al + 1]
                e1 = e1_ref[hh, pl.ds(a0, na), cs][al:al + 1]
                w = w + jnp.where(r2_ref[hh, :, cs] < n1, e1 * e2_ref[hh, :, cs], 0.0)
            p_ref[rs, cs] = (w * _gelu(act_ref[rs, cs])).astype(BF16)

    per = PEER_SLAB // PEER_NKEYS
    act(0)
    act(1)
    for s in range(nslab):
        for k in range(per):
            mask(s * per + k)
            if k == 0 and s + 2 < nslab:
                act(s + 2)
        val(s)
    acc_ref[...] += functools.reduce(lambda x, y: x + y, parts)

    @pl.when(j == pl.num_programs(1) - 1)
    def _():
        y = x_ref[...] + gt_ref[...] * acc_ref[...].T
        if final:
            y = y * lax.rsqrt(jnp.mean(y * y, axis=-1, keepdims=True) + NORM_EPS) * fg_ref[...]
        o_ref[...] = y


def _peer_main(h2, u_bf, vt_bf, n1, e1, r2, e2, x2, mod3, final_g, *, final, tt=256, te=4096):
    t, d = x2.shape
    assert (te // PEER_NKEYS) % 8 == 0 and te % PEER_SLAB == 0
    per_b = SEQ // tt
    big = pl.BlockSpec((PEER_HEADS, PEER_NKEYS, tt), lambda i, j: (0, 0, i))
    return pl.pallas_call(
        functools.partial(_peer_main_kernel, final=final),
        grid=(t // tt, PEER_EXPERTS // te),
        in_specs=[pl.BlockSpec((tt, d), lambda i, j: (i, 0)),
                  pl.BlockSpec((te, d), lambda i, j: (j, 0)),
                  pl.BlockSpec((d, te), lambda i, j: (0, j)),
                  big, big, big, big,
                  pl.BlockSpec((tt, d), lambda i, j: (i, 0)),
                  pl.BlockSpec((None, 1, d), lambda i, j: (i // per_b, 0, 5)),
                  pl.BlockSpec((1, d), lambda i, j: (0, 0))],
        out_specs=pl.BlockSpec((tt, d), lambda i, j: (i, 0)),
        out_shape=jax.ShapeDtypeStruct((t, d), F32),
        scratch_shapes=[pltpu.VMEM((d, tt), F32), pltpu.VMEM((te, tt), F32), pltpu.VMEM((te, tt), BF16),
                        pltpu.VMEM((d, tt), BF16)],
        compiler_params=_params(("parallel", "arbitrary")),
        name="peer_experts",
    )(h2, u_bf, vt_bf, n1, e1, r2, e2, x2, mod3, final_g.reshape(1, d))


def _reorder_w_in(w):
    pad = jnp.zeros((w.shape[0], 1024 - 792), w.dtype)
    return jnp.concatenate([w[:, 5400:9496], w[:, 3072:3864], pad, w[:, 0:3072], w[:, 3864:5400]], axis=1)


def kernel(x, c, positions, mod_w, mod_b, norm_mix_g, norm_ffn_g, w_in, a_conv_w, a_out, b_conv_w, b_conv_b, b_ln_g, b_ln_b, b_out, c_cmp_pos, c_cmp_w1, c_cmp_w2, c_out, d_out, w_o, peer_wq, peer_subkeys, peer_u, peer_v, final_norm_g):
    bsz, s, d = x.shape
    assert s == SEQ and d == D_MODEL
    depth = mod_w.shape[0]
    t = bsz * s
    cos, sin = _rope_tables(positions)
    mod = _modulation(c, mod_w, mod_b)
    x2 = x.reshape(t, d)
    for l in range(depth):
        mod3 = mod[l].reshape(bsz, 1, 6 * d)
        proj2 = _norm_matmul(x2, norm_mix_g[l], mod3, 0, 1, _reorder_w_in(w_in[l].astype(BF16)), out_dtype=BF16,
                             tn=PROJ_COLS // 4)
        proj3 = proj2.reshape(bsz, s, PROJ_COLS)
        ua, ub = _conv_mixers(proj3, a_conv_w[l], b_conv_w[l], b_conv_b[l], b_ln_g[l], b_ln_b[l])
        qc, kc, vc, ks, vst, kw, vwt, gt, qd, kd, vdt, kmean = _prep(proj3, cos, sin)
        kcmp, vcmpt = _compress(kc, vc, c_cmp_pos[l], c_cmp_w1[l], c_cmp_w2[l])
        oc = _nsa(qc, kcmp, vcmpt, ks, vst, kw, vwt, gt)
        od = _moba(qd, kd, vdt, kmean)
        x2 = _merge(ua.reshape(t, 512), ub.reshape(t, 512), oc.reshape(t, 512), od.reshape(t, 512),
                    proj2, x2, mod3, a_out[l].astype(BF16), b_out[l].astype(BF16), c_out[l].astype(BF16),
                    d_out[l].astype(BF16), w_o[l].astype(BF16))
        q2, h2 = _norm_matmul(x2, norm_ffn_g[l], mod3, 3, 4, peer_wq[l].astype(BF16), emit_h=True,
                              tn=PEER_HEADS * PEER_QDIM)
        n1, e1, r2, e2 = _peer_select(q2, peer_subkeys[l])
        x2 = _peer_main(h2, peer_u[l].astype(BF16), peer_v[l].T.astype(BF16), n1, e1, r2, e2,
                        x2, mod3, final_norm_g, final=(l == depth - 1))
    return x2.reshape(bsz, s, d)
```

```python
import functools

import numpy as np
import jax
import jax.numpy as jnp
from jax import lax
from jax.experimental import pallas as pl
from jax.experimental.pallas import tpu as pltpu

F32 = jnp.float32
BF16 = jnp.bfloat16
HIGHEST = lax.Precision.HIGHEST

D_MODEL = 1024
SEQ = 2048
HEAD_DIM = 64
ROPE_THETA = 10000.0
NORM_EPS = 1e-6
A_WIDTH = 512
A_CONV = 3
B_WIDTH = 512
B_CONV = 31
C_HEADS = 8
C_KV_HEADS = 2
C_GROUP = 4
CMP_BLOCK = 32
CMP_STRIDE = 16
CMP_HIDDEN = 128
N_CMP = (SEQ - CMP_BLOCK) // CMP_STRIDE + 1
SLC_BLOCK = 64
SLC_TOPN = 16
N_SLC = SEQ // SLC_BLOCK
WIN = 512
D_HEADS = 8
MOBA_BLOCK = 256
MOBA_TOPK = 3
N_MOBA = SEQ // MOBA_BLOCK
PEER_HEADS = 8
PEER_NKEYS = 128
PEER_EXPERTS = PEER_NKEYS * PEER_NKEYS
PEER_QDIM = 256
PEER_TOPK = 16

PROJ_COLS = 9728
COL_MERGE = 0
COL_KVG = 4096
COL_GATE = COL_KVG + 768
COL_A = 5120

VMEM_LIMIT = 56 * 1024 * 1024
NEG = -1e30

ATT_T = 256
CONV_T = 256
HALO = 32


def _params(sem, flags=None):
    return pltpu.CompilerParams(dimension_semantics=sem, vmem_limit_bytes=VMEM_LIMIT, flags=flags)


def _gelu(x):
    return 0.5 * x * (1.0 + lax.erf(x * np.float32(np.sqrt(0.5))))


def _rope_table_kernel(pos_ref, inv_ref, sign_ref, cos_ref, sin_ref):
    ang = pos_ref[...] * inv_ref[...]
    cos_ref[...] = jnp.cos(ang)
    sin_ref[...] = jnp.sin(ang) * sign_ref[...]


def _rope_tables(positions):
    bsz, s = positions.shape
    inv = 1.0 / (ROPE_THETA ** (jnp.arange(0, HEAD_DIM, 2, dtype=F32) / HEAD_DIM))
    inv128 = jnp.tile(inv, 4)[None, :]
    sign = jnp.tile(jnp.concatenate([-jnp.ones(32, F32), jnp.ones(32, F32)]), 2)[None, :]
    pos = positions.astype(F32).reshape(bsz * s, 1)
    t = bsz * s
    cos, sin = pl.pallas_call(
        _rope_table_kernel,
        grid=(t // SEQ,),
        in_specs=[pl.BlockSpec((SEQ, 1), lambda i: (i, 0)),
                  pl.BlockSpec((1, 128), lambda i: (0, 0)),
                  pl.BlockSpec((1, 128), lambda i: (0, 0))],
        out_specs=[pl.BlockSpec((SEQ, 128), lambda i: (i, 0))] * 2,
        out_shape=[jax.ShapeDtypeStruct((t, 128), F32)] * 2,
        compiler_params=_params(("parallel",)),
        name="rope_tables",
    )(pos, inv128, sign)
    return cos.reshape(bsz, s, 128), sin.reshape(bsz, s, 128)


def _rope(x, cos, sin):
    w = x.shape[-1]
    lane = lax.broadcasted_iota(jnp.int32, x.shape, 1)
    swapped = jnp.where(lane % 64 < 32, pltpu.roll(x, w - 32, 1), pltpu.roll(x, 32, 1))
    return x * cos + swapped * sin


def _mod_kernel(c_ref, w_ref, b_ref, o_ref):
    c = c_ref[...]
    cond = c * jax.nn.sigmoid(c)
    o_ref[...] = jnp.dot(cond, w_ref[...], precision=HIGHEST, preferred_element_type=F32) + b_ref[...]


def _modulation(c, mod_w, mod_b):
    nl, d, n = mod_w.shape
    bsz = c.shape[0]
    tn = 1536
    return pl.pallas_call(
        _mod_kernel,
        grid=(nl, n // tn),
        in_specs=[pl.BlockSpec((bsz, d), lambda l, j: (0, 0)),
                  pl.BlockSpec((None, d, tn), lambda l, j: (l, 0, j)),
                  pl.BlockSpec((None, 1, tn), lambda l, j: (l, 0, j))],
        out_specs=pl.BlockSpec((None, bsz, tn), lambda l, j: (l, 0, j)),
        out_shape=jax.ShapeDtypeStruct((nl, bsz, n), F32),
        compiler_params=_params(("parallel", "parallel")),
        name="adaln_mod",
    )(c, mod_w, mod_b.reshape(nl, 1, n))


def _norm_matmul_kernel(x_ref, g_ref, sc_ref, sh_ref, w_ref, o_ref, *rest, emit_h):
    h_scr = rest[-1]

    @pl.when(pl.program_id(1) == 0)
    def _():
        x = x_ref[...]
        y = x * lax.rsqrt(jnp.mean(x * x, axis=-1, keepdims=True) + NORM_EPS)
        h = (y * g_ref[...]) * (1.0 + sc_ref[...]) + sh_ref[...]
        h_scr[...] = h.astype(BF16)
        if emit_h:
            rest[0][...] = h.astype(BF16)

    o_ref[...] = jnp.dot(h_scr[...], w_ref[...], preferred_element_type=F32).astype(o_ref.dtype)


def _norm_matmul(x2, g, mod3, sh_blk, sc_blk, w, *, out_dtype=F32, emit_h=False, tm=1024, tn=512):
    t, d = x2.shape
    n = w.shape[1]
    per_b = SEQ // tm
    out_shape = [jax.ShapeDtypeStruct((t, n), out_dtype)]
    out_specs = [pl.BlockSpec((tm, tn), lambda i, j: (i, j))]
    if emit_h:
        out_shape.append(jax.ShapeDtypeStruct((t, d), BF16))
        out_specs.append(pl.BlockSpec((tm, d), lambda i, j: (i, 0)))
    res = pl.pallas_call(
        functools.partial(_norm_matmul_kernel, emit_h=emit_h),
        grid=(t // tm, n // tn),
        in_specs=[pl.BlockSpec((tm, d), lambda i, j: (i, 0)),
                  pl.BlockSpec((1, d), lambda i, j: (0, 0)),
                  pl.BlockSpec((None, 1, d), lambda i, j: (i // per_b, 0, sc_blk)),
                  pl.BlockSpec((None, 1, d), lambda i, j: (i // per_b, 0, sh_blk)),
                  pl.BlockSpec((d, tn), lambda i, j: (0, j))],
        out_specs=out_specs,
        out_shape=out_shape,
        scratch_shapes=[pltpu.VMEM((tm, d), BF16)],
        compiler_params=_params(("parallel", "arbitrary")),
        name="norm_matmul",
    )(x2, g.reshape(1, d), mod3, mod3, w)
    return res if emit_h else res[0]


def _conv_kernel(ab_ref, ac_ref, ax_ref, ba_ref, bg_ref, pac_ref, pax_ref, pba_ref, pbg_ref,
                 aw_ref, bw_ref, bb_ref, lng_ref, lnb_ref, ua_ref, ub_ref, ext_ref, y_ref):
    ts = ab_ref.shape[0]
    keep = (pl.program_id(1) > 0).astype(F32)

    def f32(ref):
        return ref[...].astype(F32)

    def causal_conv(w_ref, taps, bias_ref, out_ref):
        blk = 128
        reach = ((taps - 1) // 8) * 8
        for t0 in range(0, ts, blk):
            for c0 in range(0, w_ref.shape[1], 128):
                cs = slice(c0, c0 + 128)
                acc = jnp.zeros((blk, 128), F32) if bias_ref is None else jnp.zeros((blk, 128), F32) + bias_ref[:, cs]
                for r in range(min(8, taps)):
                    base = HALO + t0 - reach - r
                    ur = ext_ref[base:base + blk + reach, cs]
                    for q in range(reach // 8 + 1):
                        shift = 8 * q + r
                        if shift < taps:
                            k = taps - 1 - shift
                            acc = acc + w_ref[k:k + 1, cs] * ur[reach - 8 * q:reach - 8 * q + blk]
                out_ref[t0:t0 + blk, cs] = acc

    ext_ref[0:HALO, :] = f32(pac_ref) * f32(pax_ref) * keep
    ext_ref[HALO:HALO + ts, :] = f32(ac_ref) * f32(ax_ref)
    causal_conv(aw_ref, A_CONV, None, y_ref)
    ua_ref[...] = (f32(ab_ref) * y_ref[...]).astype(BF16)

    ext_ref[0:HALO, :] = f32(pba_ref) * jax.nn.sigmoid(f32(pbg_ref)) * keep
    ext_ref[HALO:HALO + ts, :] = f32(ba_ref) * jax.nn.sigmoid(f32(bg_ref))
    causal_conv(bw_ref, B_CONV, bb_ref, y_ref)
    acc = y_ref[...]
    mu = jnp.mean(acc, axis=-1, keepdims=True)
    cen = acc - mu
    var = jnp.mean(cen * cen, axis=-1, keepdims=True)
    y = cen * lax.rsqrt(var + NORM_EPS) * lng_ref[...] + lnb_ref[...]
    ub_ref[...] = (y * jax.nn.sigmoid(y)).astype(BF16)


def _conv_mixers(proj3, a_conv_w, b_conv_w, b_conv_b, b_ln_g, b_ln_b):
    bsz = proj3.shape[0]
    ts = CONV_T
    c0 = COL_A // 512
    r = ts // HALO

    def cur(k):
        return pl.BlockSpec((None, ts, 512), lambda b, i, k=k: (b, i, c0 + k))

    def prev(k):
        return pl.BlockSpec((None, HALO, 512), lambda b, i, k=k: (b, jnp.maximum(i * r - 1, 0), c0 + k))

    def full(shape):
        return pl.BlockSpec(shape, lambda b, i: (0,) * len(shape))

    return pl.pallas_call(
        _conv_kernel,
        grid=(bsz, SEQ // ts),
        in_specs=[cur(0), cur(1), cur(2), cur(3), cur(4), prev(1), prev(2), prev(3), prev(4),
                  full((A_CONV, A_WIDTH)), full((B_CONV, B_WIDTH)), full((1, B_WIDTH)),
                  full((1, B_WIDTH)), full((1, B_WIDTH))],
        out_specs=[pl.BlockSpec((None, ts, 512), lambda b, i: (b, i, 0))] * 2,
        out_shape=[jax.ShapeDtypeStruct((bsz, SEQ, 512), BF16)] * 2,
        scratch_shapes=[pltpu.VMEM((HALO + ts, 512), F32), pltpu.VMEM((ts, 512), F32)],
        compiler_params=_params(("parallel", "arbitrary")),
        name="conv_mixers",
    )(proj3, proj3, proj3, proj3, proj3, proj3, proj3, proj3, proj3,
      a_conv_w, b_conv_w, b_conv_b.reshape(1, -1), b_ln_g.reshape(1, -1), b_ln_b.reshape(1, -1))


VT_ROWS = 80


def _values_t(v, n_heads):
    rows = v.shape[0]
    vt = v.T
    tail = (lax.broadcasted_iota(jnp.int32, (VT_ROWS - HEAD_DIM, rows), 0) == 0).astype(F32)
    blocks = []
    for h in range(n_heads):
        blocks += [vt[h * HEAD_DIM:(h + 1) * HEAD_DIM], tail]
    return jnp.concatenate(blocks, axis=0).astype(BF16)


def _prep_kernel(cq_ref, dq_ref, dk_ref, dv_ref, kvg_ref, cos_ref, sin_ref,
                 qc_ref, kc_ref, vc_ref, ks_ref, vst_ref, kw_ref, vwt_ref, gt_ref,
                 qd_ref, kd_ref, vdt_ref, kmean_ref):
    cos = cos_ref[...]
    sin = sin_ref[...]
    cos4 = jnp.concatenate([cos] * 4, axis=1)
    sin4 = jnp.concatenate([sin] * 4, axis=1)
    scale = np.float32(HEAD_DIM ** -0.5 * np.log2(np.e))
    qc_ref[...] = _rope(cq_ref[...].astype(F32), cos4, sin4) * scale
    qd_ref[...] = _rope(dq_ref[...].astype(F32), cos4, sin4) * scale
    kd = _rope(dk_ref[...].astype(F32), cos4, sin4)
    for h in range(D_HEADS):
        kd_ref[h] = kd[:, h * 64:(h + 1) * 64].astype(BF16)
    kmean_ref[...] = jnp.mean(kd, axis=0, keepdims=True)
    vdt_ref[...] = _values_t(dv_ref[...].astype(F32), D_HEADS)

    def kvg(k):
        return kvg_ref[:, k * 128:(k + 1) * 128].astype(F32)

    kc = _rope(kvg(0), cos, sin)
    vc = kvg(1)
    ks = _rope(kvg(2), cos, sin)
    kw = _rope(kvg(4), cos, sin)
    for g in range(C_KV_HEADS):
        gs = slice(g * 64, (g + 1) * 64)
        kc_ref[g] = kc[:, gs]
        vc_ref[g] = vc[:, gs]
        ks_ref[g] = ks[:, gs].astype(BF16)
        kw_ref[g] = kw[:, gs].astype(BF16)
    vst_ref[...] = _values_t(kvg(3), C_KV_HEADS)
    vwt_ref[...] = _values_t(kvg(5), C_KV_HEADS)
    gt_ref[...] = jax.nn.sigmoid(kvg(6)).T


def _prep(proj3, cos, sin):
    bsz = proj3.shape[0]
    ts = MOBA_BLOCK
    c0 = COL_A // 512

    def col512(k):
        return pl.BlockSpec((None, ts, 512), lambda b, i: (b, i, c0 + k))

    row128 = pl.BlockSpec((None, ts, 128), lambda b, i: (b, i, 0))
    row512 = pl.BlockSpec((None, ts, 512), lambda b, i: (b, i, 0))
    col128t = pl.BlockSpec((None, 128, ts), lambda b, i: (b, 0, i))
    val2t = pl.BlockSpec((None, C_KV_HEADS * VT_ROWS, ts), lambda b, i: (b, 0, i))
    head64 = pl.BlockSpec((None, C_KV_HEADS, ts, 64), lambda b, i: (b, 0, i, 0))
    k64 = jax.ShapeDtypeStruct((bsz, C_KV_HEADS, SEQ, 64), BF16)
    t128 = jax.ShapeDtypeStruct((bsz, C_KV_HEADS * VT_ROWS, SEQ), BF16)
    return pl.pallas_call(
        _prep_kernel,
        grid=(bsz, SEQ // ts),
        in_specs=[col512(5), col512(6), col512(7), col512(8),
                  pl.BlockSpec((None, ts, 1024), lambda b, i: (b, i, COL_KVG // 1024)),
                  row128, row128],
        out_specs=[row512, head64, head64, head64, val2t, head64, val2t, col128t,
                   row512,
                   pl.BlockSpec((None, D_HEADS, ts, 64), lambda b, i: (b, 0, i, 0)),
                   pl.BlockSpec((None, D_HEADS * VT_ROWS, ts), lambda b, i: (b, 0, i)),
                   pl.BlockSpec((None, None, 1, 512), lambda b, i: (b, i, 0, 0))],
        out_shape=[jax.ShapeDtypeStruct((bsz, SEQ, 512), F32),
                   jax.ShapeDtypeStruct((bsz, C_KV_HEADS, SEQ, 64), F32),
                   jax.ShapeDtypeStruct((bsz, C_KV_HEADS, SEQ, 64), F32),
                   k64, t128, k64, t128,
                   jax.ShapeDtypeStruct((bsz, 128, SEQ), F32),
                   jax.ShapeDtypeStruct((bsz, SEQ, 512), F32),
                   jax.ShapeDtypeStruct((bsz, D_HEADS, SEQ, 64), BF16),
                   jax.ShapeDtypeStruct((bsz, D_HEADS * VT_ROWS, SEQ), BF16),
                   jax.ShapeDtypeStruct((bsz, N_MOBA, 1, 512), F32)],
        compiler_params=_params(("parallel", "parallel")),
        name="attn_prep",
    )(proj3, proj3, proj3, proj3, proj3, cos, sin)


def _compress_kernel(kc_ref, vc_ref, pos_ref, w1_ref, w2_ref, kcmp_ref, vcmp_ref):
    half = CMP_STRIDE * HEAD_DIM
    row = lax.broadcasted_iota(jnp.int32, (128, CMP_HIDDEN), 0)
    for which, (src, dst) in enumerate(((kc_ref, kcmp_ref), (vc_ref, vcmp_ref))):
        bias = jnp.dot(pos_ref[which], w1_ref[which], precision=HIGHEST, preferred_element_type=F32)[0:1]
        for g in range(C_KV_HEADS):
            chunks = src[g]
            d1 = jnp.dot(chunks, w1_ref[which, 0:half, :], precision=HIGHEST, preferred_element_type=F32)
            d2 = jnp.dot(chunks, w1_ref[which, half:2 * half, :], precision=HIGHEST, preferred_element_type=F32)
            d2 = jnp.where(row < 127, pltpu.roll(d2, 127, 0), 0.0)
            hid = _gelu(d1 + d2 + bias)
            out = jnp.dot(hid, w2_ref[which], precision=HIGHEST, preferred_element_type=F32)
            dst[g] = out if which == 0 else out.T


def _compress(kc, vc, cmp_pos, cmp_w1, cmp_w2):
    bsz = kc.shape[0]
    kc4 = kc.reshape(bsz, C_KV_HEADS, SEQ // CMP_STRIDE, CMP_STRIDE * HEAD_DIM)
    vc4 = vc.reshape(bsz, C_KV_HEADS, SEQ // CMP_STRIDE, CMP_STRIDE * HEAD_DIM)
    pos8 = jnp.broadcast_to(cmp_pos.reshape(2, 1, CMP_BLOCK * HEAD_DIM), (2, 8, CMP_BLOCK * HEAD_DIM))
    blk = pl.BlockSpec((None, C_KV_HEADS, 128, 1024), lambda b: (b, 0, 0, 0))
    out = pl.BlockSpec((None, C_KV_HEADS, 128, 64), lambda b: (b, 0, 0, 0))
    return pl.pallas_call(
        _compress_kernel,
        grid=(bsz,),
        in_specs=[blk, blk,
                  pl.BlockSpec((2, 8, 2048), lambda b: (0, 0, 0)),
                  pl.BlockSpec((2, 2048, CMP_HIDDEN), lambda b: (0, 0, 0)),
                  pl.BlockSpec((2, CMP_HIDDEN, 64), lambda b: (0, 0, 0))],
        out_specs=[out, pl.BlockSpec((None, C_KV_HEADS, 64, 128), lambda b: (b, 0, 0, 0))],
        out_shape=[jax.ShapeDtypeStruct((bsz, C_KV_HEADS, 128, 64), F32),
                   jax.ShapeDtypeStruct((bsz, C_KV_HEADS, 64, 128), F32)],
        compiler_params=_params(("parallel",)),
        name="nsa_compress",
    )(kc4, vc4, pos8, cmp_w1, cmp_w2)


def _flash_steps_t(heads, m_ref, acc_ref, groups):
    scores = [lax.dot_general(k, qb, (((1,), (1,)), ((), ())), preferred_element_type=F32)
              for (qb, k, _, _) in heads]
    probs = []
    for n, (qb, _, _, mask) in enumerate(heads):
        s = scores[n]
        tq = qb.shape[0] // groups
        if mask is not None:
            s = jnp.concatenate([jnp.where(mask, s[:, r * tq:(r + 1) * tq], NEG) for r in range(groups)], axis=1)
        sr = _stat_row(n)
        m_old = m_ref[sr, :]
        m_new = jnp.maximum(m_old, jnp.max(s, axis=0, keepdims=True))
        m_ref[sr, :] = m_new
        probs.append((jnp.exp2(m_old - m_new), jnp.exp2(s - m_new).astype(BF16)))
    for n, (_, _, vt, _) in enumerate(heads):
        alpha, p = probs[n]
        a = slice(n * VT_ROWS, (n + 1) * VT_ROWS)
        acc_ref[a, :] = alpha * acc_ref[a, :] + jnp.dot(vt, p, preferred_element_type=F32)


def _flash_out(acc_ref, n):
    return acc_ref[n * VT_ROWS:n * VT_ROWS + HEAD_DIM, :] / acc_ref[n * VT_ROWS + HEAD_DIM:n * VT_ROWS + HEAD_DIM + 1, :]


def _stat_row(row):
    return slice(8 * row, 8 * row + 1)


def _flash_init(m_ref, acc_ref):
    m_ref[...] = jnp.full(m_ref.shape, NEG, F32)
    acc_ref[...] = jnp.zeros(acc_ref.shape, F32)


def _rank_desc_rows(vals):
    n = vals.shape[0]
    row = lax.broadcasted_iota(jnp.int32, vals.shape, 0)
    rank = jnp.zeros(vals.shape, jnp.int32)
    for i in range(n):
        vi = vals[i:i + 1, :]
        ahead = (vi > vals) | ((vi == vals) & (row > i))
        rank = rank + ahead.astype(jnp.int32)
    return rank


def _nsa_kernel(q_ref, kcmp_ref, vcmpt_ref, ks_ref, vst_ref, kw_ref, vwt_ref, gt_ref, ovlt_ref, expt_ref,
                o_ref, qb_ref, sel_ref, m_ref, acc_ref, out_ref):
    tq = ATT_T
    i = pl.program_id(1)
    t0 = i * tq
    t_row = t0 + lax.broadcasted_iota(jnp.int32, (1, tq), 1)
    t_rows = jnp.concatenate([t_row] * C_GROUP, axis=1)
    blk = lax.broadcasted_iota(jnp.int32, (128, 1), 0)
    kk = lax.broadcasted_iota(jnp.int32, (tq, tq), 0)
    qq = lax.broadcasted_iota(jnp.int32, (tq, tq), 1)
    causal = kk <= qq

    def gate_row(branch, g):
        base = branch * 8 + g * C_GROUP
        return jnp.concatenate([gt_ref[base + r:base + r + 1, :] for r in range(C_GROUP)], axis=1)

    def hrows(g):
        return slice(g * HEAD_DIM, (g + 1) * HEAD_DIM)

    def vrows(g):
        return slice(g * VT_ROWS, (g + 1) * VT_ROWS)

    for g in range(C_KV_HEADS):
        qf = jnp.concatenate([q_ref[:, (g * 4 + r) * 64:(g * 4 + r + 1) * 64] for r in range(C_GROUP)], axis=0)
        qb_ref[g] = qf.astype(BF16)

        s = lax.dot_general(kcmp_ref[g], qf, (((1,), (1,)), ((), ())), precision=HIGHEST,
                            preferred_element_type=F32)
        vis = (blk * CMP_STRIDE + (CMP_BLOCK - 1)) <= t_rows
        sm = jnp.where(vis, s, NEG)
        e = jnp.where(vis, jnp.exp2(sm - jnp.max(sm, axis=0, keepdims=True)), 0.0)
        p = e / jnp.maximum(jnp.sum(e, axis=0, keepdims=True), 1e-30)
        o_cmp = jnp.dot(vcmpt_ref[g].astype(BF16), p.astype(BF16), preferred_element_type=F32)
        out_ref[hrows(g), :] = gate_row(0, g) * o_cmp

        psum = p[:, 0:tq] + p[:, tq:2 * tq] + p[:, 2 * tq:3 * tq] + p[:, 3 * tq:4 * tq]
        imp = jnp.dot(ovlt_ref[...], psum, precision=HIGHEST, preferred_element_type=F32)[0:N_SLC]
        b32 = blk[0:N_SLC]
        cur = t_row // SLC_BLOCK
        forced = (b32 == 0) | (b32 == cur) | (b32 == cur - 1)
        imp = jnp.where(forced, jnp.inf, jnp.where(b32 * SLC_BLOCK > t_row, -jnp.inf, imp))
        sel = (_rank_desc_rows(imp) < SLC_TOPN).astype(F32)
        sel_ref[g] = jnp.concatenate([sel, jnp.zeros((128 - N_SLC, tq), F32)], axis=0).astype(BF16)

    def slc_heads(jb, extra):
        start = pl.multiple_of(jb * tq, tq)
        heads = []
        for g in range(C_KV_HEADS):
            hit = jnp.dot(expt_ref[pl.ds(start, tq), :], sel_ref[g], preferred_element_type=F32) > 0.5
            mask = hit if extra is None else hit & extra
            heads.append((qb_ref[g], ks_ref[g, pl.ds(start, tq), :], vst_ref[vrows(g), pl.ds(start, tq)], mask))
        return heads

    def win_heads():
        wk = WIN + tq
        wstart = pl.multiple_of(jnp.maximum(t0 - WIN, 0), tq)
        kpos = wstart + lax.broadcasted_iota(jnp.int32, (wk, 1), 0)
        mask = (kpos <= t_row) & (kpos > t_row - WIN)
        return [(qb_ref[g], kw_ref[g, pl.ds(wstart, wk), :], vwt_ref[vrows(g), pl.ds(wstart, wk)], mask)
                for g in range(C_KV_HEADS)]

    _flash_init(m_ref, acc_ref)
    _flash_steps_t(slc_heads(i, causal) + win_heads(), m_ref, acc_ref, C_GROUP)

    def slc_body(jb, carry):
        _flash_steps_t(slc_heads(jb, None), m_ref, acc_ref, C_GROUP)
        return carry

    lax.fori_loop(0, i, slc_body, 0)
    for g in range(C_KV_HEADS):
        out_ref[hrows(g), :] += (gate_row(1, g) * _flash_out(acc_ref, g)
                                 + gate_row(2, g) * _flash_out(acc_ref, C_KV_HEADS + g))

    o_t = jnp.concatenate([out_ref[hrows(g), r * tq:(r + 1) * tq]
                           for g in range(C_KV_HEADS) for r in range(C_GROUP)], axis=0)
    o_ref[...] = o_t.T.astype(BF16)


def _nsa_constants():
    j = np.arange(128)[:, None]
    n = np.arange(128)[None, :]
    ovl_t = ((n * CMP_STRIDE < j * SLC_BLOCK + SLC_BLOCK) & (n * CMP_STRIDE + CMP_BLOCK > j * SLC_BLOCK)
             & (n < N_CMP) & (j < N_SLC)).astype(np.float32)
    expand_t = (np.arange(SEQ)[:, None] // SLC_BLOCK == np.arange(128)[None, :]).astype(np.float32)
    return jnp.asarray(ovl_t), jnp.asarray(expand_t, dtype=BF16)


def _nsa(qc, kcmp, vcmpt, ks, vst, kw, vwt, gt):
    bsz = qc.shape[0]
    tq = ATT_T
    ovl_t, expand_t = _nsa_constants()
    keys = pl.BlockSpec((None, C_KV_HEADS, SEQ, 64), lambda b, i: (b, 0, 0, 0))
    vals = pl.BlockSpec((None, C_KV_HEADS * VT_ROWS, SEQ), lambda b, i: (b, 0, 0))
    rows = C_GROUP * tq
    return pl.pallas_call(
        _nsa_kernel,
        grid=(bsz, SEQ // tq),
        in_specs=[pl.BlockSpec((None, tq, 512), lambda b, i: (b, i, 0)),
                  pl.BlockSpec((None, C_KV_HEADS, 128, 64), lambda b, i: (b, 0, 0, 0)),
                  pl.BlockSpec((None, C_KV_HEADS, 64, 128), lambda b, i: (b, 0, 0, 0)),
                  keys, vals, keys, vals,
                  pl.BlockSpec((None, 128, tq), lambda b, i: (b, 0, i)),
                  pl.BlockSpec((128, 128), lambda b, i: (0, 0)),
                  pl.BlockSpec((SEQ, 128), lambda b, i: (0, 0))],
        out_specs=pl.BlockSpec((None, tq, 512), lambda b, i: (b, i, 0)),
        out_shape=jax.ShapeDtypeStruct((bsz, SEQ, 512), BF16),
        scratch_shapes=[pltpu.VMEM((C_KV_HEADS, rows, 64), BF16), pltpu.VMEM((C_KV_HEADS, 128, tq), BF16),
                        pltpu.VMEM((8 * 2 * C_KV_HEADS, rows), F32),
                        pltpu.VMEM((2 * C_KV_HEADS * VT_ROWS, rows), F32), pltpu.VMEM((C_KV_HEADS * 64, rows), F32)],
        compiler_params=_params(("parallel", "arbitrary")),
        name="nsa_attention",
    )(qc, kcmp, vcmpt, ks, vst, kw, vwt, gt, ovl_t, expand_t)


def _moba_kernel(q_ref, k_ref, vt_ref, kmean_ref, o_ref, qb_ref, sel_ref, m_ref, acc_ref):
    tq = ATT_T
    i = pl.program_id(1)
    blk = lax.broadcasted_iota(jnp.int32, (N_MOBA, 1), 0)
    kk = lax.broadcasted_iota(jnp.int32, (tq, tq), 0)
    qq = lax.broadcasted_iota(jnp.int32, (tq, tq), 1)
    causal = kk <= qq
    past = blk < i
    for h in range(D_HEADS):
        hs = slice(h * 64, (h + 1) * 64)
        qf = q_ref[:, hs]
        qb_ref[h] = qf.astype(BF16)
        gate = lax.dot_general(kmean_ref[:, hs], qf, (((1,), (1,)), ((), ())), precision=HIGHEST,
                               preferred_element_type=F32)
        gate = jnp.where(past, gate, -jnp.inf)
        sel_ref[h] = (past & (_rank_desc_rows(gate) < MOBA_TOPK)).astype(F32)

    def tile(jb, diag):
        start = pl.multiple_of(jb * tq, tq)
        heads = []
        for h in range(D_HEADS):
            if diag:
                mask = causal
            else:
                mask = jnp.sum(jnp.where(blk == jb, sel_ref[h], 0.0), axis=0, keepdims=True) > 0.5
            heads.append((qb_ref[h], k_ref[h, pl.ds(start, tq), :],
                          vt_ref[h * VT_ROWS:(h + 1) * VT_ROWS, pl.ds(start, tq)], mask))
        _flash_steps_t(heads, m_ref, acc_ref, 1)

    _flash_init(m_ref, acc_ref)
    tile(i, True)

    def body(jb, carry):
        tile(jb, False)
        return carry

    lax.fori_loop(0, i, body, 0)
    o_t = jnp.concatenate([_flash_out(acc_ref, h) for h in range(D_HEADS)], axis=0)
    o_ref[...] = o_t.T.astype(BF16)


def _moba(qd, kd, vdt, kmean):
    bsz = qd.shape[0]
    tq = ATT_T
    return pl.pallas_call(
        _moba_kernel,
        grid=(bsz, SEQ // tq),
        in_specs=[pl.BlockSpec((None, tq, 512), lambda b, i: (b, i, 0)),
                  pl.BlockSpec((None, D_HEADS, SEQ, 64), lambda b, i: (b, 0, 0, 0)),
                  pl.BlockSpec((None, D_HEADS * VT_ROWS, SEQ), lambda b, i: (b, 0, 0)),
                  pl.BlockSpec((None, N_MOBA, 512), lambda b, i: (b, 0, 0))],
        out_specs=pl.BlockSpec((None, tq, 512), lambda b, i: (b, i, 0)),
        out_shape=jax.ShapeDtypeStruct((bsz, SEQ, 512), BF16),
        scratch_shapes=[pltpu.VMEM((D_HEADS, tq, 64), BF16), pltpu.VMEM((D_HEADS, N_MOBA, tq), F32),
                        pltpu.VMEM((8 * D_HEADS, tq), F32), pltpu.VMEM((D_HEADS * VT_ROWS, tq), F32)],
        compiler_params=_params(("parallel", "arbitrary")),
        name="moba_attention",
    )(qd, kd, vdt, kmean.reshape(bsz, N_MOBA, 512))


def _merge_kernel(ua_ref, ub_ref, oc_ref, od_ref, mg_ref, x_ref, gt_ref,
                  wa_ref, wb_ref, wc_ref, wd_ref, wo_ref, o_ref):
    d = D_MODEL
    merged = jnp.zeros(x_ref.shape, F32)
    for k, (u_ref, w_ref) in enumerate(((ua_ref, wa_ref), (ub_ref, wb_ref), (oc_ref, wc_ref), (od_ref, wd_ref))):
        y = jnp.dot(u_ref[...], w_ref[...], preferred_element_type=F32)
        merged = merged + jax.nn.sigmoid(mg_ref[:, k * d:(k + 1) * d].astype(F32)) * y
    o_ref[...] = x_ref[...] + gt_ref[...] * jnp.dot(merged.astype(BF16), wo_ref[...], preferred_element_type=F32)


def _merge(ua, ub, oc, od, proj2, x2, mod3, wa, wb, wc, wd, wo, *, tm=512):
    t, d = x2.shape
    per_b = SEQ // tm
    act = pl.BlockSpec((tm, 512), lambda i: (i, 0))
    wspec = pl.BlockSpec((512, d), lambda i: (0, 0))
    return pl.pallas_call(
        _merge_kernel,
        grid=(t // tm,),
        in_specs=[act, act, act, act,
                  pl.BlockSpec((tm, 4 * d), lambda i: (i, COL_MERGE // (4 * d))),
                  pl.BlockSpec((tm, d), lambda i: (i, 0)),
                  pl.BlockSpec((None, 1, d), lambda i: (i // per_b, 0, 2)),
                  wspec, wspec, wspec, wspec,
                  pl.BlockSpec((d, d), lambda i: (0, 0))],
        out_specs=pl.BlockSpec((tm, d), lambda i: (i, 0)),
        out_shape=jax.ShapeDtypeStruct((t, d), F32),
        compiler_params=_params(("parallel",)),
        name="merge_out",
    )(ua, ub, oc, od, proj2, x2, mod3, wa, wb, wc, wd, wo)


def _pop_max(work, idx):
    m = jnp.max(work, axis=0, keepdims=True)
    first = jnp.min(jnp.where(work == m, idx, work.shape[0]), axis=0, keepdims=True)
    return m, idx == first


_PEER_PAIRS = [(i, j) for i in range(PEER_TOPK) for j in range(PEER_TOPK) if (i + 1) * (j + 1) <= PEER_TOPK]
_PEER_PAIR_ROWS = 56
NOT_TOP = 99.0


def _peer_sel_kernel(q_ref, keys_ref, grp_ref, n1_ref, e1_ref, r2_ref, e2_ref, st_ref):
    tt = q_ref.shape[0]
    half = PEER_QDIM // 2
    for hp in range(2 * PEER_HEADS):
        qh = q_ref[:, hp * half:(hp + 1) * half]
        qn = qh * lax.rsqrt(jnp.mean(qh * qh, axis=-1, keepdims=True) + NORM_EPS)
        st_ref[hp] = lax.dot_general(keys_ref[hp], qn, (((1,), (1,)), ((), ())), precision=HIGHEST,
                                     preferred_element_type=F32)

    idx = lax.broadcasted_iota(jnp.int32, (PEER_NKEYS, 128), 0)
    pidx = lax.broadcasted_iota(jnp.int32, (_PEER_PAIR_ROWS, 128), 0)
    pad_rows = jnp.full((_PEER_PAIR_ROWS - len(_PEER_PAIRS), 128), -jnp.inf, F32)

    def pop16(work, index, tie_safe, track):
        vals, order = [], jnp.full(work.shape, NOT_TOP, F32)
        for it in range(PEER_TOPK):
            if tie_safe:
                m, hit = _pop_max(work, index)
            else:
                m = jnp.max(work, axis=0, keepdims=True)
                hit = work == m
            work = jnp.where(hit, -jnp.inf, work)
            if track:
                order = jnp.where(hit, float(it), order)
            vals.append(m)
        removed = jnp.sum(jnp.where(work == -jnp.inf, 1.0, 0.0), axis=0, keepdims=True)
        return vals, order, removed

    def select(cs, tie_safe):
        wrong = jnp.zeros((1, 128), F32)
        n_pad = float(_PEER_PAIR_ROWS - len(_PEER_PAIRS))
        for h in range(PEER_HEADS):
            scores, tops, ranks = [], [], []
            for p in range(2):
                st = st_ref[2 * h + p, :, cs]
                vals, rank, removed = pop16(st, idx, tie_safe, track=tie_safe or p == 1)
                wrong = wrong + jnp.abs(removed - float(PEER_TOPK))
                scores.append(st)
                tops.append(vals)
                ranks.append(rank)
            cand = jnp.concatenate([tops[0][i] + tops[1][j] for (i, j) in _PEER_PAIRS] + [pad_rows], axis=0)
            best, order, removed = pop16(cand, pidx, tie_safe, track=tie_safe)
            wrong = wrong + jnp.abs(removed - (float(PEER_TOPK) + n_pad))
            picked = jnp.where((order < float(PEER_TOPK)) if tie_safe else (cand >= best[-1]), 1.0, 0.0)
            z = jnp.ones_like(best[0])
            for k in range(1, PEER_TOPK):
                z = z + jnp.exp(best[k] - best[0])
            count = jnp.dot(grp_ref[...], picked.astype(BF16), preferred_element_type=F32)
            n1 = jnp.zeros(scores[0].shape, F32)
            for i in reversed(range(PEER_TOPK)):
                at_i = (ranks[0] == float(i)) if tie_safe else (scores[0] >= tops[0][i])
                n1 = jnp.where(at_i, count[i:i + 1], n1)
            n1_ref[h, :, cs] = n1
            r2_ref[h, :, cs] = ranks[1]
            e1_ref[h, :, cs] = jnp.exp(scores[0] - tops[0][0]) / z
            e2_ref[h, :, cs] = jnp.exp(scores[1] - tops[1][0])
        return wrong

    def token_chunk(c, carry):
        cs = pl.ds(pl.multiple_of(c * 128, 128), 128)
        wrong = select(cs, tie_safe=False)

        @pl.when(jnp.max(wrong) > 0.0)
        def _():
            select(cs, tie_safe=True)

        return carry

    lax.fori_loop(0, tt // 128, token_chunk, 0)


def _peer_select(q2, subkeys, *, tt=512):
    t = q2.shape[0]
    grp = np.zeros((PEER_TOPK, _PEER_PAIR_ROWS), np.float32)
    for row, (i, _) in enumerate(_PEER_PAIRS):
        grp[i, row] = 1.0
    big = pl.BlockSpec((PEER_HEADS, PEER_NKEYS, tt), lambda i: (0, 0, i))
    bshape = jax.ShapeDtypeStruct((PEER_HEADS, PEER_NKEYS, t), F32)
    half = PEER_QDIM // 2
    return pl.pallas_call(
        _peer_sel_kernel,
        grid=(t // tt,),
        in_specs=[pl.BlockSpec((tt, PEER_HEADS * PEER_QDIM), lambda i: (i, 0)),
                  pl.BlockSpec((2 * PEER_HEADS, PEER_NKEYS, half), lambda i: (0, 0, 0)),
                  pl.BlockSpec((PEER_TOPK, _PEER_PAIR_ROWS), lambda i: (0, 0))],
        out_specs=[big, big, big, big],
        out_shape=[bshape, bshape, bshape, bshape],
        scratch_shapes=[pltpu.VMEM((2 * PEER_HEADS, PEER_NKEYS, tt), F32)],
        compiler_params=_params(("parallel",)),
        name="peer_select",
    )(q2, subkeys.reshape(2 * PEER_HEADS, PEER_NKEYS, half), jnp.asarray(grp, dtype=BF16))


PEER_SLAB = 256
PEER_AHEAD = 2

def _peer_main_kernel(h_ref, u_ref, vt_ref, n1_ref, e1_ref, r2_ref, e2_ref, x_ref, gt_ref, fg_ref,
                      o_ref, acc_ref, act_ref, p_ref, ht_ref, *, final):
    j = pl.program_id(1)
    te, tt = act_ref.shape
    na = te // PEER_NKEYS

    @pl.when(j == 0)
    def _():
        acc_ref[...] = jnp.zeros(acc_ref.shape, F32)
        ht_ref[...] = h_ref[...].astype(F32).T.astype(BF16)

    a0 = pl.multiple_of(j * na, na)
    nslab = te // PEER_SLAB
    parts = []

    def act(s):
        ss = slice(s * PEER_SLAB, (s + 1) * PEER_SLAB)
        act_ref[ss, :] = jnp.dot(u_ref[ss, :], ht_ref[...], preferred_element_type=F32)

    def val(s):
        ss = slice(s * PEER_SLAB, (s + 1) * PEER_SLAB)
        parts.append(jnp.dot(vt_ref[:, ss], p_ref[ss, :], preferred_element_type=F32))

    def mask(al):
        rs = slice(al * PEER_NKEYS, (al + 1) * PEER_NKEYS)
        for c in range(tt // 128):
            cs = slice(c * 128, (c + 1) * 128)
            w = jnp.zeros((PEER_NKEYS, 128), F32)
            for hh in range(PEER_HEADS):
                n1 = n1_ref[hh, pl.ds(a0, na), cs][al:al + 1]
                e1 = e1_ref[hh, pl.ds(a0, na), cs][al:al + 1]
                w = w + jnp.where(r2_ref[hh, :, cs] < n1, e1 * e2_ref[hh, :, cs], 0.0)
            p_ref[rs, cs] = (w * _gelu(act_ref[rs, cs])).astype(BF16)

    per = PEER_SLAB // PEER_NKEYS
    for s in range(PEER_AHEAD):
        act(s)
    for s in range(nslab):
        for k in range(per):
            mask(s * per + k)
            if k == 0 and s + PEER_AHEAD < nslab:
                act(s + PEER_AHEAD)
        val(s)
    acc_ref[...] += functools.reduce(lambda x, y: x + y, parts)

    @pl.when(j == pl.num_programs(1) - 1)
    def _():
        y = x_ref[...] + gt_ref[...] * acc_ref[...].T
        if final:
            y = y * lax.rsqrt(jnp.mean(y * y, axis=-1, keepdims=True) + NORM_EPS) * fg_ref[...]
        o_ref[...] = y


def _peer_main(h2, u_bf, v_bf, n1, e1, r2, e2, x2, mod3, final_g, *, final, tt=256, te=4096):
    t, d = x2.shape
    assert (te // PEER_NKEYS) % 8 == 0 and te % PEER_SLAB == 0
    per_b = SEQ // tt
    vt_bf = v_bf.reshape(PEER_EXPERTS // te, te, d).transpose(0, 2, 1)
    big = pl.BlockSpec((PEER_HEADS, PEER_NKEYS, tt), lambda i, j: (0, 0, i))
    return pl.pallas_call(
        functools.partial(_peer_main_kernel, final=final),
        grid=(t // tt, PEER_EXPERTS // te),
        in_specs=[pl.BlockSpec((tt, d), lambda i, j: (i, 0)),
                  pl.BlockSpec((te, d), lambda i, j: (j, 0)),
                  pl.BlockSpec((None, d, te), lambda i, j: (j, 0, 0)),
                  big, big, big, big,
                  pl.BlockSpec((tt, d), lambda i, j: (i, 0)),
                  pl.BlockSpec((None, 1, d), lambda i, j: (i // per_b, 0, 5)),
                  pl.BlockSpec((1, d), lambda i, j: (0, 0))],
        out_specs=pl.BlockSpec((tt, d), lambda i, j: (i, 0)),
        out_shape=jax.ShapeDtypeStruct((t, d), F32),
        scratch_shapes=[pltpu.VMEM((d, tt), F32), pltpu.VMEM((te, tt), F32), pltpu.VMEM((te, tt), BF16),
                        pltpu.VMEM((d, tt), BF16)],
        compiler_params=_params(("parallel", "arbitrary")),
        name="peer_experts",
    )(h2, u_bf, vt_bf, n1, e1, r2, e2, x2, mod3, final_g.reshape(1, d))


def _reorder_w_in(w):
    pad = jnp.zeros((w.shape[0], 1024 - 792), w.dtype)
    return jnp.concatenate([w[:, 5400:9496], w[:, 3072:3864], pad, w[:, 0:3072], w[:, 3864:5400]], axis=1)


def kernel(x, c, positions, mod_w, mod_b, norm_mix_g, norm_ffn_g, w_in, a_conv_w, a_out, b_conv_w, b_conv_b, b_ln_g, b_ln_b, b_out, c_cmp_pos, c_cmp_w1, c_cmp_w2, c_out, d_out, w_o, peer_wq, peer_subkeys, peer_u, peer_v, final_norm_g):
    bsz, s, d = x.shape
    assert s == SEQ and d == D_MODEL
    depth = mod_w.shape[0]
    t = bsz * s
    cos, sin = _rope_tables(positions)
    mod = _modulation(c, mod_w, mod_b)
    x2 = x.reshape(t, d)
    for l in range(depth):
        mod3 = mod[l].reshape(bsz, 1, 6 * d)
        proj2 = _norm_matmul(x2, norm_mix_g[l], mod3, 0, 1, _reorder_w_in(w_in[l].astype(BF16)), out_dtype=BF16,
                             tn=PROJ_COLS // 4)
        proj3 = proj2.reshape(bsz, s, PROJ_COLS)
        ua, ub = _conv_mixers(proj3, a_conv_w[l], b_conv_w[l], b_conv_b[l], b_ln_g[l], b_ln_b[l])
        qc, kc, vc, ks, vst, kw, vwt, gt, qd, kd, vdt, kmean = _prep(proj3, cos, sin)
        kcmp, vcmpt = _compress(kc, vc, c_cmp_pos[l], c_cmp_w1[l], c_cmp_w2[l])
        oc = _nsa(qc, kcmp, vcmpt, ks, vst, kw, vwt, gt)
        od = _moba(qd, kd, vdt, kmean)
        x2 = _merge(ua.reshape(t, 512), ub.reshape(t, 512), oc.reshape(t, 512), od.reshape(t, 512),
                    proj2, x2, mod3, a_out[l].astype(BF16), b_out[l].astype(BF16), c_out[l].astype(BF16),
                    d_out[l].astype(BF16), w_o[l].astype(BF16))
        q2, h2 = _norm_matmul(x2, norm_ffn_g[l], mod3, 3, 4, peer_wq[l].astype(BF16), emit_h=True,
                              tn=PEER_HEADS * PEER_QDIM)
        n1, e1, r2, e2 = _peer_select(q2, peer_subkeys[l])
        x2 = _peer_main(h2, peer_u[l].astype(BF16), peer_v[l].astype(BF16), n1, e1, r2, e2,
                        x2, mod3, final_norm_g, final=(l == depth - 1))
    return x2.reshape(bsz, s, d)
```

```python
import functools

import numpy as np
import jax
import jax.numpy as jnp
from jax import lax
from jax.experimental import pallas as pl
from jax.experimental.pallas import tpu as pltpu

F32 = jnp.float32
BF16 = jnp.bfloat16
HIGHEST = lax.Precision.HIGHEST

D_MODEL = 1024
SEQ = 2048
HEAD_DIM = 64
ROPE_THETA = 10000.0
NORM_EPS = 1e-6
A_WIDTH = 512
A_CONV = 3
B_WIDTH = 512
B_CONV = 31
C_HEADS = 8
C_KV_HEADS = 2
C_GROUP = 4
CMP_BLOCK = 32
CMP_STRIDE = 16
CMP_HIDDEN = 128
N_CMP = (SEQ - CMP_BLOCK) // CMP_STRIDE + 1
SLC_BLOCK = 64
SLC_TOPN = 16
N_SLC = SEQ // SLC_BLOCK
WIN = 512
D_HEADS = 8
MOBA_BLOCK = 256
MOBA_TOPK = 3
N_MOBA = SEQ // MOBA_BLOCK
PEER_HEADS = 8
PEER_NKEYS = 128
PEER_EXPERTS = PEER_NKEYS * PEER_NKEYS
PEER_QDIM = 256
PEER_TOPK = 16

PROJ_COLS = 9728
COL_MERGE = 0
COL_KVG = 4096
COL_GATE = COL_KVG + 768
COL_A = 5120

VMEM_LIMIT = 56 * 1024 * 1024
NEG = -1e30

ATT_T = 256
CONV_T = 256
HALO = 32


def _params(sem, flags=None):
    return pltpu.CompilerParams(dimension_semantics=sem, vmem_limit_bytes=VMEM_LIMIT, flags=flags)


def _gelu(x):
    return 0.5 * x * (1.0 + lax.erf(x * np.float32(np.sqrt(0.5))))


def _rope_table_kernel(pos_ref, inv_ref, sign_ref, cos_ref, sin_ref):
    ang = pos_ref[...] * inv_ref[...]
    cos_ref[...] = jnp.cos(ang)
    sin_ref[...] = jnp.sin(ang) * sign_ref[...]


def _rope_tables(positions):
    bsz, s = positions.shape
    inv = 1.0 / (ROPE_THETA ** (jnp.arange(0, HEAD_DIM, 2, dtype=F32) / HEAD_DIM))
    inv128 = jnp.tile(inv, 4)[None, :]
    sign = jnp.tile(jnp.concatenate([-jnp.ones(32, F32), jnp.ones(32, F32)]), 2)[None, :]
    pos = positions.astype(F32).reshape(bsz * s, 1)
    t = bsz * s
    cos, sin = pl.pallas_call(
        _rope_table_kernel,
        grid=(t // SEQ,),
        in_specs=[pl.BlockSpec((SEQ, 1), lambda i: (i, 0)),
                  pl.BlockSpec((1, 128), lambda i: (0, 0)),
                  pl.BlockSpec((1, 128), lambda i: (0, 0))],
        out_specs=[pl.BlockSpec((SEQ, 128), lambda i: (i, 0))] * 2,
        out_shape=[jax.ShapeDtypeStruct((t, 128), F32)] * 2,
        compiler_params=_params(("parallel",)),
        name="rope_tables",
    )(pos, inv128, sign)
    return cos.reshape(bsz, s, 128), sin.reshape(bsz, s, 128)


def _rope(x, cos, sin):
    w = x.shape[-1]
    lane = lax.broadcasted_iota(jnp.int32, x.shape, 1)
    swapped = jnp.where(lane % 64 < 32, pltpu.roll(x, w - 32, 1), pltpu.roll(x, 32, 1))
    return x * cos + swapped * sin


def _mod_kernel(c_ref, w_ref, b_ref, o_ref):
    c = c_ref[...]
    cond = c * jax.nn.sigmoid(c)
    o_ref[...] = jnp.dot(cond, w_ref[...], precision=HIGHEST, preferred_element_type=F32) + b_ref[...]


def _modulation(c, mod_w, mod_b):
    nl, d, n = mod_w.shape
    bsz = c.shape[0]
    tn = 1536
    return pl.pallas_call(
        _mod_kernel,
        grid=(nl, n // tn),
        in_specs=[pl.BlockSpec((bsz, d), lambda l, j: (0, 0)),
                  pl.BlockSpec((None, d, tn), lambda l, j: (l, 0, j)),
                  pl.BlockSpec((None, 1, tn), lambda l, j: (l, 0, j))],
        out_specs=pl.BlockSpec((None, bsz, tn), lambda l, j: (l, 0, j)),
        out_shape=jax.ShapeDtypeStruct((nl, bsz, n), F32),
        compiler_params=_params(("parallel", "parallel")),
        name="adaln_mod",
    )(c, mod_w, mod_b.reshape(nl, 1, n))


def _norm_matmul_kernel(x_ref, g_ref, sc_ref, sh_ref, w_ref, o_ref, *rest, emit_h):
    h_scr = rest[-1]

    @pl.when(pl.program_id(1) == 0)
    def _():
        x = x_ref[...]
        y = x * lax.rsqrt(jnp.mean(x * x, axis=-1, keepdims=True) + NORM_EPS)
        h = (y * g_ref[...]) * (1.0 + sc_ref[...]) + sh_ref[...]
        h_scr[...] = h.astype(BF16)
        if emit_h:
            rest[0][...] = h.T.astype(BF16)

    o_ref[...] = jnp.dot(h_scr[...], w_ref[...], preferred_element_type=F32).astype(o_ref.dtype)


def _norm_matmul(x2, g, mod3, sh_blk, sc_blk, w, *, out_dtype=F32, emit_h=False, tm=1024, tn=512):
    t, d = x2.shape
    n = w.shape[1]
    per_b = SEQ // tm
    out_shape = [jax.ShapeDtypeStruct((t, n), out_dtype)]
    out_specs = [pl.BlockSpec((tm, tn), lambda i, j: (i, j))]
    if emit_h:
        out_shape.append(jax.ShapeDtypeStruct((d, t), BF16))
        out_specs.append(pl.BlockSpec((d, tm), lambda i, j: (0, i)))
    res = pl.pallas_call(
        functools.partial(_norm_matmul_kernel, emit_h=emit_h),
        grid=(t // tm, n // tn),
        in_specs=[pl.BlockSpec((tm, d), lambda i, j: (i, 0)),
                  pl.BlockSpec((1, d), lambda i, j: (0, 0)),
                  pl.BlockSpec((None, 1, d), lambda i, j: (i // per_b, 0, sc_blk)),
                  pl.BlockSpec((None, 1, d), lambda i, j: (i // per_b, 0, sh_blk)),
                  pl.BlockSpec((d, tn), lambda i, j: (0, j))],
        out_specs=out_specs,
        out_shape=out_shape,
        scratch_shapes=[pltpu.VMEM((tm, d), BF16)],
        compiler_params=_params(("parallel", "arbitrary")),
        name="norm_matmul",
    )(x2, g.reshape(1, d), mod3, mod3, w)
    return res if emit_h else res[0]


def _conv_kernel(ab_ref, ac_ref, ax_ref, ba_ref, bg_ref, pac_ref, pax_ref, pba_ref, pbg_ref,
                 aw_ref, bw_ref, bb_ref, lng_ref, lnb_ref, ua_ref, ub_ref, ext_ref, y_ref):
    ts = ab_ref.shape[0]
    keep = (pl.program_id(1) > 0).astype(F32)

    def f32(ref):
        return ref[...].astype(F32)

    def causal_conv(w_ref, taps, bias_ref, out_ref):
        blk = 128
        reach = ((taps - 1) // 8) * 8
        for t0 in range(0, ts, blk):
            for c0 in range(0, w_ref.shape[1], 128):
                cs = slice(c0, c0 + 128)
                acc = jnp.zeros((blk, 128), F32) if bias_ref is None else jnp.zeros((blk, 128), F32) + bias_ref[:, cs]
                for r in range(min(8, taps)):
                    base = HALO + t0 - reach - r
                    ur = ext_ref[base:base + blk + reach, cs]
                    for q in range(reach // 8 + 1):
                        shift = 8 * q + r
                        if shift < taps:
                            k = taps - 1 - shift
                            acc = acc + w_ref[k:k + 1, cs] * ur[reach - 8 * q:reach - 8 * q + blk]
                out_ref[t0:t0 + blk, cs] = acc

    ext_ref[0:HALO, :] = f32(pac_ref) * f32(pax_ref) * keep
    ext_ref[HALO:HALO + ts, :] = f32(ac_ref) * f32(ax_ref)
    causal_conv(aw_ref, A_CONV, None, y_ref)
    ua_ref[...] = (f32(ab_ref) * y_ref[...]).astype(BF16)

    ext_ref[0:HALO, :] = f32(pba_ref) * jax.nn.sigmoid(f32(pbg_ref)) * keep
    ext_ref[HALO:HALO + ts, :] = f32(ba_ref) * jax.nn.sigmoid(f32(bg_ref))
    causal_conv(bw_ref, B_CONV, bb_ref, y_ref)
    acc = y_ref[...]
    mu = jnp.mean(acc, axis=-1, keepdims=True)
    cen = acc - mu
    var = jnp.mean(cen * cen, axis=-1, keepdims=True)
    y = cen * lax.rsqrt(var + NORM_EPS) * lng_ref[...] + lnb_ref[...]
    ub_ref[...] = (y * jax.nn.sigmoid(y)).astype(BF16)


def _conv_mixers(proj3, a_conv_w, b_conv_w, b_conv_b, b_ln_g, b_ln_b):
    bsz = proj3.shape[0]
    ts = CONV_T
    c0 = COL_A // 512
    r = ts // HALO

    def cur(k):
        return pl.BlockSpec((None, ts, 512), lambda b, i, k=k: (b, i, c0 + k))

    def prev(k):
        return pl.BlockSpec((None, HALO, 512), lambda b, i, k=k: (b, jnp.maximum(i * r - 1, 0), c0 + k))

    def full(shape):
        return pl.BlockSpec(shape, lambda b, i: (0,) * len(shape))

    return pl.pallas_call(
        _conv_kernel,
        grid=(bsz, SEQ // ts),
        in_specs=[cur(0), cur(1), cur(2), cur(3), cur(4), prev(1), prev(2), prev(3), prev(4),
                  full((A_CONV, A_WIDTH)), full((B_CONV, B_WIDTH)), full((1, B_WIDTH)),
                  full((1, B_WIDTH)), full((1, B_WIDTH))],
        out_specs=[pl.BlockSpec((None, ts, 512), lambda b, i: (b, i, 0))] * 2,
        out_shape=[jax.ShapeDtypeStruct((bsz, SEQ, 512), BF16)] * 2,
        scratch_shapes=[pltpu.VMEM((HALO + ts, 512), F32), pltpu.VMEM((ts, 512), F32)],
        compiler_params=_params(("parallel", "arbitrary")),
        name="conv_mixers",
    )(proj3, proj3, proj3, proj3, proj3, proj3, proj3, proj3, proj3,
      a_conv_w, b_conv_w, b_conv_b.reshape(1, -1), b_ln_g.reshape(1, -1), b_ln_b.reshape(1, -1))


VT_ROWS = 80


def _values_t(v, n_heads):
    rows = v.shape[0]
    vt = v.T
    tail = (lax.broadcasted_iota(jnp.int32, (VT_ROWS - HEAD_DIM, rows), 0) == 0).astype(F32)
    blocks = []
    for h in range(n_heads):
        blocks += [vt[h * HEAD_DIM:(h + 1) * HEAD_DIM], tail]
    return jnp.concatenate(blocks, axis=0).astype(BF16)


def _prep_kernel(cq_ref, dq_ref, dk_ref, dv_ref, kvg_ref, cos_ref, sin_ref,
                 qc_ref, kc_ref, vc_ref, ks_ref, vst_ref, kw_ref, vwt_ref, gt_ref,
                 qd_ref, kd_ref, vdt_ref, kmean_ref):
    cos = cos_ref[...]
    sin = sin_ref[...]
    cos4 = jnp.concatenate([cos] * 4, axis=1)
    sin4 = jnp.concatenate([sin] * 4, axis=1)
    scale = np.float32(HEAD_DIM ** -0.5 * np.log2(np.e))
    qc_ref[...] = _rope(cq_ref[...].astype(F32), cos4, sin4) * scale
    qd_ref[...] = _rope(dq_ref[...].astype(F32), cos4, sin4) * scale
    kd = _rope(dk_ref[...].astype(F32), cos4, sin4)
    for h in range(D_HEADS):
        kd_ref[h] = kd[:, h * 64:(h + 1) * 64].astype(BF16)
    kmean_ref[...] = jnp.mean(kd, axis=0, keepdims=True)
    vdt_ref[...] = _values_t(dv_ref[...].astype(F32), D_HEADS)

    def kvg(k):
        return kvg_ref[:, k * 128:(k + 1) * 128].astype(F32)

    kc = _rope(kvg(0), cos, sin)
    vc = kvg(1)
    ks = _rope(kvg(2), cos, sin)
    kw = _rope(kvg(4), cos, sin)
    for g in range(C_KV_HEADS):
        gs = slice(g * 64, (g + 1) * 64)
        kc_ref[g] = kc[:, gs]
        vc_ref[g] = vc[:, gs]
        ks_ref[g] = ks[:, gs].astype(BF16)
        kw_ref[g] = kw[:, gs].astype(BF16)
    vst_ref[...] = _values_t(kvg(3), C_KV_HEADS)
    vwt_ref[...] = _values_t(kvg(5), C_KV_HEADS)
    gt_ref[...] = jax.nn.sigmoid(kvg(6)).T


def _prep(proj3, cos, sin):
    bsz = proj3.shape[0]
    ts = MOBA_BLOCK
    c0 = COL_A // 512

    def col512(k):
        return pl.BlockSpec((None, ts, 512), lambda b, i: (b, i, c0 + k))

    row128 = pl.BlockSpec((None, ts, 128), lambda b, i: (b, i, 0))
    row512 = pl.BlockSpec((None, ts, 512), lambda b, i: (b, i, 0))
    col128t = pl.BlockSpec((None, 128, ts), lambda b, i: (b, 0, i))
    val2t = pl.BlockSpec((None, C_KV_HEADS * VT_ROWS, ts), lambda b, i: (b, 0, i))
    head64 = pl.BlockSpec((None, C_KV_HEADS, ts, 64), lambda b, i: (b, 0, i, 0))
    k64 = jax.ShapeDtypeStruct((bsz, C_KV_HEADS, SEQ, 64), BF16)
    t128 = jax.ShapeDtypeStruct((bsz, C_KV_HEADS * VT_ROWS, SEQ), BF16)
    return pl.pallas_call(
        _prep_kernel,
        grid=(bsz, SEQ // ts),
        in_specs=[col512(5), col512(6), col512(7), col512(8),
                  pl.BlockSpec((None, ts, 1024), lambda b, i: (b, i, COL_KVG // 1024)),
                  row128, row128],
        out_specs=[row512, head64, head64, head64, val2t, head64, val2t, col128t,
                   row512,
                   pl.BlockSpec((None, D_HEADS, ts, 64), lambda b, i: (b, 0, i, 0)),
                   pl.BlockSpec((None, D_HEADS * VT_ROWS, ts), lambda b, i: (b, 0, i)),
                   pl.BlockSpec((None, None, 1, 512), lambda b, i: (b, i, 0, 0))],
        out_shape=[jax.ShapeDtypeStruct((bsz, SEQ, 512), F32),
                   jax.ShapeDtypeStruct((bsz, C_KV_HEADS, SEQ, 64), F32),
                   jax.ShapeDtypeStruct((bsz, C_KV_HEADS, SEQ, 64), F32),
                   k64, t128, k64, t128,
                   jax.ShapeDtypeStruct((bsz, 128, SEQ), F32),
                   jax.ShapeDtypeStruct((bsz, SEQ, 512), F32),
                   jax.ShapeDtypeStruct((bsz, D_HEADS, SEQ, 64), BF16),
                   jax.ShapeDtypeStruct((bsz, D_HEADS * VT_ROWS, SEQ), BF16),
                   jax.ShapeDtypeStruct((bsz, N_MOBA, 1, 512), F32)],
        compiler_params=_params(("parallel", "parallel")),
        name="attn_prep",
    )(proj3, proj3, proj3, proj3, proj3, cos, sin)


def _compress_kernel(kc_ref, vc_ref, pos_ref, w1_ref, w2_ref, kcmp_ref, vcmp_ref):
    half = CMP_STRIDE * HEAD_DIM
    row = lax.broadcasted_iota(jnp.int32, (128, CMP_HIDDEN), 0)
    for which, (src, dst) in enumerate(((kc_ref, kcmp_ref), (vc_ref, vcmp_ref))):
        bias = jnp.dot(pos_ref[which], w1_ref[which], precision=HIGHEST, preferred_element_type=F32)[0:1]
        for g in range(C_KV_HEADS):
            chunks = src[g]
            d1 = jnp.dot(chunks, w1_ref[which, 0:half, :], precision=HIGHEST, preferred_element_type=F32)
            d2 = jnp.dot(chunks, w1_ref[which, half:2 * half, :], precision=HIGHEST, preferred_element_type=F32)
            d2 = jnp.where(row < 127, pltpu.roll(d2, 127, 0), 0.0)
            hid = _gelu(d1 + d2 + bias)
            out = jnp.dot(hid, w2_ref[which], precision=HIGHEST, preferred_element_type=F32)
            dst[g] = out if which == 0 else out.T


def _compress(kc, vc, cmp_pos, cmp_w1, cmp_w2):
    bsz = kc.shape[0]
    kc4 = kc.reshape(bsz, C_KV_HEADS, SEQ // CMP_STRIDE, CMP_STRIDE * HEAD_DIM)
    vc4 = vc.reshape(bsz, C_KV_HEADS, SEQ // CMP_STRIDE, CMP_STRIDE * HEAD_DIM)
    pos8 = jnp.broadcast_to(cmp_pos.reshape(2, 1, CMP_BLOCK * HEAD_DIM), (2, 8, CMP_BLOCK * HEAD_DIM))
    blk = pl.BlockSpec((None, C_KV_HEADS, 128, 1024), lambda b: (b, 0, 0, 0))
    out = pl.BlockSpec((None, C_KV_HEADS, 128, 64), lambda b: (b, 0, 0, 0))
    return pl.pallas_call(
        _compress_kernel,
        grid=(bsz,),
        in_specs=[blk, blk,
                  pl.BlockSpec((2, 8, 2048), lambda b: (0, 0, 0)),
                  pl.BlockSpec((2, 2048, CMP_HIDDEN), lambda b: (0, 0, 0)),
                  pl.BlockSpec((2, CMP_HIDDEN, 64), lambda b: (0, 0, 0))],
        out_specs=[out, pl.BlockSpec((None, C_KV_HEADS, 64, 128), lambda b: (b, 0, 0, 0))],
        out_shape=[jax.ShapeDtypeStruct((bsz, C_KV_HEADS, 128, 64), F32),
                   jax.ShapeDtypeStruct((bsz, C_KV_HEADS, 64, 128), F32)],
        compiler_params=_params(("parallel",)),
        name="nsa_compress",
    )(kc4, vc4, pos8, cmp_w1, cmp_w2)


def _flash_steps_t(heads, m_ref, acc_ref, groups):
    scores = [lax.dot_general(k, qb, (((1,), (1,)), ((), ())), preferred_element_type=F32)
              for (qb, k, _, _) in heads]
    probs = []
    for n, (qb, _, _, mask) in enumerate(heads):
        s = scores[n]
        tq = qb.shape[0] // groups
        if mask is not None:
            s = jnp.concatenate([jnp.where(mask, s[:, r * tq:(r + 1) * tq], NEG) for r in range(groups)], axis=1)
        sr = _stat_row(n)
        m_old = m_ref[sr, :]
        m_new = jnp.maximum(m_old, jnp.max(s, axis=0, keepdims=True))
        m_ref[sr, :] = m_new
        probs.append((jnp.exp2(m_old - m_new), jnp.exp2(s - m_new).astype(BF16)))
    for n, (_, _, vt, _) in enumerate(heads):
        alpha, p = probs[n]
        a = slice(n * VT_ROWS, (n + 1) * VT_ROWS)
        acc_ref[a, :] = alpha * acc_ref[a, :] + jnp.dot(vt, p, preferred_element_type=F32)


def _flash_out(acc_ref, n):
    return acc_ref[n * VT_ROWS:n * VT_ROWS + HEAD_DIM, :] / acc_ref[n * VT_ROWS + HEAD_DIM:n * VT_ROWS + HEAD_DIM + 1, :]


def _stat_row(row):
    return slice(8 * row, 8 * row + 1)


def _flash_init(m_ref, acc_ref):
    m_ref[...] = jnp.full(m_ref.shape, NEG, F32)
    acc_ref[...] = jnp.zeros(acc_ref.shape, F32)


def _rank_desc_rows(vals):
    n = vals.shape[0]
    row = lax.broadcasted_iota(jnp.int32, vals.shape, 0)
    rank = jnp.zeros(vals.shape, jnp.int32)
    for i in range(n):
        vi = vals[i:i + 1, :]
        ahead = (vi > vals) | ((vi == vals) & (row > i))
        rank = rank + ahead.astype(jnp.int32)
    return rank


def _nsa_kernel(q_ref, kcmp_ref, vcmpt_ref, ks_ref, vst_ref, kw_ref, vwt_ref, gt_ref, ovlt_ref, expt_ref,
                o_ref, qb_ref, sel_ref, m_ref, acc_ref, out_ref):
    tq = ATT_T
    i = pl.program_id(1)
    t0 = i * tq
    t_row = t0 + lax.broadcasted_iota(jnp.int32, (1, tq), 1)
    t_rows = jnp.concatenate([t_row] * C_GROUP, axis=1)
    blk = lax.broadcasted_iota(jnp.int32, (128, 1), 0)
    kk = lax.broadcasted_iota(jnp.int32, (tq, tq), 0)
    qq = lax.broadcasted_iota(jnp.int32, (tq, tq), 1)
    causal = kk <= qq

    def gate_row(branch, g):
        base = branch * 8 + g * C_GROUP
        return jnp.concatenate([gt_ref[base + r:base + r + 1, :] for r in range(C_GROUP)], axis=1)

    def hrows(g):
        return slice(g * HEAD_DIM, (g + 1) * HEAD_DIM)

    def vrows(g):
        return slice(g * VT_ROWS, (g + 1) * VT_ROWS)

    for g in range(C_KV_HEADS):
        qf = jnp.concatenate([q_ref[:, (g * 4 + r) * 64:(g * 4 + r + 1) * 64] for r in range(C_GROUP)], axis=0)
        qb_ref[g] = qf.astype(BF16)

        s = lax.dot_general(kcmp_ref[g], qf, (((1,), (1,)), ((), ())), precision=HIGHEST,
                            preferred_element_type=F32)
        vis = (blk * CMP_STRIDE + (CMP_BLOCK - 1)) <= t_rows
        sm = jnp.where(vis, s, NEG)
        e = jnp.where(vis, jnp.exp2(sm - jnp.max(sm, axis=0, keepdims=True)), 0.0)
        p = e / jnp.maximum(jnp.sum(e, axis=0, keepdims=True), 1e-30)
        o_cmp = jnp.dot(vcmpt_ref[g].astype(BF16), p.astype(BF16), preferred_element_type=F32)
        out_ref[hrows(g), :] = gate_row(0, g) * o_cmp

        psum = p[:, 0:tq] + p[:, tq:2 * tq] + p[:, 2 * tq:3 * tq] + p[:, 3 * tq:4 * tq]
        imp = jnp.dot(ovlt_ref[...], psum, precision=HIGHEST, preferred_element_type=F32)[0:N_SLC]
        b32 = blk[0:N_SLC]
        cur = t_row // SLC_BLOCK
        forced = (b32 == 0) | (b32 == cur) | (b32 == cur - 1)
        imp = jnp.where(forced, jnp.inf, jnp.where(b32 * SLC_BLOCK > t_row, -jnp.inf, imp))
        sel = (_rank_desc_rows(imp) < SLC_TOPN).astype(F32)
        sel_ref[g] = jnp.concatenate([sel, jnp.zeros((128 - N_SLC, tq), F32)], axis=0).astype(BF16)

    def slc_heads(jb, extra):
        start = pl.multiple_of(jb * tq, tq)
        heads = []
        for g in range(C_KV_HEADS):
            hit = jnp.dot(expt_ref[pl.ds(start, tq), :], sel_ref[g], preferred_element_type=F32) > 0.5
            mask = hit if extra is None else hit & extra
            heads.append((qb_ref[g], ks_ref[g, pl.ds(start, tq), :], vst_ref[vrows(g), pl.ds(start, tq)], mask))
        return heads

    def win_heads():
        wk = WIN + tq
        wstart = pl.multiple_of(jnp.maximum(t0 - WIN, 0), tq)
        kpos = wstart + lax.broadcasted_iota(jnp.int32, (wk, 1), 0)
        mask = (kpos <= t_row) & (kpos > t_row - WIN)
        return [(qb_ref[g], kw_ref[g, pl.ds(wstart, wk), :], vwt_ref[vrows(g), pl.ds(wstart, wk)], mask)
                for g in range(C_KV_HEADS)]

    _flash_init(m_ref, acc_ref)
    _flash_steps_t(slc_heads(i, causal) + win_heads(), m_ref, acc_ref, C_GROUP)

    def slc_body(jb, carry):
        _flash_steps_t(slc_heads(jb, None), m_ref, acc_ref, C_GROUP)
        return carry

    lax.fori_loop(0, i, slc_body, 0)
    for g in range(C_KV_HEADS):
        out_ref[hrows(g), :] += (gate_row(1, g) * _flash_out(acc_ref, g)
                                 + gate_row(2, g) * _flash_out(acc_ref, C_KV_HEADS + g))

    o_t = jnp.concatenate([out_ref[hrows(g), r * tq:(r + 1) * tq]
                           for g in range(C_KV_HEADS) for r in range(C_GROUP)], axis=0)
    o_ref[...] = o_t.T.astype(BF16)


def _nsa_constants():
    j = np.arange(128)[:, None]
    n = np.arange(128)[None, :]
    ovl_t = ((n * CMP_STRIDE < j * SLC_BLOCK + SLC_BLOCK) & (n * CMP_STRIDE + CMP_BLOCK > j * SLC_BLOCK)
             & (n < N_CMP) & (j < N_SLC)).astype(np.float32)
    expand_t = (np.arange(SEQ)[:, None] // SLC_BLOCK == np.arange(128)[None, :]).astype(np.float32)
    return jnp.asarray(ovl_t), jnp.asarray(expand_t, dtype=BF16)


def _nsa(qc, kcmp, vcmpt, ks, vst, kw, vwt, gt):
    bsz = qc.shape[0]
    tq = ATT_T
    ovl_t, expand_t = _nsa_constants()
    keys = pl.BlockSpec((None, C_KV_HEADS, SEQ, 64), lambda b, i: (b, 0, 0, 0))
    vals = pl.BlockSpec((None, C_KV_HEADS * VT_ROWS, SEQ), lambda b, i: (b, 0, 0))
    rows = C_GROUP * tq
    return pl.pallas_call(
        _nsa_kernel,
        grid=(bsz, SEQ // tq),
        in_specs=[pl.BlockSpec((None, tq, 512), lambda b, i: (b, i, 0)),
                  pl.BlockSpec((None, C_KV_HEADS, 128, 64), lambda b, i: (b, 0, 0, 0)),
                  pl.BlockSpec((None, C_KV_HEADS, 64, 128), lambda b, i: (b, 0, 0, 0)),
                  keys, vals, keys, vals,
                  pl.BlockSpec((None, 128, tq), lambda b, i: (b, 0, i)),
                  pl.BlockSpec((128, 128), lambda b, i: (0, 0)),
                  pl.BlockSpec((SEQ, 128), lambda b, i: (0, 0))],
        out_specs=pl.BlockSpec((None, tq, 512), lambda b, i: (b, i, 0)),
        out_shape=jax.ShapeDtypeStruct((bsz, SEQ, 512), BF16),
        scratch_shapes=[pltpu.VMEM((C_KV_HEADS, rows, 64), BF16), pltpu.VMEM((C_KV_HEADS, 128, tq), BF16),
                        pltpu.VMEM((8 * 2 * C_KV_HEADS, rows), F32),
                        pltpu.VMEM((2 * C_KV_HEADS * VT_ROWS, rows), F32), pltpu.VMEM((C_KV_HEADS * 64, rows), F32)],
        compiler_params=_params(("parallel", "arbitrary")),
        name="nsa_attention",
    )(qc, kcmp, vcmpt, ks, vst, kw, vwt, gt, ovl_t, expand_t)


def _moba_kernel(q_ref, k_ref, vt_ref, kmean_ref, o_ref, qb_ref, sel_ref, m_ref, acc_ref):
    tq = ATT_T
    i = pl.program_id(1)
    blk = lax.broadcasted_iota(jnp.int32, (N_MOBA, 1), 0)
    kk = lax.broadcasted_iota(jnp.int32, (tq, tq), 0)
    qq = lax.broadcasted_iota(jnp.int32, (tq, tq), 1)
    causal = kk <= qq
    past = blk < i
    for h in range(D_HEADS):
        hs = slice(h * 64, (h + 1) * 64)
        qf = q_ref[:, hs]
        qb_ref[h] = qf.astype(BF16)
        gate = lax.dot_general(kmean_ref[:, hs], qf, (((1,), (1,)), ((), ())), precision=HIGHEST,
                               preferred_element_type=F32)
        gate = jnp.where(past, gate, -jnp.inf)
        sel_ref[h] = (past & (_rank_desc_rows(gate) < MOBA_TOPK)).astype(F32)

    def tile(jb, diag):
        start = pl.multiple_of(jb * tq, tq)
        heads = []
        for h in range(D_HEADS):
            if diag:
                mask = causal
            else:
                mask = jnp.sum(jnp.where(blk == jb, sel_ref[h], 0.0), axis=0, keepdims=True) > 0.5
            heads.append((qb_ref[h], k_ref[h, pl.ds(start, tq), :],
                          vt_ref[h * VT_ROWS:(h + 1) * VT_ROWS, pl.ds(start, tq)], mask))
        _flash_steps_t(heads, m_ref, acc_ref, 1)

    _flash_init(m_ref, acc_ref)
    tile(i, True)

    def body(jb, carry):
        tile(jb, False)
        return carry

    lax.fori_loop(0, i, body, 0)
    o_t = jnp.concatenate([_flash_out(acc_ref, h) for h in range(D_HEADS)], axis=0)
    o_ref[...] = o_t.T.astype(BF16)


def _moba(qd, kd, vdt, kmean):
    bsz = qd.shape[0]
    tq = ATT_T
    return pl.pallas_call(
        _moba_kernel,
        grid=(bsz, SEQ // tq),
        in_specs=[pl.BlockSpec((None, tq, 512), lambda b, i: (b, i, 0)),
                  pl.BlockSpec((None, D_HEADS, SEQ, 64), lambda b, i: (b, 0, 0, 0)),
                  pl.BlockSpec((None, D_HEADS * VT_ROWS, SEQ), lambda b, i: (b, 0, 0)),
                  pl.BlockSpec((None, N_MOBA, 512), lambda b, i: (b, 0, 0))],
        out_specs=pl.BlockSpec((None, tq, 512), lambda b, i: (b, i, 0)),
        out_shape=jax.ShapeDtypeStruct((bsz, SEQ, 512), BF16),
        scratch_shapes=[pltpu.VMEM((D_HEADS, tq, 64), BF16), pltpu.VMEM((D_HEADS, N_MOBA, tq), F32),
                        pltpu.VMEM((8 * D_HEADS, tq), F32), pltpu.VMEM((D_HEADS * VT_ROWS, tq), F32)],
        compiler_params=_params(("parallel", "arbitrary")),
        name="moba_attention",
    )(qd, kd, vdt, kmean.reshape(bsz, N_MOBA, 512))


def _merge_kernel(ua_ref, ub_ref, oc_ref, od_ref, mg_ref, x_ref, gt_ref,
                  wa_ref, wb_ref, wc_ref, wd_ref, wo_ref, o_ref):
    d = D_MODEL
    merged = jnp.zeros(x_ref.shape, F32)
    for k, (u_ref, w_ref) in enumerate(((ua_ref, wa_ref), (ub_ref, wb_ref), (oc_ref, wc_ref), (od_ref, wd_ref))):
        y = jnp.dot(u_ref[...], w_ref[...], preferred_element_type=F32)
        merged = merged + jax.nn.sigmoid(mg_ref[:, k * d:(k + 1) * d].astype(F32)) * y
    o_ref[...] = x_ref[...] + gt_ref[...] * jnp.dot(merged.astype(BF16), wo_ref[...], preferred_element_type=F32)


def _merge(ua, ub, oc, od, proj2, x2, mod3, wa, wb, wc, wd, wo, *, tm=512):
    t, d = x2.shape
    per_b = SEQ // tm
    act = pl.BlockSpec((tm, 512), lambda i: (i, 0))
    wspec = pl.BlockSpec((512, d), lambda i: (0, 0))
    return pl.pallas_call(
        _merge_kernel,
        grid=(t // tm,),
        in_specs=[act, act, act, act,
                  pl.BlockSpec((tm, 4 * d), lambda i: (i, COL_MERGE // (4 * d))),
                  pl.BlockSpec((tm, d), lambda i: (i, 0)),
                  pl.BlockSpec((None, 1, d), lambda i: (i // per_b, 0, 2)),
                  wspec, wspec, wspec, wspec,
                  pl.BlockSpec((d, d), lambda i: (0, 0))],
        out_specs=pl.BlockSpec((tm, d), lambda i: (i, 0)),
        out_shape=jax.ShapeDtypeStruct((t, d), F32),
        compiler_params=_params(("parallel",)),
        name="merge_out",
    )(ua, ub, oc, od, proj2, x2, mod3, wa, wb, wc, wd, wo)


def _pop_max(work, idx):
    m = jnp.max(work, axis=0, keepdims=True)
    first = jnp.min(jnp.where(work == m, idx, work.shape[0]), axis=0, keepdims=True)
    return m, idx == first


_PEER_PAIRS = [(i, j) for i in range(PEER_TOPK) for j in range(PEER_TOPK) if (i + 1) * (j + 1) <= PEER_TOPK]
_PEER_PAIR_ROWS = 56
NOT_TOP = 99.0


def _peer_sel_kernel(q_ref, keys_ref, grp_ref, n1_ref, e1_ref, r2_ref, e2_ref, st_ref):
    tt = q_ref.shape[0]
    half = PEER_QDIM // 2
    for hp in range(2 * PEER_HEADS):
        qh = q_ref[:, hp * half:(hp + 1) * half]
        qn = qh * lax.rsqrt(jnp.mean(qh * qh, axis=-1, keepdims=True) + NORM_EPS)
        st_ref[hp] = lax.dot_general(keys_ref[hp], qn, (((1,), (1,)), ((), ())), precision=HIGHEST,
                                     preferred_element_type=F32)

    idx = lax.broadcasted_iota(jnp.int32, (PEER_NKEYS, 128), 0)
    pidx = lax.broadcasted_iota(jnp.int32, (_PEER_PAIR_ROWS, 128), 0)
    pad_rows = jnp.full((_PEER_PAIR_ROWS - len(_PEER_PAIRS), 128), -jnp.inf, F32)

    def pop16(work, index, tie_safe, track):
        vals, order = [], jnp.full(work.shape, NOT_TOP, F32)
        for it in range(PEER_TOPK):
            if tie_safe:
                m, hit = _pop_max(work, index)
            else:
                m = jnp.max(work, axis=0, keepdims=True)
                hit = work == m
            work = jnp.where(hit, -jnp.inf, work)
            if track:
                order = jnp.where(hit, float(it), order)
            vals.append(m)
        removed = jnp.sum(jnp.where(work == -jnp.inf, 1.0, 0.0), axis=0, keepdims=True)
        return vals, order, removed

    def select(cs, tie_safe):
        wrong = jnp.zeros((1, 128), F32)
        n_pad = float(_PEER_PAIR_ROWS - len(_PEER_PAIRS))
        for h in range(PEER_HEADS):
            scores, tops, ranks = [], [], []
            for p in range(2):
                st = st_ref[2 * h + p, :, cs]
                vals, rank, removed = pop16(st, idx, tie_safe, track=tie_safe or p == 1)
                wrong = wrong + jnp.abs(removed - float(PEER_TOPK))
                scores.append(st)
                tops.append(vals)
                ranks.append(rank)
            cand = jnp.concatenate([tops[0][i] + tops[1][j] for (i, j) in _PEER_PAIRS] + [pad_rows], axis=0)
            best, order, removed = pop16(cand, pidx, tie_safe, track=tie_safe)
            wrong = wrong + jnp.abs(removed - (float(PEER_TOPK) + n_pad))
            picked = jnp.where((order < float(PEER_TOPK)) if tie_safe else (cand >= best[-1]), 1.0, 0.0)
            z = jnp.ones_like(best[0])
            for k in range(1, PEER_TOPK):
                z = z + jnp.exp(best[k] - best[0])
            count = jnp.dot(grp_ref[...], picked.astype(BF16), preferred_element_type=F32)
            n1 = jnp.zeros(scores[0].shape, F32)
            for i in reversed(range(PEER_TOPK)):
                at_i = (ranks[0] == float(i)) if tie_safe else (scores[0] >= tops[0][i])
                n1 = jnp.where(at_i, count[i:i + 1], n1)
            n1_ref[h, :, cs] = n1
            r2_ref[h, :, cs] = ranks[1]
            e1_ref[h, :, cs] = jnp.exp(scores[0] - tops[0][0]) / z
            e2_ref[h, :, cs] = jnp.exp(scores[1] - tops[1][0])
        return wrong

    def token_chunk(c, carry):
        cs = pl.ds(pl.multiple_of(c * 128, 128), 128)
        wrong = select(cs, tie_safe=False)

        @pl.when(jnp.max(wrong) > 0.0)
        def _():
            select(cs, tie_safe=True)

        return carry

    lax.fori_loop(0, tt // 128, token_chunk, 0)


def _peer_select(q2, subkeys, *, tt=512):
    t = q2.shape[0]
    grp = np.zeros((PEER_TOPK, _PEER_PAIR_ROWS), np.float32)
    for row, (i, _) in enumerate(_PEER_PAIRS):
        grp[i, row] = 1.0
    big = pl.BlockSpec((PEER_HEADS, PEER_NKEYS, tt), lambda i: (0, 0, i))
    bshape = jax.ShapeDtypeStruct((PEER_HEADS, PEER_NKEYS, t), F32)
    half = PEER_QDIM // 2
    return pl.pallas_call(
        _peer_sel_kernel,
        grid=(t // tt,),
        in_specs=[pl.BlockSpec((tt, PEER_HEADS * PEER_QDIM), lambda i: (i, 0)),
                  pl.BlockSpec((2 * PEER_HEADS, PEER_NKEYS, half), lambda i: (0, 0, 0)),
                  pl.BlockSpec((PEER_TOPK, _PEER_PAIR_ROWS), lambda i: (0, 0))],
        out_specs=[big, big, big, big],
        out_shape=[bshape, bshape, bshape, bshape],
        scratch_shapes=[pltpu.VMEM((2 * PEER_HEADS, PEER_NKEYS, tt), F32)],
        compiler_params=_params(("parallel",)),
        name="peer_select",
    )(q2, subkeys.reshape(2 * PEER_HEADS, PEER_NKEYS, half), jnp.asarray(grp, dtype=BF16))


PEER_SLAB = 256
PEER_AHEAD = 2

def _peer_main_kernel(ht_ref, u_ref, vt_ref, n1_ref, e1_ref, r2_ref, e2_ref, acc_in_ref, acc_ref, act_ref, p_ref):
    j = pl.program_id(0)
    te, tt = act_ref.shape
    na = te // PEER_NKEYS

    a0 = pl.multiple_of(j * na, na)
    nslab = te // PEER_SLAB
    parts = []

    def act(s):
        ss = slice(s * PEER_SLAB, (s + 1) * PEER_SLAB)
        act_ref[ss, :] = jnp.dot(u_ref[ss, :], ht_ref[...], preferred_element_type=F32)

    def val(s):
        ss = slice(s * PEER_SLAB, (s + 1) * PEER_SLAB)
        parts.append(jnp.dot(vt_ref[:, ss], p_ref[ss, :], preferred_element_type=F32))

    def mask(al):
        rs = slice(al * PEER_NKEYS, (al + 1) * PEER_NKEYS)
        for c in range(tt // 128):
            cs = slice(c * 128, (c + 1) * 128)
            w = jnp.zeros((PEER_NKEYS, 128), F32)
            for hh in range(PEER_HEADS):
                n1 = n1_ref[hh, pl.ds(a0, na), cs][al:al + 1]
                e1 = e1_ref[hh, pl.ds(a0, na), cs][al:al + 1]
                w = w + jnp.where(r2_ref[hh, :, cs] < n1, e1 * e2_ref[hh, :, cs], 0.0)
            p_ref[rs, cs] = (w * _gelu(act_ref[rs, cs])).astype(BF16)

    per = PEER_SLAB // PEER_NKEYS
    for s in range(PEER_AHEAD):
        act(s)
    for s in range(nslab):
        for k in range(per):
            mask(s * per + k)
            if k == 0 and s + PEER_AHEAD < nslab:
                act(s + PEER_AHEAD)
        val(s)
    acc_ref[...] = acc_in_ref[...] + functools.reduce(lambda x, y: x + y, parts)


def _peer_out_kernel(acc_ref, x_ref, gt_ref, fg_ref, o_ref, *, final):
    y = x_ref[...] + gt_ref[...] * acc_ref[...].T
    if final:
        y = y * lax.rsqrt(jnp.mean(y * y, axis=-1, keepdims=True) + NORM_EPS) * fg_ref[...]
    o_ref[...] = y


def _peer_main(h2t, u_bf, v_bf, n1, e1, r2, e2, x2, mod3, final_g, *, final, tt=256, te=4096):
    t, d = x2.shape
    assert (te // PEER_NKEYS) % 8 == 0 and te % PEER_SLAB == 0
    per_b = SEQ // tt
    n_tok = t // tt
    vt_bf = v_bf.reshape(PEER_EXPERTS // te, te, d).transpose(0, 2, 1)
    big = pl.BlockSpec((PEER_HEADS, PEER_NKEYS, tt), lambda j, i: (0, 0, i))
    acc_spec = pl.BlockSpec((None, d, tt), lambda j, i: (i, 0, 0))
    acc = pl.pallas_call(
        _peer_main_kernel,
        grid=(PEER_EXPERTS // te, n_tok),
        in_specs=[pl.BlockSpec((d, tt), lambda j, i: (0, i)),
                  pl.BlockSpec((te, d), lambda j, i: (j, 0)),
                  pl.BlockSpec((None, d, te), lambda j, i: (j, 0, 0)),
                  big, big, big, big, acc_spec],
        out_specs=acc_spec,
        out_shape=jax.ShapeDtypeStruct((n_tok, d, tt), F32),
        input_output_aliases={7: 0},
        scratch_shapes=[pltpu.VMEM((te, tt), F32), pltpu.VMEM((te, tt), BF16)],
        compiler_params=_params(("arbitrary", "arbitrary")),
        name="peer_experts",
    )(h2t, u_bf, vt_bf, n1, e1, r2, e2, jnp.zeros((n_tok, d, tt), F32))
    return pl.pallas_call(
        functools.partial(_peer_out_kernel, final=final),
        grid=(n_tok,),
        in_specs=[pl.BlockSpec((None, d, tt), lambda i: (i, 0, 0)),
                  pl.BlockSpec((tt, d), lambda i: (i, 0)),
                  pl.BlockSpec((None, 1, d), lambda i: (i // per_b, 0, 5)),
                  pl.BlockSpec((1, d), lambda i: (0, 0))],
        out_specs=pl.BlockSpec((tt, d), lambda i: (i, 0)),
        out_shape=jax.ShapeDtypeStruct((t, d), F32),
        compiler_params=_params(("parallel",)),
        name="peer_residual",
    )(acc, x2, mod3, final_g.reshape(1, d))


def _reorder_w_in(w):
    pad = jnp.zeros((w.shape[0], 1024 - 792), w.dtype)
    return jnp.concatenate([w[:, 5400:9496], w[:, 3072:3864], pad, w[:, 0:3072], w[:, 3864:5400]], axis=1)


def kernel(x, c, positions, mod_w, mod_b, norm_mix_g, norm_ffn_g, w_in, a_conv_w, a_out, b_conv_w, b_conv_b, b_ln_g, b_ln_b, b_out, c_cmp_pos, c_cmp_w1, c_cmp_w2, c_out, d_out, w_o, peer_wq, peer_subkeys, peer_u, peer_v, final_norm_g):
    bsz, s, d = x.shape
    assert s == SEQ and d == D_MODEL
    depth = mod_w.shape[0]
    t = bsz * s
    cos, sin = _rope_tables(positions)
    mod = _modulation(c, mod_w, mod_b)
    x2 = x.reshape(t, d)
    for l in range(depth):
        mod3 = mod[l].reshape(bsz, 1, 6 * d)
        proj2 = _norm_matmul(x2, norm_mix_g[l], mod3, 0, 1, _reorder_w_in(w_in[l].astype(BF16)), out_dtype=BF16,
                             tn=PROJ_COLS // 4)
        proj3 = proj2.reshape(bsz, s, PROJ_COLS)
        ua, ub = _conv_mixers(proj3, a_conv_w[l], b_conv_w[l], b_conv_b[l], b_ln_g[l], b_ln_b[l])
        qc, kc, vc, ks, vst, kw, vwt, gt, qd, kd, vdt, kmean = _prep(proj3, cos, sin)
        kcmp, vcmpt = _compress(kc, vc, c_cmp_pos[l], c_cmp_w1[l], c_cmp_w2[l])
        oc = _nsa(qc, kcmp, vcmpt, ks, vst, kw, vwt, gt)
        od = _moba(qd, kd, vdt, kmean)
        x2 = _merge(ua.reshape(t, 512), ub.reshape(t, 512), oc.reshape(t, 512), od.reshape(t, 512),
                    proj2, x2, mod3, a_out[l].astype(BF16), b_out[l].astype(BF16), c_out[l].astype(BF16),
                    d_out[l].astype(BF16), w_o[l].astype(BF16))
        q2, h2 = _norm_matmul(x2, norm_ffn_g[l], mod3, 3, 4, peer_wq[l].astype(BF16), emit_h=True,
                              tn=PEER_HEADS * PEER_QDIM)
        n1, e1, r2, e2 = _peer_select(q2, peer_subkeys[l])
        x2 = _peer_main(h2, peer_u[l].astype(BF16), peer_v[l].astype(BF16), n1, e1, r2, e2,
                        x2, mod3, final_norm_g, final=(l == depth - 1))
    return x2.reshape(bsz, s, d)
```

```python
import functools

import numpy as np
import jax
import jax.numpy as jnp
from jax import lax
from jax.experimental import pallas as pl
from jax.experimental.pallas import tpu as pltpu

F32 = jnp.float32
BF16 = jnp.bfloat16
HIGHEST = lax.Precision.HIGHEST

D_MODEL = 1024
SEQ = 2048
HEAD_DIM = 64
ROPE_THETA = 10000.0
NORM_EPS = 1e-6
A_WIDTH = 512
A_CONV = 3
B_WIDTH = 512
B_CONV = 31
C_HEADS = 8
C_KV_HEADS = 2
C_GROUP = 4
CMP_BLOCK = 32
CMP_STRIDE = 16
CMP_HIDDEN = 128
N_CMP = (SEQ - CMP_BLOCK) // CMP_STRIDE + 1
SLC_BLOCK = 64
SLC_TOPN = 16
N_SLC = SEQ // SLC_BLOCK
WIN = 512
D_HEADS = 8
MOBA_BLOCK = 256
MOBA_TOPK = 3
N_MOBA = SEQ // MOBA_BLOCK
PEER_HEADS = 8
PEER_NKEYS = 128
PEER_EXPERTS = PEER_NKEYS * PEER_NKEYS
PEER_QDIM = 256
PEER_TOPK = 16

PROJ_COLS = 9728
COL_MERGE = 0
COL_KVG = 4096
COL_GATE = COL_KVG + 768
COL_A = 5120

VMEM_LIMIT = 56 * 1024 * 1024
NEG = -1e30

ATT_T = 256
CONV_T = 512
HALO = 32


def _params(sem, flags=None):
    return pltpu.CompilerParams(dimension_semantics=sem, vmem_limit_bytes=VMEM_LIMIT, flags=flags)


def _gelu(x):
    return 0.5 * x * (1.0 + lax.erf(x * np.float32(np.sqrt(0.5))))


def _rope_table_kernel(pos_ref, inv_ref, sign_ref, cos_ref, sin_ref):
    ang = pos_ref[...] * inv_ref[...]
    cos_ref[...] = jnp.cos(ang)
    sin_ref[...] = jnp.sin(ang) * sign_ref[...]


def _rope_tables(positions):
    bsz, s = positions.shape
    inv = 1.0 / (ROPE_THETA ** (jnp.arange(0, HEAD_DIM, 2, dtype=F32) / HEAD_DIM))
    inv128 = jnp.tile(inv, 4)[None, :]
    sign = jnp.tile(jnp.concatenate([-jnp.ones(32, F32), jnp.ones(32, F32)]), 2)[None, :]
    pos = positions.astype(F32).reshape(bsz * s, 1)
    t = bsz * s
    cos, sin = pl.pallas_call(
        _rope_table_kernel,
        grid=(t // SEQ,),
        in_specs=[pl.BlockSpec((SEQ, 1), lambda i: (i, 0)),
                  pl.BlockSpec((1, 128), lambda i: (0, 0)),
                  pl.BlockSpec((1, 128), lambda i: (0, 0))],
        out_specs=[pl.BlockSpec((SEQ, 128), lambda i: (i, 0))] * 2,
        out_shape=[jax.ShapeDtypeStruct((t, 128), F32)] * 2,
        compiler_params=_params(("parallel",)),
        name="rope_tables",
    )(pos, inv128, sign)
    return cos.reshape(bsz, s, 128), sin.reshape(bsz, s, 128)


def _rope(x, cos, sin):
    w = x.shape[-1]
    lane = lax.broadcasted_iota(jnp.int32, x.shape, 1)
    swapped = jnp.where(lane % 64 < 32, pltpu.roll(x, w - 32, 1), pltpu.roll(x, 32, 1))
    return x * cos + swapped * sin


def _mod_kernel(c_ref, w_ref, b_ref, o_ref):
    c = c_ref[...]
    cond = c * jax.nn.sigmoid(c)
    o_ref[...] = jnp.dot(cond, w_ref[...], precision=HIGHEST, preferred_element_type=F32) + b_ref[...]


def _modulation(c, mod_w, mod_b):
    nl, d, n = mod_w.shape
    bsz = c.shape[0]
    tn = 1536
    return pl.pallas_call(
        _mod_kernel,
        grid=(nl, n // tn),
        in_specs=[pl.BlockSpec((bsz, d), lambda l, j: (0, 0)),
                  pl.BlockSpec((None, d, tn), lambda l, j: (l, 0, j)),
                  pl.BlockSpec((None, 1, tn), lambda l, j: (l, 0, j))],
        out_specs=pl.BlockSpec((None, bsz, tn), lambda l, j: (l, 0, j)),
        out_shape=jax.ShapeDtypeStruct((nl, bsz, n), F32),
        compiler_params=_params(("parallel", "parallel")),
        name="adaln_mod",
    )(c, mod_w, mod_b.reshape(nl, 1, n))


def _norm_matmul_kernel(x_ref, g_ref, sc_ref, sh_ref, w_ref, o_ref, *rest, emit_h):
    h_scr = rest[-1]

    @pl.when(pl.program_id(1) == 0)
    def _():
        x = x_ref[...]
        y = x * lax.rsqrt(jnp.mean(x * x, axis=-1, keepdims=True) + NORM_EPS)
        h = (y * g_ref[...]) * (1.0 + sc_ref[...]) + sh_ref[...]
        h_scr[...] = h.astype(BF16)
        if emit_h:
            rest[0][...] = h.astype(BF16)

    o_ref[...] = jnp.dot(h_scr[...], w_ref[...], preferred_element_type=F32).astype(o_ref.dtype)


def _norm_matmul(x2, g, mod3, sh_blk, sc_blk, w, *, out_dtype=F32, emit_h=False, tm=1024, tn=512):
    t, d = x2.shape
    n = w.shape[1]
    per_b = SEQ // tm
    out_shape = [jax.ShapeDtypeStruct((t, n), out_dtype)]
    out_specs = [pl.BlockSpec((tm, tn), lambda i, j: (i, j))]
    if emit_h:
        out_shape.append(jax.ShapeDtypeStruct((t, d), BF16))
        out_specs.append(pl.BlockSpec((tm, d), lambda i, j: (i, 0)))
    res = pl.pallas_call(
        functools.partial(_norm_matmul_kernel, emit_h=emit_h),
        grid=(t // tm, n // tn),
        in_specs=[pl.BlockSpec((tm, d), lambda i, j: (i, 0)),
                  pl.BlockSpec((1, d), lambda i, j: (0, 0)),
                  pl.BlockSpec((None, 1, d), lambda i, j: (i // per_b, 0, sc_blk)),
                  pl.BlockSpec((None, 1, d), lambda i, j: (i // per_b, 0, sh_blk)),
                  pl.BlockSpec((d, tn), lambda i, j: (0, j))],
        out_specs=out_specs,
        out_shape=out_shape,
        scratch_shapes=[pltpu.VMEM((tm, d), BF16)],
        compiler_params=_params(("parallel", "arbitrary")),
        name="norm_matmul",
    )(x2, g.reshape(1, d), mod3, mod3, w)
    return res if emit_h else res[0]


def _conv_kernel(ab_ref, ac_ref, ax_ref, ba_ref, bg_ref, pac_ref, pax_ref, pba_ref, pbg_ref,
                 aw_ref, bw_ref, bb_ref, lng_ref, lnb_ref, ua_ref, ub_ref, ext_ref, y_ref):
    ts = ab_ref.shape[0]
    keep = (pl.program_id(1) > 0).astype(F32)

    def f32(ref):
        return ref[...].astype(F32)

    def causal_conv(w_ref, taps, bias_ref, out_ref):
        blk = 128
        reach = ((taps - 1) // 8) * 8
        for t0 in range(0, ts, blk):
            for c0 in range(0, w_ref.shape[1], 128):
                cs = slice(c0, c0 + 128)
                acc = jnp.zeros((blk, 128), F32) if bias_ref is None else jnp.zeros((blk, 128), F32) + bias_ref[:, cs]
                for r in range(min(8, taps)):
                    base = HALO + t0 - reach - r
                    ur = ext_ref[base:base + blk + reach, cs]
                    for q in range(reach // 8 + 1):
                        shift = 8 * q + r
                        if shift < taps:
                            k = taps - 1 - shift
                            acc = acc + w_ref[k:k + 1, cs] * ur[reach - 8 * q:reach - 8 * q + blk]
                out_ref[t0:t0 + blk, cs] = acc

    ext_ref[0:HALO, :] = f32(pac_ref) * f32(pax_ref) * keep
    ext_ref[HALO:HALO + ts, :] = f32(ac_ref) * f32(ax_ref)
    causal_conv(aw_ref, A_CONV, None, y_ref)
    ua_ref[...] = (f32(ab_ref) * y_ref[...]).astype(BF16)

    ext_ref[0:HALO, :] = f32(pba_ref) * jax.nn.sigmoid(f32(pbg_ref)) * keep
    ext_ref[HALO:HALO + ts, :] = f32(ba_ref) * jax.nn.sigmoid(f32(bg_ref))
    causal_conv(bw_ref, B_CONV, bb_ref, y_ref)
    acc = y_ref[...]
    mu = jnp.mean(acc, axis=-1, keepdims=True)
    cen = acc - mu
    var = jnp.mean(cen * cen, axis=-1, keepdims=True)
    y = cen * lax.rsqrt(var + NORM_EPS) * lng_ref[...] + lnb_ref[...]
    ub_ref[...] = (y * jax.nn.sigmoid(y)).astype(BF16)


def _conv_mixers(proj3, a_conv_w, b_conv_w, b_conv_b, b_ln_g, b_ln_b):
    bsz = proj3.shape[0]
    ts = CONV_T
    c0 = COL_A // 512
    r = ts // HALO

    def cur(k):
        return pl.BlockSpec((None, ts, 512), lambda b, i, k=k: (b, i, c0 + k))

    def prev(k):
        return pl.BlockSpec((None, HALO, 512), lambda b, i, k=k: (b, jnp.maximum(i * r - 1, 0), c0 + k))

    def full(shape):
        return pl.BlockSpec(shape, lambda b, i: (0,) * len(shape))

    return pl.pallas_call(
        _conv_kernel,
        grid=(bsz, SEQ // ts),
        in_specs=[cur(0), cur(1), cur(2), cur(3), cur(4), prev(1), prev(2), prev(3), prev(4),
                  full((A_CONV, A_WIDTH)), full((B_CONV, B_WIDTH)), full((1, B_WIDTH)),
                  full((1, B_WIDTH)), full((1, B_WIDTH))],
        out_specs=[pl.BlockSpec((None, ts, 512), lambda b, i: (b, i, 0))] * 2,
        out_shape=[jax.ShapeDtypeStruct((bsz, SEQ, 512), BF16)] * 2,
        scratch_shapes=[pltpu.VMEM((HALO + ts, 512), F32), pltpu.VMEM((ts, 512), F32)],
        compiler_params=_params(("parallel", "arbitrary")),
        name="conv_mixers",
    )(proj3, proj3, proj3, proj3, proj3, proj3, proj3, proj3, proj3,
      a_conv_w, b_conv_w, b_conv_b.reshape(1, -1), b_ln_g.reshape(1, -1), b_ln_b.reshape(1, -1))


VT_ROWS = 80


def _values_t(v, n_heads):
    rows = v.shape[0]
    vt = v.T
    tail = (lax.broadcasted_iota(jnp.int32, (VT_ROWS - HEAD_DIM, rows), 0) == 0).astype(F32)
    blocks = []
    for h in range(n_heads):
        blocks += [vt[h * HEAD_DIM:(h + 1) * HEAD_DIM], tail]
    return jnp.concatenate(blocks, axis=0).astype(BF16)


def _prep_kernel(cq_ref, dq_ref, dk_ref, dv_ref, kvg_ref, cos_ref, sin_ref,
                 qc_ref, kc_ref, vc_ref, ks_ref, vst_ref, kw_ref, vwt_ref, gt_ref,
                 qd_ref, kd_ref, vdt_ref, kmean_ref):
    cos = cos_ref[...]
    sin = sin_ref[...]
    cos4 = jnp.concatenate([cos] * 4, axis=1)
    sin4 = jnp.concatenate([sin] * 4, axis=1)
    scale = np.float32(HEAD_DIM ** -0.5 * np.log2(np.e))
    qc_ref[...] = _rope(cq_ref[...].astype(F32), cos4, sin4) * scale
    qd_ref[...] = _rope(dq_ref[...].astype(F32), cos4, sin4) * scale
    kd = _rope(dk_ref[...].astype(F32), cos4, sin4)
    for h in range(D_HEADS):
        kd_ref[h] = kd[:, h * 64:(h + 1) * 64].astype(BF16)
    kmean_ref[...] = jnp.mean(kd, axis=0, keepdims=True)
    vdt_ref[...] = _values_t(dv_ref[...].astype(F32), D_HEADS)

    def kvg(k):
        return kvg_ref[:, k * 128:(k + 1) * 128].astype(F32)

    kc = _rope(kvg(0), cos, sin)
    vc = kvg(1)
    ks = _rope(kvg(2), cos, sin)
    kw = _rope(kvg(4), cos, sin)
    for g in range(C_KV_HEADS):
        gs = slice(g * 64, (g + 1) * 64)
        kc_ref[g] = kc[:, gs]
        vc_ref[g] = vc[:, gs]
        ks_ref[g] = ks[:, gs].astype(BF16)
        kw_ref[g] = kw[:, gs].astype(BF16)
    vst_ref[...] = _values_t(kvg(3), C_KV_HEADS)
    vwt_ref[...] = _values_t(kvg(5), C_KV_HEADS)
    gt_ref[...] = jax.nn.sigmoid(kvg(6)).T


def _prep(proj3, cos, sin):
    bsz = proj3.shape[0]
    ts = MOBA_BLOCK
    c0 = COL_A // 512

    def col512(k):
        return pl.BlockSpec((None, ts, 512), lambda b, i: (b, i, c0 + k))

    row128 = pl.BlockSpec((None, ts, 128), lambda b, i: (b, i, 0))
    row512 = pl.BlockSpec((None, ts, 512), lambda b, i: (b, i, 0))
    col128t = pl.BlockSpec((None, 128, ts), lambda b, i: (b, 0, i))
    val2t = pl.BlockSpec((None, C_KV_HEADS * VT_ROWS, ts), lambda b, i: (b, 0, i))
    head64 = pl.BlockSpec((None, C_KV_HEADS, ts, 64), lambda b, i: (b, 0, i, 0))
    k64 = jax.ShapeDtypeStruct((bsz, C_KV_HEADS, SEQ, 64), BF16)
    t128 = jax.ShapeDtypeStruct((bsz, C_KV_HEADS * VT_ROWS, SEQ), BF16)
    return pl.pallas_call(
        _prep_kernel,
        grid=(bsz, SEQ // ts),
        in_specs=[col512(5), col512(6), col512(7), col512(8),
                  pl.BlockSpec((None, ts, 1024), lambda b, i: (b, i, COL_KVG // 1024)),
                  row128, row128],
        out_specs=[row512, head64, head64, head64, val2t, head64, val2t, col128t,
                   row512,
                   pl.BlockSpec((None, D_HEADS, ts, 64), lambda b, i: (b, 0, i, 0)),
                   pl.BlockSpec((None, D_HEADS * VT_ROWS, ts), lambda b, i: (b, 0, i)),
                   pl.BlockSpec((None, None, 1, 512), lambda b, i: (b, i, 0, 0))],
        out_shape=[jax.ShapeDtypeStruct((bsz, SEQ, 512), F32),
                   jax.ShapeDtypeStruct((bsz, C_KV_HEADS, SEQ, 64), F32),
                   jax.ShapeDtypeStruct((bsz, C_KV_HEADS, SEQ, 64), F32),
                   k64, t128, k64, t128,
                   jax.ShapeDtypeStruct((bsz, 128, SEQ), F32),
                   jax.ShapeDtypeStruct((bsz, SEQ, 512), F32),
                   jax.ShapeDtypeStruct((bsz, D_HEADS, SEQ, 64), BF16),
                   jax.ShapeDtypeStruct((bsz, D_HEADS * VT_ROWS, SEQ), BF16),
                   jax.ShapeDtypeStruct((bsz, N_MOBA, 1, 512), F32)],
        compiler_params=_params(("parallel", "parallel")),
        name="attn_prep",
    )(proj3, proj3, proj3, proj3, proj3, cos, sin)


def _compress_kernel(kc_ref, vc_ref, pos_ref, w1_ref, w2_ref, kcmp_ref, vcmp_ref):
    half = CMP_STRIDE * HEAD_DIM
    row = lax.broadcasted_iota(jnp.int32, (128, CMP_HIDDEN), 0)
    for which, (src, dst) in enumerate(((kc_ref, kcmp_ref), (vc_ref, vcmp_ref))):
        bias = jnp.dot(pos_ref[which], w1_ref[which], precision=HIGHEST, preferred_element_type=F32)[0:1]
        for g in range(C_KV_HEADS):
            chunks = src[g]
            d1 = jnp.dot(chunks, w1_ref[which, 0:half, :], precision=HIGHEST, preferred_element_type=F32)
            d2 = jnp.dot(chunks, w1_ref[which, half:2 * half, :], precision=HIGHEST, preferred_element_type=F32)
            d2 = jnp.where(row < 127, pltpu.roll(d2, 127, 0), 0.0)
            hid = _gelu(d1 + d2 + bias)
            out = jnp.dot(hid, w2_ref[which], precision=HIGHEST, preferred_element_type=F32)
            dst[g] = out if which == 0 else out.T


def _compress(kc, vc, cmp_pos, cmp_w1, cmp_w2):
    bsz = kc.shape[0]
    kc4 = kc.reshape(bsz, C_KV_HEADS, SEQ // CMP_STRIDE, CMP_STRIDE * HEAD_DIM)
    vc4 = vc.reshape(bsz, C_KV_HEADS, SEQ // CMP_STRIDE, CMP_STRIDE * HEAD_DIM)
    pos8 = jnp.broadcast_to(cmp_pos.reshape(2, 1, CMP_BLOCK * HEAD_DIM), (2, 8, CMP_BLOCK * HEAD_DIM))
    blk = pl.BlockSpec((None, C_KV_HEADS, 128, 1024), lambda b: (b, 0, 0, 0))
    out = pl.BlockSpec((None, C_KV_HEADS, 128, 64), lambda b: (b, 0, 0, 0))
    return pl.pallas_call(
        _compress_kernel,
        grid=(bsz,),
        in_specs=[blk, blk,
                  pl.BlockSpec((2, 8, 2048), lambda b: (0, 0, 0)),
                  pl.BlockSpec((2, 2048, CMP_HIDDEN), lambda b: (0, 0, 0)),
                  pl.BlockSpec((2, CMP_HIDDEN, 64), lambda b: (0, 0, 0))],
        out_specs=[out, pl.BlockSpec((None, C_KV_HEADS, 64, 128), lambda b: (b, 0, 0, 0))],
        out_shape=[jax.ShapeDtypeStruct((bsz, C_KV_HEADS, 128, 64), F32),
                   jax.ShapeDtypeStruct((bsz, C_KV_HEADS, 64, 128), F32)],
        compiler_params=_params(("parallel",)),
        name="nsa_compress",
    )(kc4, vc4, pos8, cmp_w1, cmp_w2)


def _flash_steps_t(heads, m_ref, acc_ref, groups):
    scores = [lax.dot_general(k, qb, (((1,), (1,)), ((), ())), preferred_element_type=F32)
              for (qb, k, _, _) in heads]
    probs = []
    for n, (qb, _, _, mask) in enumerate(heads):
        s = scores[n]
        tq = qb.shape[0] // groups
        if mask is not None:
            s = jnp.concatenate([jnp.where(mask, s[:, r * tq:(r + 1) * tq], NEG) for r in range(groups)], axis=1)
        sr = _stat_row(n)
        m_old = m_ref[sr, :]
        m_new = jnp.maximum(m_old, jnp.max(s, axis=0, keepdims=True))
        m_ref[sr, :] = m_new
        probs.append((jnp.exp2(m_old - m_new), jnp.exp2(s - m_new).astype(BF16)))
    for n, (_, _, vt, _) in enumerate(heads):
        alpha, p = probs[n]
        a = slice(n * VT_ROWS, (n + 1) * VT_ROWS)
        acc_ref[a, :] = alpha * acc_ref[a, :] + jnp.dot(vt, p, preferred_element_type=F32)


def _flash_out(acc_ref, n):
    return acc_ref[n * VT_ROWS:n * VT_ROWS + HEAD_DIM, :] / acc_ref[n * VT_ROWS + HEAD_DIM:n * VT_ROWS + HEAD_DIM + 1, :]


def _stat_row(row):
    return slice(8 * row, 8 * row + 1)


def _flash_init(m_ref, acc_ref):
    m_ref[...] = jnp.full(m_ref.shape, NEG, F32)
    acc_ref[...] = jnp.zeros(acc_ref.shape, F32)


def _rank_desc_rows(vals):
    n = vals.shape[0]
    row = lax.broadcasted_iota(jnp.int32, vals.shape, 0)
    rank = jnp.zeros(vals.shape, jnp.int32)
    for i in range(n):
        vi = vals[i:i + 1, :]
        ahead = (vi > vals) | ((vi == vals) & (row > i))
        rank = rank + ahead.astype(jnp.int32)
    return rank


def _nsa_kernel(q_ref, kcmp_ref, vcmpt_ref, ks_ref, vst_ref, kw_ref, vwt_ref, gt_ref, ovlt_ref, expt_ref,
                o_ref, qb_ref, sel_ref, m_ref, acc_ref, out_ref):
    tq = ATT_T
    i = pl.program_id(1)
    t0 = i * tq
    t_row = t0 + lax.broadcasted_iota(jnp.int32, (1, tq), 1)
    t_rows = jnp.concatenate([t_row] * C_GROUP, axis=1)
    blk = lax.broadcasted_iota(jnp.int32, (128, 1), 0)
    kk = lax.broadcasted_iota(jnp.int32, (tq, tq), 0)
    qq = lax.broadcasted_iota(jnp.int32, (tq, tq), 1)
    causal = kk <= qq

    def gate_row(branch, g):
        base = branch * 8 + g * C_GROUP
        return jnp.concatenate([gt_ref[base + r:base + r + 1, :] for r in range(C_GROUP)], axis=1)

    def hrows(g):
        return slice(g * HEAD_DIM, (g + 1) * HEAD_DIM)

    def vrows(g):
        return slice(g * VT_ROWS, (g + 1) * VT_ROWS)

    for g in range(C_KV_HEADS):
        qf = jnp.concatenate([q_ref[:, (g * 4 + r) * 64:(g * 4 + r + 1) * 64] for r in range(C_GROUP)], axis=0)
        qb_ref[g] = qf.astype(BF16)

        s = lax.dot_general(kcmp_ref[g], qf, (((1,), (1,)), ((), ())), precision=HIGHEST,
                            preferred_element_type=F32)
        vis = (blk * CMP_STRIDE + (CMP_BLOCK - 1)) <= t_rows
        sm = jnp.where(vis, s, NEG)
        e = jnp.where(vis, jnp.exp2(sm - jnp.max(sm, axis=0, keepdims=True)), 0.0)
        p = e / jnp.maximum(jnp.sum(e, axis=0, keepdims=True), 1e-30)
        o_cmp = jnp.dot(vcmpt_ref[g].astype(BF16), p.astype(BF16), preferred_element_type=F32)
        out_ref[hrows(g), :] = gate_row(0, g) * o_cmp

        psum = p[:, 0:tq] + p[:, tq:2 * tq] + p[:, 2 * tq:3 * tq] + p[:, 3 * tq:4 * tq]
        imp = jnp.dot(ovlt_ref[...], psum, precision=HIGHEST, preferred_element_type=F32)[0:N_SLC]
        b32 = blk[0:N_SLC]
        cur = t_row // SLC_BLOCK
        forced = (b32 == 0) | (b32 == cur) | (b32 == cur - 1)
        imp = jnp.where(forced, jnp.inf, jnp.where(b32 * SLC_BLOCK > t_row, -jnp.inf, imp))
        sel = (_rank_desc_rows(imp) < SLC_TOPN).astype(F32)
        sel_ref[g] = jnp.concatenate([sel, jnp.zeros((128 - N_SLC, tq), F32)], axis=0).astype(BF16)

    def slc_heads(jb, extra):
        start = pl.multiple_of(jb * tq, tq)
        heads = []
        for g in range(C_KV_HEADS):
            hit = jnp.dot(expt_ref[pl.ds(start, tq), :], sel_ref[g], preferred_element_type=F32) > 0.5
            mask = hit if extra is None else hit & extra
            heads.append((qb_ref[g], ks_ref[g, pl.ds(start, tq), :], vst_ref[vrows(g), pl.ds(start, tq)], mask))
        return heads

    def win_heads():
        wk = WIN + tq
        wstart = pl.multiple_of(jnp.maximum(t0 - WIN, 0), tq)
        kpos = wstart + lax.broadcasted_iota(jnp.int32, (wk, 1), 0)
        mask = (kpos <= t_row) & (kpos > t_row - WIN)
        return [(qb_ref[g], kw_ref[g, pl.ds(wstart, wk), :], vwt_ref[vrows(g), pl.ds(wstart, wk)], mask)
                for g in range(C_KV_HEADS)]

    _flash_init(m_ref, acc_ref)
    _flash_steps_t(slc_heads(i, causal) + win_heads(), m_ref, acc_ref, C_GROUP)

    def slc_body(jb, carry):
        _flash_steps_t(slc_heads(jb, None), m_ref, acc_ref, C_GROUP)
        return carry

    lax.fori_loop(0, i, slc_body, 0)
    for g in range(C_KV_HEADS):
        out_ref[hrows(g), :] += (gate_row(1, g) * _flash_out(acc_ref, g)
                                 + gate_row(2, g) * _flash_out(acc_ref, C_KV_HEADS + g))

    o_t = jnp.concatenate([out_ref[hrows(g), r * tq:(r + 1) * tq]
                           for g in range(C_KV_HEADS) for r in range(C_GROUP)], axis=0)
    o_ref[...] = o_t.T.astype(BF16)


def _nsa_constants():
    j = np.arange(128)[:, None]
    n = np.arange(128)[None, :]
    ovl_t = ((n * CMP_STRIDE < j * SLC_BLOCK + SLC_BLOCK) & (n * CMP_STRIDE + CMP_BLOCK > j * SLC_BLOCK)
             & (n < N_CMP) & (j < N_SLC)).astype(np.float32)
    expand_t = (np.arange(SEQ)[:, None] // SLC_BLOCK == np.arange(128)[None, :]).astype(np.float32)
    return jnp.asarray(ovl_t), jnp.asarray(expand_t, dtype=BF16)


def _nsa(qc, kcmp, vcmpt, ks, vst, kw, vwt, gt):
    bsz = qc.shape[0]
    tq = ATT_T
    ovl_t, expand_t = _nsa_constants()
    keys = pl.BlockSpec((None, C_KV_HEADS, SEQ, 64), lambda b, i: (b, 0, 0, 0))
    vals = pl.BlockSpec((None, C_KV_HEADS * VT_ROWS, SEQ), lambda b, i: (b, 0, 0))
    rows = C_GROUP * tq
    return pl.pallas_call(
        _nsa_kernel,
        grid=(bsz, SEQ // tq),
        in_specs=[pl.BlockSpec((None, tq, 512), lambda b, i: (b, i, 0)),
                  pl.BlockSpec((None, C_KV_HEADS, 128, 64), lambda b, i: (b, 0, 0, 0)),
                  pl.BlockSpec((None, C_KV_HEADS, 64, 128), lambda b, i: (b, 0, 0, 0)),
                  keys, vals, keys, vals,
                  pl.BlockSpec((None, 128, tq), lambda b, i: (b, 0, i)),
                  pl.BlockSpec((128, 128), lambda b, i: (0, 0)),
                  pl.BlockSpec((SEQ, 128), lambda b, i: (0, 0))],
        out_specs=pl.BlockSpec((None, tq, 512), lambda b, i: (b, i, 0)),
        out_shape=jax.ShapeDtypeStruct((bsz, SEQ, 512), BF16),
        scratch_shapes=[pltpu.VMEM((C_KV_HEADS, rows, 64), BF16), pltpu.VMEM((C_KV_HEADS, 128, tq), BF16),
                        pltpu.VMEM((8 * 2 * C_KV_HEADS, rows), F32),
                        pltpu.VMEM((2 * C_KV_HEADS * VT_ROWS, rows), F32), pltpu.VMEM((C_KV_HEADS * 64, rows), F32)],
        compiler_params=_params(("parallel", "arbitrary")),
        name="nsa_attention",
    )(qc, kcmp, vcmpt, ks, vst, kw, vwt, gt, ovl_t, expand_t)


def _moba_kernel(q_ref, k_ref, vt_ref, kmean_ref, o_ref, qb_ref, sel_ref, m_ref, acc_ref):
    tq = ATT_T
    i = pl.program_id(1)
    blk = lax.broadcasted_iota(jnp.int32, (N_MOBA, 1), 0)
    kk = lax.broadcasted_iota(jnp.int32, (tq, tq), 0)
    qq = lax.broadcasted_iota(jnp.int32, (tq, tq), 1)
    causal = kk <= qq
    past = blk < i
    for h in range(D_HEADS):
        hs = slice(h * 64, (h + 1) * 64)
        qf = q_ref[:, hs]
        qb_ref[h] = qf.astype(BF16)
        gate = lax.dot_general(kmean_ref[:, hs], qf, (((1,), (1,)), ((), ())), precision=HIGHEST,
                               preferred_element_type=F32)
        gate = jnp.where(past, gate, -jnp.inf)
        sel_ref[h] = (past & (_rank_desc_rows(gate) < MOBA_TOPK)).astype(F32)

    def tile(jb, diag):
        start = pl.multiple_of(jb * tq, tq)
        heads = []
        for h in range(D_HEADS):
            if diag:
                mask = causal
            else:
                mask = jnp.sum(jnp.where(blk == jb, sel_ref[h], 0.0), axis=0, keepdims=True) > 0.5
            heads.append((qb_ref[h], k_ref[h, pl.ds(start, tq), :],
                          vt_ref[h * VT_ROWS:(h + 1) * VT_ROWS, pl.ds(start, tq)], mask))
        _flash_steps_t(heads, m_ref, acc_ref, 1)

    _flash_init(m_ref, acc_ref)
    tile(i, True)

    def body(jb, carry):
        tile(jb, False)
        return carry

    lax.fori_loop(0, i, body, 0)
    o_t = jnp.concatenate([_flash_out(acc_ref, h) for h in range(D_HEADS)], axis=0)
    o_ref[...] = o_t.T.astype(BF16)


def _moba(qd, kd, vdt, kmean):
    bsz = qd.shape[0]
    tq = ATT_T
    return pl.pallas_call(
        _moba_kernel,
        grid=(bsz, SEQ // tq),
        in_specs=[pl.BlockSpec((None, tq, 512), lambda b, i: (b, i, 0)),
                  pl.BlockSpec((None, D_HEADS, SEQ, 64), lambda b, i: (b, 0, 0, 0)),
                  pl.BlockSpec((None, D_HEADS * VT_ROWS, SEQ), lambda b, i: (b, 0, 0)),
                  pl.BlockSpec((None, N_MOBA, 512), lambda b, i: (b, 0, 0))],
        out_specs=pl.BlockSpec((None, tq, 512), lambda b, i: (b, i, 0)),
        out_shape=jax.ShapeDtypeStruct((bsz, SEQ, 512), BF16),
        scratch_shapes=[pltpu.VMEM((D_HEADS, tq, 64), BF16), pltpu.VMEM((D_HEADS, N_MOBA, tq), F32),
                        pltpu.VMEM((8 * D_HEADS, tq), F32), pltpu.VMEM((D_HEADS * VT_ROWS, tq), F32)],
        compiler_params=_params(("parallel", "arbitrary")),
        name="moba_attention",
    )(qd, kd, vdt, kmean.reshape(bsz, N_MOBA, 512))


def _merge_kernel(ua_ref, ub_ref, oc_ref, od_ref, mg_ref, x_ref, gt_ref,
                  wa_ref, wb_ref, wc_ref, wd_ref, wo_ref, o_ref):
    d = D_MODEL
    merged = jnp.zeros(x_ref.shape, F32)
    for k, (u_ref, w_ref) in enumerate(((ua_ref, wa_ref), (ub_ref, wb_ref), (oc_ref, wc_ref), (od_ref, wd_ref))):
        y = jnp.dot(u_ref[...], w_ref[...], preferred_element_type=F32)
        merged = merged + jax.nn.sigmoid(mg_ref[:, k * d:(k + 1) * d].astype(F32)) * y
    o_ref[...] = x_ref[...] + gt_ref[...] * jnp.dot(merged.astype(BF16), wo_ref[...], preferred_element_type=F32)


def _merge(ua, ub, oc, od, proj2, x2, mod3, wa, wb, wc, wd, wo, *, tm=1024):
    t, d = x2.shape
    per_b = SEQ // tm
    act = pl.BlockSpec((tm, 512), lambda i: (i, 0))
    wspec = pl.BlockSpec((512, d), lambda i: (0, 0))
    return pl.pallas_call(
        _merge_kernel,
        grid=(t // tm,),
        in_specs=[act, act, act, act,
                  pl.BlockSpec((tm, 4 * d), lambda i: (i, COL_MERGE // (4 * d))),
                  pl.BlockSpec((tm, d), lambda i: (i, 0)),
                  pl.BlockSpec((None, 1, d), lambda i: (i // per_b, 0, 2)),
                  wspec, wspec, wspec, wspec,
                  pl.BlockSpec((d, d), lambda i: (0, 0))],
        out_specs=pl.BlockSpec((tm, d), lambda i: (i, 0)),
        out_shape=jax.ShapeDtypeStruct((t, d), F32),
        compiler_params=_params(("parallel",)),
        name="merge_out",
    )(ua, ub, oc, od, proj2, x2, mod3, wa, wb, wc, wd, wo)


def _pop_max(work, idx):
    m = jnp.max(work, axis=0, keepdims=True)
    first = jnp.min(jnp.where(work == m, idx, work.shape[0]), axis=0, keepdims=True)
    return m, idx == first


_PEER_PAIRS = [(i, j) for i in range(PEER_TOPK) for j in range(PEER_TOPK) if (i + 1) * (j + 1) <= PEER_TOPK]
_PEER_PAIR_ROWS = 56
NOT_TOP = 99.0


def _peer_sel_kernel(q_ref, keys_ref, grp_ref, n1_ref, e1_ref, r2_ref, e2_ref, st_ref):
    tt = q_ref.shape[0]
    half = PEER_QDIM // 2
    for hp in range(2 * PEER_HEADS):
        qh = q_ref[:, hp * half:(hp + 1) * half]
        qn = qh * lax.rsqrt(jnp.mean(qh * qh, axis=-1, keepdims=True) + NORM_EPS)
        st_ref[hp] = lax.dot_general(keys_ref[hp], qn, (((1,), (1,)), ((), ())), precision=HIGHEST,
                                     preferred_element_type=F32)

    idx = lax.broadcasted_iota(jnp.int32, (PEER_NKEYS, 128), 0)
    pidx = lax.broadcasted_iota(jnp.int32, (_PEER_PAIR_ROWS, 128), 0)
    pad_rows = jnp.full((_PEER_PAIR_ROWS - len(_PEER_PAIRS), 128), -jnp.inf, F32)

    def pop16(work, index, tie_safe, track):
        vals, order = [], jnp.full(work.shape, NOT_TOP, F32)
        for it in range(PEER_TOPK):
            if tie_safe:
                m, hit = _pop_max(work, index)
            else:
                m = jnp.max(work, axis=0, keepdims=True)
                hit = work == m
            work = jnp.where(hit, -jnp.inf, work)
            if track:
                order = jnp.where(hit, float(it), order)
            vals.append(m)
        removed = jnp.sum(jnp.where(work == -jnp.inf, 1.0, 0.0), axis=0, keepdims=True)
        return vals, order, removed

    def select(cs, tie_safe):
        wrong = jnp.zeros((1, 128), F32)
        n_pad = float(_PEER_PAIR_ROWS - len(_PEER_PAIRS))
        for h in range(PEER_HEADS):
            scores, tops, ranks = [], [], []
            for p in range(2):
                st = st_ref[2 * h + p, :, cs]
                vals, rank, removed = pop16(st, idx, tie_safe, track=tie_safe or p == 1)
                wrong = wrong + jnp.abs(removed - float(PEER_TOPK))
                scores.append(st)
                tops.append(vals)
                ranks.append(rank)
            cand = jnp.concatenate([tops[0][i] + tops[1][j] for (i, j) in _PEER_PAIRS] + [pad_rows], axis=0)
            best, order, removed = pop16(cand, pidx, tie_safe, track=tie_safe)
            wrong = wrong + jnp.abs(removed - (float(PEER_TOPK) + n_pad))
            picked = jnp.where((order < float(PEER_TOPK)) if tie_safe else (cand >= best[-1]), 1.0, 0.0)
            z = jnp.ones_like(best[0])
            for k in range(1, PEER_TOPK):
                z = z + jnp.exp(best[k] - best[0])
            count = jnp.dot(grp_ref[...], picked.astype(BF16), preferred_element_type=F32)
            n1 = jnp.zeros(scores[0].shape, F32)
            for i in reversed(range(PEER_TOPK)):
                at_i = (ranks[0] == float(i)) if tie_safe else (scores[0] >= tops[0][i])
                n1 = jnp.where(at_i, count[i:i + 1], n1)
            n1_ref[h, :, cs] = n1
            r2_ref[h, :, cs] = ranks[1]
            e1_ref[h, :, cs] = jnp.exp(scores[0] - tops[0][0]) / z
            e2_ref[h, :, cs] = jnp.exp(scores[1] - tops[1][0])
        return wrong

    def token_chunk(c, carry):
        cs = pl.ds(pl.multiple_of(c * 128, 128), 128)
        wrong = select(cs, tie_safe=False)

        @pl.when(jnp.max(wrong) > 0.0)
        def _():
            select(cs, tie_safe=True)

        return carry

    lax.fori_loop(0, tt // 128, token_chunk, 0)


def _peer_select(q2, subkeys, *, tt=512):
    t = q2.shape[0]
    grp = np.zeros((PEER_TOPK, _PEER_PAIR_ROWS), np.float32)
    for row, (i, _) in enumerate(_PEER_PAIRS):
        grp[i, row] = 1.0
    big = pl.BlockSpec((PEER_HEADS, PEER_NKEYS, tt), lambda i: (0, 0, i))
    bshape = jax.ShapeDtypeStruct((PEER_HEADS, PEER_NKEYS, t), F32)
    half = PEER_QDIM // 2
    return pl.pallas_call(
        _peer_sel_kernel,
        grid=(t // tt,),
        in_specs=[pl.BlockSpec((tt, PEER_HEADS * PEER_QDIM), lambda i: (i, 0)),
                  pl.BlockSpec((2 * PEER_HEADS, PEER_NKEYS, half), lambda i: (0, 0, 0)),
                  pl.BlockSpec((PEER_TOPK, _PEER_PAIR_ROWS), lambda i: (0, 0))],
        out_specs=[big, big, big, big],
        out_shape=[bshape, bshape, bshape, bshape],
        scratch_shapes=[pltpu.VMEM((2 * PEER_HEADS, PEER_NKEYS, tt), F32)],
        compiler_params=_params(("parallel",)),
        name="peer_select",
    )(q2, subkeys.reshape(2 * PEER_HEADS, PEER_NKEYS, half), jnp.asarray(grp, dtype=BF16))


PEER_SLAB = 256

def _peer_main_kernel(h_ref, u_ref, vt_ref, n1_ref, e1_ref, r2_ref, e2_ref, x_ref, gt_ref, fg_ref,
                      o_ref, acc_ref, act_ref, p_ref, ht_ref, *, final):
    j = pl.program_id(1)
    te, tt = act_ref.shape
    na = te // PEER_NKEYS

    @pl.when(j == 0)
    def _():
        acc_ref[...] = jnp.zeros(acc_ref.shape, F32)
        ht_ref[...] = h_ref[...].astype(F32).T.astype(BF16)

    a0 = pl.multiple_of(j * na, na)
    nslab = te // PEER_SLAB
    parts = []

    def act(s):
        ss = slice(s * PEER_SLAB, (s + 1) * PEER_SLAB)
        act_ref[ss, :] = jnp.dot(u_ref[ss, :], ht_ref[...], preferred_element_type=F32)

    def val(s):
        ss = slice(s * PEER_SLAB, (s + 1) * PEER_SLAB)
        parts.append(jnp.dot(vt_ref[:, ss], p_ref[ss, :], preferred_element_type=F32))

    def mask(al):
        rs = slice(al * PEER_NKEYS, (al + 1) * PEER_NKEYS)
        for c in range(tt // 128):
            cs = slice(c * 128, (c + 1) * 128)
            w = jnp.zeros((PEER_NKEYS, 128), F32)
            for hh in range(PEER_HEADS):
                n1 = n1_ref[hh, pl.ds(a0, na), cs][al:al + 1]
                e1 = e1_ref[hh, pl.ds(a0, na), cs][al:al + 1]
                w = w + jnp.where(r2_ref[hh, :, cs] < n1, e1 * e2_ref[hh, :, cs], 0.0)
            p_ref[rs, cs] = (w * _gelu(act_ref[rs, cs])).astype(BF16)

    per = PEER_SLAB // PEER_NKEYS
    act(0)
    act(1)
    for s in range(nslab):
        for k in range(per):
            mask(s * per + k)
            if k == 0 and s + 2 < nslab:
                act(s + 2)
        val(s)
    acc_ref[...] += functools.reduce(lambda x, y: x + y, parts)

    @pl.when(j == pl.num_programs(1) - 1)
    def _():
        y = x_ref[...] + gt_ref[...] * acc_ref[...].T
        if final:
            y = y * lax.rsqrt(jnp.mean(y * y, axis=-1, keepdims=True) + NORM_EPS) * fg_ref[...]
        o_ref[...] = y


def _peer_main(h2, u_bf, vt_bf, n1, e1, r2, e2, x2, mod3, final_g, *, final, tt=256, te=4096):
    t, d = x2.shape
    assert (te // PEER_NKEYS) % 8 == 0 and te % PEER_SLAB == 0
    per_b = SEQ // tt
    big = pl.BlockSpec((PEER_HEADS, PEER_NKEYS, tt), lambda i, j: (0, 0, i))
    return pl.pallas_call(
        functools.partial(_peer_main_kernel, final=final),
        grid=(t // tt, PEER_EXPERTS // te),
        in_specs=[pl.BlockSpec((tt, d), lambda i, j: (i, 0)),
                  pl.BlockSpec((te, d), lambda i, j: (j, 0)),
                  pl.BlockSpec((d, te), lambda i, j: (0, j)),
                  big, big, big, big,
                  pl.BlockSpec((tt, d), lambda i, j: (i, 0)),
                  pl.BlockSpec((None, 1, d), lambda i, j: (i // per_b, 0, 5)),
                  pl.BlockSpec((1, d), lambda i, j: (0, 0))],
        out_specs=pl.BlockSpec((tt, d), lambda i, j: (i, 0)),
        out_shape=jax.ShapeDtypeStruct((t, d), F32),
        scratch_shapes=[pltpu.VMEM((d, tt), F32), pltpu.VMEM((te, tt), F32), pltpu.VMEM((te, tt), BF16),
                        pltpu.VMEM((d, tt), BF16)],
        compiler_params=_params(("parallel", "arbitrary")),
        name="peer_experts",
    )(h2, u_bf, vt_bf, n1, e1, r2, e2, x2, mod3, final_g.reshape(1, d))


def _reorder_w_in(w):
    pad = jnp.zeros((w.shape[0], 1024 - 792), w.dtype)
    return jnp.concatenate([w[:, 5400:9496], w[:, 3072:3864], pad, w[:, 0:3072], w[:, 3864:5400]], axis=1)


def kernel(x, c, positions, mod_w, mod_b, norm_mix_g, norm_ffn_g, w_in, a_conv_w, a_out, b_conv_w, b_conv_b, b_ln_g, b_ln_b, b_out, c_cmp_pos, c_cmp_w1, c_cmp_w2, c_out, d_out, w_o, peer_wq, peer_subkeys, peer_u, peer_v, final_norm_g):
    bsz, s, d = x.shape
    assert s == SEQ and d == D_MODEL
    depth = mod_w.shape[0]
    t = bsz * s
    cos, sin = _rope_tables(positions)
    mod = _modulation(c, mod_w, mod_b)
    x2 = x.reshape(t, d)
    for l in range(depth):
        mod3 = mod[l].reshape(bsz, 1, 6 * d)
        proj2 = _norm_matmul(x2, norm_mix_g[l], mod3, 0, 1, _reorder_w_in(w_in[l].astype(BF16)), out_dtype=BF16,
                             tn=PROJ_COLS // 4)
        proj3 = proj2.reshape(bsz, s, PROJ_COLS)
        ua, ub = _conv_mixers(proj3, a_conv_w[l], b_conv_w[l], b_conv_b[l], b_ln_g[l], b_ln_b[l])
        qc, kc, vc, ks, vst, kw, vwt, gt, qd, kd, vdt, kmean = _prep(proj3, cos, sin)
        kcmp, vcmpt = _compress(kc, vc, c_cmp_pos[l], c_cmp_w1[l], c_cmp_w2[l])
        oc = _nsa(qc, kcmp, vcmpt, ks, vst, kw, vwt, gt)
        od = _moba(qd, kd, vdt, kmean)
        x2 = _merge(ua.reshape(t, 512), ub.reshape(t, 512), oc.reshape(t, 512), od.reshape(t, 512),
                    proj2, x2, mod3, a_out[l].astype(BF16), b_out[l].astype(BF16), c_out[l].astype(BF16),
                    d_out[l].astype(BF16), w_o[l].astype(BF16))
        q2, h2 = _norm_matmul(x2, norm_ffn_g[l], mod3, 3, 4, peer_wq[l].astype(BF16), emit_h=True,
                              tn=PEER_HEADS * PEER_QDIM)
        n1, e1, r2, e2 = _peer_select(q2, peer_subkeys[l])
        x2 = _peer_main(h2, peer_u[l].astype(BF16), peer_v[l].T.astype(BF16), n1, e1, r2, e2,
                        x2, mod3, final_norm_g, final=(l == depth - 1))
    return x2.reshape(bsz, s, d)
```

```python
import functools

import numpy as np
import jax
import jax.numpy as jnp
from jax import lax
from jax.experimental import pallas as pl
from jax.experimental.pallas import tpu as pltpu

F32 = jnp.float32
BF16 = jnp.bfloat16
HIGHEST = lax.Precision.HIGHEST

D_MODEL = 1024
SEQ = 2048
HEAD_DIM = 64
ROPE_THETA = 10000.0
NORM_EPS = 1e-6
A_WIDTH = 512
A_CONV = 3
B_WIDTH = 512
B_CONV = 31
C_HEADS = 8
C_KV_HEADS = 2
C_GROUP = 4
CMP_BLOCK = 32
CMP_STRIDE = 16
CMP_HIDDEN = 128
N_CMP = (SEQ - CMP_BLOCK) // CMP_STRIDE + 1
SLC_BLOCK = 64
SLC_TOPN = 16
N_SLC = SEQ // SLC_BLOCK
WIN = 512
D_HEADS = 8
MOBA_BLOCK = 256
MOBA_TOPK = 3
N_MOBA = SEQ // MOBA_BLOCK
PEER_HEADS = 8
PEER_NKEYS = 128
PEER_EXPERTS = PEER_NKEYS * PEER_NKEYS
PEER_QDIM = 256
PEER_TOPK = 16

PROJ_COLS = 9728
COL_MERGE = 0
COL_KVG = 4096
COL_GATE = COL_KVG + 768
COL_A = 5120

VMEM_LIMIT = 56 * 1024 * 1024
NEG = -1e30

ATT_T = 256
CONV_T = 256
HALO = 32


def _params(sem, flags=None):
    return pltpu.CompilerParams(dimension_semantics=sem, vmem_limit_bytes=VMEM_LIMIT, flags=flags)


def _gelu(x):
    return 0.5 * x * (1.0 + lax.erf(x * np.float32(np.sqrt(0.5))))


def _rope_table_kernel(pos_ref, inv_ref, sign_ref, cos_ref, sin_ref):
    ang = pos_ref[...] * inv_ref[...]
    cos_ref[...] = jnp.cos(ang)
    sin_ref[...] = jnp.sin(ang) * sign_ref[...]


def _rope_tables(positions):
    bsz, s = positions.shape
    inv = 1.0 / (ROPE_THETA ** (jnp.arange(0, HEAD_DIM, 2, dtype=F32) / HEAD_DIM))
    inv128 = jnp.tile(inv, 4)[None, :]
    sign = jnp.tile(jnp.concatenate([-jnp.ones(32, F32), jnp.ones(32, F32)]), 2)[None, :]
    pos = positions.astype(F32).reshape(bsz * s, 1)
    t = bsz * s
    cos, sin = pl.pallas_call(
        _rope_table_kernel,
        grid=(t // SEQ,),
        in_specs=[pl.BlockSpec((SEQ, 1), lambda i: (i, 0)),
                  pl.BlockSpec((1, 128), lambda i: (0, 0)),
                  pl.BlockSpec((1, 128), lambda i: (0, 0))],
        out_specs=[pl.BlockSpec((SEQ, 128), lambda i: (i, 0))] * 2,
        out_shape=[jax.ShapeDtypeStruct((t, 128), F32)] * 2,
        compiler_params=_params(("parallel",)),
        name="rope_tables",
    )(pos, inv128, sign)
    return cos.reshape(bsz, s, 128), sin.reshape(bsz, s, 128)


def _rope(x, cos, sin):
    w = x.shape[-1]
    lane = lax.broadcasted_iota(jnp.int32, x.shape, 1)
    swapped = jnp.where(lane % 64 < 32, pltpu.roll(x, w - 32, 1), pltpu.roll(x, 32, 1))
    return x * cos + swapped * sin


def _mod_kernel(c_ref, w_ref, b_ref, o_ref):
    c = c_ref[...]
    cond = c * jax.nn.sigmoid(c)
    o_ref[...] = jnp.dot(cond, w_ref[...], precision=HIGHEST, preferred_element_type=F32) + b_ref[...]


def _modulation(c, mod_w, mod_b):
    nl, d, n = mod_w.shape
    bsz = c.shape[0]
    tn = 1536
    return pl.pallas_call(
        _mod_kernel,
        grid=(nl, n // tn),
        in_specs=[pl.BlockSpec((bsz, d), lambda l, j: (0, 0)),
                  pl.BlockSpec((None, d, tn), lambda l, j: (l, 0, j)),
                  pl.BlockSpec((None, 1, tn), lambda l, j: (l, 0, j))],
        out_specs=pl.BlockSpec((None, bsz, tn), lambda l, j: (l, 0, j)),
        out_shape=jax.ShapeDtypeStruct((nl, bsz, n), F32),
        compiler_params=_params(("parallel", "parallel")),
        name="adaln_mod",
    )(c, mod_w, mod_b.reshape(nl, 1, n))


def _norm_matmul_kernel(x_ref, g_ref, sc_ref, sh_ref, w_ref, o_ref, *rest, emit_h):
    h_scr = rest[-1]

    @pl.when(pl.program_id(1) == 0)
    def _():
        x = x_ref[...]
        y = x * lax.rsqrt(jnp.mean(x * x, axis=-1, keepdims=True) + NORM_EPS)
        h = (y * g_ref[...]) * (1.0 + sc_ref[...]) + sh_ref[...]
        h_scr[...] = h.astype(BF16)
        if emit_h:
            rest[0][...] = h.astype(BF16)

    o_ref[...] = jnp.dot(h_scr[...], w_ref[...], preferred_element_type=F32).astype(o_ref.dtype)


def _norm_matmul(x2, g, mod3, sh_blk, sc_blk, w, *, out_dtype=F32, emit_h=False, tm=1024, tn=512):
    t, d = x2.shape
    n = w.shape[1]
    per_b = SEQ // tm
    out_shape = [jax.ShapeDtypeStruct((t, n), out_dtype)]
    out_specs = [pl.BlockSpec((tm, tn), lambda i, j: (i, j))]
    if emit_h:
        out_shape.append(jax.ShapeDtypeStruct((t, d), BF16))
        out_specs.append(pl.BlockSpec((tm, d), lambda i, j: (i, 0)))
    res = pl.pallas_call(
        functools.partial(_norm_matmul_kernel, emit_h=emit_h),
        grid=(t // tm, n // tn),
        in_specs=[pl.BlockSpec((tm, d), lambda i, j: (i, 0)),
                  pl.BlockSpec((1, d), lambda i, j: (0, 0)),
                  pl.BlockSpec((None, 1, d), lambda i, j: (i // per_b, 0, sc_blk)),
                  pl.BlockSpec((None, 1, d), lambda i, j: (i // per_b, 0, sh_blk)),
                  pl.BlockSpec((d, tn), lambda i, j: (0, j))],
        out_specs=out_specs,
        out_shape=out_shape,
        scratch_shapes=[pltpu.VMEM((tm, d), BF16)],
        compiler_params=_params(("parallel", "arbitrary")),
        name="norm_matmul",
    )(x2, g.reshape(1, d), mod3, mod3, w)
    return res if emit_h else res[0]


def _conv_kernel(ab_ref, ac_ref, ax_ref, ba_ref, bg_ref, pac_ref, pax_ref, pba_ref, pbg_ref,
                 aw_ref, bw_ref, bb_ref, lng_ref, lnb_ref, ua_ref, ub_ref, ext_ref, y_ref):
    ts = ab_ref.shape[0]
    keep = (pl.program_id(1) > 0).astype(F32)

    def f32(ref):
        return ref[...].astype(F32)

    def causal_conv(w_ref, taps, bias_ref, out_ref):
        blk = 128
        reach = ((taps - 1) // 8) * 8
        for t0 in range(0, ts, blk):
            for c0 in range(0, w_ref.shape[1], 128):
                cs = slice(c0, c0 + 128)
                acc = jnp.zeros((blk, 128), F32) if bias_ref is None else jnp.zeros((blk, 128), F32) + bias_ref[:, cs]
                for r in range(min(8, taps)):
                    base = HALO + t0 - reach - r
                    ur = ext_ref[base:base + blk + reach, cs]
                    for q in range(reach // 8 + 1):
                        shift = 8 * q + r
                        if shift < taps:
                            k = taps - 1 - shift
                            acc = acc + w_ref[k:k + 1, cs] * ur[reach - 8 * q:reach - 8 * q + blk]
                out_ref[t0:t0 + blk, cs] = acc

    ext_ref[0:HALO, :] = f32(pac_ref) * f32(pax_ref) * keep
    ext_ref[HALO:HALO + ts, :] = f32(ac_ref) * f32(ax_ref)
    causal_conv(aw_ref, A_CONV, None, y_ref)
    ua_ref[...] = (f32(ab_ref) * y_ref[...]).astype(BF16)

    ext_ref[0:HALO, :] = f32(pba_ref) * jax.nn.sigmoid(f32(pbg_ref)) * keep
    ext_ref[HALO:HALO + ts, :] = f32(ba_ref) * jax.nn.sigmoid(f32(bg_ref))
    causal_conv(bw_ref, B_CONV, bb_ref, y_ref)
    acc = y_ref[...]
    mu = jnp.mean(acc, axis=-1, keepdims=True)
    cen = acc - mu
    var = jnp.mean(cen * cen, axis=-1, keepdims=True)
    y = cen * lax.rsqrt(var + NORM_EPS) * lng_ref[...] + lnb_ref[...]
    ub_ref[...] = (y * jax.nn.sigmoid(y)).astype(BF16)


def _conv_mixers(proj3, a_conv_w, b_conv_w, b_conv_b, b_ln_g, b_ln_b):
    bsz = proj3.shape[0]
    ts = CONV_T
    c0 = COL_A // 512
    r = ts // HALO

    def cur(k):
        return pl.BlockSpec((None, ts, 512), lambda b, i, k=k: (b, i, c0 + k))

    def prev(k):
        return pl.BlockSpec((None, HALO, 512), lambda b, i, k=k: (b, jnp.maximum(i * r - 1, 0), c0 + k))

    def full(shape):
        return pl.BlockSpec(shape, lambda b, i: (0,) * len(shape))

    return pl.pallas_call(
        _conv_kernel,
        grid=(bsz, SEQ // ts),
        in_specs=[cur(0), cur(1), cur(2), cur(3), cur(4), prev(1), prev(2), prev(3), prev(4),
                  full((A_CONV, A_WIDTH)), full((B_CONV, B_WIDTH)), full((1, B_WIDTH)),
                  full((1, B_WIDTH)), full((1, B_WIDTH))],
        out_specs=[pl.BlockSpec((None, ts, 512), lambda b, i: (b, i, 0))] * 2,
        out_shape=[jax.ShapeDtypeStruct((bsz, SEQ, 512), BF16)] * 2,
        scratch_shapes=[pltpu.VMEM((HALO + ts, 512), F32), pltpu.VMEM((ts, 512), F32)],
        compiler_params=_params(("parallel", "arbitrary")),
        name="conv_mixers",
    )(proj3, proj3, proj3, proj3, proj3, proj3, proj3, proj3, proj3,
      a_conv_w, b_conv_w, b_conv_b.reshape(1, -1), b_ln_g.reshape(1, -1), b_ln_b.reshape(1, -1))


VT_ROWS = 80


def _values_t(v, n_heads):
    rows = v.shape[0]
    vt = v.T
    tail = (lax.broadcasted_iota(jnp.int32, (VT_ROWS - HEAD_DIM, rows), 0) == 0).astype(F32)
    blocks = []
    for h in range(n_heads):
        blocks += [vt[h * HEAD_DIM:(h + 1) * HEAD_DIM], tail]
    return jnp.concatenate(blocks, axis=0).astype(BF16)


def _prep_kernel(cq_ref, dq_ref, dk_ref, dv_ref, kvg_ref, cos_ref, sin_ref,
                 qc_ref, kc_ref, vc_ref, ks_ref, vst_ref, kw_ref, vwt_ref, gt_ref,
                 qd_ref, kd_ref, vdt_ref, kmean_ref):
    cos = cos_ref[...]
    sin = sin_ref[...]
    cos4 = jnp.concatenate([cos] * 4, axis=1)
    sin4 = jnp.concatenate([sin] * 4, axis=1)
    scale = np.float32(HEAD_DIM ** -0.5 * np.log2(np.e))
    qc_ref[...] = _rope(cq_ref[...].astype(F32), cos4, sin4) * scale
    qd_ref[...] = _rope(dq_ref[...].astype(F32), cos4, sin4) * scale
    kd = _rope(dk_ref[...].astype(F32), cos4, sin4)
    for h in range(D_HEADS):
        kd_ref[h] = kd[:, h * 64:(h + 1) * 64].astype(BF16)
    kmean_ref[...] = jnp.mean(kd, axis=0, keepdims=True)
    vdt_ref[...] = _values_t(dv_ref[...].astype(F32), D_HEADS)

    def kvg(k):
        return kvg_ref[:, k * 128:(k + 1) * 128].astype(F32)

    kc = _rope(kvg(0), cos, sin)
    vc = kvg(1)
    ks = _rope(kvg(2), cos, sin)
    kw = _rope(kvg(4), cos, sin)
    for g in range(C_KV_HEADS):
        gs = slice(g * 64, (g + 1) * 64)
        kc_ref[g] = kc[:, gs]
        vc_ref[g] = vc[:, gs]
        ks_ref[g] = ks[:, gs].astype(BF16)
        kw_ref[g] = kw[:, gs].astype(BF16)
    vst_ref[...] = _values_t(kvg(3), C_KV_HEADS)
    vwt_ref[...] = _values_t(kvg(5), C_KV_HEADS)
    gt_ref[...] = jax.nn.sigmoid(kvg(6)).T


def _prep(proj3, cos, sin):
    bsz = proj3.shape[0]
    ts = MOBA_BLOCK
    c0 = COL_A // 512

    def col512(k):
        return pl.BlockSpec((None, ts, 512), lambda b, i: (b, i, c0 + k))

    row128 = pl.BlockSpec((None, ts, 128), lambda b, i: (b, i, 0))
    row512 = pl.BlockSpec((None, ts, 512), lambda b, i: (b, i, 0))
    col128t = pl.BlockSpec((None, 128, ts), lambda b, i: (b, 0, i))
    val2t = pl.BlockSpec((None, C_KV_HEADS * VT_ROWS, ts), lambda b, i: (b, 0, i))
    head64 = pl.BlockSpec((None, C_KV_HEADS, ts, 64), lambda b, i: (b, 0, i, 0))
    k64 = jax.ShapeDtypeStruct((bsz, C_KV_HEADS, SEQ, 64), BF16)
    t128 = jax.ShapeDtypeStruct((bsz, C_KV_HEADS * VT_ROWS, SEQ), BF16)
    return pl.pallas_call(
        _prep_kernel,
        grid=(bsz, SEQ // ts),
        in_specs=[col512(5), col512(6), col512(7), col512(8),
                  pl.BlockSpec((None, ts, 1024), lambda b, i: (b, i, COL_KVG // 1024)),
                  row128, row128],
        out_specs=[row512, head64, head64, head64, val2t, head64, val2t, col128t,
                   row512,
                   pl.BlockSpec((None, D_HEADS, ts, 64), lambda b, i: (b, 0, i, 0)),
                   pl.BlockSpec((None, D_HEADS * VT_ROWS, ts), lambda b, i: (b, 0, i)),
                   pl.BlockSpec((None, None, 1, 512), lambda b, i: (b, i, 0, 0))],
        out_shape=[jax.ShapeDtypeStruct((bsz, SEQ, 512), F32),
                   jax.ShapeDtypeStruct((bsz, C_KV_HEADS, SEQ, 64), F32),
                   jax.ShapeDtypeStruct((bsz, C_KV_HEADS, SEQ, 64), F32),
                   k64, t128, k64, t128,
                   jax.ShapeDtypeStruct((bsz, 128, SEQ), F32),
                   jax.ShapeDtypeStruct((bsz, SEQ, 512), F32),
                   jax.ShapeDtypeStruct((bsz, D_HEADS, SEQ, 64), BF16),
                   jax.ShapeDtypeStruct((bsz, D_HEADS * VT_ROWS, SEQ), BF16),
                   jax.ShapeDtypeStruct((bsz, N_MOBA, 1, 512), F32)],
        compiler_params=_params(("parallel", "parallel")),
        name="attn_prep",
    )(proj3, proj3, proj3, proj3, proj3, cos, sin)


def _compress_kernel(kc_ref, vc_ref, pos_ref, w1_ref, w2_ref, kcmp_ref, vcmp_ref):
    half = CMP_STRIDE * HEAD_DIM
    row = lax.broadcasted_iota(jnp.int32, (128, CMP_HIDDEN), 0)
    for which, (src, dst) in enumerate(((kc_ref, kcmp_ref), (vc_ref, vcmp_ref))):
        bias = jnp.dot(pos_ref[which], w1_ref[which], precision=HIGHEST, preferred_element_type=F32)[0:1]
        for g in range(C_KV_HEADS):
            chunks = src[g]
            d1 = jnp.dot(chunks, w1_ref[which, 0:half, :], precision=HIGHEST, preferred_element_type=F32)
            d2 = jnp.dot(chunks, w1_ref[which, half:2 * half, :], precision=HIGHEST, preferred_element_type=F32)
            d2 = jnp.where(row < 127, pltpu.roll(d2, 127, 0), 0.0)
            hid = _gelu(d1 + d2 + bias)
            out = jnp.dot(hid, w2_ref[which], precision=HIGHEST, preferred_element_type=F32)
            dst[g] = out if which == 0 else out.T


def _compress(kc, vc, cmp_pos, cmp_w1, cmp_w2):
    bsz = kc.shape[0]
    kc4 = kc.reshape(bsz, C_KV_HEADS, SEQ // CMP_STRIDE, CMP_STRIDE * HEAD_DIM)
    vc4 = vc.reshape(bsz, C_KV_HEADS, SEQ // CMP_STRIDE, CMP_STRIDE * HEAD_DIM)
    pos8 = jnp.broadcast_to(cmp_pos.reshape(2, 1, CMP_BLOCK * HEAD_DIM), (2, 8, CMP_BLOCK * HEAD_DIM))
    blk = pl.BlockSpec((None, C_KV_HEADS, 128, 1024), lambda b: (b, 0, 0, 0))
    out = pl.BlockSpec((None, C_KV_HEADS, 128, 64), lambda b: (b, 0, 0, 0))
    return pl.pallas_call(
        _compress_kernel,
        grid=(bsz,),
        in_specs=[blk, blk,
                  pl.BlockSpec((2, 8, 2048), lambda b: (0, 0, 0)),
                  pl.BlockSpec((2, 2048, CMP_HIDDEN), lambda b: (0, 0, 0)),
                  pl.BlockSpec((2, CMP_HIDDEN, 64), lambda b: (0, 0, 0))],
        out_specs=[out, pl.BlockSpec((None, C_KV_HEADS, 64, 128), lambda b: (b, 0, 0, 0))],
        out_shape=[jax.ShapeDtypeStruct((bsz, C_KV_HEADS, 128, 64), F32),
                   jax.ShapeDtypeStruct((bsz, C_KV_HEADS, 64, 128), F32)],
        compiler_params=_params(("parallel",)),
        name="nsa_compress",
    )(kc4, vc4, pos8, cmp_w1, cmp_w2)


def _flash_steps_t(heads, m_ref=None, acc_ref=None, groups=1):
    heads = [h if len(h) == 8 else (*h, m_ref, acc_ref, n, groups) for n, h in enumerate(heads)]
    scores = [lax.dot_general(h[1], h[0], (((1,), (1,)), ((), ())), preferred_element_type=F32)
              for h in heads]
    probs = []
    for (qb, _, _, mask, mr, _, n, grp), s in zip(heads, scores):
        tq = qb.shape[0] // grp
        if mask is not None:
            s = jnp.concatenate([jnp.where(mask, s[:, r * tq:(r + 1) * tq], NEG) for r in range(grp)], axis=1)
        sr = _stat_row(n)
        m_old = mr[sr, :]
        m_new = jnp.maximum(m_old, jnp.max(s, axis=0, keepdims=True))
        mr[sr, :] = m_new
        probs.append((jnp.exp2(m_old - m_new), jnp.exp2(s - m_new).astype(BF16)))
    for (_, _, vt, _, _, ar, n, _), (alpha, p) in zip(heads, probs):
        a = slice(n * VT_ROWS, (n + 1) * VT_ROWS)
        ar[a, :] = alpha * ar[a, :] + jnp.dot(vt, p, preferred_element_type=F32)


def _flash_out(acc_ref, n):
    return acc_ref[n * VT_ROWS:n * VT_ROWS + HEAD_DIM, :] / acc_ref[n * VT_ROWS + HEAD_DIM:n * VT_ROWS + HEAD_DIM + 1, :]


def _stat_row(row):
    return slice(8 * row, 8 * row + 1)


def _flash_init(m_ref, acc_ref):
    m_ref[...] = jnp.full(m_ref.shape, NEG, F32)
    acc_ref[...] = jnp.zeros(acc_ref.shape, F32)


def _rank_desc_rows(vals):
    n = vals.shape[0]
    row = lax.broadcasted_iota(jnp.int32, vals.shape, 0)
    rank = jnp.zeros(vals.shape, jnp.int32)
    for i in range(n):
        vi = vals[i:i + 1, :]
        ahead = (vi > vals) | ((vi == vals) & (row > i))
        rank = rank + ahead.astype(jnp.int32)
    return rank


def _nsa_kernel(q_ref, kcmp_ref, vcmpt_ref, ks_ref, vst_ref, kw_ref, vwt_ref, gt_ref, ovlt_ref, expt_ref,
                o_ref, qb_ref, sel_ref, m_ref, acc_ref, out_ref):
    tq = ATT_T
    i = pl.program_id(1)
    t0 = i * tq
    dstart = pl.multiple_of(t0, tq)
    t_row = t0 + lax.broadcasted_iota(jnp.int32, (1, tq), 1)
    t_rows = jnp.concatenate([t_row] * C_GROUP, axis=1)
    blk = lax.broadcasted_iota(jnp.int32, (128, 1), 0)
    kk = lax.broadcasted_iota(jnp.int32, (tq, tq), 0)
    qq = lax.broadcasted_iota(jnp.int32, (tq, tq), 1)
    causal = kk <= qq
    win_tail = kk > qq

    def gate_row(branch, g):
        base = branch * 8 + g * C_GROUP
        return jnp.concatenate([gt_ref[base + r:base + r + 1, :] for r in range(C_GROUP)], axis=1)

    def hrows(g):
        return slice(g * HEAD_DIM, (g + 1) * HEAD_DIM)

    def vrows(g):
        return slice(g * VT_ROWS, (g + 1) * VT_ROWS)

    for g in range(C_KV_HEADS):
        qf = jnp.concatenate([q_ref[:, (g * 4 + r) * 64:(g * 4 + r + 1) * 64] for r in range(C_GROUP)], axis=0)
        qb_ref[g] = qf.astype(BF16)

        s = lax.dot_general(kcmp_ref[g], qf, (((1,), (1,)), ((), ())), precision=HIGHEST,
                            preferred_element_type=F32)
        vis = (blk * CMP_STRIDE + (CMP_BLOCK - 1)) <= t_rows
        sm = jnp.where(vis, s, NEG)
        e = jnp.where(vis, jnp.exp2(sm - jnp.max(sm, axis=0, keepdims=True)), 0.0)
        p = e / jnp.maximum(jnp.sum(e, axis=0, keepdims=True), 1e-30)
        o_cmp = jnp.dot(vcmpt_ref[g].astype(BF16), p.astype(BF16), preferred_element_type=F32)
        out_ref[hrows(g), :] = gate_row(0, g) * o_cmp

        psum = p[:, 0:tq] + p[:, tq:2 * tq] + p[:, 2 * tq:3 * tq] + p[:, 3 * tq:4 * tq]
        imp = jnp.dot(ovlt_ref[...], psum, precision=HIGHEST, preferred_element_type=F32)[0:N_SLC]
        b32 = blk[0:N_SLC]
        cur = t_row // SLC_BLOCK
        forced = (b32 == 0) | (b32 == cur) | (b32 == cur - 1)
        imp = jnp.where(forced, jnp.inf, jnp.where(b32 * SLC_BLOCK > t_row, -jnp.inf, imp))
        sel = (_rank_desc_rows(imp) < SLC_TOPN).astype(F32)
        sel_ref[g] = jnp.concatenate([sel, jnp.zeros((128 - N_SLC, tq), F32)], axis=0).astype(BF16)

    def slc_tile(jb, extra):
        start = pl.multiple_of(jb * tq, tq)
        heads = []
        for g in range(C_KV_HEADS):
            hit = jnp.dot(expt_ref[pl.ds(start, tq), :], sel_ref[g], preferred_element_type=F32) > 0.5
            mask = hit if extra is None else hit & extra
            heads.append((qb_ref[g], ks_ref[g, pl.ds(start, tq), :], vst_ref[vrows(g), pl.ds(start, tq)], mask))
        _flash_steps_t(heads, m_ref, acc_ref, C_GROUP)

    def win_tile(start, mask):
        heads = [(qb_ref[g], kw_ref[g, pl.ds(start, tq), :], vwt_ref[vrows(g), pl.ds(start, tq)], mask)
                 for g in range(C_KV_HEADS)]
        _flash_steps_t(heads, m_ref, acc_ref, C_GROUP)

    def add_branch(branch):
        for g in range(C_KV_HEADS):
            out_ref[hrows(g), :] += gate_row(branch, g) * _flash_out(acc_ref, g)

    _flash_init(m_ref, acc_ref)
    slc_tile(i, causal)

    def slc_body(jb, carry):
        slc_tile(jb, None)
        return carry

    lax.fori_loop(0, i, slc_body, 0)
    add_branch(1)

    _flash_init(m_ref, acc_ref)
    win_tile(dstart, causal)

    @pl.when(i >= 1)
    def _():
        win_tile(pl.multiple_of(t0 - tq, tq), None)

    @pl.when(i >= 2)
    def _():
        win_tile(pl.multiple_of(t0 - 2 * tq, tq), win_tail)

    add_branch(2)

    o_t = jnp.concatenate([out_ref[hrows(g), r * tq:(r + 1) * tq]
                           for g in range(C_KV_HEADS) for r in range(C_GROUP)], axis=0)
    o_ref[...] = o_t.T.astype(BF16)


def _nsa_constants():
    j = np.arange(128)[:, None]
    n = np.arange(128)[None, :]
    ovl_t = ((n * CMP_STRIDE < j * SLC_BLOCK + SLC_BLOCK) & (n * CMP_STRIDE + CMP_BLOCK > j * SLC_BLOCK)
             & (n < N_CMP) & (j < N_SLC)).astype(np.float32)
    expand_t = (np.arange(SEQ)[:, None] // SLC_BLOCK == np.arange(128)[None, :]).astype(np.float32)
    return jnp.asarray(ovl_t), jnp.asarray(expand_t, dtype=BF16)


def _nsa(qc, kcmp, vcmpt, ks, vst, kw, vwt, gt):
    bsz = qc.shape[0]
    tq = ATT_T
    ovl_t, expand_t = _nsa_constants()
    keys = pl.BlockSpec((None, C_KV_HEADS, SEQ, 64), lambda b, i: (b, 0, 0, 0))
    vals = pl.BlockSpec((None, C_KV_HEADS * VT_ROWS, SEQ), lambda b, i: (b, 0, 0))
    rows = C_GROUP * tq
    return pl.pallas_call(
        _nsa_kernel,
        grid=(bsz, SEQ // tq),
        in_specs=[pl.BlockSpec((None, tq, 512), lambda b, i: (b, i, 0)),
                  pl.BlockSpec((None, C_KV_HEADS, 128, 64), lambda b, i: (b, 0, 0, 0)),
                  pl.BlockSpec((None, C_KV_HEADS, 64, 128), lambda b, i: (b, 0, 0, 0)),
                  keys, vals, keys, vals,
                  pl.BlockSpec((None, 128, tq), lambda b, i: (b, 0, i)),
                  pl.BlockSpec((128, 128), lambda b, i: (0, 0)),
                  pl.BlockSpec((SEQ, 128), lambda b, i: (0, 0))],
        out_specs=pl.BlockSpec((None, tq, 512), lambda b, i: (b, i, 0)),
        out_shape=jax.ShapeDtypeStruct((bsz, SEQ, 512), BF16),
        scratch_shapes=[pltpu.VMEM((C_KV_HEADS, rows, 64), BF16), pltpu.VMEM((C_KV_HEADS, 128, tq), BF16),
                        pltpu.VMEM((8 * C_KV_HEADS, rows), F32),
                        pltpu.VMEM((C_KV_HEADS * VT_ROWS, rows), F32), pltpu.VMEM((C_KV_HEADS * 64, rows), F32)],
        compiler_params=_params(("parallel", "arbitrary")),
        name="nsa_attention",
    )(qc, kcmp, vcmpt, ks, vst, kw, vwt, gt, ovl_t, expand_t)


def _moba_kernel(q_ref, k_ref, vt_ref, kmean_ref, o_ref, qb_ref, sel_ref, m_ref, acc_ref):
    tq = ATT_T
    i = pl.program_id(1)
    blk = lax.broadcasted_iota(jnp.int32, (N_MOBA, 1), 0)
    kk = lax.broadcasted_iota(jnp.int32, (tq, tq), 0)
    qq = lax.broadcasted_iota(jnp.int32, (tq, tq), 1)
    causal = kk <= qq
    past = blk < i
    for h in range(D_HEADS):
        hs = slice(h * 64, (h + 1) * 64)
        qf = q_ref[:, hs]
        qb_ref[h] = qf.astype(BF16)
        gate = lax.dot_general(kmean_ref[:, hs], qf, (((1,), (1,)), ((), ())), precision=HIGHEST,
                               preferred_element_type=F32)
        gate = jnp.where(past, gate, -jnp.inf)
        sel_ref[h] = (past & (_rank_desc_rows(gate) < MOBA_TOPK)).astype(F32)

    def tile(jb, diag):
        start = pl.multiple_of(jb * tq, tq)
        heads = []
        for h in range(D_HEADS):
            if diag:
                mask = causal
            else:
                mask = jnp.sum(jnp.where(blk == jb, sel_ref[h], 0.0), axis=0, keepdims=True) > 0.5
            heads.append((qb_ref[h], k_ref[h, pl.ds(start, tq), :],
                          vt_ref[h * VT_ROWS:(h + 1) * VT_ROWS, pl.ds(start, tq)], mask))
        _flash_steps_t(heads, m_ref, acc_ref, 1)

    _flash_init(m_ref, acc_ref)
    tile(i, True)

    def body(jb, carry):
        tile(jb, False)
        return carry

    lax.fori_loop(0, i, body, 0)
    o_t = jnp.concatenate([_flash_out(acc_ref, h) for h in range(D_HEADS)], axis=0)
    o_ref[...] = o_t.T.astype(BF16)


def _moba(qd, kd, vdt, kmean):
    bsz = qd.shape[0]
    tq = ATT_T
    return pl.pallas_call(
        _moba_kernel,
        grid=(bsz, SEQ // tq),
        in_specs=[pl.BlockSpec((None, tq, 512), lambda b, i: (b, i, 0)),
                  pl.BlockSpec((None, D_HEADS, SEQ, 64), lambda b, i: (b, 0, 0, 0)),
                  pl.BlockSpec((None, D_HEADS * VT_ROWS, SEQ), lambda b, i: (b, 0, 0)),
                  pl.BlockSpec((None, N_MOBA, 512), lambda b, i: (b, 0, 0))],
        out_specs=pl.BlockSpec((None, tq, 512), lambda b, i: (b, i, 0)),
        out_shape=jax.ShapeDtypeStruct((bsz, SEQ, 512), BF16),
        scratch_shapes=[pltpu.VMEM((D_HEADS, tq, 64), BF16), pltpu.VMEM((D_HEADS, N_MOBA, tq), F32),
                        pltpu.VMEM((8 * D_HEADS, tq), F32), pltpu.VMEM((D_HEADS * VT_ROWS, tq), F32)],
        compiler_params=_params(("parallel", "arbitrary")),
        name="moba_attention",
    )(qd, kd, vdt, kmean.reshape(bsz, N_MOBA, 512))


def _attn_kernel(q_ref, kcmp_ref, vcmpt_ref, ks_ref, vst_ref, kw_ref, vwt_ref, gt_ref, ovlt_ref, expt_ref,
                 qd_ref, kd_ref, vdt_ref, kmean_ref, oc_ref, od_ref,
                 qb_ref, sel_ref, m_ref, acc_ref, out_ref, qbd_ref, seld_ref, md_ref, accd_ref):
    tq = ATT_T
    i = pl.program_id(1)
    t0 = i * tq
    t_row = t0 + lax.broadcasted_iota(jnp.int32, (1, tq), 1)
    t_rows = jnp.concatenate([t_row] * C_GROUP, axis=1)
    blk = lax.broadcasted_iota(jnp.int32, (128, 1), 0)
    kk = lax.broadcasted_iota(jnp.int32, (tq, tq), 0)
    qq = lax.broadcasted_iota(jnp.int32, (tq, tq), 1)
    causal = kk <= qq

    def gate_row(branch, g):
        base = branch * 8 + g * C_GROUP
        return jnp.concatenate([gt_ref[base + r:base + r + 1, :] for r in range(C_GROUP)], axis=1)

    def hrows(g):
        return slice(g * HEAD_DIM, (g + 1) * HEAD_DIM)

    def vrows(g):
        return slice(g * VT_ROWS, (g + 1) * VT_ROWS)

    for g in range(C_KV_HEADS):
        qf = jnp.concatenate([q_ref[:, (g * 4 + r) * 64:(g * 4 + r + 1) * 64] for r in range(C_GROUP)], axis=0)
        qb_ref[g] = qf.astype(BF16)
        s = lax.dot_general(kcmp_ref[g], qf, (((1,), (1,)), ((), ())), precision=HIGHEST,
                            preferred_element_type=F32)
        vis = (blk * CMP_STRIDE + (CMP_BLOCK - 1)) <= t_rows
        sm = jnp.where(vis, s, NEG)
        e = jnp.where(vis, jnp.exp2(sm - jnp.max(sm, axis=0, keepdims=True)), 0.0)
        p = e / jnp.maximum(jnp.sum(e, axis=0, keepdims=True), 1e-30)
        o_cmp = jnp.dot(vcmpt_ref[g].astype(BF16), p.astype(BF16), preferred_element_type=F32)
        out_ref[hrows(g), :] = gate_row(0, g) * o_cmp
        psum = p[:, 0:tq] + p[:, tq:2 * tq] + p[:, 2 * tq:3 * tq] + p[:, 3 * tq:4 * tq]
        imp = jnp.dot(ovlt_ref[...], psum, precision=HIGHEST, preferred_element_type=F32)[0:N_SLC]
        b32 = blk[0:N_SLC]
        cur = t_row // SLC_BLOCK
        forced = (b32 == 0) | (b32 == cur) | (b32 == cur - 1)
        imp = jnp.where(forced, jnp.inf, jnp.where(b32 * SLC_BLOCK > t_row, -jnp.inf, imp))
        sel = (_rank_desc_rows(imp) < SLC_TOPN).astype(F32)
        sel_ref[g] = jnp.concatenate([sel, jnp.zeros((128 - N_SLC, tq), F32)], axis=0).astype(BF16)

    blk8 = lax.broadcasted_iota(jnp.int32, (N_MOBA, 1), 0)
    past = blk8 < i
    for h in range(D_HEADS):
        hs = slice(h * 64, (h + 1) * 64)
        qf = qd_ref[:, hs]
        qbd_ref[h] = qf.astype(BF16)
        gate = lax.dot_general(kmean_ref[:, hs], qf, (((1,), (1,)), ((), ())), precision=HIGHEST,
                               preferred_element_type=F32)
        gate = jnp.where(past, gate, -jnp.inf)
        seld_ref[h] = (past & (_rank_desc_rows(gate) < MOBA_TOPK)).astype(F32)

    def slc_heads(jb, extra):
        start = pl.multiple_of(jb * tq, tq)
        heads = []
        for g in range(C_KV_HEADS):
            hit = jnp.dot(expt_ref[pl.ds(start, tq), :], sel_ref[g], preferred_element_type=F32) > 0.5
            mask = hit if extra is None else hit & extra
            heads.append((qb_ref[g], ks_ref[g, pl.ds(start, tq), :], vst_ref[vrows(g), pl.ds(start, tq)], mask,
                          m_ref, acc_ref, g, C_GROUP))
        return heads

    def win_heads():
        wk = WIN + tq
        wstart = pl.multiple_of(jnp.maximum(t0 - WIN, 0), tq)
        kpos = wstart + lax.broadcasted_iota(jnp.int32, (wk, 1), 0)
        mask = (kpos <= t_row) & (kpos > t_row - WIN)
        return [(qb_ref[g], kw_ref[g, pl.ds(wstart, wk), :], vwt_ref[vrows(g), pl.ds(wstart, wk)], mask,
                 m_ref, acc_ref, C_KV_HEADS + g, C_GROUP) for g in range(C_KV_HEADS)]

    def moba_heads(jb, diag):
        start = pl.multiple_of(jb * tq, tq)
        heads = []
        for h in range(D_HEADS):
            if diag:
                mask = causal
            else:
                mask = jnp.sum(jnp.where(blk8 == jb, seld_ref[h], 0.0), axis=0, keepdims=True) > 0.5
            heads.append((qbd_ref[h], kd_ref[h, pl.ds(start, tq), :],
                          vdt_ref[h * VT_ROWS:(h + 1) * VT_ROWS, pl.ds(start, tq)], mask, md_ref, accd_ref, h, 1))
        return heads

    _flash_init(m_ref, acc_ref)
    _flash_init(md_ref, accd_ref)
    _flash_steps_t(slc_heads(i, causal) + win_heads() + moba_heads(i, True))

    def body(jb, carry):
        _flash_steps_t(slc_heads(jb, None) + moba_heads(jb, False))
        return carry

    lax.fori_loop(0, i, body, 0)
    for g in range(C_KV_HEADS):
        out_ref[hrows(g), :] += (gate_row(1, g) * _flash_out(acc_ref, g)
                                 + gate_row(2, g) * _flash_out(acc_ref, C_KV_HEADS + g))
    o_t = jnp.concatenate([out_ref[hrows(g), r * tq:(r + 1) * tq]
                           for g in range(C_KV_HEADS) for r in range(C_GROUP)], axis=0)
    oc_ref[...] = o_t.T.astype(BF16)
    od_t = jnp.concatenate([_flash_out(accd_ref, h) for h in range(D_HEADS)], axis=0)
    od_ref[...] = od_t.T.astype(BF16)


def _attention(qc, kcmp, vcmpt, ks, vst, kw, vwt, gt, qd, kd, vdt, kmean):
    bsz = qc.shape[0]
    tq = ATT_T
    ovl_t, expand_t = _nsa_constants()
    keys = pl.BlockSpec((None, C_KV_HEADS, SEQ, 64), lambda b, i: (b, 0, 0, 0))
    vals = pl.BlockSpec((None, C_KV_HEADS * VT_ROWS, SEQ), lambda b, i: (b, 0, 0))
    qtile = pl.BlockSpec((None, tq, 512), lambda b, i: (b, i, 0))
    rows = C_GROUP * tq
    oshape = jax.ShapeDtypeStruct((bsz, SEQ, 512), BF16)
    return pl.pallas_call(
        _attn_kernel,
        grid=(bsz, SEQ // tq),
        in_specs=[qtile,
                  pl.BlockSpec((None, C_KV_HEADS, 128, 64), lambda b, i: (b, 0, 0, 0)),
                  pl.BlockSpec((None, C_KV_HEADS, 64, 128), lambda b, i: (b, 0, 0, 0)),
                  keys, vals, keys, vals,
                  pl.BlockSpec((None, 128, tq), lambda b, i: (b, 0, i)),
                  pl.BlockSpec((128, 128), lambda b, i: (0, 0)),
                  pl.BlockSpec((SEQ, 128), lambda b, i: (0, 0)),
                  qtile,
                  pl.BlockSpec((None, D_HEADS, SEQ, 64), lambda b, i: (b, 0, 0, 0)),
                  pl.BlockSpec((None, D_HEADS * VT_ROWS, SEQ), lambda b, i: (b, 0, 0)),
                  pl.BlockSpec((None, N_MOBA, 512), lambda b, i: (b, 0, 0))],
        out_specs=[qtile, qtile],
        out_shape=[oshape, oshape],
        scratch_shapes=[pltpu.VMEM((C_KV_HEADS, rows, 64), BF16), pltpu.VMEM((C_KV_HEADS, 128, tq), BF16),
                        pltpu.VMEM((8 * 2 * C_KV_HEADS, rows), F32),
                        pltpu.VMEM((2 * C_KV_HEADS * VT_ROWS, rows), F32), pltpu.VMEM((C_KV_HEADS * 64, rows), F32),
                        pltpu.VMEM((D_HEADS, tq, 64), BF16), pltpu.VMEM((D_HEADS, N_MOBA, tq), F32),
                        pltpu.VMEM((8 * D_HEADS, tq), F32), pltpu.VMEM((D_HEADS * VT_ROWS, tq), F32)],
        compiler_params=_params(("parallel", "arbitrary")),
        name="nsa_moba_attention",
    )(qc, kcmp, vcmpt, ks, vst, kw, vwt, gt, ovl_t, expand_t, qd, kd, vdt, kmean.reshape(bsz, N_MOBA, 512))


def _merge_kernel(ua_ref, ub_ref, oc_ref, od_ref, mg_ref, x_ref, gt_ref,
                  wa_ref, wb_ref, wc_ref, wd_ref, wo_ref, o_ref):
    d = D_MODEL
    merged = jnp.zeros(x_ref.shape, F32)
    for k, (u_ref, w_ref) in enumerate(((ua_ref, wa_ref), (ub_ref, wb_ref), (oc_ref, wc_ref), (od_ref, wd_ref))):
        y = jnp.dot(u_ref[...], w_ref[...], preferred_element_type=F32)
        merged = merged + jax.nn.sigmoid(mg_ref[:, k * d:(k + 1) * d].astype(F32)) * y
    o_ref[...] = x_ref[...] + gt_ref[...] * jnp.dot(merged.astype(BF16), wo_ref[...], preferred_element_type=F32)


def _merge(ua, ub, oc, od, proj2, x2, mod3, wa, wb, wc, wd, wo, *, tm=512):
    t, d = x2.shape
    per_b = SEQ // tm
    act = pl.BlockSpec((tm, 512), lambda i: (i, 0))
    wspec = pl.BlockSpec((512, d), lambda i: (0, 0))
    return pl.pallas_call(
        _merge_kernel,
        grid=(t // tm,),
        in_specs=[act, act, act, act,
                  pl.BlockSpec((tm, 4 * d), lambda i: (i, COL_MERGE // (4 * d))),
                  pl.BlockSpec((tm, d), lambda i: (i, 0)),
                  pl.BlockSpec((None, 1, d), lambda i: (i // per_b, 0, 2)),
                  wspec, wspec, wspec, wspec,
                  pl.BlockSpec((d, d), lambda i: (0, 0))],
        out_specs=pl.BlockSpec((tm, d), lambda i: (i, 0)),
        out_shape=jax.ShapeDtypeStruct((t, d), F32),
        compiler_params=_params(("parallel",)),
        name="merge_out",
    )(ua, ub, oc, od, proj2, x2, mod3, wa, wb, wc, wd, wo)


def _pop_max(work, idx):
    m = jnp.max(work, axis=0, keepdims=True)
    first = jnp.min(jnp.where(work == m, idx, work.shape[0]), axis=0, keepdims=True)
    return m, idx == first


_PEER_PAIRS = [(i, j) for i in range(PEER_TOPK) for j in range(PEER_TOPK) if (i + 1) * (j + 1) <= PEER_TOPK]
_PEER_PAIR_ROWS = 56
NOT_TOP = 99.0


def _peer_sel_kernel(q_ref, keys_ref, grp_ref, n1_ref, e1_ref, r2_ref, e2_ref, st_ref):
    tt = q_ref.shape[0]
    half = PEER_QDIM // 2
    for hp in range(2 * PEER_HEADS):
        qh = q_ref[:, hp * half:(hp + 1) * half]
        qn = qh * lax.rsqrt(jnp.mean(qh * qh, axis=-1, keepdims=True) + NORM_EPS)
        st_ref[hp] = lax.dot_general(keys_ref[hp], qn, (((1,), (1,)), ((), ())), precision=HIGHEST,
                                     preferred_element_type=F32)

    idx = lax.broadcasted_iota(jnp.int32, (PEER_NKEYS, 128), 0)
    pidx = lax.broadcasted_iota(jnp.int32, (_PEER_PAIR_ROWS, 128), 0)
    pad_rows = jnp.full((_PEER_PAIR_ROWS - len(_PEER_PAIRS), 128), -jnp.inf, F32)

    def pop16(work, index, tie_safe, track):
        vals, order = [], jnp.full(work.shape, NOT_TOP, F32)
        for it in range(PEER_TOPK):
            if tie_safe:
                m, hit = _pop_max(work, index)
            else:
                m = jnp.max(work, axis=0, keepdims=True)
                hit = work == m
            work = jnp.where(hit, -jnp.inf, work)
            if track:
                order = jnp.where(hit, float(it), order)
            vals.append(m)
        removed = jnp.sum(jnp.where(work == -jnp.inf, 1.0, 0.0), axis=0, keepdims=True)
        return vals, order, removed

    def select(cs, tie_safe):
        wrong = jnp.zeros((1, 128), F32)
        n_pad = float(_PEER_PAIR_ROWS - len(_PEER_PAIRS))
        for h in range(PEER_HEADS):
            scores, tops, ranks = [], [], []
            for p in range(2):
                st = st_ref[2 * h + p, :, cs]
                vals, rank, removed = pop16(st, idx, tie_safe, track=tie_safe or p == 1)
                wrong = wrong + jnp.abs(removed - float(PEER_TOPK))
                scores.append(st)
                tops.append(vals)
                ranks.append(rank)
            cand = jnp.concatenate([tops[0][i] + tops[1][j] for (i, j) in _PEER_PAIRS] + [pad_rows], axis=0)
            best, order, removed = pop16(cand, pidx, tie_safe, track=tie_safe)
            wrong = wrong + jnp.abs(removed - (float(PEER_TOPK) + n_pad))
            picked = jnp.where((order < float(PEER_TOPK)) if tie_safe else (cand >= best[-1]), 1.0, 0.0)
            z = jnp.ones_like(best[0])
            for k in range(1, PEER_TOPK):
                z = z + jnp.exp(best[k] - best[0])
            count = jnp.dot(grp_ref[...], picked.astype(BF16), preferred_element_type=F32)
            n1 = jnp.zeros(scores[0].shape, F32)
            for i in reversed(range(PEER_TOPK)):
                at_i = (ranks[0] == float(i)) if tie_safe else (scores[0] >= tops[0][i])
                n1 = jnp.where(at_i, count[i:i + 1], n1)
            n1_ref[h, :, cs] = n1
            r2_ref[h, :, cs] = ranks[1]
            e1_ref[h, :, cs] = jnp.exp(scores[0] - tops[0][0]) / z
            e2_ref[h, :, cs] = jnp.exp(scores[1] - tops[1][0])
        return wrong

    def token_chunk(c, carry):
        cs = pl.ds(pl.multiple_of(c * 128, 128), 128)
        wrong = select(cs, tie_safe=False)

        @pl.when(jnp.max(wrong) > 0.0)
        def _():
            select(cs, tie_safe=True)

        return carry

    lax.fori_loop(0, tt // 128, token_chunk, 0)


def _peer_select(q2, subkeys, *, tt=512):
    t = q2.shape[0]
    grp = np.zeros((PEER_TOPK, _PEER_PAIR_ROWS), np.float32)
    for row, (i, _) in enumerate(_PEER_PAIRS):
        grp[i, row] = 1.0
    big = pl.BlockSpec((PEER_HEADS, PEER_NKEYS, tt), lambda i: (0, 0, i))
    bshape = jax.ShapeDtypeStruct((PEER_HEADS, PEER_NKEYS, t), F32)
    half = PEER_QDIM // 2
    return pl.pallas_call(
        _peer_sel_kernel,
        grid=(t // tt,),
        in_specs=[pl.BlockSpec((tt, PEER_HEADS * PEER_QDIM), lambda i: (i, 0)),
                  pl.BlockSpec((2 * PEER_HEADS, PEER_NKEYS, half), lambda i: (0, 0, 0)),
                  pl.BlockSpec((PEER_TOPK, _PEER_PAIR_ROWS), lambda i: (0, 0))],
        out_specs=[big, big, big, big],
        out_shape=[bshape, bshape, bshape, bshape],
        scratch_shapes=[pltpu.VMEM((2 * PEER_HEADS, PEER_NKEYS, tt), F32)],
        compiler_params=_params(("parallel",)),
        name="peer_select",
    )(q2, subkeys.reshape(2 * PEER_HEADS, PEER_NKEYS, half), jnp.asarray(grp, dtype=BF16))


PEER_SLAB = 256

def _peer_main_kernel(h_ref, u_ref, vt_ref, n1_ref, e1_ref, r2_ref, e2_ref, x_ref, gt_ref, fg_ref,
                      o_ref, acc_ref, act_ref, p_ref, ht_ref, *, final):
    j = pl.program_id(1)
    te, tt = act_ref.shape
    na = te // PEER_NKEYS

    @pl.when(j == 0)
    def _():
        acc_ref[...] = jnp.zeros(acc_ref.shape, F32)
        ht_ref[...] = h_ref[...].astype(F32).T.astype(BF16)

    a0 = pl.multiple_of(j * na, na)
    nslab = te // PEER_SLAB
    parts = []

    def act(s):
        ss = slice(s * PEER_SLAB, (s + 1) * PEER_SLAB)
        act_ref[ss, :] = jnp.dot(u_ref[ss, :], ht_ref[...], preferred_element_type=F32)

    def val(s):
        ss = slice(s * PEER_SLAB, (s + 1) * PEER_SLAB)
        parts.append(jnp.dot(vt_ref[:, ss], p_ref[ss, :], preferred_element_type=F32))

    def mask(al):
        rs = slice(al * PEER_NKEYS, (al + 1) * PEER_NKEYS)
        for c in range(tt // 128):
            cs = slice(c * 128, (c + 1) * 128)
            w = jnp.zeros((PEER_NKEYS, 128), F32)
            for hh in range(PEER_HEADS):
                n1 = n1_ref[hh, pl.ds(a0, na), cs][al:al + 1]
                e1 = e1_ref[hh, pl.ds(a0, na), cs][al:al + 1]
                w = w + jnp.where(r2_ref[hh, :, cs] < n1, e1 * e2_ref[hh, :, cs], 0.0)
            p_ref[rs, cs] = (w * _gelu(act_ref[rs, cs])).astype(BF16)

    per = PEER_SLAB // PEER_NKEYS
    act(0)
    act(1)
    for s in range(nslab):
        for k in range(per):
            mask(s * per + k)
            if k == 0 and s + 2 < nslab:
                act(s + 2)
        val(s)
    acc_ref[...] += functools.reduce(lambda x, y: x + y, parts)

    @pl.when(j == pl.num_programs(1) - 1)
    def _():
        y = x_ref[...] + gt_ref[...] * acc_ref[...].T
        if final:
            y = y * lax.rsqrt(jnp.mean(y * y, axis=-1, keepdims=True) + NORM_EPS) * fg_ref[...]
        o_ref[...] = y


def _peer_main(h2, u_bf, vt_bf, n1, e1, r2, e2, x2, mod3, final_g, *, final, tt=256, te=4096):
    t, d = x2.shape
    assert (te // PEER_NKEYS) % 8 == 0 and te % PEER_SLAB == 0
    per_b = SEQ // tt
    big = pl.BlockSpec((PEER_HEADS, PEER_NKEYS, tt), lambda i, j: (0, 0, i))
    return pl.pallas_call(
        functools.partial(_peer_main_kernel, final=final),
        grid=(t // tt, PEER_EXPERTS // te),
        in_specs=[pl.BlockSpec((tt, d), lambda i, j: (i, 0)),
                  pl.BlockSpec((te, d), lambda i, j: (j, 0)),
                  pl.BlockSpec((d, te), lambda i, j: (0, j)),
                  big, big, big, big,
                  pl.BlockSpec((tt, d), lambda i, j: (i, 0)),
                  pl.BlockSpec((None, 1, d), lambda i, j: (i // per_b, 0, 5)),
                  pl.BlockSpec((1, d), lambda i, j: (0, 0))],
        out_specs=pl.BlockSpec((tt, d), lambda i, j: (i, 0)),
        out_shape=jax.ShapeDtypeStruct((t, d), F32),
        scratch_shapes=[pltpu.VMEM((d, tt), F32), pltpu.VMEM((te, tt), F32), pltpu.VMEM((te, tt), BF16),
                        pltpu.VMEM((d, tt), BF16)],
        compiler_params=_params(("parallel", "arbitrary")),
        name="peer_experts",
    )(h2, u_bf, vt_bf, n1, e1, r2, e2, x2, mod3, final_g.reshape(1, d))


def _reorder_w_in(w):
    pad = jnp.zeros((w.shape[0], 1024 - 792), w.dtype)
    return jnp.concatenate([w[:, 5400:9496], w[:, 3072:3864], pad, w[:, 0:3072], w[:, 3864:5400]], axis=1)


def kernel(x, c, positions, mod_w, mod_b, norm_mix_g, norm_ffn_g, w_in, a_conv_w, a_out, b_conv_w, b_conv_b, b_ln_g, b_ln_b, b_out, c_cmp_pos, c_cmp_w1, c_cmp_w2, c_out, d_out, w_o, peer_wq, peer_subkeys, peer_u, peer_v, final_norm_g):
    bsz, s, d = x.shape
    assert s == SEQ and d == D_MODEL
    depth = mod_w.shape[0]
    t = bsz * s
    cos, sin = _rope_tables(positions)
    mod = _modulation(c, mod_w, mod_b)
    x2 = x.reshape(t, d)
    for l in range(depth):
        mod3 = mod[l].reshape(bsz, 1, 6 * d)
        proj2 = _norm_matmul(x2, norm_mix_g[l], mod3, 0, 1, _reorder_w_in(w_in[l].astype(BF16)), out_dtype=BF16,
                             tn=PROJ_COLS // 4)
        proj3 = proj2.reshape(bsz, s, PROJ_COLS)
        ua, ub = _conv_mixers(proj3, a_conv_w[l], b_conv_w[l], b_conv_b[l], b_ln_g[l], b_ln_b[l])
        qc, kc, vc, ks, vst, kw, vwt, gt, qd, kd, vdt, kmean = _prep(proj3, cos, sin)
        kcmp, vcmpt = _compress(kc, vc, c_cmp_pos[l], c_cmp_w1[l], c_cmp_w2[l])
        oc, od = _attention(qc, kcmp, vcmpt, ks, vst, kw, vwt, gt, qd, kd, vdt, kmean)
        x2 = _merge(ua.reshape(t, 512), ub.reshape(t, 512), oc.reshape(t, 512), od.reshape(t, 512),
                    proj2, x2, mod3, a_out[l].astype(BF16), b_out[l].astype(BF16), c_out[l].astype(BF16),
                    d_out[l].astype(BF16), w_o[l].astype(BF16))
        q2, h2 = _norm_matmul(x2, norm_ffn_g[l], mod3, 3, 4, peer_wq[l].astype(BF16), emit_h=True,
                              tn=PEER_HEADS * PEER_QDIM)
        n1, e1, r2, e2 = _peer_select(q2, peer_subkeys[l])
        x2 = _peer_main(h2, peer_u[l].astype(BF16), peer_v[l].T.astype(BF16), n1, e1, r2, e2,
                        x2, mod3, final_norm_g, final=(l == depth - 1))
    return x2.reshape(bsz, s, d)
```

```python
import functools

import numpy as np
import jax
import jax.numpy as jnp
from jax import lax
from jax.experimental import pallas as pl
from jax.experimental.pallas import tpu as pltpu

F32 = jnp.float32
BF16 = jnp.bfloat16
HIGHEST = lax.Precision.HIGHEST

D_MODEL = 1024
SEQ = 2048
HEAD_DIM = 64
ROPE_THETA = 10000.0
NORM_EPS = 1e-6
A_WIDTH = 512
A_CONV = 3
B_WIDTH = 512
B_CONV = 31
C_HEADS = 8
C_KV_HEADS = 2
C_GROUP = 4
CMP_BLOCK = 32
CMP_STRIDE = 16
CMP_HIDDEN = 128
N_CMP = (SEQ - CMP_BLOCK) // CMP_STRIDE + 1
SLC_BLOCK = 64
SLC_TOPN = 16
N_SLC = SEQ // SLC_BLOCK
WIN = 512
D_HEADS = 8
MOBA_BLOCK = 256
MOBA_TOPK = 3
N_MOBA = SEQ // MOBA_BLOCK
PEER_HEADS = 8
PEER_NKEYS = 128
PEER_EXPERTS = PEER_NKEYS * PEER_NKEYS
PEER_QDIM = 256
PEER_TOPK = 16

PROJ_COLS = 9728
COL_MERGE = 0
COL_KVG = 4096
COL_GATE = COL_KVG + 768
COL_A = 5120

VMEM_LIMIT = 56 * 1024 * 1024
NEG = -1e30

ATT_T = 256
CONV_T = 256
HALO = 32


def _params(sem, flags=None):
    return pltpu.CompilerParams(dimension_semantics=sem, vmem_limit_bytes=VMEM_LIMIT, flags=flags)


def _gelu(x):
    return 0.5 * x * (1.0 + lax.erf(x * np.float32(np.sqrt(0.5))))


def _rope_table_kernel(pos_ref, inv_ref, sign_ref, cos_ref, sin_ref):
    ang = pos_ref[...] * inv_ref[...]
    cos_ref[...] = jnp.cos(ang)
    sin_ref[...] = jnp.sin(ang) * sign_ref[...]


def _rope_tables(positions):
    bsz, s = positions.shape
    inv = 1.0 / (ROPE_THETA ** (jnp.arange(0, HEAD_DIM, 2, dtype=F32) / HEAD_DIM))
    inv128 = jnp.tile(inv, 4)[None, :]
    sign = jnp.tile(jnp.concatenate([-jnp.ones(32, F32), jnp.ones(32, F32)]), 2)[None, :]
    pos = positions.astype(F32).reshape(bsz * s, 1)
    t = bsz * s
    cos, sin = pl.pallas_call(
        _rope_table_kernel,
        grid=(t // SEQ,),
        in_specs=[pl.BlockSpec((SEQ, 1), lambda i: (i, 0)),
                  pl.BlockSpec((1, 128), lambda i: (0, 0)),
                  pl.BlockSpec((1, 128), lambda i: (0, 0))],
        out_specs=[pl.BlockSpec((SEQ, 128), lambda i: (i, 0))] * 2,
        out_shape=[jax.ShapeDtypeStruct((t, 128), F32)] * 2,
        compiler_params=_params(("parallel",)),
        name="rope_tables",
    )(pos, inv128, sign)
    return cos.reshape(bsz, s, 128), sin.reshape(bsz, s, 128)


def _rope(x, cos, sin):
    w = x.shape[-1]
    lane = lax.broadcasted_iota(jnp.int32, x.shape, 1)
    swapped = jnp.where(lane % 64 < 32, pltpu.roll(x, w - 32, 1), pltpu.roll(x, 32, 1))
    return x * cos + swapped * sin


def _mod_kernel(c_ref, w_ref, b_ref, o_ref):
    c = c_ref[...]
    cond = c * jax.nn.sigmoid(c)
    o_ref[...] = jnp.dot(cond, w_ref[...], precision=HIGHEST, preferred_element_type=F32) + b_ref[...]


def _modulation(c, mod_w, mod_b):
    nl, d, n = mod_w.shape
    bsz = c.shape[0]
    tn = 1536
    return pl.pallas_call(
        _mod_kernel,
        grid=(nl, n // tn),
        in_specs=[pl.BlockSpec((bsz, d), lambda l, j: (0, 0)),
                  pl.BlockSpec((None, d, tn), lambda l, j: (l, 0, j)),
                  pl.BlockSpec((None, 1, tn), lambda l, j: (l, 0, j))],
        out_specs=pl.BlockSpec((None, bsz, tn), lambda l, j: (l, 0, j)),
        out_shape=jax.ShapeDtypeStruct((nl, bsz, n), F32),
        compiler_params=_params(("parallel", "parallel")),
        name="adaln_mod",
    )(c, mod_w, mod_b.reshape(nl, 1, n))


def _norm_matmul_kernel(x_ref, g_ref, sc_ref, sh_ref, w_ref, o_ref, *rest, emit_h):
    h_scr = rest[-1]

    @pl.when(pl.program_id(1) == 0)
    def _():
        x = x_ref[...]
        y = x * lax.rsqrt(jnp.mean(x * x, axis=-1, keepdims=True) + NORM_EPS)
        h = (y * g_ref[...]) * (1.0 + sc_ref[...]) + sh_ref[...]
        h_scr[...] = h.astype(BF16)
        if emit_h:
            rest[0][...] = h.astype(BF16)

    o_ref[...] = jnp.dot(h_scr[...], w_ref[...], preferred_element_type=F32).astype(o_ref.dtype)


def _norm_matmul(x2, g, mod3, sh_blk, sc_blk, w, *, out_dtype=F32, emit_h=False, tm=1024, tn=512):
    t, d = x2.shape
    n = w.shape[1]
    per_b = SEQ // tm
    out_shape = [jax.ShapeDtypeStruct((t, n), out_dtype)]
    out_specs = [pl.BlockSpec((tm, tn), lambda i, j: (i, j))]
    if emit_h:
        out_shape.append(jax.ShapeDtypeStruct((t, d), BF16))
        out_specs.append(pl.BlockSpec((tm, d), lambda i, j: (i, 0)))
    res = pl.pallas_call(
        functools.partial(_norm_matmul_kernel, emit_h=emit_h),
        grid=(t // tm, n // tn),
        in_specs=[pl.BlockSpec((tm, d), lambda i, j: (i, 0)),
                  pl.BlockSpec((1, d), lambda i, j: (0, 0)),
                  pl.BlockSpec((None, 1, d), lambda i, j: (i // per_b, 0, sc_blk)),
                  pl.BlockSpec((None, 1, d), lambda i, j: (i // per_b, 0, sh_blk)),
                  pl.BlockSpec((d, tn), lambda i, j: (0, j))],
        out_specs=out_specs,
        out_shape=out_shape,
        scratch_shapes=[pltpu.VMEM((tm, d), BF16)],
        compiler_params=_params(("parallel", "arbitrary")),
        name="norm_matmul",
    )(x2, g.reshape(1, d), mod3, mod3, w)
    return res if emit_h else res[0]


def _conv_kernel(ab_ref, ac_ref, ax_ref, ba_ref, bg_ref, pac_ref, pax_ref, pba_ref, pbg_ref,
                 aw_ref, bw_ref, bb_ref, lng_ref, lnb_ref, ua_ref, ub_ref, ext_ref, y_ref):
    ts = ab_ref.shape[0]
    keep = (pl.program_id(1) > 0).astype(F32)

    def f32(ref):
        return ref[...].astype(F32)

    def causal_conv(w_ref, taps, bias_ref, out_ref):
        blk = 128
        reach = ((taps - 1) // 8) * 8
        for t0 in range(0, ts, blk):
            for c0 in range(0, w_ref.shape[1], 128):
                cs = slice(c0, c0 + 128)
                acc = jnp.zeros((blk, 128), F32) if bias_ref is None else jnp.zeros((blk, 128), F32) + bias_ref[:, cs]
                for r in range(min(8, taps)):
                    base = HALO + t0 - reach - r
                    ur = ext_ref[base:base + blk + reach, cs]
                    for q in range(reach // 8 + 1):
                        shift = 8 * q + r
                        if shift < taps:
                            k = taps - 1 - shift
                            acc = acc + w_ref[k:k + 1, cs] * ur[reach - 8 * q:reach - 8 * q + blk]
                out_ref[t0:t0 + blk, cs] = acc

    ext_ref[0:HALO, :] = f32(pac_ref) * f32(pax_ref) * keep
    ext_ref[HALO:HALO + ts, :] = f32(ac_ref) * f32(ax_ref)
    causal_conv(aw_ref, A_CONV, None, y_ref)
    ua_ref[...] = (f32(ab_ref) * y_ref[...]).astype(BF16)

    ext_ref[0:HALO, :] = f32(pba_ref) * jax.nn.sigmoid(f32(pbg_ref)) * keep
    ext_ref[HALO:HALO + ts, :] = f32(ba_ref) * jax.nn.sigmoid(f32(bg_ref))
    causal_conv(bw_ref, B_CONV, bb_ref, y_ref)
    acc = y_ref[...]
    mu = jnp.mean(acc, axis=-1, keepdims=True)
    cen = acc - mu
    var = jnp.mean(cen * cen, axis=-1, keepdims=True)
    y = cen * lax.rsqrt(var + NORM_EPS) * lng_ref[...] + lnb_ref[...]
    ub_ref[...] = (y * jax.nn.sigmoid(y)).astype(BF16)


def _conv_mixers(proj3, a_conv_w, b_conv_w, b_conv_b, b_ln_g, b_ln_b):
    bsz = proj3.shape[0]
    ts = CONV_T
    c0 = COL_A // 512
    r = ts // HALO

    def cur(k):
        return pl.BlockSpec((None, ts, 512), lambda b, i, k=k: (b, i, c0 + k))

    def prev(k):
        return pl.BlockSpec((None, HALO, 512), lambda b, i, k=k: (b, jnp.maximum(i * r - 1, 0), c0 + k))

    def full(shape):
        return pl.BlockSpec(shape, lambda b, i: (0,) * len(shape))

    return pl.pallas_call(
        _conv_kernel,
        grid=(bsz, SEQ // ts),
        in_specs=[cur(0), cur(1), cur(2), cur(3), cur(4), prev(1), prev(2), prev(3), prev(4),
                  full((A_CONV, A_WIDTH)), full((B_CONV, B_WIDTH)), full((1, B_WIDTH)),
                  full((1, B_WIDTH)), full((1, B_WIDTH))],
        out_specs=[pl.BlockSpec((None, ts, 512), lambda b, i: (b, i, 0))] * 2,
        out_shape=[jax.ShapeDtypeStruct((bsz, SEQ, 512), BF16)] * 2,
        scratch_shapes=[pltpu.VMEM((HALO + ts, 512), F32), pltpu.VMEM((ts, 512), F32)],
        compiler_params=_params(("parallel", "arbitrary")),
        name="conv_mixers",
    )(proj3, proj3, proj3, proj3, proj3, proj3, proj3, proj3, proj3,
      a_conv_w, b_conv_w, b_conv_b.reshape(1, -1), b_ln_g.reshape(1, -1), b_ln_b.reshape(1, -1))


VT_ROWS = 80


def _values_t(v, n_heads):
    rows = v.shape[0]
    vt = v.T
    tail = (lax.broadcasted_iota(jnp.int32, (VT_ROWS - HEAD_DIM, rows), 0) == 0).astype(F32)
    blocks = []
    for h in range(n_heads):
        blocks += [vt[h * HEAD_DIM:(h + 1) * HEAD_DIM], tail]
    return jnp.concatenate(blocks, axis=0).astype(BF16)


def _prep_kernel(cq_ref, dq_ref, dk_ref, dv_ref, kvg_ref, cos_ref, sin_ref,
                 qc_ref, kc_ref, vc_ref, ks_ref, vst_ref, kw_ref, vwt_ref, gt_ref,
                 qd_ref, kd_ref, vdt_ref, kmean_ref):
    cos = cos_ref[...]
    sin = sin_ref[...]
    cos4 = jnp.concatenate([cos] * 4, axis=1)
    sin4 = jnp.concatenate([sin] * 4, axis=1)
    scale = np.float32(HEAD_DIM ** -0.5 * np.log2(np.e))
    qc_ref[...] = _rope(cq_ref[...].astype(F32), cos4, sin4) * scale
    qd_ref[...] = _rope(dq_ref[...].astype(F32), cos4, sin4) * scale
    kd = _rope(dk_ref[...].astype(F32), cos4, sin4)
    for h in range(D_HEADS):
        kd_ref[h] = kd[:, h * 64:(h + 1) * 64].astype(BF16)
    kmean_ref[...] = jnp.mean(kd, axis=0, keepdims=True)
    vdt_ref[...] = _values_t(dv_ref[...].astype(F32), D_HEADS)

    def kvg(k):
        return kvg_ref[:, k * 128:(k + 1) * 128].astype(F32)

    kc = _rope(kvg(0), cos, sin)
    vc = kvg(1)
    ks = _rope(kvg(2), cos, sin)
    kw = _rope(kvg(4), cos, sin)
    for g in range(C_KV_HEADS):
        gs = slice(g * 64, (g + 1) * 64)
        kc_ref[g] = kc[:, gs]
        vc_ref[g] = vc[:, gs]
        ks_ref[g] = ks[:, gs].astype(BF16)
        kw_ref[g] = kw[:, gs].astype(BF16)
    vst_ref[...] = _values_t(kvg(3), C_KV_HEADS)
    vwt_ref[...] = _values_t(kvg(5), C_KV_HEADS)
    gt_ref[...] = jax.nn.sigmoid(kvg(6)).T


def _prep(proj3, cos, sin):
    bsz = proj3.shape[0]
    ts = MOBA_BLOCK
    c0 = COL_A // 512

    def col512(k):
        return pl.BlockSpec((None, ts, 512), lambda b, i: (b, i, c0 + k))

    row128 = pl.BlockSpec((None, ts, 128), lambda b, i: (b, i, 0))
    row512 = pl.BlockSpec((None, ts, 512), lambda b, i: (b, i, 0))
    col128t = pl.BlockSpec((None, 128, ts), lambda b, i: (b, 0, i))
    val2t = pl.BlockSpec((None, C_KV_HEADS * VT_ROWS, ts), lambda b, i: (b, 0, i))
    head64 = pl.BlockSpec((None, C_KV_HEADS, ts, 64), lambda b, i: (b, 0, i, 0))
    k64 = jax.ShapeDtypeStruct((bsz, C_KV_HEADS, SEQ, 64), BF16)
    t128 = jax.ShapeDtypeStruct((bsz, C_KV_HEADS * VT_ROWS, SEQ), BF16)
    return pl.pallas_call(
        _prep_kernel,
        grid=(bsz, SEQ // ts),
        in_specs=[col512(5), col512(6), col512(7), col512(8),
                  pl.BlockSpec((None, ts, 1024), lambda b, i: (b, i, COL_KVG // 1024)),
                  row128, row128],
        out_specs=[row512, head64, head64, head64, val2t, head64, val2t, col128t,
                   row512,
                   pl.BlockSpec((None, D_HEADS, ts, 64), lambda b, i: (b, 0, i, 0)),
                   pl.BlockSpec((None, D_HEADS * VT_ROWS, ts), lambda b, i: (b, 0, i)),
                   pl.BlockSpec((None, None, 1, 512), lambda b, i: (b, i, 0, 0))],
        out_shape=[jax.ShapeDtypeStruct((bsz, SEQ, 512), F32),
                   jax.ShapeDtypeStruct((bsz, C_KV_HEADS, SEQ, 64), F32),
                   jax.ShapeDtypeStruct((bsz, C_KV_HEADS, SEQ, 64), F32),
                   k64, t128, k64, t128,
                   jax.ShapeDtypeStruct((bsz, 128, SEQ), F32),
                   jax.ShapeDtypeStruct((bsz, SEQ, 512), F32),
                   jax.ShapeDtypeStruct((bsz, D_HEADS, SEQ, 64), BF16),
                   jax.ShapeDtypeStruct((bsz, D_HEADS * VT_ROWS, SEQ), BF16),
                   jax.ShapeDtypeStruct((bsz, N_MOBA, 1, 512), F32)],
        compiler_params=_params(("parallel", "parallel")),
        name="attn_prep",
    )(proj3, proj3, proj3, proj3, proj3, cos, sin)


def _compress_kernel(kc_ref, vc_ref, pos_ref, w1_ref, w2_ref, kcmp_ref, vcmp_ref):
    half = CMP_STRIDE * HEAD_DIM
    row = lax.broadcasted_iota(jnp.int32, (128, CMP_HIDDEN), 0)
    for which, (src, dst) in enumerate(((kc_ref, kcmp_ref), (vc_ref, vcmp_ref))):
        bias = jnp.dot(pos_ref[which], w1_ref[which], precision=HIGHEST, preferred_element_type=F32)[0:1]
        for g in range(C_KV_HEADS):
            chunks = src[g]
            d1 = jnp.dot(chunks, w1_ref[which, 0:half, :], precision=HIGHEST, preferred_element_type=F32)
            d2 = jnp.dot(chunks, w1_ref[which, half:2 * half, :], precision=HIGHEST, preferred_element_type=F32)
            d2 = jnp.where(row < 127, pltpu.roll(d2, 127, 0), 0.0)
            hid = _gelu(d1 + d2 + bias)
            out = jnp.dot(hid, w2_ref[which], precision=HIGHEST, preferred_element_type=F32)
            dst[g] = out if which == 0 else out.T


def _compress(kc, vc, cmp_pos, cmp_w1, cmp_w2):
    bsz = kc.shape[0]
    kc4 = kc.reshape(bsz, C_KV_HEADS, SEQ // CMP_STRIDE, CMP_STRIDE * HEAD_DIM)
    vc4 = vc.reshape(bsz, C_KV_HEADS, SEQ // CMP_STRIDE, CMP_STRIDE * HEAD_DIM)
    pos8 = jnp.broadcast_to(cmp_pos.reshape(2, 1, CMP_BLOCK * HEAD_DIM), (2, 8, CMP_BLOCK * HEAD_DIM))
    blk = pl.BlockSpec((None, C_KV_HEADS, 128, 1024), lambda b: (b, 0, 0, 0))
    out = pl.BlockSpec((None, C_KV_HEADS, 128, 64), lambda b: (b, 0, 0, 0))
    return pl.pallas_call(
        _compress_kernel,
        grid=(bsz,),
        in_specs=[blk, blk,
                  pl.BlockSpec((2, 8, 2048), lambda b: (0, 0, 0)),
                  pl.BlockSpec((2, 2048, CMP_HIDDEN), lambda b: (0, 0, 0)),
                  pl.BlockSpec((2, CMP_HIDDEN, 64), lambda b: (0, 0, 0))],
        out_specs=[out, pl.BlockSpec((None, C_KV_HEADS, 64, 128), lambda b: (b, 0, 0, 0))],
        out_shape=[jax.ShapeDtypeStruct((bsz, C_KV_HEADS, 128, 64), F32),
                   jax.ShapeDtypeStruct((bsz, C_KV_HEADS, 64, 128), F32)],
        compiler_params=_params(("parallel",)),
        name="nsa_compress",
    )(kc4, vc4, pos8, cmp_w1, cmp_w2)


def _flash_steps_t(heads, m_ref=None, acc_ref=None, groups=1):
    heads = [h if len(h) == 8 else (*h, m_ref, acc_ref, n, groups) for n, h in enumerate(heads)]
    scores = [lax.dot_general(h[1], h[0], (((1,), (1,)), ((), ())), preferred_element_type=F32)
              for h in heads]
    probs = []
    for (qb, _, _, mask, mr, _, n, grp), s in zip(heads, scores):
        tq = qb.shape[0] // grp
        if mask is not None:
            s = jnp.concatenate([jnp.where(mask, s[:, r * tq:(r + 1) * tq], NEG) for r in range(grp)], axis=1)
        sr = _stat_row(n)
        m_old = mr[sr, :]
        m_new = jnp.maximum(m_old, jnp.max(s, axis=0, keepdims=True))
        mr[sr, :] = m_new
        probs.append((jnp.exp2(m_old - m_new), jnp.exp2(s - m_new).astype(BF16)))
    for (_, _, vt, _, _, ar, n, _), (alpha, p) in zip(heads, probs):
        a = slice(n * VT_ROWS, (n + 1) * VT_ROWS)
        ar[a, :] = alpha * ar[a, :] + jnp.dot(vt, p, preferred_element_type=F32)


def _flash_out(acc_ref, n):
    return acc_ref[n * VT_ROWS:n * VT_ROWS + HEAD_DIM, :] / acc_ref[n * VT_ROWS + HEAD_DIM:n * VT_ROWS + HEAD_DIM + 1, :]


def _stat_row(row):
    return slice(8 * row, 8 * row + 1)


def _flash_init(m_ref, acc_ref):
    m_ref[...] = jnp.full(m_ref.shape, NEG, F32)
    acc_ref[...] = jnp.zeros(acc_ref.shape, F32)


def _rank_desc_rows(vals):
    n = vals.shape[0]
    row = lax.broadcasted_iota(jnp.int32, vals.shape, 0)
    rank = jnp.zeros(vals.shape, jnp.int32)
    for i in range(n):
        vi = vals[i:i + 1, :]
        ahead = (vi > vals) | ((vi == vals) & (row > i))
        rank = rank + ahead.astype(jnp.int32)
    return rank


def _nsa_kernel(q_ref, kcmp_ref, vcmpt_ref, ks_ref, vst_ref, kw_ref, vwt_ref, gt_ref, ovlt_ref, expt_ref,
                o_ref, qb_ref, sel_ref, m_ref, acc_ref, out_ref):
    tq = ATT_T
    i = pl.program_id(1)
    t0 = i * tq
    dstart = pl.multiple_of(t0, tq)
    t_row = t0 + lax.broadcasted_iota(jnp.int32, (1, tq), 1)
    t_rows = jnp.concatenate([t_row] * C_GROUP, axis=1)
    blk = lax.broadcasted_iota(jnp.int32, (128, 1), 0)
    kk = lax.broadcasted_iota(jnp.int32, (tq, tq), 0)
    qq = lax.broadcasted_iota(jnp.int32, (tq, tq), 1)
    causal = kk <= qq
    win_tail = kk > qq

    def gate_row(branch, g):
        base = branch * 8 + g * C_GROUP
        return jnp.concatenate([gt_ref[base + r:base + r + 1, :] for r in range(C_GROUP)], axis=1)

    def hrows(g):
        return slice(g * HEAD_DIM, (g + 1) * HEAD_DIM)

    def vrows(g):
        return slice(g * VT_ROWS, (g + 1) * VT_ROWS)

    for g in range(C_KV_HEADS):
        qf = jnp.concatenate([q_ref[:, (g * 4 + r) * 64:(g * 4 + r + 1) * 64] for r in range(C_GROUP)], axis=0)
        qb_ref[g] = qf.astype(BF16)

        s = lax.dot_general(kcmp_ref[g], qf, (((1,), (1,)), ((), ())), precision=HIGHEST,
                            preferred_element_type=F32)
        vis = (blk * CMP_STRIDE + (CMP_BLOCK - 1)) <= t_rows
        sm = jnp.where(vis, s, NEG)
        e = jnp.where(vis, jnp.exp2(sm - jnp.max(sm, axis=0, keepdims=True)), 0.0)
        p = e / jnp.maximum(jnp.sum(e, axis=0, keepdims=True), 1e-30)
        o_cmp = jnp.dot(vcmpt_ref[g].astype(BF16), p.astype(BF16), preferred_element_type=F32)
        out_ref[hrows(g), :] = gate_row(0, g) * o_cmp

        psum = p[:, 0:tq] + p[:, tq:2 * tq] + p[:, 2 * tq:3 * tq] + p[:, 3 * tq:4 * tq]
        imp = jnp.dot(ovlt_ref[...], psum, precision=HIGHEST, preferred_element_type=F32)[0:N_SLC]
        b32 = blk[0:N_SLC]
        cur = t_row // SLC_BLOCK
        forced = (b32 == 0) | (b32 == cur) | (b32 == cur - 1)
        imp = jnp.where(forced, jnp.inf, jnp.where(b32 * SLC_BLOCK > t_row, -jnp.inf, imp))
        sel = (_rank_desc_rows(imp) < SLC_TOPN).astype(F32)
        sel_ref[g] = jnp.concatenate([sel, jnp.zeros((128 - N_SLC, tq), F32)], axis=0).astype(BF16)

    def slc_tile(jb, extra):
        start = pl.multiple_of(jb * tq, tq)
        heads = []
        for g in range(C_KV_HEADS):
            hit = jnp.dot(expt_ref[pl.ds(start, tq), :], sel_ref[g], preferred_element_type=F32) > 0.5
            mask = hit if extra is None else hit & extra
            heads.append((qb_ref[g], ks_ref[g, pl.ds(start, tq), :], vst_ref[vrows(g), pl.ds(start, tq)], mask))
        _flash_steps_t(heads, m_ref, acc_ref, C_GROUP)

    def win_tile(start, mask):
        heads = [(qb_ref[g], kw_ref[g, pl.ds(start, tq), :], vwt_ref[vrows(g), pl.ds(start, tq)], mask)
                 for g in range(C_KV_HEADS)]
        _flash_steps_t(heads, m_ref, acc_ref, C_GROUP)

    def add_branch(branch):
        for g in range(C_KV_HEADS):
            out_ref[hrows(g), :] += gate_row(branch, g) * _flash_out(acc_ref, g)

    _flash_init(m_ref, acc_ref)
    slc_tile(i, causal)

    def slc_body(jb, carry):
        slc_tile(jb, None)
        return carry

    lax.fori_loop(0, i, slc_body, 0)
    add_branch(1)

    _flash_init(m_ref, acc_ref)
    win_tile(dstart, causal)

    @pl.when(i >= 1)
    def _():
        win_tile(pl.multiple_of(t0 - tq, tq), None)

    @pl.when(i >= 2)
    def _():
        win_tile(pl.multiple_of(t0 - 2 * tq, tq), win_tail)

    add_branch(2)

    o_t = jnp.concatenate([out_ref[hrows(g), r * tq:(r + 1) * tq]
                           for g in range(C_KV_HEADS) for r in range(C_GROUP)], axis=0)
    o_ref[...] = o_t.T.astype(BF16)


def _nsa_constants():
    j = np.arange(128)[:, None]
    n = np.arange(128)[None, :]
    ovl_t = ((n * CMP_STRIDE < j * SLC_BLOCK + SLC_BLOCK) & (n * CMP_STRIDE + CMP_BLOCK > j * SLC_BLOCK)
             & (n < N_CMP) & (j < N_SLC)).astype(np.float32)
    expand_t = (np.arange(SEQ)[:, None] // SLC_BLOCK == np.arange(128)[None, :]).astype(np.float32)
    return jnp.asarray(ovl_t), jnp.asarray(expand_t, dtype=BF16)


def _nsa(qc, kcmp, vcmpt, ks, vst, kw, vwt, gt):
    bsz = qc.shape[0]
    tq = ATT_T
    ovl_t, expand_t = _nsa_constants()
    keys = pl.BlockSpec((None, C_KV_HEADS, SEQ, 64), lambda b, i: (b, 0, 0, 0))
    vals = pl.BlockSpec((None, C_KV_HEADS * VT_ROWS, SEQ), lambda b, i: (b, 0, 0))
    rows = C_GROUP * tq
    return pl.pallas_call(
        _nsa_kernel,
        grid=(bsz, SEQ // tq),
        in_specs=[pl.BlockSpec((None, tq, 512), lambda b, i: (b, i, 0)),
                  pl.BlockSpec((None, C_KV_HEADS, 128, 64), lambda b, i: (b, 0, 0, 0)),
                  pl.BlockSpec((None, C_KV_HEADS, 64, 128), lambda b, i: (b, 0, 0, 0)),
                  keys, vals, keys, vals,
                  pl.BlockSpec((None, 128, tq), lambda b, i: (b, 0, i)),
                  pl.BlockSpec((128, 128), lambda b, i: (0, 0)),
                  pl.BlockSpec((SEQ, 128), lambda b, i: (0, 0))],
        out_specs=pl.BlockSpec((None, tq, 512), lambda b, i: (b, i, 0)),
        out_shape=jax.ShapeDtypeStruct((bsz, SEQ, 512), BF16),
        scratch_shapes=[pltpu.VMEM((C_KV_HEADS, rows, 64), BF16), pltpu.VMEM((C_KV_HEADS, 128, tq), BF16),
                        pltpu.VMEM((8 * C_KV_HEADS, rows), F32),
                        pltpu.VMEM((C_KV_HEADS * VT_ROWS, rows), F32), pltpu.VMEM((C_KV_HEADS * 64, rows), F32)],
        compiler_params=_params(("parallel", "arbitrary")),
        name="nsa_attention",
    )(qc, kcmp, vcmpt, ks, vst, kw, vwt, gt, ovl_t, expand_t)


def _moba_kernel(q_ref, k_ref, vt_ref, kmean_ref, o_ref, qb_ref, sel_ref, m_ref, acc_ref):
    tq = ATT_T
    i = pl.program_id(1)
    blk = lax.broadcasted_iota(jnp.int32, (N_MOBA, 1), 0)
    kk = lax.broadcasted_iota(jnp.int32, (tq, tq), 0)
    qq = lax.broadcasted_iota(jnp.int32, (tq, tq), 1)
    causal = kk <= qq
    past = blk < i
    for h in range(D_HEADS):
        hs = slice(h * 64, (h + 1) * 64)
        qf = q_ref[:, hs]
        qb_ref[h] = qf.astype(BF16)
        gate = lax.dot_general(kmean_ref[:, hs], qf, (((1,), (1,)), ((), ())), precision=HIGHEST,
                               preferred_element_type=F32)
        gate = jnp.where(past, gate, -jnp.inf)
        sel_ref[h] = (past & (_rank_desc_rows(gate) < MOBA_TOPK)).astype(F32)

    def tile(jb, diag):
        start = pl.multiple_of(jb * tq, tq)
        heads = []
        for h in range(D_HEADS):
            if diag:
                mask = causal
            else:
                mask = jnp.sum(jnp.where(blk == jb, sel_ref[h], 0.0), axis=0, keepdims=True) > 0.5
            heads.append((qb_ref[h], k_ref[h, pl.ds(start, tq), :],
                          vt_ref[h * VT_ROWS:(h + 1) * VT_ROWS, pl.ds(start, tq)], mask))
        _flash_steps_t(heads, m_ref, acc_ref, 1)

    _flash_init(m_ref, acc_ref)
    tile(i, True)

    def body(jb, carry):
        tile(jb, False)
        return carry

    lax.fori_loop(0, i, body, 0)
    o_t = jnp.concatenate([_flash_out(acc_ref, h) for h in range(D_HEADS)], axis=0)
    o_ref[...] = o_t.T.astype(BF16)


def _moba(qd, kd, vdt, kmean):
    bsz = qd.shape[0]
    tq = ATT_T
    return pl.pallas_call(
        _moba_kernel,
        grid=(bsz, SEQ // tq),
        in_specs=[pl.BlockSpec((None, tq, 512), lambda b, i: (b, i, 0)),
                  pl.BlockSpec((None, D_HEADS, SEQ, 64), lambda b, i: (b, 0, 0, 0)),
                  pl.BlockSpec((None, D_HEADS * VT_ROWS, SEQ), lambda b, i: (b, 0, 0)),
                  pl.BlockSpec((None, N_MOBA, 512), lambda b, i: (b, 0, 0))],
        out_specs=pl.BlockSpec((None, tq, 512), lambda b, i: (b, i, 0)),
        out_shape=jax.ShapeDtypeStruct((bsz, SEQ, 512), BF16),
        scratch_shapes=[pltpu.VMEM((D_HEADS, tq, 64), BF16), pltpu.VMEM((D_HEADS, N_MOBA, tq), F32),
                        pltpu.VMEM((8 * D_HEADS, tq), F32), pltpu.VMEM((D_HEADS * VT_ROWS, tq), F32)],
        compiler_params=_params(("parallel", "arbitrary")),
        name="moba_attention",
    )(qd, kd, vdt, kmean.reshape(bsz, N_MOBA, 512))


def _attn_kernel(q_ref, kcmp_ref, vcmpt_ref, ks_ref, vst_ref, kw_ref, vwt_ref, gt_ref, ovlt_ref, expt_ref,
                 qd_ref, kd_ref, vdt_ref, kmean_ref, oc_ref, od_ref,
                 qb_ref, sel_ref, m_ref, acc_ref, out_ref, qbd_ref, seld_ref, md_ref, accd_ref):
    tq = ATT_T
    i = pl.program_id(1)
    t0 = i * tq
    t_row = t0 + lax.broadcasted_iota(jnp.int32, (1, tq), 1)
    t_rows = jnp.concatenate([t_row] * C_GROUP, axis=1)
    blk = lax.broadcasted_iota(jnp.int32, (128, 1), 0)
    kk = lax.broadcasted_iota(jnp.int32, (tq, tq), 0)
    qq = lax.broadcasted_iota(jnp.int32, (tq, tq), 1)
    causal = kk <= qq

    def gate_row(branch, g):
        base = branch * 8 + g * C_GROUP
        return jnp.concatenate([gt_ref[base + r:base + r + 1, :] for r in range(C_GROUP)], axis=1)

    def hrows(g):
        return slice(g * HEAD_DIM, (g + 1) * HEAD_DIM)

    def vrows(g):
        return slice(g * VT_ROWS, (g + 1) * VT_ROWS)

    for g in range(C_KV_HEADS):
        qf = jnp.concatenate([q_ref[:, (g * 4 + r) * 64:(g * 4 + r + 1) * 64] for r in range(C_GROUP)], axis=0)
        qb_ref[g] = qf.astype(BF16)
        s = lax.dot_general(kcmp_ref[g], qf, (((1,), (1,)), ((), ())), precision=HIGHEST,
                            preferred_element_type=F32)
        vis = (blk * CMP_STRIDE + (CMP_BLOCK - 1)) <= t_rows
        sm = jnp.where(vis, s, NEG)
        e = jnp.where(vis, jnp.exp2(sm - jnp.max(sm, axis=0, keepdims=True)), 0.0)
        p = e / jnp.maximum(jnp.sum(e, axis=0, keepdims=True), 1e-30)
        o_cmp = jnp.dot(vcmpt_ref[g].astype(BF16), p.astype(BF16), preferred_element_type=F32)
        out_ref[hrows(g), :] = gate_row(0, g) * o_cmp
        psum = p[:, 0:tq] + p[:, tq:2 * tq] + p[:, 2 * tq:3 * tq] + p[:, 3 * tq:4 * tq]
        imp = jnp.dot(ovlt_ref[...], psum, precision=HIGHEST, preferred_element_type=F32)[0:N_SLC]
        b32 = blk[0:N_SLC]
        cur = t_row // SLC_BLOCK
        forced = (b32 == 0) | (b32 == cur) | (b32 == cur - 1)
        imp = jnp.where(forced, jnp.inf, jnp.where(b32 * SLC_BLOCK > t_row, -jnp.inf, imp))
        sel = (_rank_desc_rows(imp) < SLC_TOPN).astype(F32)
        sel_ref[g] = jnp.concatenate([sel, jnp.zeros((128 - N_SLC, tq), F32)], axis=0).astype(BF16)

    blk8 = lax.broadcasted_iota(jnp.int32, (N_MOBA, 1), 0)
    past = blk8 < i
    for h in range(D_HEADS):
        hs = slice(h * 64, (h + 1) * 64)
        qf = qd_ref[:, hs]
        qbd_ref[h] = qf.astype(BF16)
        gate = lax.dot_general(kmean_ref[:, hs], qf, (((1,), (1,)), ((), ())), precision=HIGHEST,
                               preferred_element_type=F32)
        gate = jnp.where(past, gate, -jnp.inf)
        seld_ref[h] = (past & (_rank_desc_rows(gate) < MOBA_TOPK)).astype(F32)

    def slc_heads(jb, extra):
        start = pl.multiple_of(jb * tq, tq)
        heads = []
        for g in range(C_KV_HEADS):
            hit = jnp.dot(expt_ref[pl.ds(start, tq), :], sel_ref[g], preferred_element_type=F32) > 0.5
            mask = hit if extra is None else hit & extra
            heads.append((qb_ref[g], ks_ref[g, pl.ds(start, tq), :], vst_ref[vrows(g), pl.ds(start, tq)], mask,
                          m_ref, acc_ref, g, C_GROUP))
        return heads

    def win_heads():
        wk = WIN + tq
        wstart = pl.multiple_of(jnp.maximum(t0 - WIN, 0), tq)
        kpos = wstart + lax.broadcasted_iota(jnp.int32, (wk, 1), 0)
        mask = (kpos <= t_row) & (kpos > t_row - WIN)
        return [(qb_ref[g], kw_ref[g, pl.ds(wstart, wk), :], vwt_ref[vrows(g), pl.ds(wstart, wk)], mask,
                 m_ref, acc_ref, C_KV_HEADS + g, C_GROUP) for g in range(C_KV_HEADS)]

    def moba_heads(jb, diag):
        start = pl.multiple_of(jb * tq, tq)
        heads = []
        for h in range(D_HEADS):
            if diag:
                mask = causal
            else:
                mask = jnp.sum(jnp.where(blk8 == jb, seld_ref[h], 0.0), axis=0, keepdims=True) > 0.5
            heads.append((qbd_ref[h], kd_ref[h, pl.ds(start, tq), :],
                          vdt_ref[h * VT_ROWS:(h + 1) * VT_ROWS, pl.ds(start, tq)], mask, md_ref, accd_ref, h, 1))
        return heads

    _flash_init(m_ref, acc_ref)
    _flash_init(md_ref, accd_ref)
    _flash_steps_t(slc_heads(i, causal) + win_heads() + moba_heads(i, True))

    def body(jb, carry):
        _flash_steps_t(slc_heads(jb, None) + moba_heads(jb, False))
        return carry

    lax.fori_loop(0, i, body, 0)
    for g in range(C_KV_HEADS):
        out_ref[hrows(g), :] += (gate_row(1, g) * _flash_out(acc_ref, g)
                                 + gate_row(2, g) * _flash_out(acc_ref, C_KV_HEADS + g))
    o_t = jnp.concatenate([out_ref[hrows(g), r * tq:(r + 1) * tq]
                           for g in range(C_KV_HEADS) for r in range(C_GROUP)], axis=0)
    oc_ref[...] = o_t.T.astype(BF16)
    od_t = jnp.concatenate([_flash_out(accd_ref, h) for h in range(D_HEADS)], axis=0)
    od_ref[...] = od_t.T.astype(BF16)


def _attention(qc, kcmp, vcmpt, ks, vst, kw, vwt, gt, qd, kd, vdt, kmean):
    bsz = qc.shape[0]
    tq = ATT_T
    ovl_t, expand_t = _nsa_constants()
    keys = pl.BlockSpec((None, C_KV_HEADS, SEQ, 64), lambda b, i: (b, 0, 0, 0))
    vals = pl.BlockSpec((None, C_KV_HEADS * VT_ROWS, SEQ), lambda b, i: (b, 0, 0))
    qtile = pl.BlockSpec((None, tq, 512), lambda b, i: (b, i, 0))
    rows = C_GROUP * tq
    oshape = jax.ShapeDtypeStruct((bsz, SEQ, 512), BF16)
    return pl.pallas_call(
        _attn_kernel,
        grid=(bsz, SEQ // tq),
        in_specs=[qtile,
                  pl.BlockSpec((None, C_KV_HEADS, 128, 64), lambda b, i: (b, 0, 0, 0)),
                  pl.BlockSpec((None, C_KV_HEADS, 64, 128), lambda b, i: (b, 0, 0, 0)),
                  keys, vals, keys, vals,
                  pl.BlockSpec((None, 128, tq), lambda b, i: (b, 0, i)),
                  pl.BlockSpec((128, 128), lambda b, i: (0, 0)),
                  pl.BlockSpec((SEQ, 128), lambda b, i: (0, 0)),
                  qtile,
                  pl.BlockSpec((None, D_HEADS, SEQ, 64), lambda b, i: (b, 0, 0, 0)),
                  pl.BlockSpec((None, D_HEADS * VT_ROWS, SEQ), lambda b, i: (b, 0, 0)),
                  pl.BlockSpec((None, N_MOBA, 512), lambda b, i: (b, 0, 0))],
        out_specs=[qtile, qtile],
        out_shape=[oshape, oshape],
        scratch_shapes=[pltpu.VMEM((C_KV_HEADS, rows, 64), BF16), pltpu.VMEM((C_KV_HEADS, 128, tq), BF16),
                        pltpu.VMEM((8 * 2 * C_KV_HEADS, rows), F32),
                        pltpu.VMEM((2 * C_KV_HEADS * VT_ROWS, rows), F32), pltpu.VMEM((C_KV_HEADS * 64, rows), F32),
                        pltpu.VMEM((D_HEADS, tq, 64), BF16), pltpu.VMEM((D_HEADS, N_MOBA, tq), F32),
                        pltpu.VMEM((8 * D_HEADS, tq), F32), pltpu.VMEM((D_HEADS * VT_ROWS, tq), F32)],
        compiler_params=_params(("parallel", "arbitrary")),
        name="nsa_moba_attention",
    )(qc, kcmp, vcmpt, ks, vst, kw, vwt, gt, ovl_t, expand_t, qd, kd, vdt, kmean.reshape(bsz, N_MOBA, 512))


N_CONV_IN, N_ATTN_IN = 14, 14


def _mixers_kernel(*refs):
    conv_in = refs[:N_CONV_IN]
    attn_in = refs[N_CONV_IN:N_CONV_IN + N_ATTN_IN]
    ua_ref, ub_ref, oc_ref, od_ref = refs[N_CONV_IN + N_ATTN_IN:N_CONV_IN + N_ATTN_IN + 4]
    scratch = refs[N_CONV_IN + N_ATTN_IN + 4:]
    _conv_kernel(*conv_in, ua_ref, ub_ref, *scratch[:2])
    _attn_kernel(*attn_in, oc_ref, od_ref, *scratch[2:])


def _mixers(proj3, a_conv_w, b_conv_w, b_conv_b, b_ln_g, b_ln_b, qc, kcmp, vcmpt, ks, vst, kw, vwt, gt, qd, kd, vdt, kmean):
    bsz = proj3.shape[0]
    tq = ATT_T
    assert CONV_T == ATT_T
    c0 = COL_A // 512
    r = tq // HALO

    def cur(k):
        return pl.BlockSpec((None, tq, 512), lambda b, i, k=k: (b, i, c0 + k))

    def prev(k):
        return pl.BlockSpec((None, HALO, 512), lambda b, i, k=k: (b, jnp.maximum(i * r - 1, 0), c0 + k))

    def full(shape):
        return pl.BlockSpec(shape, lambda b, i: (0,) * len(shape))

    ovl_t, expand_t = _nsa_constants()
    keys = pl.BlockSpec((None, C_KV_HEADS, SEQ, 64), lambda b, i: (b, 0, 0, 0))
    vals = pl.BlockSpec((None, C_KV_HEADS * VT_ROWS, SEQ), lambda b, i: (b, 0, 0))
    qtile = pl.BlockSpec((None, tq, 512), lambda b, i: (b, i, 0))
    rows = C_GROUP * tq
    oshape = jax.ShapeDtypeStruct((bsz, SEQ, 512), BF16)
    conv_specs = [cur(0), cur(1), cur(2), cur(3), cur(4), prev(1), prev(2), prev(3), prev(4),
                  full((A_CONV, A_WIDTH)), full((B_CONV, B_WIDTH)), full((1, B_WIDTH)),
                  full((1, B_WIDTH)), full((1, B_WIDTH))]
    attn_specs = [qtile,
                  pl.BlockSpec((None, C_KV_HEADS, 128, 64), lambda b, i: (b, 0, 0, 0)),
                  pl.BlockSpec((None, C_KV_HEADS, 64, 128), lambda b, i: (b, 0, 0, 0)),
                  keys, vals, keys, vals,
                  pl.BlockSpec((None, 128, tq), lambda b, i: (b, 0, i)),
                  pl.BlockSpec((128, 128), lambda b, i: (0, 0)),
                  pl.BlockSpec((SEQ, 128), lambda b, i: (0, 0)),
                  qtile,
                  pl.BlockSpec((None, D_HEADS, SEQ, 64), lambda b, i: (b, 0, 0, 0)),
                  pl.BlockSpec((None, D_HEADS * VT_ROWS, SEQ), lambda b, i: (b, 0, 0)),
                  pl.BlockSpec((None, N_MOBA, 512), lambda b, i: (b, 0, 0))]
    assert len(conv_specs) == N_CONV_IN and len(attn_specs) == N_ATTN_IN
    return pl.pallas_call(
        _mixers_kernel,
        grid=(bsz, SEQ // tq),
        in_specs=conv_specs + attn_specs,
        out_specs=[qtile] * 4,
        out_shape=[oshape] * 4,
        scratch_shapes=[pltpu.VMEM((HALO + tq, 512), F32), pltpu.VMEM((tq, 512), F32),
                        pltpu.VMEM((C_KV_HEADS, rows, 64), BF16), pltpu.VMEM((C_KV_HEADS, 128, tq), BF16),
                        pltpu.VMEM((8 * 2 * C_KV_HEADS, rows), F32),
                        pltpu.VMEM((2 * C_KV_HEADS * VT_ROWS, rows), F32), pltpu.VMEM((C_KV_HEADS * 64, rows), F32),
                        pltpu.VMEM((D_HEADS, tq, 64), BF16), pltpu.VMEM((D_HEADS, N_MOBA, tq), F32),
                        pltpu.VMEM((8 * D_HEADS, tq), F32), pltpu.VMEM((D_HEADS * VT_ROWS, tq), F32)],
        compiler_params=_params(("parallel", "arbitrary")),
        name="token_mixers",
    )(proj3, proj3, proj3, proj3, proj3, proj3, proj3, proj3, proj3,
      a_conv_w, b_conv_w, b_conv_b.reshape(1, -1), b_ln_g.reshape(1, -1), b_ln_b.reshape(1, -1),
      qc, kcmp, vcmpt, ks, vst, kw, vwt, gt, ovl_t, expand_t, qd, kd, vdt, kmean.reshape(bsz, N_MOBA, 512))


def _merge_kernel(ua_ref, ub_ref, oc_ref, od_ref, mg_ref, x_ref, gt_ref,
                  wa_ref, wb_ref, wc_ref, wd_ref, wo_ref, o_ref):
    d = D_MODEL
    merged = jnp.zeros(x_ref.shape, F32)
    for k, (u_ref, w_ref) in enumerate(((ua_ref, wa_ref), (ub_ref, wb_ref), (oc_ref, wc_ref), (od_ref, wd_ref))):
        y = jnp.dot(u_ref[...], w_ref[...], preferred_element_type=F32)
        merged = merged + jax.nn.sigmoid(mg_ref[:, k * d:(k + 1) * d].astype(F32)) * y
    o_ref[...] = x_ref[...] + gt_ref[...] * jnp.dot(merged.astype(BF16), wo_ref[...], preferred_element_type=F32)


def _merge(ua, ub, oc, od, proj2, x2, mod3, wa, wb, wc, wd, wo, *, tm=512):
    t, d = x2.shape
    per_b = SEQ // tm
    act = pl.BlockSpec((tm, 512), lambda i: (i, 0))
    wspec = pl.BlockSpec((512, d), lambda i: (0, 0))
    return pl.pallas_call(
        _merge_kernel,
        grid=(t // tm,),
        in_specs=[act, act, act, act,
                  pl.BlockSpec((tm, 4 * d), lambda i: (i, COL_MERGE // (4 * d))),
                  pl.BlockSpec((tm, d), lambda i: (i, 0)),
                  pl.BlockSpec((None, 1, d), lambda i: (i // per_b, 0, 2)),
                  wspec, wspec, wspec, wspec,
                  pl.BlockSpec((d, d), lambda i: (0, 0))],
        out_specs=pl.BlockSpec((tm, d), lambda i: (i, 0)),
        out_shape=jax.ShapeDtypeStruct((t, d), F32),
        compiler_params=_params(("parallel",)),
        name="merge_out",
    )(ua, ub, oc, od, proj2, x2, mod3, wa, wb, wc, wd, wo)


def _pop_max(work, idx):
    m = jnp.max(work, axis=0, keepdims=True)
    first = jnp.min(jnp.where(work == m, idx, work.shape[0]), axis=0, keepdims=True)
    return m, idx == first


_PEER_PAIRS = [(i, j) for i in range(PEER_TOPK) for j in range(PEER_TOPK) if (i + 1) * (j + 1) <= PEER_TOPK]
_PEER_PAIR_ROWS = 56
NOT_TOP = 99.0


def _peer_sel_kernel(q_ref, keys_ref, grp_ref, n1_ref, e1_ref, r2_ref, e2_ref, st_ref):
    tt = q_ref.shape[0]
    half = PEER_QDIM // 2
    for hp in range(2 * PEER_HEADS):
        qh = q_ref[:, hp * half:(hp + 1) * half]
        qn = qh * lax.rsqrt(jnp.mean(qh * qh, axis=-1, keepdims=True) + NORM_EPS)
        st_ref[hp] = lax.dot_general(keys_ref[hp], qn, (((1,), (1,)), ((), ())), precision=HIGHEST,
                                     preferred_element_type=F32)

    idx = lax.broadcasted_iota(jnp.int32, (PEER_NKEYS, 128), 0)
    pidx = lax.broadcasted_iota(jnp.int32, (_PEER_PAIR_ROWS, 128), 0)
    pad_rows = jnp.full((_PEER_PAIR_ROWS - len(_PEER_PAIRS), 128), -jnp.inf, F32)

    def pop16(work, index, tie_safe, track):
        vals, order = [], jnp.full(work.shape, NOT_TOP, F32)
        for it in range(PEER_TOPK):
            if tie_safe:
                m, hit = _pop_max(work, index)
            else:
                m = jnp.max(work, axis=0, keepdims=True)
                hit = work == m
            work = jnp.where(hit, -jnp.inf, work)
            if track:
                order = jnp.where(hit, float(it), order)
            vals.append(m)
        removed = jnp.sum(jnp.where(work == -jnp.inf, 1.0, 0.0), axis=0, keepdims=True)
        return vals, order, removed

    def select(cs, tie_safe):
        wrong = jnp.zeros((1, 128), F32)
        n_pad = float(_PEER_PAIR_ROWS - len(_PEER_PAIRS))
        for h in range(PEER_HEADS):
            scores, tops, ranks = [], [], []
            for p in range(2):
                st = st_ref[2 * h + p, :, cs]
                vals, rank, removed = pop16(st, idx, tie_safe, track=tie_safe or p == 1)
                wrong = wrong + jnp.abs(removed - float(PEER_TOPK))
                scores.append(st)
                tops.append(vals)
                ranks.append(rank)
            cand = jnp.concatenate([tops[0][i] + tops[1][j] for (i, j) in _PEER_PAIRS] + [pad_rows], axis=0)
            best, order, removed = pop16(cand, pidx, tie_safe, track=tie_safe)
            wrong = wrong + jnp.abs(removed - (float(PEER_TOPK) + n_pad))
            picked = jnp.where((order < float(PEER_TOPK)) if tie_safe else (cand >= best[-1]), 1.0, 0.0)
            z = jnp.ones_like(best[0])
            for k in range(1, PEER_TOPK):
                z = z + jnp.exp(best[k] - best[0])
            count = jnp.dot(grp_ref[...], picked.astype(BF16), preferred_element_type=F32)
            n1 = jnp.zeros(scores[0].shape, F32)
            for i in reversed(range(PEER_TOPK)):
                at_i = (ranks[0] == float(i)) if tie_safe else (scores[0] >= tops[0][i])
                n1 = jnp.where(at_i, count[i:i + 1], n1)
            n1_ref[h, :, cs] = n1
            r2_ref[h, :, cs] = ranks[1]
            e1_ref[h, :, cs] = jnp.exp(scores[0] - tops[0][0]) / z
            e2_ref[h, :, cs] = jnp.exp(scores[1] - tops[1][0])
        return wrong

    def token_chunk(c, carry):
        cs = pl.ds(pl.multiple_of(c * 128, 128), 128)
        wrong = select(cs, tie_safe=False)

        @pl.when(jnp.max(wrong) > 0.0)
        def _():
            select(cs, tie_safe=True)

        return carry

    lax.fori_loop(0, tt // 128, token_chunk, 0)


def _peer_select(q2, subkeys, *, tt=512):
    t = q2.shape[0]
    grp = np.zeros((PEER_TOPK, _PEER_PAIR_ROWS), np.float32)
    for row, (i, _) in enumerate(_PEER_PAIRS):
        grp[i, row] = 1.0
    big = pl.BlockSpec((PEER_HEADS, PEER_NKEYS, tt), lambda i: (0, 0, i))
    bshape = jax.ShapeDtypeStruct((PEER_HEADS, PEER_NKEYS, t), F32)
    half = PEER_QDIM // 2
    return pl.pallas_call(
        _peer_sel_kernel,
        grid=(t // tt,),
        in_specs=[pl.BlockSpec((tt, PEER_HEADS * PEER_QDIM), lambda i: (i, 0)),
                  pl.BlockSpec((2 * PEER_HEADS, PEER_NKEYS, half), lambda i: (0, 0, 0)),
                  pl.BlockSpec((PEER_TOPK, _PEER_PAIR_ROWS), lambda i: (0, 0))],
        out_specs=[big, big, big, big],
        out_shape=[bshape, bshape, bshape, bshape],
        scratch_shapes=[pltpu.VMEM((2 * PEER_HEADS, PEER_NKEYS, tt), F32)],
        compiler_params=_params(("parallel",)),
        name="peer_select",
    )(q2, subkeys.reshape(2 * PEER_HEADS, PEER_NKEYS, half), jnp.asarray(grp, dtype=BF16))


PEER_SLAB = 256

def _peer_main_kernel(h_ref, u_ref, vt_ref, n1_ref, e1_ref, r2_ref, e2_ref, x_ref, gt_ref, fg_ref,
                      o_ref, acc_ref, act_ref, p_ref, ht_ref, *, final):
    j = pl.program_id(1)
    te, tt = act_ref.shape
    na = te // PEER_NKEYS

    @pl.when(j == 0)
    def _():
        acc_ref[...] = jnp.zeros(acc_ref.shape, F32)
        ht_ref[...] = h_ref[...].astype(F32).T.astype(BF16)

    a0 = pl.multiple_of(j * na, na)
    nslab = te // PEER_SLAB
    parts = []

    def act(s):
        ss = slice(s * PEER_SLAB, (s + 1) * PEER_SLAB)
        act_ref[ss, :] = jnp.dot(u_ref[ss, :], ht_ref[...], preferred_element_type=F32)

    def val(s):
        ss = slice(s * PEER_SLAB, (s + 1) * PEER_SLAB)
        parts.append(jnp.dot(vt_ref[:, ss], p_ref[ss, :], preferred_element_type=F32))

    def mask(al):
        rs = slice(al * PEER_NKEYS, (al + 1) * PEER_NKEYS)
        for c in range(tt // 128):
            cs = slice(c * 128, (c + 1) * 128)
            w = jnp.zeros((PEER_NKEYS, 128), F32)
            for hh in range(PEER_HEADS):
                n1 = n1_ref[hh, pl.ds(a0, na), cs][al:al + 1]
                e1 = e1_ref[hh, pl.ds(a0, na), cs][al:al + 1]
                w = w + jnp.where(r2_ref[hh, :, cs] < n1, e1 * e2_ref[hh, :, cs], 0.0)
            p_ref[rs, cs] = (w * _gelu(act_ref[rs, cs])).astype(BF16)

    per = PEER_SLAB // PEER_NKEYS
    act(0)
    act(1)
    for s in range(nslab):
        for k in range(per):
            mask(s * per + k)
            if k == 0 and s + 2 < nslab:
                act(s + 2)
        val(s)
    acc_ref[...] += functools.reduce(lambda x, y: x + y, parts)

    @pl.when(j == pl.num_programs(1) - 1)
    def _():
        y = x_ref[...] + gt_ref[...] * acc_ref[...].T
        if final:
            y = y * lax.rsqrt(jnp.mean(y * y, axis=-1, keepdims=True) + NORM_EPS) * fg_ref[...]
        o_ref[...] = y


def _peer_main(h2, u_bf, vt_bf, n1, e1, r2, e2, x2, mod3, final_g, *, final, tt=256, te=4096):
    t, d = x2.shape
    assert (te // PEER_NKEYS) % 8 == 0 and te % PEER_SLAB == 0
    per_b = SEQ // tt
    big = pl.BlockSpec((PEER_HEADS, PEER_NKEYS, tt), lambda i, j: (0, 0, i))
    return pl.pallas_call(
        functools.partial(_peer_main_kernel, final=final),
        grid=(t // tt, PEER_EXPERTS // te),
        in_specs=[pl.BlockSpec((tt, d), lambda i, j: (i, 0)),
                  pl.BlockSpec((te, d), lambda i, j: (j, 0)),
                  pl.BlockSpec((d, te), lambda i, j: (0, j)),
                  big, big, big, big,
                  pl.BlockSpec((tt, d), lambda i, j: (i, 0)),
                  pl.BlockSpec((None, 1, d), lambda i, j: (i // per_b, 0, 5)),
                  pl.BlockSpec((1, d), lambda i, j: (0, 0))],
        out_specs=pl.BlockSpec((tt, d), lambda i, j: (i, 0)),
        out_shape=jax.ShapeDtypeStruct((t, d), F32),
        scratch_shapes=[pltpu.VMEM((d, tt), F32), pltpu.VMEM((te, tt), F32), pltpu.VMEM((te, tt), BF16),
                        pltpu.VMEM((d, tt), BF16)],
        compiler_params=_params(("parallel", "arbitrary")),
        name="peer_experts",
    )(h2, u_bf, vt_bf, n1, e1, r2, e2, x2, mod3, final_g.reshape(1, d))


def _reorder_w_in(w):
    pad = jnp.zeros((w.shape[0], 1024 - 792), w.dtype)
    return jnp.concatenate([w[:, 5400:9496], w[:, 3072:3864], pad, w[:, 0:3072], w[:, 3864:5400]], axis=1)


def kernel(x, c, positions, mod_w, mod_b, norm_mix_g, norm_ffn_g, w_in, a_conv_w, a_out, b_conv_w, b_conv_b, b_ln_g, b_ln_b, b_out, c_cmp_pos, c_cmp_w1, c_cmp_w2, c_out, d_out, w_o, peer_wq, peer_subkeys, peer_u, peer_v, final_norm_g):
    bsz, s, d = x.shape
    assert s == SEQ and d == D_MODEL
    depth = mod_w.shape[0]
    t = bsz * s
    cos, sin = _rope_tables(positions)
    mod = _modulation(c, mod_w, mod_b)
    x2 = x.reshape(t, d)
    for l in range(depth):
        mod3 = mod[l].reshape(bsz, 1, 6 * d)
        proj2 = _norm_matmul(x2, norm_mix_g[l], mod3, 0, 1, _reorder_w_in(w_in[l].astype(BF16)), out_dtype=BF16,
                             tn=PROJ_COLS // 4)
        proj3 = proj2.reshape(bsz, s, PROJ_COLS)
        qc, kc, vc, ks, vst, kw, vwt, gt, qd, kd, vdt, kmean = _prep(proj3, cos, sin)
        kcmp, vcmpt = _compress(kc, vc, c_cmp_pos[l], c_cmp_w1[l], c_cmp_w2[l])
        ua, ub, oc, od = _mixers(proj3, a_conv_w[l], b_conv_w[l], b_conv_b[l], b_ln_g[l], b_ln_b[l],
                                 qc, kcmp, vcmpt, ks, vst, kw, vwt, gt, qd, kd, vdt, kmean)
        x2 = _merge(ua.reshape(t, 512), ub.reshape(t, 512), oc.reshape(t, 512), od.reshape(t, 512),
                    proj2, x2, mod3, a_out[l].astype(BF16), b_out[l].astype(BF16), c_out[l].astype(BF16),
                    d_out[l].astype(BF16), w_o[l].astype(BF16))
        q2, h2 = _norm_matmul(x2, norm_ffn_g[l], mod3, 3, 4, peer_wq[l].astype(BF16), emit_h=True,
                              tn=PEER_HEADS * PEER_QDIM)
        n1, e1, r2, e2 = _peer_select(q2, peer_subkeys[l])
        x2 = _peer_main(h2, peer_u[l].astype(BF16), peer_v[l].T.astype(BF16), n1, e1, r2, e2,
                        x2, mod3, final_norm_g, final=(l == depth - 1))
    return x2.reshape(bsz, s, d)
```
